```python
import jax
import jax.numpy as jnp
from jax import lax
import numpy as np

D_MODEL = 1024
BATCH = 16
SEQ = 256
DEPTH = 4
DEC_BATCH = 4
DEC_SEQ = 2048
PAST_LEN = 512

GRID_W = 64
N_EVEN = (DEPTH + 1) // 2
N_ODD = DEPTH // 2
HEAD_DIM = 64
N_HEADS_A = 8
N_KV_A = 2
GROUP_A = N_HEADS_A // N_KV_A
WINDOW = 128
ATTN_BLOCK = 128
SCALE_A = HEAD_DIM ** -0.5
ROPE_BASE = 10000.0
ROPE_PAIRS = HEAD_DIM // 4
N_HEADS_B = 4
HGRN_DK = 128
HGRN_DV = 128
HGRN_CHUNK = 16
CONV_WIDTH = 31
N_EXPERTS = 32
TOP_K = 4
D_EXPERT = D_MODEL
SWIGLU_LIMIT = 7.0
SWIGLU_ALPHA = 1.702
MOE_BLOCK = 128
LN_EPS = 1e-5
RMS_EPS = 1e-6
MASK_VALUE = -1e9
LB_FLOOR = 1e-30
DN_ALPHA = (2 * DEPTH) ** 0.25
DN_BETA = (8 * DEPTH) ** -0.25
A_Q = N_HEADS_A * HEAD_DIM
A_KV = N_KV_A * HEAD_DIM
B_QK = N_HEADS_B * HGRN_DK
B_V = N_HEADS_B * HGRN_DV
IN_SIZES = (A_Q, A_KV, A_KV, B_QK, B_V, B_QK, B_QK, B_V)
IN_SPLITS = tuple(sum(IN_SIZES[:i + 1]) for i in range(len(IN_SIZES) - 1))
D_IN_EVEN = sum(IN_SIZES)
D_MIX = A_Q + B_V

kernel_name = 'hybrid_dit_swa_hgrn2_conformer_moe_step'


def layer_norm(x, g, b):
    xf = x.astype(jnp.float32)
    mu = jnp.mean(xf, -1, keepdims=True)
    var = jnp.mean(jnp.square(xf - mu), -1, keepdims=True)
    return ((xf - mu) * lax.rsqrt(var + LN_EPS) * g + b).astype(x.dtype)


def modulate(x, shift, scale):
    return x * (1 + scale) + shift


def adaln(cond, w, b):
    m = jax.nn.silu(cond) @ w + b
    return jnp.split(m[..., None, :], 6, axis=-1)


def axial_rope(x):
    t_len = x.shape[1]
    rows = t_len // GRID_W
    row, col = jnp.meshgrid(jnp.arange(rows), jnp.arange(GRID_W), indexing='ij')
    pos = jnp.stack([row.reshape(-1), col.reshape(-1)], -1).astype(jnp.float32)
    inv_freq = ROPE_BASE ** (-jnp.arange(ROPE_PAIRS, dtype=jnp.float32) / ROPE_PAIRS)
    ang = pos[:, :, None] * inv_freq
    cos = jnp.cos(ang)[None, :, None]
    sin = jnp.sin(ang)[None, :, None]
    xr = x.astype(jnp.float32).reshape(x.shape[:3] + (2, 2, ROPE_PAIRS))
    x1, x2 = xr[..., 0, :], xr[..., 1, :]
    out = jnp.stack([x1 * cos - x2 * sin, x2 * cos + x1 * sin], axis=-2)
    return out.reshape(x.shape).astype(x.dtype)


def sink_softmax(s, sink):
    sk = sink.astype(jnp.float32).reshape(N_KV_A, GROUP_A, 1, 1)
    m = jnp.maximum(jnp.max(s, -1, keepdims=True), sk)
    p = jnp.exp(s - m)
    return p / (jnp.sum(p, -1, keepdims=True) + jnp.exp(sk - m))


def context_attention(q, k, v, sink):
    b, s_len = q.shape[:2]
    nq = s_len // ATTN_BLOCK
    qb = q.reshape(b, nq, ATTN_BLOCK, N_KV_A, GROUP_A, HEAD_DIM).swapaxes(0, 1)

    def one_block(qblk):
        sc = jnp.einsum('bqkgd,bskd->bkgqs', qblk, k, preferred_element_type=jnp.float32) * SCALE_A
        p = sink_softmax(sc, sink).astype(v.dtype)
        return jnp.einsum('bkgqs,bskd->bqkgd', p, v)

    out = lax.map(one_block, qb)
    return out.swapaxes(0, 1).reshape(b, s_len, A_Q)


def latent_attention(q, k, v, ck, cv, sink):
    b, t_len = q.shape[:2]
    nb = t_len // ATTN_BLOCK
    pad = jnp.zeros((b, ATTN_BLOCK, N_KV_A, HEAD_DIM), k.dtype)

    def band(a):
        blk = jnp.concatenate([pad, a, pad], 1).reshape(b, nb + 2, ATTN_BLOCK, N_KV_A, HEAD_DIM)
        return jnp.concatenate([blk[:, :-2], blk[:, 1:-1], blk[:, 2:]], axis=2)

    kb, vb = band(k), band(v)
    qb = q.reshape(b, nb, ATTN_BLOCK, N_KV_A, GROUP_A, HEAD_DIM)
    qi = jnp.arange(ATTN_BLOCK)[:, None]
    kj = jnp.arange(3 * ATTN_BLOCK)[None, :]
    rel = kj - ATTN_BLOCK - qi
    key_pos = (jnp.arange(nb)[:, None, None] - 1) * ATTN_BLOCK + kj[None]
    mask = (jnp.abs(rel) <= WINDOW)[None] & (key_pos >= 0) & (key_pos < t_len)
    s_band = jnp.einsum('bnqkgd,bnskd->bnkgqs', qb, kb, preferred_element_type=jnp.float32)
    s_band = jnp.where(mask[None, :, None, None], s_band * SCALE_A, MASK_VALUE)
    s_ctx = jnp.einsum('bnqkgd,bpkd->bnkgqp', qb, ck, preferred_element_type=jnp.float32) * SCALE_A
    p = sink_softmax(jnp.concatenate([s_band, s_ctx], -1), sink).astype(v.dtype)
    n_band = 3 * ATTN_BLOCK
    out = (jnp.einsum('bnkgqs,bnskd->bnqkgd', p[..., :n_band], vb)
           + jnp.einsum('bnkgqp,bpkd->bnqkgd', p[..., n_band:], cv))
    return out.reshape(b, t_len, A_Q)


def hgrn_chunk_scan(q, k, v, log_f, s0):
    b, h, t_len, dk = q.shape
    dv = v.shape[-1]
    n = t_len // HGRN_CHUNK
    q, k, log_f = (a.reshape(b, h, n, HGRN_CHUNK, dk) for a in (q, k, log_f))
    v = v.reshape(b, h, n, HGRN_CHUNK, dv)
    cum = jnp.cumsum(log_f, axis=3)
    cum_last = cum[:, :, :, -1]
    causal = jnp.tril(jnp.ones((HGRN_CHUNK, HGRN_CHUNK), dtype=bool))[:, :, None]
    diff = cum[:, :, :, :, None, :] - cum[:, :, :, None, :, :]
    decay = jnp.exp(jnp.where(causal, diff, MASK_VALUE))
    scores = jnp.einsum('bhntk,bhnsk,bhntsk->bhnts', q, k, decay)
    o_intra = jnp.einsum('bhnts,bhnsv->bhntv', scores, v)
    u = jnp.einsum('bhnck,bhncv->bhnkv', k * jnp.exp(cum_last[:, :, :, None] - cum), v)

    def step(s, inp):
        dec, du = inp
        return dec[..., None] * s + du, s

    s_final, s_start = lax.scan(step, s0.astype(jnp.float32),
                                (jnp.moveaxis(jnp.exp(cum_last), 2, 0), jnp.moveaxis(u, 2, 0)))
    o_inter = jnp.einsum('bhnck,nbhkv->bhncv', q * jnp.exp(cum), s_start)
    return (o_intra + o_inter).reshape(b, h, t_len, dv), s_final


def to_heads(a, d):
    b, t = a.shape[:2]
    return a.reshape(b, t, N_HEADS_B, d).transpose(0, 2, 1, 3).astype(jnp.float32)


def hgrn_gate(f_pre, lb):
    log_f = jnp.logaddexp(jnp.log(jnp.maximum(lb, LB_FLOOR)), jnp.log1p(-lb) + jax.nn.log_sigmoid(f_pre))
    return -jnp.expm1(log_f), log_f


def hgrn_mixer(q_pre, i_pre, ff_pre, fb_pre, g_pre, lb_f, lb_b, norm_w, s0_f, s0_b):
    q = jax.nn.silu(to_heads(q_pre, HGRN_DK))
    v = to_heads(i_pre, HGRN_DV)
    k_f, logf_f = hgrn_gate(to_heads(ff_pre, HGRN_DK), lb_f.astype(jnp.float32).reshape(N_HEADS_B, 1, HGRN_DK))
    k_b, logf_b = hgrn_gate(to_heads(fb_pre, HGRN_DK), lb_b.astype(jnp.float32).reshape(N_HEADS_B, 1, HGRN_DK))
    o_f, s_f = hgrn_chunk_scan(q, k_f, v, logf_f, s0_f)
    flip = lambda a: jnp.flip(a, axis=2)
    o_b, s_b = hgrn_chunk_scan(flip(q), flip(k_b), flip(v), flip(logf_b), s0_b)
    o = o_f + flip(o_b)
    o = o * lax.rsqrt(jnp.mean(o * o, -1, keepdims=True) + RMS_EPS) * norm_w.astype(jnp.float32).reshape(N_HEADS_B, 1, HGRN_DV)
    b, t = g_pre.shape[:2]
    o = o.transpose(0, 2, 1, 3).reshape(b, t, B_V)
    return (o * jax.nn.silu(g_pre.astype(jnp.float32))).astype(g_pre.dtype), s_f, s_b


def split_even(h, w_in, b_in):
    b, t = h.shape[:2]
    qa, ka, va, qb, ib, ff, fb, go = jnp.split(h @ w_in + b_in, IN_SPLITS, axis=-1)
    qa = qa.reshape(b, t, N_HEADS_A, HEAD_DIM)
    ka = ka.reshape(b, t, N_KV_A, HEAD_DIM)
    va = va.reshape(b, t, N_KV_A, HEAD_DIM)
    return qa, ka, va, qb, ib, ff, fb, go


def even_mixer_context(h, w_in, b_in, sink, lb_f, lb_b, norm_w, w_out):
    qa, ka, va, qb, ib, ff, fb, go = split_even(h, w_in, b_in)
    attn = context_attention(qa, ka, va, sink)
    s0 = jnp.zeros((h.shape[0], N_HEADS_B, HGRN_DK, HGRN_DV), jnp.float32)
    rec, s_f, s_b = hgrn_mixer(qb, ib, ff, fb, go, lb_f, lb_b, norm_w, s0, s0)
    y = jnp.concatenate([attn, rec], -1) @ w_out
    return y, ka, va, jnp.stack([s_f, s_b], axis=1)


def even_mixer_latent(h, ck, cv, cs, w_in, b_in, sink, lb_f, lb_b, norm_w, w_out):
    qa, ka, va, qb, ib, ff, fb, go = split_even(h, w_in, b_in)
    attn = latent_attention(axial_rope(qa), axial_rope(ka), va, ck, cv, sink)
    rec, _, _ = hgrn_mixer(qb, ib, ff, fb, go, lb_f, lb_b, norm_w, cs[:, 0], cs[:, 1])
    return jnp.concatenate([attn, rec], -1) @ w_out


def conformer_conv(h, w_in, b_in, dw, dw_b, ln_g, ln_b, w_out, b_out):
    a, gt = jnp.split(h @ w_in + b_in, 2, axis=-1)
    u = a * jax.nn.sigmoid(gt)
    u = lax.conv_general_dilated(u, dw[:, None, :], window_strides=(1,),
                                 padding=((CONV_WIDTH // 2, CONV_WIDTH // 2),),
                                 dimension_numbers=('NWC', 'WIO', 'NWC'),
                                 feature_group_count=D_MODEL) + dw_b
    u = jax.nn.silu(layer_norm(u, ln_g, ln_b))
    return u @ w_out + b_out


def moe_ffn(x, w_r, b_r, w_g, b_g, w_u, b_u, w_d, b_d):
    shp = x.shape
    xt = x.reshape(-1, D_MODEL)
    n_tok = xt.shape[0]
    logits = (xt @ w_r + b_r).astype(jnp.float32)
    top_val, top_idx = lax.top_k(logits, TOP_K)
    gates = jax.nn.softmax(top_val, axis=-1)
    n_asg = n_tok * TOP_K
    flat_e = top_idx.reshape(-1)
    order = jnp.argsort(flat_e)
    sorted_e = flat_e[order]
    sorted_tok = (order // TOP_K).astype(jnp.int32)
    sorted_gate = gates.reshape(-1)[order]
    counts = jnp.bincount(flat_e, length=N_EXPERTS)
    starts = jnp.cumsum(counts) - counts
    padded = (counts + MOE_BLOCK - 1) // MOE_BLOCK * MOE_BLOCK
    padded_ends = jnp.cumsum(padded)
    padded_starts = padded_ends - padded
    dest = padded_starts[sorted_e] + jnp.arange(n_asg) - starts[sorted_e]
    n_rows = n_asg + N_EXPERTS * MOE_BLOCK
    n_blocks = n_rows // MOE_BLOCK
    row_tok = jnp.full((n_rows,), n_tok, jnp.int32).at[dest].set(sorted_tok)
    row_gate = jnp.zeros((n_rows,), jnp.float32).at[dest].set(sorted_gate)
    block_e = jnp.minimum(jnp.searchsorted(padded_ends, jnp.arange(n_blocks) * MOE_BLOCK, side='right'),
                          N_EXPERTS - 1)
    x_rows = jnp.concatenate([xt, jnp.zeros((1, D_MODEL), xt.dtype)], 0)[row_tok]
    x_rows = x_rows.reshape(n_blocks, MOE_BLOCK, D_MODEL)

    def expert_block(args):
        xb, e = args
        gt = jnp.minimum(xb @ w_g[e] + b_g[e], SWIGLU_LIMIT)
        up = jnp.clip(xb @ w_u[e] + b_u[e], -SWIGLU_LIMIT, SWIGLU_LIMIT)
        return ((up + 1) * gt * jax.nn.sigmoid(SWIGLU_ALPHA * gt)) @ w_d[e] + b_d[e]

    y_rows = lax.map(expert_block, (x_rows, block_e)).reshape(n_rows, D_MODEL)
    y = jnp.zeros((n_tok + 1, D_MODEL), jnp.float32).at[row_tok].add(y_rows.astype(jnp.float32) * row_gate[:, None])
    return y[:n_tok].reshape(shp).astype(x.dtype)


def setup_inputs(seed: int = 0) -> dict:
    key = jax.random.key(seed)
    ks = iter(jax.random.split(key, 48))
    D = D_MODEL

    def nrm(shape, s=1.0):
        return jax.random.normal(next(ks), shape, jnp.float32) * s

    return {
        'x_prompt': nrm((BATCH, SEQ, D)),
        'x_sample': nrm((DEC_BATCH, DEC_SEQ, D)),
        'c': nrm((DEC_BATCH, D)),
        'cache_k': nrm((DEC_BATCH, N_EVEN, PAST_LEN, N_KV_A, HEAD_DIM)),
        'cache_v': nrm((DEC_BATCH, N_EVEN, PAST_LEN, N_KV_A, HEAD_DIM)),
        'state_hgrn': nrm((DEC_BATCH, N_EVEN, 2, N_HEADS_B, HGRN_DK, HGRN_DV), 0.5),
        'c_ctx': nrm((D,)),
        'w_ada': nrm((DEPTH, D, 6 * D), D ** -0.5),
        'b_ada': nrm((DEPTH, 6 * D), 0.01),
        'ln_g': 1.0 + nrm((DEPTH, 2, D), 0.02),
        'ln_b': nrm((DEPTH, 2, D), 0.02),
        'w_in_even': nrm((N_EVEN, D, D_IN_EVEN), D ** -0.5),
        'b_in_even': nrm((N_EVEN, D_IN_EVEN), 0.01),
        'attn_sink': nrm((N_EVEN, N_HEADS_A), 0.5),
        'hgrn_lb': nrm((2, N_EVEN, B_QK), 0.5),
        'hgrn_norm': 1.0 + nrm((N_EVEN, B_V), 0.02),
        'w_out_even': nrm((N_EVEN, D_MIX, D), D_MIX ** -0.5 * DN_BETA),
        'conv_w_in': nrm((N_ODD, D, 2 * D), D ** -0.5),
        'conv_b_in': nrm((N_ODD, 2 * D), 0.01),
        'conv_dw': nrm((N_ODD, CONV_WIDTH, D), CONV_WIDTH ** -0.5),
        'conv_dw_b': nrm((N_ODD, D), 0.01),
        'conv_ln_g': 1.0 + nrm((N_ODD, D), 0.02),
        'conv_ln_b': nrm((N_ODD, D), 0.02),
        'conv_w_out': nrm((N_ODD, D, D), D ** -0.5 * DN_BETA),
        'conv_b_out': nrm((N_ODD, D), 0.01),
        'router_w': nrm((DEPTH, D, N_EXPERTS), D ** -0.5),
        'router_b': nrm((DEPTH, N_EXPERTS), 0.01),
        'moe_w_gate': nrm((DEPTH, N_EXPERTS, D, D_EXPERT), D ** -0.5),
        'moe_b_gate': nrm((DEPTH, N_EXPERTS, D_EXPERT), 0.01),
        'moe_w_up': nrm((DEPTH, N_EXPERTS, D, D_EXPERT), D ** -0.5),
        'moe_b_up': nrm((DEPTH, N_EXPERTS, D_EXPERT), 0.01),
        'moe_w_down': nrm((DEPTH, N_EXPERTS, D_EXPERT, D), D_EXPERT ** -0.5 * DN_BETA),
        'moe_b_down': nrm((DEPTH, N_EXPERTS, D), 0.01),
    }


def reference(x_prompt, x_sample, c, cache_k, cache_v, state_hgrn, c_ctx, w_ada, b_ada, ln_g, ln_b,
              w_in_even, b_in_even, attn_sink, hgrn_lb, hgrn_norm, w_out_even,
              conv_w_in, conv_b_in, conv_dw, conv_dw_b, conv_ln_g, conv_ln_b, conv_w_out, conv_b_out,
              router_w, router_b, moe_w_gate, moe_b_gate, moe_w_up, moe_b_up, moe_w_down, moe_b_down):
    lb = jax.nn.softmax(hgrn_lb.astype(jnp.float32), axis=1)
    lb = jnp.cumsum(lb, axis=1) - lb[:, :1]
    xp, xs = x_prompt, x_sample
    new_k, new_v, new_s = [], [], []
    for layer in range(DEPTH):
        j = layer // 2
        sh1p, sc1p, g1p, sh2p, sc2p, g2p = adaln(c_ctx, w_ada[layer], b_ada[layer])
        sh1s, sc1s, g1s, sh2s, sc2s, g2s = adaln(c, w_ada[layer], b_ada[layer])
        hp = modulate(xp, sh1p, sc1p)
        hs = modulate(xs, sh1s, sc1s)
        if layer % 2 == 0:
            ev = (w_in_even[j], b_in_even[j], attn_sink[j], lb[0, j], lb[1, j], hgrn_norm[j], w_out_even[j])
            yp, kc, vc, stc = even_mixer_context(hp, *ev)
            ys = even_mixer_latent(hs, cache_k[:, j], cache_v[:, j], state_hgrn[:, j], *ev)
            new_k.append(kc)
            new_v.append(vc)
            new_s.append(stc)
        else:
            cp = (conv_w_in[j], conv_b_in[j], conv_dw[j], conv_dw_b[j], conv_ln_g[j], conv_ln_b[j],
                  conv_w_out[j], conv_b_out[j])
            yp = conformer_conv(hp, *cp)
            ys = conformer_conv(hs, *cp)
        xp = layer_norm(DN_ALPHA * xp + g1p * yp, ln_g[layer, 0], ln_b[layer, 0])
        xs = layer_norm(DN_ALPHA * xs + g1s * ys, ln_g[layer, 0], ln_b[layer, 0])
        mw = (router_w[layer], router_b[layer], moe_w_gate[layer], moe_b_gate[layer], moe_w_up[layer],
              moe_b_up[layer], moe_w_down[layer], moe_b_down[layer])
        xp = layer_norm(DN_ALPHA * xp + g2p * moe_ffn(modulate(xp, sh2p, sc2p), *mw), ln_g[layer, 1], ln_b[layer, 1])
        xs = layer_norm(DN_ALPHA * xs + g2s * moe_ffn(modulate(xs, sh2s, sc2s), *mw), ln_g[layer, 1], ln_b[layer, 1])
    return (xp, xs, jnp.stack(new_k, axis=1), jnp.stack(new_v, axis=1), jnp.stack(new_s, axis=1))
```

```python
import functools

import jax
import jax.numpy as jnp
import numpy as np
from jax import lax
from jax.experimental import pallas as pl
from jax.experimental.pallas import tpu as pltpu

D_MODEL = 1024
BATCH = 16
SEQ = 256
DEPTH = 4
DEC_BATCH = 4
DEC_SEQ = 2048
PAST_LEN = 512
GRID_W = 64
N_EVEN = (DEPTH + 1) // 2
N_ODD = DEPTH // 2
HEAD_DIM = 64
N_HEADS_A = 8
N_KV_A = 2
GROUP_A = N_HEADS_A // N_KV_A
WINDOW = 128
ATTN_BLOCK = 128
SCALE_A = HEAD_DIM ** -0.5
ROPE_BASE = 10000.0
ROPE_PAIRS = HEAD_DIM // 4
N_HEADS_B = 4
HGRN_DK = 128
HGRN_DV = 128
CONV_WIDTH = 31
N_EXPERTS = 32
TOP_K = 4
D_EXPERT = D_MODEL
SWIGLU_LIMIT = 7.0
SWIGLU_ALPHA = 1.702
LN_EPS = 1e-5
RMS_EPS = 1e-6
MASK_VALUE = -1e9
LB_FLOOR = 1e-30
DN_ALPHA = (2 * DEPTH) ** 0.25
A_Q = N_HEADS_A * HEAD_DIM
A_KV = N_KV_A * HEAD_DIM
B_QK = N_HEADS_B * HGRN_DK
B_V = N_HEADS_B * HGRN_DV
IN_SIZES = (A_Q, A_KV, A_KV, B_QK, B_V, B_QK, B_QK, B_V)
D_IN_EVEN = sum(IN_SIZES)

N_PROMPT = BATCH * SEQ
N_SAMPLE = DEC_BATCH * DEC_SEQ
N_TOK = N_PROMPT + N_SAMPLE
N_COND = 1 + DEC_BATCH
COND_ROWS = 8

LANES = 128
SUBLANES = 8
VMEM_LIMIT = 56 * 1024 * 1024

ROW_BLOCK = 256
HG_BLOCK = 128
HG_SUB = 16
MOE_ROWS = 256
MOE_BLOCKS = (N_TOK * TOP_K + N_EXPERTS * (MOE_ROWS - 1)) // MOE_ROWS + 1
MOE_R = MOE_BLOCKS * MOE_ROWS
COMB_TOK = 128

F32 = jnp.float32
BF16 = jnp.bfloat16
HIGHEST = lax.Precision.HIGHEST


def _cond_of_block(i, rows):
    start = i * rows
    return jnp.where(start < N_PROMPT, 0, 1 + (start - N_PROMPT) // DEC_SEQ)


def _cparams(sem):
    return pltpu.CompilerParams(dimension_semantics=sem, vmem_limit_bytes=VMEM_LIMIT)


def _layer_norm(z, g, b):
    mu = jnp.mean(z, axis=-1, keepdims=True)
    zc = z - mu
    var = jnp.mean(zc * zc, axis=-1, keepdims=True)
    return zc * lax.rsqrt(var + LN_EPS) * g + b


def _dot(a, b):
    return jnp.dot(a, b, preferred_element_type=F32)


def _dot_nt(a, b):
    return lax.dot_general(a, b, (((1,), (1,)), ((), ())), preferred_element_type=F32)


ADA_TN = 1536


def _adaln_kernel(cond_ref, w_ref, b_ref, o_ref):
    c = cond_ref[...]
    s = c * jax.nn.sigmoid(c)
    o_ref[0] = jnp.dot(s, w_ref[0], precision=HIGHEST, preferred_element_type=F32) + b_ref[0]


def adaln_all(cond, w_ada, b_ada):
    n_out = 6 * D_MODEL
    return pl.pallas_call(
        _adaln_kernel,
        grid=(DEPTH, n_out // ADA_TN),
        in_specs=[
            pl.BlockSpec((COND_ROWS, D_MODEL), lambda l, n: (0, 0)),
            pl.BlockSpec((1, D_MODEL, ADA_TN), lambda l, n: (l, 0, n)),
            pl.BlockSpec((1, 1, ADA_TN), lambda l, n: (l, 0, n)),
        ],
        out_specs=pl.BlockSpec((1, COND_ROWS, ADA_TN), lambda l, n: (l, 0, n)),
        out_shape=jax.ShapeDtypeStruct((DEPTH, COND_ROWS, n_out), F32),
        compiler_params=_cparams(("parallel", "parallel")),
        name="adaln",
    )(cond, w_ada, b_ada.reshape(DEPTH, 1, n_out))


def _rope_tables():
    t = np.arange(DEC_SEQ)
    d = np.arange(LANES) % HEAD_DIM
    axis = d // (2 * ROPE_PAIRS)
    half = (d // ROPE_PAIRS) % 2
    pair = d % ROPE_PAIRS
    pos = jnp.where(axis[None, :] == 0, (t // GRID_W)[:, None], (t % GRID_W)[:, None]).astype(F32)
    inv_freq = ROPE_BASE ** (-jnp.arange(ROPE_PAIRS, dtype=F32) / ROPE_PAIRS)
    ang = pos * inv_freq[pair][None, :]
    sign = jnp.where(half[None, :] == 0, -1.0, 1.0).astype(F32)
    return jnp.cos(ang), jnp.sin(ang) * sign


def _rope(x, cos, sin_signed):
    lane = lax.broadcasted_iota(jnp.int32, x.shape, 1)
    first_half = (lane // ROPE_PAIRS) % 2 == 0
    partner = jnp.where(first_half, pltpu.roll(x, LANES - ROPE_PAIRS, 1), pltpu.roll(x, ROPE_PAIRS, 1))
    return x * cos + partner * sin_signed


def _inproj_kernel(x_ref, mod_ref, w_ref, b_ref, cos_ref, sin_ref,
                   q_ref, k_ref, v_ref, qb_ref, ib_ref, ff_ref, fb_ref, go_ref):
    i = pl.program_id(0)
    shift = mod_ref[0, 0:1, :]
    scale = mod_ref[0, 1:2, :]
    h = (x_ref[...] * (1.0 + scale) + shift).astype(BF16)
    y = _dot(h, w_ref[...]) + b_ref[...]
    offs = np.cumsum((0,) + IN_SIZES)
    q = y[:, offs[0]:offs[1]]
    k = y[:, offs[1]:offs[2]]
    v_ref[...] = y[:, offs[2]:offs[3]]
    qb_ref[...] = y[:, offs[3]:offs[4]]
    ib_ref[...] = y[:, offs[4]:offs[5]]
    ff_ref[...] = y[:, offs[5]:offs[6]]
    fb_ref[...] = y[:, offs[6]:offs[7]]
    go_ref[...] = y[:, offs[7]:offs[8]]
    is_latent = i * ROW_BLOCK >= N_PROMPT

    @pl.when(jnp.logical_not(is_latent))
    def _():
        q_ref[...] = q
        k_ref[...] = k

    @pl.when(is_latent)
    def _():
        cos = cos_ref[...]
        sin = sin_ref[...]
        for c in range(A_Q // LANES):
            q_ref[:, c * LANES:(c + 1) * LANES] = _rope(q[:, c * LANES:(c + 1) * LANES], cos, sin)
        k_ref[...] = _rope(k, cos, sin)


def inproj_even(x, mods, w_bf, b, cos, sin):
    nblk = N_TOK // ROW_BLOCK
    pos_blocks = DEC_SEQ // ROW_BLOCK

    def pos_map(i):
        return (jnp.maximum(i - N_PROMPT // ROW_BLOCK, 0) % pos_blocks, 0)

    row = lambda i: (i, 0)
    widths = (A_Q, A_KV, A_KV, B_QK, B_V, B_QK, B_QK, B_V)
    return pl.pallas_call(
        _inproj_kernel,
        grid=(nblk,),
        in_specs=[
            pl.BlockSpec((ROW_BLOCK, D_MODEL), row),
            pl.BlockSpec((1, 6, D_MODEL), lambda i: (_cond_of_block(i, ROW_BLOCK), 0, 0)),
            pl.BlockSpec((D_MODEL, D_IN_EVEN), lambda i: (0, 0)),
            pl.BlockSpec((1, D_IN_EVEN), lambda i: (0, 0)),
            pl.BlockSpec((ROW_BLOCK, LANES), pos_map),
            pl.BlockSpec((ROW_BLOCK, LANES), pos_map),
        ],
        out_specs=[pl.BlockSpec((ROW_BLOCK, w), row) for w in widths],
        out_shape=[jax.ShapeDtypeStruct((N_TOK, w), F32) for w in widths],
        compiler_params=_cparams(("parallel",)),
        name="inproj_even",
    )(x, mods, w_bf, b.reshape(1, D_IN_EVEN), cos, sin)


def _sink_attend(q, keys, vals, sink, masks):
    scores = []
    for kk, mask in zip(keys, masks):
        s = _dot_nt(q, kk) * SCALE_A
        if mask is not None:
            s = jnp.where(mask, s, MASK_VALUE)
        scores.append(s)
    m = sink
    for s in scores:
        m = jnp.maximum(m, jnp.max(s, axis=-1, keepdims=True))
    denom = jnp.exp(sink - m)
    acc = None
    for s, vv in zip(scores, vals):
        p = jnp.exp(s - m)
        denom = denom + jnp.sum(p, axis=-1, keepdims=True)
        pv = _dot(p.astype(BF16), vv)
        acc = pv if acc is None else acc + pv
    return acc / denom


def _attn_ctx_kernel(sink_ref, q_ref, k_ref, v_ref, o_ref):
    k = k_ref[...].astype(BF16)
    v = v_ref[...].astype(BF16)
    q = q_ref[...].astype(BF16)
    for h in range(N_HEADS_A):
        kv = h // GROUP_A
        qh = q[:, h * HEAD_DIM:(h + 1) * HEAD_DIM]
        kh = k[:, kv * HEAD_DIM:(kv + 1) * HEAD_DIM]
        vh = v[:, kv * HEAD_DIM:(kv + 1) * HEAD_DIM]
        o_ref[:, h * HEAD_DIM:(h + 1) * HEAD_DIM] = _sink_attend(qh, [kh], [vh], sink_ref[h], [None])


def attn_context(sink, q, k, v):
    row = lambda b: (b, 0)
    return pl.pallas_call(
        _attn_ctx_kernel,
        grid=(BATCH,),
        in_specs=[
            pl.BlockSpec(memory_space=pltpu.SMEM),
            pl.BlockSpec((SEQ, A_Q), row),
            pl.BlockSpec((SEQ, A_KV), row),
            pl.BlockSpec((SEQ, A_KV), row),
        ],
        out_specs=pl.BlockSpec((SEQ, A_Q), row),
        out_shape=jax.ShapeDtypeStruct((N_PROMPT, A_Q), F32),
        compiler_params=_cparams(("parallel",)),
        name="attn_context",
    )(sink, q, k, v)


def _attn_lat_kernel(sink_ref, q_ref, kp_ref, kc_ref, kn_ref, vp_ref, vc_ref, vn_ref, ck_ref, cv_ref, o_ref):
    n = pl.program_id(1)
    nb = DEC_SEQ // ATTN_BLOCK
    qi = lax.broadcasted_iota(jnp.int32, (ATTN_BLOCK, ATTN_BLOCK), 0)
    kj = lax.broadcasted_iota(jnp.int32, (ATTN_BLOCK, ATTN_BLOCK), 1)
    mask_prev = jnp.logical_and(kj - qi >= ATTN_BLOCK - WINDOW, n > 0)
    mask_next = jnp.logical_and(kj - qi <= WINDOW - ATTN_BLOCK, n < nb - 1)
    masks = [mask_prev, None, mask_next, None]
    q = q_ref[...].astype(BF16)
    kband = [r[...].astype(BF16) for r in (kp_ref, kc_ref, kn_ref)]
    vband = [r[...].astype(BF16) for r in (vp_ref, vc_ref, vn_ref)]
    ck = ck_ref[0].astype(BF16)
    cv = cv_ref[0].astype(BF16)
    for h in range(N_HEADS_A):
        kv = h // GROUP_A
        sl = slice(kv * HEAD_DIM, (kv + 1) * HEAD_DIM)
        qh = q[:, h * HEAD_DIM:(h + 1) * HEAD_DIM]
        keys = [kb[:, sl] for kb in kband] + [ck[:, sl]]
        vals = [vb[:, sl] for vb in vband] + [cv[:, sl]]
        o_ref[:, h * HEAD_DIM:(h + 1) * HEAD_DIM] = _sink_attend(qh, keys, vals, sink_ref[h], masks)


def attn_latent(sink, q, k, v, cache_k, cache_v):
    nb = DEC_SEQ // ATTN_BLOCK
    base = N_PROMPT // ATTN_BLOCK

    def blk(delta):
        return lambda b, n: (base + b * nb + jnp.clip(n + delta, 0, nb - 1), 0)

    kv_spec = lambda delta: pl.BlockSpec((ATTN_BLOCK, A_KV), blk(delta))
    cache_spec = pl.BlockSpec((1, PAST_LEN, A_KV), lambda b, n: (b, 0, 0))
    return pl.pallas_call(
        _attn_lat_kernel,
        grid=(DEC_BATCH, nb),
        in_specs=[
            pl.BlockSpec(memory_space=pltpu.SMEM),
            pl.BlockSpec((ATTN_BLOCK, A_Q), blk(0)),
            kv_spec(-1), kv_spec(0), kv_spec(1),
            kv_spec(-1), kv_spec(0), kv_spec(1),
            cache_spec, cache_spec,
        ],
        out_specs=pl.BlockSpec((ATTN_BLOCK, A_Q), lambda b, n: (b * nb + n, 0)),
        out_shape=jax.ShapeDtypeStruct((N_SAMPLE, A_Q), F32),
        compiler_params=_cparams(("parallel", "parallel")),
        name="attn_latent",
    )(sink, q, k, k, k, v, v, v, cache_k, cache_v)


def _hgrn_gate(f_pre, lb):
    log_f = jnp.logaddexp(jnp.log(jnp.maximum(lb, LB_FLOOR)), jnp.log1p(-lb) + jax.nn.log_sigmoid(f_pre))
    return 1.0 - jnp.exp(log_f), log_f


def _scan_rows(x, reverse):
    n = x.shape[0]
    row = lax.broadcasted_iota(jnp.int32, x.shape, 0)
    sh = 1
    while sh < n:
        if reverse:
            x = x + jnp.where(row < n - sh, pltpu.roll(x, n - sh, 0), 0.0)
        else:
            x = x + jnp.where(row >= sh, pltpu.roll(x, sh, 0), 0.0)
        sh *= 2
    return x


def _chunk_bcast(x, c, pick, shift):
    nc = HG_BLOCK // c
    rows = x.reshape(nc, c, x.shape[-1])[:, pick:pick + 1, :]
    zero = jnp.zeros((1, 1, x.shape[-1]), x.dtype)
    if shift == -1:
        rows = jnp.concatenate([zero, rows[:-1]], axis=0)
    elif shift == 1:
        rows = jnp.concatenate([rows[1:], zero], axis=0)
    return jnp.broadcast_to(rows, (nc, c, x.shape[-1])).reshape(HG_BLOCK, x.shape[-1])


def _hgrn_block(q, kk, v, log_f, st, reverse):
    cum = _scan_rows(log_f, reverse)
    tot = cum[0:1, :] if reverse else cum[HG_BLOCK - 1:HG_BLOCK, :]
    o = _dot_nt((q * jnp.exp(cum)).astype(BF16), st.astype(BF16))
    kd = (kk * jnp.exp(tot - cum)).astype(BF16)
    v_bf = v.astype(BF16)
    u_t = lax.dot_general(v_bf, kd, (((0,), (0,)), ((), ())), preferred_element_type=F32)
    st_new = st * jnp.exp(tot) + u_t
    ti = lax.broadcasted_iota(jnp.int32, (HG_BLOCK, HG_BLOCK), 0)
    si = lax.broadcasted_iota(jnp.int32, (HG_BLOCK, HG_BLOCK), 1)
    a = jnp.zeros((HG_BLOCK, HG_BLOCK), F32)
    c = HG_SUB
    while c < HG_BLOCK:
        if reverse:
            bound_t = _chunk_bcast(cum, c, 0, 1)
            bound_s = _chunk_bcast(cum, c, 0, 0)
            mask = jnp.logical_and((ti // c) % 2 == 0, si // c == ti // c + 1)
        else:
            bound_t = _chunk_bcast(cum, c, c - 1, -1)
            bound_s = _chunk_bcast(cum, c, c - 1, 0)
            mask = jnp.logical_and((ti // c) % 2 == 1, si // c == ti // c - 1)
        qc = (q * jnp.exp(cum - bound_t)).astype(BF16)
        kc = (kk * jnp.exp(bound_s - cum)).astype(BF16)
        a = a + jnp.where(mask, _dot_nt(qc, kc), 0.0)
        c *= 2
    o = o + _dot(a.astype(BF16), v_bf)
    nsub = HG_BLOCK // HG_SUB
    q3 = q.reshape(nsub, HG_SUB, HGRN_DK)
    k3 = kk.reshape(nsub, HG_SUB, HGRN_DK)
    v3 = v.reshape(nsub, HG_SUB, HGRN_DV)
    cum3 = cum.reshape(nsub, HG_SUB, HGRN_DK)
    t_off = lax.broadcasted_iota(jnp.int32, (1, HG_SUB, 1), 1)
    od = jnp.zeros((nsub, HG_SUB, HGRN_DV), F32)
    for s in range(HG_SUB):
        causal = (t_off <= s) if reverse else (t_off >= s)
        decay = jnp.exp(jnp.where(causal, cum3 - cum3[:, s:s + 1, :], MASK_VALUE))
        score = jnp.sum(q3 * decay * k3[:, s:s + 1, :], axis=-1, keepdims=True)
        od = od + score * v3[:, s:s + 1, :]
    return o + od.reshape(HG_BLOCK, HGRN_DV), st_new


def _hgrn_kernel(*refs, n_blocks, has_state_in, has_state_out):
    refs = list(refs)
    qb_ref, ib_ref, ff_ref, fb_ref, go_ref, lbf_ref, lbb_ref, nw_ref = refs[:8]
    refs = refs[8:]
    s0_ref = refs.pop(0) if has_state_in else None
    o_ref = refs.pop(0)
    so_ref = refs.pop(0) if has_state_out else None
    acc_ref = refs.pop(0)

    def run(reverse):
        f_ref, lb_ref = (fb_ref, lbb_ref) if reverse else (ff_ref, lbf_ref)
        lb = lb_ref[...]
        if has_state_in:
            st0 = s0_ref[0, 0, 1 if reverse else 0, 0].T
        else:
            st0 = jnp.zeros((HGRN_DV, HGRN_DK), F32)

        def body(it, st):
            blk = (n_blocks - 1 - it) if reverse else it
            rows = pl.ds(pl.multiple_of(blk * HG_BLOCK, HG_BLOCK), HG_BLOCK)
            qpre = qb_ref[rows, :]
            q = qpre * jax.nn.sigmoid(qpre)
            kk, log_f = _hgrn_gate(f_ref[rows, :], lb)
            o, st = _hgrn_block(q, kk, ib_ref[rows, :], log_f, st, reverse)
            if reverse:
                acc_ref[rows, :] = acc_ref[rows, :] + o
            else:
                acc_ref[rows, :] = o
            return st

        return lax.fori_loop(0, n_blocks, body, st0)

    st_f = run(False)
    st_b = run(True)
    if has_state_out:
        so_ref[0, 0, 0] = st_f.T
        so_ref[0, 1, 0] = st_b.T
    o = acc_ref[...]
    o = o * lax.rsqrt(jnp.mean(o * o, axis=-1, keepdims=True) + RMS_EPS) * nw_ref[...]
    g = go_ref[...]
    o_ref[...] = o * (g * jax.nn.sigmoid(g))


def hgrn_mixer(qb, ib, ff, fb, go, lb_f, lb_b, norm_w, *, latent, state_in=None, layer_j=0):
    t_len = DEC_SEQ if latent else SEQ
    n_seq = DEC_BATCH if latent else BATCH
    row0 = N_PROMPT // t_len if latent else 0
    tok = pl.BlockSpec((t_len, HGRN_DK), lambda b, h: (row0 + b, h))
    vec = pl.BlockSpec((1, HGRN_DK), lambda b, h: (0, h))
    in_specs = [tok] * 5 + [vec] * 3
    args = [qb, ib, ff, fb, go, lb_f.reshape(1, B_QK), lb_b.reshape(1, B_QK), norm_w.reshape(1, B_V)]
    if latent:
        in_specs.append(pl.BlockSpec((1, 1, 2, 1, HGRN_DK, HGRN_DV), lambda b, h: (b, layer_j, 0, h, 0, 0)))
        args.append(state_in)
    out_specs = [pl.BlockSpec((t_len, HGRN_DV), lambda b, h: (b, h))]
    out_shape = [jax.ShapeDtypeStruct((n_seq * t_len, B_V), F32)]
    if not latent:
        out_specs.append(pl.BlockSpec((1, 2, 1, HGRN_DK, HGRN_DV), lambda b, h: (b, 0, h, 0, 0)))
        out_shape.append(jax.ShapeDtypeStruct((BATCH, 2, N_HEADS_B, HGRN_DK, HGRN_DV), F32))
    kern = functools.partial(_hgrn_kernel, n_blocks=t_len // HG_BLOCK, has_state_in=latent, has_state_out=not latent)
    return pl.pallas_call(
        kern,
        grid=(n_seq, N_HEADS_B),
        in_specs=in_specs,
        out_specs=out_specs,
        out_shape=out_shape,
        scratch_shapes=[pltpu.VMEM((t_len, HGRN_DV), F32)],
        compiler_params=_cparams(("parallel", "parallel")),
        name="hgrn_latent" if latent else "hgrn_context",
    )(*args)


def _outproj_kernel(x_ref, mod_ref, ap_ref, as_ref, rp_ref, rs_ref, w_ref, g_ref, b_ref, o_ref):
    is_latent = pl.program_id(0) * ROW_BLOCK >= N_PROMPT
    attn = jnp.where(is_latent, as_ref[...], ap_ref[...]).astype(BF16)
    rec = jnp.where(is_latent, rs_ref[...], rp_ref[...]).astype(BF16)
    y = _dot(attn, w_ref[0:A_Q, :]) + _dot(rec, w_ref[A_Q:A_Q + B_V, :])
    z = DN_ALPHA * x_ref[...] + mod_ref[0, 2:3, :] * y
    o_ref[...] = _layer_norm(z, g_ref[...], b_ref[...])


def outproj_even(x, mods, attn_p, attn_s, rec_p, rec_s, w_bf, ln_g, ln_b):
    nblk = N_TOK // ROW_BLOCK
    npb = N_PROMPT // ROW_BLOCK
    row = lambda i: (i, 0)
    prow = lambda i: (jnp.minimum(i, npb - 1), 0)
    srow = lambda i: (jnp.maximum(i - npb, 0), 0)
    vec = pl.BlockSpec((1, D_MODEL), lambda i: (0, 0))
    return pl.pallas_call(
        _outproj_kernel,
        grid=(nblk,),
        in_specs=[
            pl.BlockSpec((ROW_BLOCK, D_MODEL), row),
            pl.BlockSpec((1, 6, D_MODEL), lambda i: (_cond_of_block(i, ROW_BLOCK), 0, 0)),
            pl.BlockSpec((ROW_BLOCK, A_Q), prow),
            pl.BlockSpec((ROW_BLOCK, A_Q), srow),
            pl.BlockSpec((ROW_BLOCK, B_V), prow),
            pl.BlockSpec((ROW_BLOCK, B_V), srow),
            pl.BlockSpec((A_Q + B_V, D_MODEL), lambda i: (0, 0)),
            vec, vec,
        ],
        out_specs=pl.BlockSpec((ROW_BLOCK, D_MODEL), row),
        out_shape=jax.ShapeDtypeStruct((N_TOK, D_MODEL), F32),
        compiler_params=_cparams(("parallel",)),
        name="outproj_even",
    )(x, mods, attn_p, attn_s, rec_p, rec_s, w_bf, ln_g.reshape(1, D_MODEL), ln_b.reshape(1, D_MODEL))


def _conv_in_kernel(x_ref, mod_ref, w_ref, b_ref, u_ref):
    h = (x_ref[...] * (1.0 + mod_ref[0, 1:2, :]) + mod_ref[0, 0:1, :]).astype(BF16)
    a = _dot(h, w_ref[:, 0:D_MODEL]) + b_ref[:, 0:D_MODEL]
    gt = _dot(h, w_ref[:, D_MODEL:2 * D_MODEL]) + b_ref[:, D_MODEL:2 * D_MODEL]
    u_ref[...] = a * jax.nn.sigmoid(gt)


def conv_in(x, mods, w_bf, b):
    row = lambda i: (i, 0)
    return pl.pallas_call(
        _conv_in_kernel,
        grid=(N_TOK // ROW_BLOCK,),
        in_specs=[
            pl.BlockSpec((ROW_BLOCK, D_MODEL), row),
            pl.BlockSpec((1, 6, D_MODEL), lambda i: (_cond_of_block(i, ROW_BLOCK), 0, 0)),
            pl.BlockSpec((D_MODEL, 2 * D_MODEL), lambda i: (0, 0)),
            pl.BlockSpec((1, 2 * D_MODEL), lambda i: (0, 0)),
        ],
        out_specs=pl.BlockSpec((ROW_BLOCK, D_MODEL), row),
        out_shape=jax.ShapeDtypeStruct((N_TOK, D_MODEL), F32),
        compiler_params=_cparams(("parallel",)),
        name="conv_in",
    )(x, mods, w_bf, b.reshape(1, 2 * D_MODEL))


CONV_HALO = 16
CONV_LANES = 256


def _conv_out_kernel(x_ref, mod_ref, up_ref, uc_ref, un_ref, dw_ref, dwb_ref, cg_ref, cb_ref,
                     w_ref, b_ref, g_ref, bb_ref, o_ref, pad_ref, acc_ref):
    i = pl.program_id(0)
    blocks_per_seq = DEC_SEQ // ROW_BLOCK
    j = i - N_PROMPT // ROW_BLOCK
    is_latent = j >= 0
    has_prev = jnp.logical_and(is_latent, j % blocks_per_seq != 0)
    has_next = jnp.logical_and(is_latent, j % blocks_per_seq != blocks_per_seq - 1)
    pad_ref[0:CONV_HALO, :] = jnp.where(has_prev, up_ref[...], 0.0)
    pad_ref[CONV_HALO:CONV_HALO + ROW_BLOCK, :] = uc_ref[...]
    pad_ref[CONV_HALO + ROW_BLOCK:, :] = jnp.where(has_next, un_ref[...], 0.0)
    first = CONV_HALO - CONV_WIDTH // 2
    for c in range(D_MODEL // CONV_LANES):
        lanes = slice(c * CONV_LANES, (c + 1) * CONV_LANES)
        acc = jnp.zeros((ROW_BLOCK, CONV_LANES), F32)
        for tap in range(CONV_WIDTH):
            acc = acc + pad_ref[first + tap:first + tap + ROW_BLOCK, lanes] * dw_ref[tap:tap + 1, lanes]
        acc_ref[:, lanes] = acc + dwb_ref[:, lanes]
    u = _layer_norm(acc_ref[...], cg_ref[...], cb_ref[...])
    u = (u * jax.nn.sigmoid(u)).astype(BF16)
    y = _dot(u, w_ref[...]) + b_ref[...]
    z = DN_ALPHA * x_ref[...] + mod_ref[0, 2:3, :] * y
    o_ref[...] = _layer_norm(z, g_ref[...], bb_ref[...])


def conv_out(x, mods, u, dw, dw_b, cln_g, cln_b, w_bf, b_out, ln_g, ln_b):
    nblk = N_TOK // ROW_BLOCK
    ratio = ROW_BLOCK // CONV_HALO
    nhalo = N_TOK // CONV_HALO
    row = lambda i: (i, 0)
    vec = pl.BlockSpec((1, D_MODEL), lambda i: (0, 0))
    r1 = lambda a: a.reshape(1, D_MODEL)
    return pl.pallas_call(
        _conv_out_kernel,
        grid=(nblk,),
        in_specs=[
            pl.BlockSpec((ROW_BLOCK, D_MODEL), row),
            pl.BlockSpec((1, 6, D_MODEL), lambda i: (_cond_of_block(i, ROW_BLOCK), 0, 0)),
            pl.BlockSpec((CONV_HALO, D_MODEL), lambda i: (jnp.maximum(i * ratio - 1, 0), 0)),
            pl.BlockSpec((ROW_BLOCK, D_MODEL), row),
            pl.BlockSpec((CONV_HALO, D_MODEL), lambda i: (jnp.minimum((i + 1) * ratio, nhalo - 1), 0)),
            pl.BlockSpec((CONV_WIDTH, D_MODEL), lambda i: (0, 0)),
            vec, vec, vec,
            pl.BlockSpec((D_MODEL, D_MODEL), lambda i: (0, 0)),
            vec, vec, vec,
        ],
        out_specs=pl.BlockSpec((ROW_BLOCK, D_MODEL), row),
        out_shape=jax.ShapeDtypeStruct((N_TOK, D_MODEL), F32),
        scratch_shapes=[pltpu.VMEM((ROW_BLOCK + 2 * CONV_HALO, D_MODEL), F32),
                        pltpu.VMEM((ROW_BLOCK, D_MODEL), F32)],
        compiler_params=_cparams(("parallel",)),
        name="conv_out",
    )(x, mods, u, u, u, dw, r1(dw_b), r1(cln_g), r1(cln_b), w_bf, r1(b_out), r1(ln_g), r1(ln_b))


def _modulate2(x_ref, mod_ref):
    return x_ref[...] * (1.0 + mod_ref[0, 4:5, :]) + mod_ref[0, 3:4, :]


def _route_kernel(x_ref, mod_ref, wr_ref, br_ref, idx_ref, gate_ref, rank_ref, cnt_ref, carry_ref):
    i = pl.program_id(0)

    @pl.when(i == 0)
    def _():
        carry_ref[...] = jnp.zeros_like(carry_ref)

    h = _modulate2(x_ref, mod_ref)
    logits = lax.dot_general(wr_ref[...], h, (((1,), (1,)), ((), ())), precision=HIGHEST,
                             preferred_element_type=F32) + br_ref[...]
    eidx = lax.broadcasted_iota(jnp.int32, logits.shape, 0)
    vals = logits
    sels, tops = [], []
    for k in range(TOP_K):
        m = jnp.max(vals, axis=0, keepdims=True)
        idx = jnp.min(jnp.where(vals == m, eidx, N_EXPERTS), axis=0, keepdims=True)
        sel = eidx == idx
        idx_ref[k:k + 1, :] = idx
        sels.append(sel)
        tops.append(m)
        vals = jnp.where(sel, -jnp.inf, vals)
    exps = [jnp.exp(t - tops[0]) for t in tops]
    total = exps[0] + exps[1] + exps[2] + exps[3]
    for k in range(TOP_K):
        gate_ref[k:k + 1, :] = exps[k] / total
    onehot = jnp.zeros(logits.shape, F32)
    for sel in sels:
        onehot = onehot + sel.astype(F32)
    ta = lax.broadcasted_iota(jnp.int32, (ROW_BLOCK, ROW_BLOCK), 0)
    tb = lax.broadcasted_iota(jnp.int32, (ROW_BLOCK, ROW_BLOCK), 1)
    before = _dot(onehot.astype(BF16), (ta < tb).astype(BF16)) + carry_ref[:, 0:1]
    for k in range(TOP_K):
        rank = jnp.sum(jnp.where(sels[k], before, 0.0), axis=0, keepdims=True)
        rank_ref[k:k + 1, :] = rank.astype(jnp.int32)
    carry = carry_ref[...] + jnp.sum(onehot, axis=1, keepdims=True)
    carry_ref[...] = carry
    cnt_ref[...] = carry.astype(jnp.int32)


def moe_route(x, mods, wr_t, b_r):
    tok = pl.BlockSpec((TOP_K, ROW_BLOCK), lambda i: (0, i))
    return pl.pallas_call(
        _route_kernel,
        grid=(N_TOK // ROW_BLOCK,),
        in_specs=[
            pl.BlockSpec((ROW_BLOCK, D_MODEL), lambda i: (i, 0)),
            pl.BlockSpec((1, 6, D_MODEL), lambda i: (_cond_of_block(i, ROW_BLOCK), 0, 0)),
            pl.BlockSpec((N_EXPERTS, D_MODEL), lambda i: (0, 0)),
            pl.BlockSpec((N_EXPERTS, 1), lambda i: (0, 0)),
        ],
        out_specs=[tok, tok, tok, pl.BlockSpec((N_EXPERTS, LANES), lambda i: (0, 0))],
        out_shape=[jax.ShapeDtypeStruct((TOP_K, N_TOK), jnp.int32),
                   jax.ShapeDtypeStruct((TOP_K, N_TOK), F32),
                   jax.ShapeDtypeStruct((TOP_K, N_TOK), jnp.int32),
                   jax.ShapeDtypeStruct((N_EXPERTS, LANES), jnp.int32)],
        scratch_shapes=[pltpu.VMEM((N_EXPERTS, LANES), F32)],
        compiler_params=_cparams(("arbitrary",)),
        name="moe_route",
    )(x, mods, wr_t, b_r.reshape(N_EXPERTS, 1))


HALF_D = D_MODEL // 2
HI_MASK = 0xFFFF0000


def _pack_bf16_pairs(h):
    lo = lax.bitcast_convert_type(h[:, :HALF_D].astype(BF16).astype(F32), jnp.uint32)
    hi = lax.bitcast_convert_type(h[:, HALF_D:].astype(BF16).astype(F32), jnp.uint32)
    return (lo >> 16) | (hi & jnp.uint32(HI_MASK))


def _unpack_bf16_pairs(w):
    lo = lax.bitcast_convert_type(w << 16, F32).astype(BF16)
    hi = lax.bitcast_convert_type(w & jnp.uint32(HI_MASK), F32).astype(BF16)
    return lo, hi


def _scatter_kernel(dest_ref, zstart_ref, x_ref, mod_ref, xs_ref, pk_ref, zero_ref, sem, zsem):
    i = pl.program_id(0)

    def zero_copy(start):
        start = pl.multiple_of(start, MOE_ROWS)
        return pltpu.make_async_copy(zero_ref, xs_ref.at[pl.ds(start, MOE_ROWS), :], zsem)

    def zero_blocks(fn):
        for e in range(N_EXPERTS):
            @pl.when(zstart_ref[e] >= 0)
            def _():
                fn(zero_copy(jnp.maximum(zstart_ref[e], 0)))

        def unused(b, carry):
            fn(zero_copy(b * MOE_ROWS))
            return carry

        lax.fori_loop(zstart_ref[N_EXPERTS], MOE_BLOCKS, unused, 0)

    @pl.when(i == 0)
    def _():
        zero_ref[...] = jnp.zeros_like(zero_ref)
        zero_blocks(lambda cp: cp.start())
        zero_blocks(lambda cp: cp.wait())

    pk_ref[...] = _pack_bf16_pairs(_modulate2(x_ref, mod_ref))
    base = i * ROW_BLOCK

    def issue(t, carry):
        for k in range(TOP_K):
            row = dest_ref[k * N_TOK + base + t]
            pltpu.make_async_copy(pk_ref.at[pl.ds(t, 1), :], xs_ref.at[pl.ds(row, 1), :], sem).start()
        return carry

    lax.fori_loop(0, ROW_BLOCK, issue, 0, unroll=8)
    for k in range(TOP_K):
        pltpu.make_async_copy(pk_ref, xs_ref.at[pl.ds(0, ROW_BLOCK), :], sem).wait()


def moe_scatter(dest_flat, zstart, x, mods):
    return pl.pallas_call(
        _scatter_kernel,
        grid_spec=pltpu.PrefetchScalarGridSpec(
            num_scalar_prefetch=2,
            grid=(N_TOK // ROW_BLOCK,),
            in_specs=[
                pl.BlockSpec((ROW_BLOCK, D_MODEL), lambda i, d, z: (i, 0)),
                pl.BlockSpec((1, 6, D_MODEL), lambda i, d, z: (_cond_of_block(i, ROW_BLOCK), 0, 0)),
            ],
            out_specs=pl.BlockSpec(memory_space=pl.ANY),
            scratch_shapes=[pltpu.VMEM((ROW_BLOCK, HALF_D), jnp.uint32),
                            pltpu.VMEM((MOE_ROWS, HALF_D), jnp.uint32),
                            pltpu.SemaphoreType.DMA, pltpu.SemaphoreType.DMA],
        ),
        out_shape=jax.ShapeDtypeStruct((MOE_R, HALF_D), jnp.uint32),
        compiler_params=_cparams(("arbitrary",)),
        name="moe_scatter",
    )(dest_flat, zstart, x, mods)


def _expert_kernel(be_ref, nused_ref, xs_ref, wg_ref, wu_ref, wd_ref, bg_ref, bu_ref, bd_ref,
                   ys_ref, wg_bf, wu_bf, wd_bf):
    i = pl.program_id(0)
    new_expert = jnp.logical_or(i == 0, be_ref[i] != be_ref[jnp.maximum(i - 1, 0)])

    @pl.when(new_expert)
    def _():
        wg_bf[...] = wg_ref[0, 0].astype(BF16)
        wu_bf[...] = wu_ref[0, 0].astype(BF16)
        wd_bf[...] = wd_ref[0, 0].astype(BF16)

    @pl.when(i < nused_ref[0])
    def _():
        lo, hi = _unpack_bf16_pairs(xs_ref[...])
        gt = _dot(lo, wg_bf[0:HALF_D, :]) + _dot(hi, wg_bf[HALF_D:, :]) + bg_ref[0, 0]
        up = _dot(lo, wu_bf[0:HALF_D, :]) + _dot(hi, wu_bf[HALF_D:, :]) + bu_ref[0, 0]
        gt = jnp.minimum(gt, SWIGLU_LIMIT)
        up = jnp.clip(up, -SWIGLU_LIMIT, SWIGLU_LIMIT)
        act = ((up + 1.0) * gt * jax.nn.sigmoid(SWIGLU_ALPHA * gt)).astype(BF16)
        ys_ref[...] = _dot(act, wd_bf[...]) + bd_ref[0, 0]

    @pl.when(i >= nused_ref[0])
    def _():
        ys_ref[...] = jnp.zeros_like(ys_ref)


def moe_experts(block_e, n_used, xs, w_g, w_u, w_d, b_g, b_u, b_d, layer):
    rows = lambda i, be, nu: (i, 0)
    wspec = pl.BlockSpec((1, 1, D_MODEL, D_EXPERT), lambda i, be, nu: (layer, be[i], 0, 0))
    bspec = pl.BlockSpec((1, 1, 1, D_EXPERT), lambda i, be, nu: (layer, be[i], 0, 0))
    r4 = lambda b: b.reshape(DEPTH, N_EXPERTS, 1, D_EXPERT)
    return pl.pallas_call(
        _expert_kernel,
        grid_spec=pltpu.PrefetchScalarGridSpec(
            num_scalar_prefetch=2,
            grid=(MOE_BLOCKS,),
            in_specs=[pl.BlockSpec((MOE_ROWS, HALF_D), rows), wspec, wspec, wspec, bspec, bspec, bspec],
            out_specs=pl.BlockSpec((MOE_ROWS, D_MODEL), rows),
            scratch_shapes=[pltpu.VMEM((D_MODEL, D_EXPERT), BF16)] * 3,
        ),
        out_shape=jax.ShapeDtypeStruct((MOE_R, D_MODEL), F32),
        compiler_params=_cparams(("arbitrary",)),
        name="moe_experts",
    )(block_e, n_used, xs, w_g, w_u, w_d, r4(b_g), r4(b_u), r4(b_d))


def _combine_kernel(dest_ref, x_ref, mod_ref, gate_ref, ys_ref, g_ref, b_ref, o_ref, buf_ref, sem):
    i = pl.program_id(0)
    base = i * COMB_TOK

    def issue(t, carry):
        for k in range(TOP_K):
            row = dest_ref[k * N_TOK + base + t]
            pltpu.make_async_copy(ys_ref.at[pl.ds(row, 1), :], buf_ref.at[k, pl.ds(t, 1), :], sem).start()
        return carry

    lax.fori_loop(0, COMB_TOK, issue, 0, unroll=8)
    eye = (lax.broadcasted_iota(jnp.int32, (COMB_TOK, COMB_TOK), 0)
           == lax.broadcasted_iota(jnp.int32, (COMB_TOK, COMB_TOK), 1))
    gates = gate_ref[...]
    cols = [jnp.sum(jnp.where(eye, gates[k:k + 1, :], 0.0), axis=1, keepdims=True) for k in range(TOP_K)]
    for k in range(TOP_K):
        pltpu.make_async_copy(ys_ref.at[pl.ds(0, COMB_TOK), :], buf_ref.at[k], sem).wait()
    y = cols[0] * buf_ref[0]
    for k in range(1, TOP_K):
        y = y + cols[k] * buf_ref[k]
    z = DN_ALPHA * x_ref[...] + mod_ref[0, 5:6, :] * y
    o_ref[...] = _layer_norm(z, g_ref[...], b_ref[...])


def moe_combine(dest_flat, x, mods, gates_t, ys, ln_g, ln_b):
    vec = pl.BlockSpec((1, D_MODEL), lambda i, d: (0, 0))
    return pl.pallas_call(
        _combine_kernel,
        grid_spec=pltpu.PrefetchScalarGridSpec(
            num_scalar_prefetch=1,
            grid=(N_TOK // COMB_TOK,),
            in_specs=[
                pl.BlockSpec((COMB_TOK, D_MODEL), lambda i, d: (i, 0)),
                pl.BlockSpec((1, 6, D_MODEL), lambda i, d: (_cond_of_block(i, COMB_TOK), 0, 0)),
                pl.BlockSpec((TOP_K, COMB_TOK), lambda i, d: (0, i)),
                pl.BlockSpec(memory_space=pl.ANY),
                vec, vec,
            ],
            out_specs=pl.BlockSpec((COMB_TOK, D_MODEL), lambda i, d: (i, 0)),
            scratch_shapes=[pltpu.VMEM((TOP_K, COMB_TOK, D_MODEL), F32), pltpu.SemaphoreType.DMA],
        ),
        out_shape=jax.ShapeDtypeStruct((N_TOK, D_MODEL), F32),
        compiler_params=_cparams(("arbitrary",)),
        name="moe_combine",
    )(dest_flat, x, mods, gates_t, ys, ln_g.reshape(1, D_MODEL), ln_b.reshape(1, D_MODEL))


def moe_layer(x, mods, layer, router_w, router_b, w_g, b_g, w_u, b_u, w_d, b_d, ln_g, ln_b):
    idx_t, gates_t, rank_t, counts = moe_route(x, mods, router_w[layer].T, router_b[layer])
    counts = counts[:, 0]
    padded = (counts + MOE_ROWS - 1) // MOE_ROWS * MOE_ROWS
    ends = jnp.cumsum(padded)
    base = ends - padded
    n_used = (ends[-1] // MOE_ROWS).astype(jnp.int32)
    block_start = jnp.arange(MOE_BLOCKS, dtype=jnp.int32) * MOE_ROWS
    block_e = jnp.sum(block_start[:, None] >= ends[None, :], axis=1).astype(jnp.int32)
    block_e = jnp.minimum(block_e, block_e[jnp.maximum(n_used - 1, 0)])
    zstart = jnp.where(padded > 0, ends - MOE_ROWS, -1).astype(jnp.int32)
    zstart = jnp.concatenate([zstart, n_used.reshape(1)])
    onehot = idx_t[:, :, None] == jnp.arange(N_EXPERTS, dtype=jnp.int32)[None, None, :]
    dest = rank_t + jnp.sum(jnp.where(onehot, base[None, None, :], 0), axis=-1)
    dest_flat = dest.reshape(-1).astype(jnp.int32)
    xs = moe_scatter(dest_flat, zstart, x, mods)
    ys = moe_experts(block_e, n_used.reshape(1), xs, w_g, w_u, w_d, b_g, b_u, b_d, layer)
    return moe_combine(dest_flat, x, mods, gates_t, ys, ln_g, ln_b)


def kernel(x_prompt, x_sample, c, cache_k, cache_v, state_hgrn, c_ctx, w_ada, b_ada, ln_g, ln_b,
           w_in_even, b_in_even, attn_sink, hgrn_lb, hgrn_norm, w_out_even,
           conv_w_in, conv_b_in, conv_dw, conv_dw_b, conv_ln_g, conv_ln_b, conv_w_out, conv_b_out,
           router_w, router_b, moe_w_gate, moe_b_gate, moe_w_up, moe_b_up, moe_w_down, moe_b_down):
    x = jnp.concatenate([x_prompt.reshape(N_PROMPT, D_MODEL), x_sample.reshape(N_SAMPLE, D_MODEL)], axis=0)
    cond = jnp.concatenate([c_ctx[None, :], c, jnp.zeros((COND_ROWS - N_COND, D_MODEL), F32)], axis=0)
    mods_all = adaln_all(cond, w_ada, b_ada).reshape(DEPTH, COND_ROWS, 6, D_MODEL)
    lb = jax.nn.softmax(hgrn_lb.astype(F32), axis=1)
    lb = jnp.cumsum(lb, axis=1) - lb[:, :1]
    cos, sin = _rope_tables()
    new_k, new_v, new_s = [], [], []
    for layer in range(DEPTH):
        j = layer // 2
        mods = mods_all[layer]
        if layer % 2 == 0:
            q, k, v, qb, ib, ff, fb, go = inproj_even(x, mods, w_in_even[j].astype(BF16), b_in_even[j], cos, sin)
            new_k.append(k[:N_PROMPT].reshape(BATCH, SEQ, N_KV_A, HEAD_DIM))
            new_v.append(v[:N_PROMPT].reshape(BATCH, SEQ, N_KV_A, HEAD_DIM))
            attn_p = attn_context(attn_sink[j], q, k, v)
            attn_s = attn_latent(attn_sink[j], q, k, v,
                                 cache_k[:, j].reshape(DEC_BATCH, PAST_LEN, A_KV),
                                 cache_v[:, j].reshape(DEC_BATCH, PAST_LEN, A_KV))
            hg = (qb, ib, ff, fb, go, lb[0, j], lb[1, j], hgrn_norm[j])
            rec_p, states = hgrn_mixer(*hg, latent=False)
            rec_s, = hgrn_mixer(*hg, latent=True, state_in=state_hgrn, layer_j=j)
            new_s.append(states)
            x = outproj_even(x, mods, attn_p, attn_s, rec_p, rec_s, w_out_even[j].astype(BF16),
                             ln_g[layer, 0], ln_b[layer, 0])
        else:
            u = conv_in(x, mods, conv_w_in[j].astype(BF16), conv_b_in[j])
            x = conv_out(x, mods, u, conv_dw[j], conv_dw_b[j], conv_ln_g[j], conv_ln_b[j],
                         conv_w_out[j].astype(BF16), conv_b_out[j], ln_g[layer, 0], ln_b[layer, 0])
        x = moe_layer(x, mods, layer, router_w, router_b, moe_w_gate, moe_b_gate, moe_w_up, moe_b_up,
                      moe_w_down, moe_b_down, ln_g[layer, 1], ln_b[layer, 1])
    return (x[:N_PROMPT].reshape(BATCH, SEQ, D_MODEL),
            x[N_PROMPT:].reshape(DEC_BATCH, DEC_SEQ, D_MODEL),
            jnp.stack(new_k, axis=1), jnp.stack(new_v, axis=1), jnp.stack(new_s, axis=1))
```

```python
import functools

import jax
import jax.numpy as jnp
import numpy as np
from jax import lax
from jax.experimental import pallas as pl
from jax.experimental.pallas import tpu as pltpu

D_MODEL = 1024
BATCH = 16
SEQ = 256
DEPTH = 4
DEC_BATCH = 4
DEC_SEQ = 2048
PAST_LEN = 512
GRID_W = 64
N_EVEN = (DEPTH + 1) // 2
N_ODD = DEPTH // 2
HEAD_DIM = 64
N_HEADS_A = 8
N_KV_A = 2
GROUP_A = N_HEADS_A // N_KV_A
WINDOW = 128
ATTN_BLOCK = 128
SCALE_A = HEAD_DIM ** -0.5
ROPE_BASE = 10000.0
ROPE_PAIRS = HEAD_DIM // 4
N_HEADS_B = 4
HGRN_DK = 128
HGRN_DV = 128
CONV_WIDTH = 31
N_EXPERTS = 32
TOP_K = 4
D_EXPERT = D_MODEL
SWIGLU_LIMIT = 7.0
SWIGLU_ALPHA = 1.702
LN_EPS = 1e-5
RMS_EPS = 1e-6
MASK_VALUE = -1e9
LB_FLOOR = 1e-30
DN_ALPHA = (2 * DEPTH) ** 0.25
A_Q = N_HEADS_A * HEAD_DIM
A_KV = N_KV_A * HEAD_DIM
B_QK = N_HEADS_B * HGRN_DK
B_V = N_HEADS_B * HGRN_DV
IN_SIZES = (A_Q, A_KV, A_KV, B_QK, B_V, B_QK, B_QK, B_V)
D_IN_EVEN = sum(IN_SIZES)

N_PROMPT = BATCH * SEQ
N_SAMPLE = DEC_BATCH * DEC_SEQ
N_TOK = N_PROMPT + N_SAMPLE
N_COND = 1 + DEC_BATCH
COND_ROWS = 8

LANES = 128
SUBLANES = 8
VMEM_LIMIT = 56 * 1024 * 1024

ROW_BLOCK = 256
HG_BLOCK = 128
HG_SUB = 8
MOE_ROWS = 256
MOE_BLOCKS = (N_TOK * TOP_K + N_EXPERTS * (MOE_ROWS - 1)) // MOE_ROWS + 1
MOE_R = MOE_BLOCKS * MOE_ROWS
COMB_TOK = 128

F32 = jnp.float32
BF16 = jnp.bfloat16
HIGHEST = lax.Precision.HIGHEST


def _cond_of_block(i, rows):
    start = i * rows
    return jnp.where(start < N_PROMPT, 0, 1 + (start - N_PROMPT) // DEC_SEQ)


def _cparams(sem):
    return pltpu.CompilerParams(dimension_semantics=sem, vmem_limit_bytes=VMEM_LIMIT)


def _layer_norm(z, g, b):
    mu = jnp.mean(z, axis=-1, keepdims=True)
    zc = z - mu
    var = jnp.mean(zc * zc, axis=-1, keepdims=True)
    return zc * lax.rsqrt(var + LN_EPS) * g + b


def _dot(a, b):
    return jnp.dot(a, b, preferred_element_type=F32)


def _dot_nt(a, b):
    return lax.dot_general(a, b, (((1,), (1,)), ((), ())), preferred_element_type=F32)


ADA_TN = 1536


def _adaln_kernel(cond_ref, w_ref, b_ref, o_ref):
    c = cond_ref[...]
    s = c * jax.nn.sigmoid(c)
    o_ref[0] = jnp.dot(s, w_ref[0], precision=HIGHEST, preferred_element_type=F32) + b_ref[0]


def adaln_all(cond, w_ada, b_ada):
    n_out = 6 * D_MODEL
    return pl.pallas_call(
        _adaln_kernel,
        grid=(DEPTH, n_out // ADA_TN),
        in_specs=[
            pl.BlockSpec((COND_ROWS, D_MODEL), lambda l, n: (0, 0)),
            pl.BlockSpec((1, D_MODEL, ADA_TN), lambda l, n: (l, 0, n)),
            pl.BlockSpec((1, 1, ADA_TN), lambda l, n: (l, 0, n)),
        ],
        out_specs=pl.BlockSpec((1, COND_ROWS, ADA_TN), lambda l, n: (l, 0, n)),
        out_shape=jax.ShapeDtypeStruct((DEPTH, COND_ROWS, n_out), F32),
        compiler_params=_cparams(("parallel", "parallel")),
        name="adaln",
    )(cond, w_ada, b_ada.reshape(DEPTH, 1, n_out))


def _rope_tables():
    t = np.arange(DEC_SEQ)
    d = np.arange(LANES) % HEAD_DIM
    axis = d // (2 * ROPE_PAIRS)
    half = (d // ROPE_PAIRS) % 2
    pair = d % ROPE_PAIRS
    pos = jnp.where(axis[None, :] == 0, (t // GRID_W)[:, None], (t % GRID_W)[:, None]).astype(F32)
    inv_freq = ROPE_BASE ** (-jnp.arange(ROPE_PAIRS, dtype=F32) / ROPE_PAIRS)
    ang = pos * inv_freq[pair][None, :]
    sign = jnp.where(half[None, :] == 0, -1.0, 1.0).astype(F32)
    return jnp.cos(ang), jnp.sin(ang) * sign


def _rope(x, cos, sin_signed):
    lane = lax.broadcasted_iota(jnp.int32, x.shape, 1)
    first_half = (lane // ROPE_PAIRS) % 2 == 0
    partner = jnp.where(first_half, pltpu.roll(x, LANES - ROPE_PAIRS, 1), pltpu.roll(x, ROPE_PAIRS, 1))
    return x * cos + partner * sin_signed


def _inproj_kernel(x_ref, mod_ref, w_ref, b_ref, cos_ref, sin_ref,
                   q_ref, k_ref, v_ref, qb_ref, ib_ref, ff_ref, fb_ref, go_ref):
    i = pl.program_id(0)
    shift = mod_ref[0, 0:1, :]
    scale = mod_ref[0, 1:2, :]
    h = (x_ref[...] * (1.0 + scale) + shift).astype(BF16)
    y = _dot(h, w_ref[...]) + b_ref[...]
    offs = np.cumsum((0,) + IN_SIZES)
    q = y[:, offs[0]:offs[1]]
    k = y[:, offs[1]:offs[2]]
    v_ref[...] = y[:, offs[2]:offs[3]]
    qb_ref[...] = y[:, offs[3]:offs[4]]
    ib_ref[...] = y[:, offs[4]:offs[5]]
    ff_ref[...] = y[:, offs[5]:offs[6]]
    fb_ref[...] = y[:, offs[6]:offs[7]]
    go_ref[...] = y[:, offs[7]:offs[8]]
    is_latent = i * ROW_BLOCK >= N_PROMPT

    @pl.when(jnp.logical_not(is_latent))
    def _():
        q_ref[...] = q
        k_ref[...] = k

    @pl.when(is_latent)
    def _():
        cos = cos_ref[...]
        sin = sin_ref[...]
        for c in range(A_Q // LANES):
            q_ref[:, c * LANES:(c + 1) * LANES] = _rope(q[:, c * LANES:(c + 1) * LANES], cos, sin)
        k_ref[...] = _rope(k, cos, sin)


def inproj_even(x, mods, w_bf, b, cos, sin):
    nblk = N_TOK // ROW_BLOCK
    pos_blocks = DEC_SEQ // ROW_BLOCK

    def pos_map(i):
        return (jnp.maximum(i - N_PROMPT // ROW_BLOCK, 0) % pos_blocks, 0)

    row = lambda i: (i, 0)
    widths = (A_Q, A_KV, A_KV, B_QK, B_V, B_QK, B_QK, B_V)
    return pl.pallas_call(
        _inproj_kernel,
        grid=(nblk,),
        in_specs=[
            pl.BlockSpec((ROW_BLOCK, D_MODEL), row),
            pl.BlockSpec((1, 6, D_MODEL), lambda i: (_cond_of_block(i, ROW_BLOCK), 0, 0)),
            pl.BlockSpec((D_MODEL, D_IN_EVEN), lambda i: (0, 0)),
            pl.BlockSpec((1, D_IN_EVEN), lambda i: (0, 0)),
            pl.BlockSpec((ROW_BLOCK, LANES), pos_map),
            pl.BlockSpec((ROW_BLOCK, LANES), pos_map),
        ],
        out_specs=[pl.BlockSpec((ROW_BLOCK, w), row) for w in widths],
        out_shape=[jax.ShapeDtypeStruct((N_TOK, w), F32) for w in widths],
        compiler_params=_cparams(("parallel",)),
        name="inproj_even",
    )(x, mods, w_bf, b.reshape(1, D_IN_EVEN), cos, sin)


def _sink_attend(q, keys, vals, sink, masks):
    scores = []
    for kk, mask in zip(keys, masks):
        s = _dot_nt(q, kk) * SCALE_A
        if mask is not None:
            s = jnp.where(mask, s, MASK_VALUE)
        scores.append(s)
    m = sink
    for s in scores:
        m = jnp.maximum(m, jnp.max(s, axis=-1, keepdims=True))
    denom = jnp.exp(sink - m)
    acc = None
    for s, vv in zip(scores, vals):
        p = jnp.exp(s - m)
        denom = denom + jnp.sum(p, axis=-1, keepdims=True)
        pv = _dot(p.astype(BF16), vv)
        acc = pv if acc is None else acc + pv
    return acc / denom


def _attn_ctx_kernel(sink_ref, q_ref, k_ref, v_ref, o_ref):
    k = k_ref[...].astype(BF16)
    v = v_ref[...].astype(BF16)
    q = q_ref[...].astype(BF16)
    for h in range(N_HEADS_A):
        kv = h // GROUP_A
        qh = q[:, h * HEAD_DIM:(h + 1) * HEAD_DIM]
        kh = k[:, kv * HEAD_DIM:(kv + 1) * HEAD_DIM]
        vh = v[:, kv * HEAD_DIM:(kv + 1) * HEAD_DIM]
        o_ref[:, h * HEAD_DIM:(h + 1) * HEAD_DIM] = _sink_attend(qh, [kh], [vh], sink_ref[h], [None])


def attn_context(sink, q, k, v):
    row = lambda b: (b, 0)
    return pl.pallas_call(
        _attn_ctx_kernel,
        grid=(BATCH,),
        in_specs=[
            pl.BlockSpec(memory_space=pltpu.SMEM),
            pl.BlockSpec((SEQ, A_Q), row),
            pl.BlockSpec((SEQ, A_KV), row),
            pl.BlockSpec((SEQ, A_KV), row),
        ],
        out_specs=pl.BlockSpec((SEQ, A_Q), row),
        out_shape=jax.ShapeDtypeStruct((N_PROMPT, A_Q), F32),
        compiler_params=_cparams(("parallel",)),
        name="attn_context",
    )(sink, q, k, v)


def _attn_lat_kernel(sink_ref, q_ref, kp_ref, kc_ref, kn_ref, vp_ref, vc_ref, vn_ref, ck_ref, cv_ref, o_ref):
    n = pl.program_id(1)
    nb = DEC_SEQ // ATTN_BLOCK
    qi = lax.broadcasted_iota(jnp.int32, (ATTN_BLOCK, ATTN_BLOCK), 0)
    kj = lax.broadcasted_iota(jnp.int32, (ATTN_BLOCK, ATTN_BLOCK), 1)
    mask_prev = jnp.logical_and(kj - qi >= ATTN_BLOCK - WINDOW, n > 0)
    mask_next = jnp.logical_and(kj - qi <= WINDOW - ATTN_BLOCK, n < nb - 1)
    masks = [mask_prev, None, mask_next, None]
    q = q_ref[...].astype(BF16)
    kband = [r[...].astype(BF16) for r in (kp_ref, kc_ref, kn_ref)]
    vband = [r[...].astype(BF16) for r in (vp_ref, vc_ref, vn_ref)]
    ck = ck_ref[0].astype(BF16)
    cv = cv_ref[0].astype(BF16)
    for h in range(N_HEADS_A):
        kv = h // GROUP_A
        sl = slice(kv * HEAD_DIM, (kv + 1) * HEAD_DIM)
        qh = q[:, h * HEAD_DIM:(h + 1) * HEAD_DIM]
        keys = [kb[:, sl] for kb in kband] + [ck[:, sl]]
        vals = [vb[:, sl] for vb in vband] + [cv[:, sl]]
        o_ref[:, h * HEAD_DIM:(h + 1) * HEAD_DIM] = _sink_attend(qh, keys, vals, sink_ref[h], masks)


def attn_latent(sink, q, k, v, cache_k, cache_v):
    nb = DEC_SEQ // ATTN_BLOCK
    base = N_PROMPT // ATTN_BLOCK

    def blk(delta):
        return lambda b, n: (base + b * nb + jnp.clip(n + delta, 0, nb - 1), 0)

    kv_spec = lambda delta: pl.BlockSpec((ATTN_BLOCK, A_KV), blk(delta))
    cache_spec = pl.BlockSpec((1, PAST_LEN, A_KV), lambda b, n: (b, 0, 0))
    return pl.pallas_call(
        _attn_lat_kernel,
        grid=(DEC_BATCH, nb),
        in_specs=[
            pl.BlockSpec(memory_space=pltpu.SMEM),
            pl.BlockSpec((ATTN_BLOCK, A_Q), blk(0)),
            kv_spec(-1), kv_spec(0), kv_spec(1),
            kv_spec(-1), kv_spec(0), kv_spec(1),
            cache_spec, cache_spec,
        ],
        out_specs=pl.BlockSpec((ATTN_BLOCK, A_Q), lambda b, n: (b * nb + n, 0)),
        out_shape=jax.ShapeDtypeStruct((N_SAMPLE, A_Q), F32),
        compiler_params=_cparams(("parallel", "parallel")),
        name="attn_latent",
    )(sink, q, k, k, k, v, v, v, cache_k, cache_v)


def _hgrn_gate(f_pre, lb):
    log_f = jnp.logaddexp(jnp.log(jnp.maximum(lb, LB_FLOOR)), jnp.log1p(-lb) + jax.nn.log_sigmoid(f_pre))
    return 1.0 - jnp.exp(log_f), log_f


def _scan_rows(x, reverse):
    n = x.shape[0]
    row = lax.broadcasted_iota(jnp.int32, x.shape, 0)
    sh = 1
    while sh < n:
        if reverse:
            x = x + jnp.where(row < n - sh, pltpu.roll(x, n - sh, 0), 0.0)
        else:
            x = x + jnp.where(row >= sh, pltpu.roll(x, sh, 0), 0.0)
        sh *= 2
    return x


def _chunk_bcast(x, c, pick, shift):
    nc = HG_BLOCK // c
    rows = x.reshape(nc, c, x.shape[-1])[:, pick:pick + 1, :]
    zero = jnp.zeros((1, 1, x.shape[-1]), x.dtype)
    if shift == -1:
        rows = jnp.concatenate([zero, rows[:-1]], axis=0)
    elif shift == 1:
        rows = jnp.concatenate([rows[1:], zero], axis=0)
    return jnp.broadcast_to(rows, (nc, c, x.shape[-1])).reshape(HG_BLOCK, x.shape[-1])


def _hgrn_block(q, kk, v, log_f, st, reverse):
    cum = _scan_rows(log_f, reverse)
    tot = cum[0:1, :] if reverse else cum[HG_BLOCK - 1:HG_BLOCK, :]
    o = _dot_nt((q * jnp.exp(cum)).astype(BF16), st.astype(BF16))
    kd = (kk * jnp.exp(tot - cum)).astype(BF16)
    v_bf = v.astype(BF16)
    u_t = lax.dot_general(v_bf, kd, (((0,), (0,)), ((), ())), preferred_element_type=F32)
    st_new = st * jnp.exp(tot) + u_t
    ti = lax.broadcasted_iota(jnp.int32, (HG_BLOCK, HG_BLOCK), 0)
    si = lax.broadcasted_iota(jnp.int32, (HG_BLOCK, HG_BLOCK), 1)
    a = jnp.zeros((HG_BLOCK, HG_BLOCK), F32)
    c = HG_SUB
    while c < HG_BLOCK:
        if reverse:
            bound_t = _chunk_bcast(cum, c, 0, 1)
            bound_s = _chunk_bcast(cum, c, 0, 0)
            mask = jnp.logical_and((ti // c) % 2 == 0, si // c == ti // c + 1)
        else:
            bound_t = _chunk_bcast(cum, c, c - 1, -1)
            bound_s = _chunk_bcast(cum, c, c - 1, 0)
            mask = jnp.logical_and((ti // c) % 2 == 1, si // c == ti // c - 1)
        qc = (q * jnp.exp(cum - bound_t)).astype(BF16)
        kc = (kk * jnp.exp(bound_s - cum)).astype(BF16)
        a = a + jnp.where(mask, _dot_nt(qc, kc), 0.0)
        c *= 2
    o = o + _dot(a.astype(BF16), v_bf)
    nsub = HG_BLOCK // HG_SUB
    q3 = q.reshape(nsub, HG_SUB, HGRN_DK)
    k3 = kk.reshape(nsub, HG_SUB, HGRN_DK)
    v3 = v.reshape(nsub, HG_SUB, HGRN_DV)
    cum3 = cum.reshape(nsub, HG_SUB, HGRN_DK)
    t_off = lax.broadcasted_iota(jnp.int32, (1, HG_SUB, 1), 1)
    od = jnp.zeros((nsub, HG_SUB, HGRN_DV), F32)
    for s in range(HG_SUB):
        causal = (t_off <= s) if reverse else (t_off >= s)
        decay = jnp.exp(jnp.where(causal, cum3 - cum3[:, s:s + 1, :], MASK_VALUE))
        score = jnp.sum(q3 * decay * k3[:, s:s + 1, :], axis=-1, keepdims=True)
        od = od + score * v3[:, s:s + 1, :]
    return o + od.reshape(HG_BLOCK, HGRN_DV), st_new


def _hgrn_kernel(*refs, n_blocks, has_state_in, has_state_out):
    refs = list(refs)
    qb_ref, ib_ref, ff_ref, fb_ref, go_ref, lbf_ref, lbb_ref, nw_ref = refs[:8]
    refs = refs[8:]
    s0_ref = refs.pop(0) if has_state_in else None
    o_ref = refs.pop(0)
    so_ref = refs.pop(0) if has_state_out else None
    acc_ref = refs.pop(0)

    def run(reverse):
        f_ref, lb_ref = (fb_ref, lbb_ref) if reverse else (ff_ref, lbf_ref)
        lb = lb_ref[...]
        if has_state_in:
            st0 = s0_ref[0, 0, 1 if reverse else 0, 0].T
        else:
            st0 = jnp.zeros((HGRN_DV, HGRN_DK), F32)

        def body(it, st):
            blk = (n_blocks - 1 - it) if reverse else it
            rows = pl.ds(pl.multiple_of(blk * HG_BLOCK, HG_BLOCK), HG_BLOCK)
            qpre = qb_ref[rows, :]
            q = qpre * jax.nn.sigmoid(qpre)
            kk, log_f = _hgrn_gate(f_ref[rows, :], lb)
            o, st = _hgrn_block(q, kk, ib_ref[rows, :], log_f, st, reverse)
            if reverse:
                acc_ref[rows, :] = acc_ref[rows, :] + o
            else:
                acc_ref[rows, :] = o
            return st

        return lax.fori_loop(0, n_blocks, body, st0)

    st_f = run(False)
    st_b = run(True)
    if has_state_out:
        so_ref[0, 0, 0] = st_f.T
        so_ref[0, 1, 0] = st_b.T
    o = acc_ref[...]
    o = o * lax.rsqrt(jnp.mean(o * o, axis=-1, keepdims=True) + RMS_EPS) * nw_ref[...]
    g = go_ref[...]
    o_ref[...] = o * (g * jax.nn.sigmoid(g))


def hgrn_mixer(qb, ib, ff, fb, go, lb_f, lb_b, norm_w, *, latent, state_in=None, layer_j=0):
    t_len = DEC_SEQ if latent else SEQ
    n_seq = DEC_BATCH if latent else BATCH
    row0 = N_PROMPT // t_len if latent else 0
    tok = pl.BlockSpec((t_len, HGRN_DK), lambda b, h: (row0 + b, h))
    vec = pl.BlockSpec((1, HGRN_DK), lambda b, h: (0, h))
    in_specs = [tok] * 5 + [vec] * 3
    args = [qb, ib, ff, fb, go, lb_f.reshape(1, B_QK), lb_b.reshape(1, B_QK), norm_w.reshape(1, B_V)]
    if latent:
        in_specs.append(pl.BlockSpec((1, 1, 2, 1, HGRN_DK, HGRN_DV), lambda b, h: (b, layer_j, 0, h, 0, 0)))
        args.append(state_in)
    out_specs = [pl.BlockSpec((t_len, HGRN_DV), lambda b, h: (b, h))]
    out_shape = [jax.ShapeDtypeStruct((n_seq * t_len, B_V), F32)]
    if not latent:
        out_specs.append(pl.BlockSpec((1, 2, 1, HGRN_DK, HGRN_DV), lambda b, h: (b, 0, h, 0, 0)))
        out_shape.append(jax.ShapeDtypeStruct((BATCH, 2, N_HEADS_B, HGRN_DK, HGRN_DV), F32))
    kern = functools.partial(_hgrn_kernel, n_blocks=t_len // HG_BLOCK, has_state_in=latent, has_state_out=not latent)
    return pl.pallas_call(
        kern,
        grid=(n_seq, N_HEADS_B),
        in_specs=in_specs,
        out_specs=out_specs,
        out_shape=out_shape,
        scratch_shapes=[pltpu.VMEM((t_len, HGRN_DV), F32)],
        compiler_params=_cparams(("parallel", "parallel")),
        name="hgrn_latent" if latent else "hgrn_context",
    )(*args)


def _outproj_kernel(x_ref, mod_ref, ap_ref, as_ref, rp_ref, rs_ref, w_ref, g_ref, b_ref, o_ref):
    is_latent = pl.program_id(0) * ROW_BLOCK >= N_PROMPT
    attn = jnp.where(is_latent, as_ref[...], ap_ref[...]).astype(BF16)
    rec = jnp.where(is_latent, rs_ref[...], rp_ref[...]).astype(BF16)
    y = _dot(attn, w_ref[0:A_Q, :]) + _dot(rec, w_ref[A_Q:A_Q + B_V, :])
    z = DN_ALPHA * x_ref[...] + mod_ref[0, 2:3, :] * y
    o_ref[...] = _layer_norm(z, g_ref[...], b_ref[...])


def outproj_even(x, mods, attn_p, attn_s, rec_p, rec_s, w_bf, ln_g, ln_b):
    nblk = N_TOK // ROW_BLOCK
    npb = N_PROMPT // ROW_BLOCK
    row = lambda i: (i, 0)
    prow = lambda i: (jnp.minimum(i, npb - 1), 0)
    srow = lambda i: (jnp.maximum(i - npb, 0), 0)
    vec = pl.BlockSpec((1, D_MODEL), lambda i: (0, 0))
    return pl.pallas_call(
        _outproj_kernel,
        grid=(nblk,),
        in_specs=[
            pl.BlockSpec((ROW_BLOCK, D_MODEL), row),
            pl.BlockSpec((1, 6, D_MODEL), lambda i: (_cond_of_block(i, ROW_BLOCK), 0, 0)),
            pl.BlockSpec((ROW_BLOCK, A_Q), prow),
            pl.BlockSpec((ROW_BLOCK, A_Q), srow),
            pl.BlockSpec((ROW_BLOCK, B_V), prow),
            pl.BlockSpec((ROW_BLOCK, B_V), srow),
            pl.BlockSpec((A_Q + B_V, D_MODEL), lambda i: (0, 0)),
            vec, vec,
        ],
        out_specs=pl.BlockSpec((ROW_BLOCK, D_MODEL), row),
        out_shape=jax.ShapeDtypeStruct((N_TOK, D_MODEL), F32),
        compiler_params=_cparams(("parallel",)),
        name="outproj_even",
    )(x, mods, attn_p, attn_s, rec_p, rec_s, w_bf, ln_g.reshape(1, D_MODEL), ln_b.reshape(1, D_MODEL))


def _conv_in_kernel(x_ref, mod_ref, w_ref, b_ref, u_ref):
    h = (x_ref[...] * (1.0 + mod_ref[0, 1:2, :]) + mod_ref[0, 0:1, :]).astype(BF16)
    a = _dot(h, w_ref[:, 0:D_MODEL]) + b_ref[:, 0:D_MODEL]
    gt = _dot(h, w_ref[:, D_MODEL:2 * D_MODEL]) + b_ref[:, D_MODEL:2 * D_MODEL]
    u_ref[...] = a * jax.nn.sigmoid(gt)


def conv_in(x, mods, w_bf, b):
    row = lambda i: (i, 0)
    return pl.pallas_call(
        _conv_in_kernel,
        grid=(N_TOK // ROW_BLOCK,),
        in_specs=[
            pl.BlockSpec((ROW_BLOCK, D_MODEL), row),
            pl.BlockSpec((1, 6, D_MODEL), lambda i: (_cond_of_block(i, ROW_BLOCK), 0, 0)),
            pl.BlockSpec((D_MODEL, 2 * D_MODEL), lambda i: (0, 0)),
            pl.BlockSpec((1, 2 * D_MODEL), lambda i: (0, 0)),
        ],
        out_specs=pl.BlockSpec((ROW_BLOCK, D_MODEL), row),
        out_shape=jax.ShapeDtypeStruct((N_TOK, D_MODEL), F32),
        compiler_params=_cparams(("parallel",)),
        name="conv_in",
    )(x, mods, w_bf, b.reshape(1, 2 * D_MODEL))


CONV_HALO = 16
CONV_LANES = 256


def _conv_out_kernel(x_ref, mod_ref, up_ref, uc_ref, un_ref, dw_ref, dwb_ref, cg_ref, cb_ref,
                     w_ref, b_ref, g_ref, bb_ref, o_ref, pad_ref, acc_ref):
    i = pl.program_id(0)
    blocks_per_seq = DEC_SEQ // ROW_BLOCK
    j = i - N_PROMPT // ROW_BLOCK
    is_latent = j >= 0
    has_prev = jnp.logical_and(is_latent, j % blocks_per_seq != 0)
    has_next = jnp.logical_and(is_latent, j % blocks_per_seq != blocks_per_seq - 1)
    pad_ref[0:CONV_HALO, :] = jnp.where(has_prev, up_ref[...], 0.0)
    pad_ref[CONV_HALO:CONV_HALO + ROW_BLOCK, :] = uc_ref[...]
    pad_ref[CONV_HALO + ROW_BLOCK:, :] = jnp.where(has_next, un_ref[...], 0.0)
    first = CONV_HALO - CONV_WIDTH // 2
    for c in range(D_MODEL // CONV_LANES):
        lanes = slice(c * CONV_LANES, (c + 1) * CONV_LANES)
        acc = jnp.zeros((ROW_BLOCK, CONV_LANES), F32)
        for tap in range(CONV_WIDTH):
            acc = acc + pad_ref[first + tap:first + tap + ROW_BLOCK, lanes] * dw_ref[tap:tap + 1, lanes]
        acc_ref[:, lanes] = acc + dwb_ref[:, lanes]
    u = _layer_norm(acc_ref[...], cg_ref[...], cb_ref[...])
    u = (u * jax.nn.sigmoid(u)).astype(BF16)
    y = _dot(u, w_ref[...]) + b_ref[...]
    z = DN_ALPHA * x_ref[...] + mod_ref[0, 2:3, :] * y
    o_ref[...] = _layer_norm(z, g_ref[...], bb_ref[...])


def conv_out(x, mods, u, dw, dw_b, cln_g, cln_b, w_bf, b_out, ln_g, ln_b):
    nblk = N_TOK // ROW_BLOCK
    ratio = ROW_BLOCK // CONV_HALO
    nhalo = N_TOK // CONV_HALO
    row = lambda i: (i, 0)
    vec = pl.BlockSpec((1, D_MODEL), lambda i: (0, 0))
    r1 = lambda a: a.reshape(1, D_MODEL)
    return pl.pallas_call(
        _conv_out_kernel,
        grid=(nblk,),
        in_specs=[
            pl.BlockSpec((ROW_BLOCK, D_MODEL), row),
            pl.BlockSpec((1, 6, D_MODEL), lambda i: (_cond_of_block(i, ROW_BLOCK), 0, 0)),
            pl.BlockSpec((CONV_HALO, D_MODEL), lambda i: (jnp.maximum(i * ratio - 1, 0), 0)),
            pl.BlockSpec((ROW_BLOCK, D_MODEL), row),
            pl.BlockSpec((CONV_HALO, D_MODEL), lambda i: (jnp.minimum((i + 1) * ratio, nhalo - 1), 0)),
            pl.BlockSpec((CONV_WIDTH, D_MODEL), lambda i: (0, 0)),
            vec, vec, vec,
            pl.BlockSpec((D_MODEL, D_MODEL), lambda i: (0, 0)),
            vec, vec, vec,
        ],
        out_specs=pl.BlockSpec((ROW_BLOCK, D_MODEL), row),
        out_shape=jax.ShapeDtypeStruct((N_TOK, D_MODEL), F32),
        scratch_shapes=[pltpu.VMEM((ROW_BLOCK + 2 * CONV_HALO, D_MODEL), F32),
                        pltpu.VMEM((ROW_BLOCK, D_MODEL), F32)],
        compiler_params=_cparams(("parallel",)),
        name="conv_out",
    )(x, mods, u, u, u, dw, r1(dw_b), r1(cln_g), r1(cln_b), w_bf, r1(b_out), r1(ln_g), r1(ln_b))


def _modulate2(x_ref, mod_ref):
    return x_ref[...] * (1.0 + mod_ref[0, 4:5, :]) + mod_ref[0, 3:4, :]


def _route_kernel(x_ref, mod_ref, wr_ref, br_ref, idx_ref, gate_ref, rank_ref, cnt_ref, pk_ref, carry_ref):
    i = pl.program_id(0)

    @pl.when(i == 0)
    def _():
        carry_ref[...] = jnp.zeros_like(carry_ref)

    h = _modulate2(x_ref, mod_ref)
    pk_ref[...] = _pack_bf16_pairs(h)
    logits = lax.dot_general(wr_ref[...], h, (((1,), (1,)), ((), ())), precision=HIGHEST,
                             preferred_element_type=F32) + br_ref[...]
    eidx = lax.broadcasted_iota(jnp.int32, logits.shape, 0)
    vals = logits
    sels, tops = [], []
    for k in range(TOP_K):
        m = jnp.max(vals, axis=0, keepdims=True)
        idx = jnp.min(jnp.where(vals == m, eidx, N_EXPERTS), axis=0, keepdims=True)
        sel = eidx == idx
        idx_ref[k:k + 1, :] = idx
        sels.append(sel)
        tops.append(m)
        vals = jnp.where(sel, -jnp.inf, vals)
    exps = [jnp.exp(t - tops[0]) for t in tops]
    total = exps[0] + exps[1] + exps[2] + exps[3]
    for k in range(TOP_K):
        gate_ref[k:k + 1, :] = exps[k] / total
    onehot = jnp.zeros(logits.shape, F32)
    for sel in sels:
        onehot = onehot + sel.astype(F32)
    ta = lax.broadcasted_iota(jnp.int32, (ROW_BLOCK, ROW_BLOCK), 0)
    tb = lax.broadcasted_iota(jnp.int32, (ROW_BLOCK, ROW_BLOCK), 1)
    before = _dot(onehot.astype(BF16), (ta < tb).astype(BF16)) + carry_ref[:, 0:1]
    for k in range(TOP_K):
        rank = jnp.sum(jnp.where(sels[k], before, 0.0), axis=0, keepdims=True)
        rank_ref[k:k + 1, :] = rank.astype(jnp.int32)
    carry = carry_ref[...] + jnp.sum(onehot, axis=1, keepdims=True)
    carry_ref[...] = carry
    cnt_ref[...] = carry.astype(jnp.int32)


def moe_route(x, mods, wr_t, b_r):
    tok = pl.BlockSpec((TOP_K, ROW_BLOCK), lambda i: (0, i))
    return pl.pallas_call(
        _route_kernel,
        grid=(N_TOK // ROW_BLOCK,),
        in_specs=[
            pl.BlockSpec((ROW_BLOCK, D_MODEL), lambda i: (i, 0)),
            pl.BlockSpec((1, 6, D_MODEL), lambda i: (_cond_of_block(i, ROW_BLOCK), 0, 0)),
            pl.BlockSpec((N_EXPERTS, D_MODEL), lambda i: (0, 0)),
            pl.BlockSpec((N_EXPERTS, 1), lambda i: (0, 0)),
        ],
        out_specs=[tok, tok, tok, pl.BlockSpec((N_EXPERTS, LANES), lambda i: (0, 0)),
                   pl.BlockSpec((ROW_BLOCK, HALF_D), lambda i: (i, 0))],
        out_shape=[jax.ShapeDtypeStruct((TOP_K, N_TOK), jnp.int32),
                   jax.ShapeDtypeStruct((TOP_K, N_TOK), F32),
                   jax.ShapeDtypeStruct((TOP_K, N_TOK), jnp.int32),
                   jax.ShapeDtypeStruct((N_EXPERTS, LANES), jnp.int32),
                   jax.ShapeDtypeStruct((N_TOK, HALF_D), jnp.uint32)],
        scratch_shapes=[pltpu.VMEM((N_EXPERTS, LANES), F32)],
        compiler_params=_cparams(("arbitrary",)),
        name="moe_route",
    )(x, mods, wr_t, b_r.reshape(N_EXPERTS, 1))


HALF_D = D_MODEL // 2
HI_MASK = 0xFFFF0000


def _pack_bf16_pairs(h):
    lo = lax.bitcast_convert_type(h[:, :HALF_D].astype(BF16).astype(F32), jnp.uint32)
    hi = lax.bitcast_convert_type(h[:, HALF_D:].astype(BF16).astype(F32), jnp.uint32)
    return (lo >> 16) | (hi & jnp.uint32(HI_MASK))


def _unpack_bf16_pairs(w):
    lo = lax.bitcast_convert_type(w << 16, F32).astype(BF16)
    hi = lax.bitcast_convert_type(w & jnp.uint32(HI_MASK), F32).astype(BF16)
    return lo, hi


N_ASG = N_TOK * TOP_K
TRASH_ROWS = 2 * MOE_ROWS


INV_STEPS = 64
INV_ROWS = MOE_R // INV_STEPS
INV_ASG = N_ASG // INV_STEPS


def _invert_kernel(dest_ref, slot_ref, tok_ref):
    g = pl.program_id(0)

    @pl.when(g < INV_STEPS)
    def _():
        def init(n, carry):
            r = g * INV_ROWS + n
            slot_ref[r] = N_ASG + r % TRASH_ROWS
            tok_ref[r] = 0
            return carry

        lax.fori_loop(0, INV_ROWS, init, 0, unroll=8)

    @pl.when(g >= INV_STEPS)
    def _():
        first = (g - INV_STEPS) * INV_ASG
        first_tok = first % N_TOK

        def fill(n, carry):
            r = dest_ref[first + n]
            slot_ref[r] = first + n
            tok_ref[r] = first_tok + n
            return carry

        lax.fori_loop(0, INV_ASG, fill, 0, unroll=8)


def moe_invert(dest_flat):
    smem = pl.BlockSpec(memory_space=pltpu.SMEM)
    assert N_TOK % INV_ASG == 0 and MOE_R % INV_STEPS == 0
    return pl.pallas_call(
        _invert_kernel,
        grid=(2 * INV_STEPS,),
        in_specs=[smem],
        out_specs=[smem, smem],
        out_shape=[jax.ShapeDtypeStruct((MOE_R,), jnp.int32)] * 2,
        compiler_params=_cparams(("arbitrary",)),
        name="moe_invert",
    )(dest_flat)


N_CHUNK = 256
N_STAGES = 2 * (D_EXPERT // N_CHUNK)
ROWS_PER_STAGE = MOE_ROWS // N_STAGES


def _expert_kernel(be_ref, nused_ref, slot_ref, tok_ref, pk_ref, wg_ref, wu_ref, wd_ref, bg_ref, bu_ref, bd_ref,
                   ytok_ref, xbuf, ybuf, act_ref, wg_bf, wu_bf, wd_bf, gsem, ssem):
    i = pl.program_id(0)
    n_used = nused_ref[0]
    cur = i % 2
    nxt = 1 - cur

    def gather_rows(block, dst_slot, rows):
        for r in rows:
            tok = tok_ref[block * MOE_ROWS + r]
            pltpu.make_async_copy(pk_ref.at[pl.ds(tok, 1), :], xbuf.at[dst_slot, pl.ds(r, 1), :],
                                  gsem.at[dst_slot]).start()

    def scatter_rows(block, src_slot, rows):
        for r in rows:
            row = slot_ref[block * MOE_ROWS + r]
            pltpu.make_async_copy(ybuf.at[src_slot, pl.ds(r, 1), :], ytok_ref.at[pl.ds(row, 1), :],
                                  ssem.at[src_slot]).start()

    def wait_gather(slot):
        pltpu.make_async_copy(pk_ref.at[pl.ds(0, MOE_ROWS), :], xbuf.at[slot], gsem.at[slot]).wait()

    def wait_scatter(slot):
        pltpu.make_async_copy(ybuf.at[slot], ytok_ref.at[pl.ds(0, MOE_ROWS), :], ssem.at[slot]).wait()

    @pl.when(i == 0)
    def _():
        ybuf[...] = jnp.zeros_like(ybuf)
        for s in range(2):
            pltpu.make_async_copy(ybuf.at[s], ytok_ref.at[pl.ds(N_ASG + s * MOE_ROWS, MOE_ROWS), :], ssem.at[s]).start()
        for s in range(2):
            wait_scatter(s)
        gather_rows(0, 0, range(MOE_ROWS))

    @pl.when(jnp.logical_and(i >= 2, i - 2 < n_used))
    def _():
        wait_scatter(cur)

    def compute(do_gather, do_scatter):
        wait_gather(cur)
        new_expert = jnp.logical_or(i == 0, be_ref[i] != be_ref[jnp.maximum(i - 1, 0)])

        @pl.when(new_expert)
        def _():
            wg_bf[...] = wg_ref[0, 0].astype(BF16)
            wu_bf[...] = wu_ref[0, 0].astype(BF16)
            wd_bf[...] = wd_ref[0, 0].astype(BF16)

        def side_traffic(stage):
            rows = range(stage * ROWS_PER_STAGE, (stage + 1) * ROWS_PER_STAGE)
            if do_gather:
                gather_rows(i + 1, nxt, rows)
            if do_scatter:
                scatter_rows(i - 1, nxt, rows)

        lo, hi = _unpack_bf16_pairs(xbuf[cur])
        n_chunks = D_EXPERT // N_CHUNK
        for n in range(n_chunks):
            side_traffic(n)
            cols = slice(n * N_CHUNK, (n + 1) * N_CHUNK)
            gt = _dot(lo, wg_bf[0:HALF_D, cols]) + _dot(hi, wg_bf[HALF_D:, cols]) + bg_ref[0, 0, :, cols]
            up = _dot(lo, wu_bf[0:HALF_D, cols]) + _dot(hi, wu_bf[HALF_D:, cols]) + bu_ref[0, 0, :, cols]
            gt = jnp.minimum(gt, SWIGLU_LIMIT)
            up = jnp.clip(up, -SWIGLU_LIMIT, SWIGLU_LIMIT)
            act_ref[:, cols] = ((up + 1.0) * gt * jax.nn.sigmoid(SWIGLU_ALPHA * gt)).astype(BF16)
        for n in range(n_chunks):
            side_traffic(n_chunks + n)
            cols = slice(n * N_CHUNK, (n + 1) * N_CHUNK)
            ybuf[cur, :, cols] = _dot(act_ref[...], wd_bf[:, cols]) + bd_ref[0, 0, :, cols]

    is_block = i < n_used
    has_next = i + 1 < n_used
    for do_gather in (False, True):
        for do_scatter in (False, True):
            want_gather = has_next if do_gather else jnp.logical_not(has_next)
            want_scatter = (i >= 1) if do_scatter else (i < 1)
            cond = jnp.logical_and(is_block, jnp.logical_and(want_gather, want_scatter))
            pl.when(cond)(functools.partial(compute, do_gather, do_scatter))

    @pl.when(jnp.logical_and(i == n_used, i >= 1))
    def _():
        scatter_rows(i - 1, nxt, range(MOE_ROWS))


def moe_experts(block_e, n_used, slot, tok, packed, w_g, w_u, w_d, b_g, b_u, b_d, layer):
    def wmap(i, be, nu, sl, tk):
        return (layer, be[jnp.minimum(i, MOE_BLOCKS - 1)], 0, 0)

    wspec = pl.BlockSpec((1, 1, D_MODEL, D_EXPERT), wmap)
    bspec = pl.BlockSpec((1, 1, 1, D_EXPERT), wmap)
    r4 = lambda b: b.reshape(DEPTH, N_EXPERTS, 1, D_EXPERT)
    return pl.pallas_call(
        _expert_kernel,
        grid_spec=pltpu.PrefetchScalarGridSpec(
            num_scalar_prefetch=4,
            grid=(MOE_BLOCKS + 2,),
            in_specs=[pl.BlockSpec(memory_space=pl.ANY), wspec, wspec, wspec, bspec, bspec, bspec],
            out_specs=pl.BlockSpec(memory_space=pl.ANY),
            scratch_shapes=[pltpu.VMEM((2, MOE_ROWS, HALF_D), jnp.uint32),
                            pltpu.VMEM((2, MOE_ROWS, D_MODEL), F32),
                            pltpu.VMEM((MOE_ROWS, D_EXPERT), BF16)]
                           + [pltpu.VMEM((D_MODEL, D_EXPERT), BF16)] * 3
                           + [pltpu.SemaphoreType.DMA((2,)), pltpu.SemaphoreType.DMA((2,))],
        ),
        out_shape=jax.ShapeDtypeStruct((N_ASG + TRASH_ROWS, D_MODEL), F32),
        compiler_params=_cparams(("arbitrary",)),
        name="moe_experts",
    )(block_e, n_used, slot, tok, packed, w_g, w_u, w_d, r4(b_g), r4(b_u), r4(b_d))


def _combine_kernel(x_ref, mod_ref, gate_ref, y0_ref, y1_ref, y2_ref, y3_ref, g_ref, b_ref, o_ref):
    eye = (lax.broadcasted_iota(jnp.int32, (COMB_TOK, COMB_TOK), 0)
           == lax.broadcasted_iota(jnp.int32, (COMB_TOK, COMB_TOK), 1))
    gates = gate_ref[...]
    y = None
    for k, yk_ref in enumerate((y0_ref, y1_ref, y2_ref, y3_ref)):
        col = jnp.sum(jnp.where(eye, gates[k:k + 1, :], 0.0), axis=1, keepdims=True)
        y = col * yk_ref[...] if y is None else y + col * yk_ref[...]
    z = DN_ALPHA * x_ref[...] + mod_ref[0, 5:6, :] * y
    o_ref[...] = _layer_norm(z, g_ref[...], b_ref[...])


def moe_combine(x, mods, gates_t, ytok, ln_g, ln_b):
    vec = pl.BlockSpec((1, D_MODEL), lambda i: (0, 0))
    blocks_per_k = N_TOK // COMB_TOK
    yspec = lambda k: pl.BlockSpec((COMB_TOK, D_MODEL), lambda i: (k * blocks_per_k + i, 0))
    return pl.pallas_call(
        _combine_kernel,
        grid=(N_TOK // COMB_TOK,),
        in_specs=[
            pl.BlockSpec((COMB_TOK, D_MODEL), lambda i: (i, 0)),
            pl.BlockSpec((1, 6, D_MODEL), lambda i: (_cond_of_block(i, COMB_TOK), 0, 0)),
            pl.BlockSpec((TOP_K, COMB_TOK), lambda i: (0, i)),
            yspec(0), yspec(1), yspec(2), yspec(3),
            vec, vec,
        ],
        out_specs=pl.BlockSpec((COMB_TOK, D_MODEL), lambda i: (i, 0)),
        out_shape=jax.ShapeDtypeStruct((N_TOK, D_MODEL), F32),
        compiler_params=_cparams(("parallel",)),
        name="moe_combine",
    )(x, mods, gates_t, ytok, ytok, ytok, ytok, ln_g.reshape(1, D_MODEL), ln_b.reshape(1, D_MODEL))


def moe_layer(x, mods, layer, router_w, router_b, w_g, b_g, w_u, b_u, w_d, b_d, ln_g, ln_b):
    idx_t, gates_t, rank_t, counts, packed = moe_route(x, mods, router_w[layer].T, router_b[layer])
    counts = counts[:, 0]
    padded = (counts + MOE_ROWS - 1) // MOE_ROWS * MOE_ROWS
    ends = jnp.cumsum(padded)
    base = ends - padded
    n_used = (ends[-1] // MOE_ROWS).astype(jnp.int32)
    block_start = jnp.arange(MOE_BLOCKS, dtype=jnp.int32) * MOE_ROWS
    block_e = jnp.sum(block_start[:, None] >= ends[None, :], axis=1).astype(jnp.int32)
    block_e = jnp.minimum(block_e, block_e[jnp.maximum(n_used - 1, 0)])
    onehot = idx_t[:, :, None] == jnp.arange(N_EXPERTS, dtype=jnp.int32)[None, None, :]
    dest = rank_t + jnp.sum(jnp.where(onehot, base[None, None, :], 0), axis=-1)
    slot, tok = moe_invert(dest.reshape(-1).astype(jnp.int32))
    ytok = moe_experts(block_e, n_used.reshape(1), slot, tok, packed, w_g, w_u, w_d, b_g, b_u, b_d, layer)
    return moe_combine(x, mods, gates_t, ytok, ln_g, ln_b)


def kernel(x_prompt, x_sample, c, cache_k, cache_v, state_hgrn, c_ctx, w_ada, b_ada, ln_g, ln_b,
           w_in_even, b_in_even, attn_sink, hgrn_lb, hgrn_norm, w_out_even,
           conv_w_in, conv_b_in, conv_dw, conv_dw_b, conv_ln_g, conv_ln_b, conv_w_out, conv_b_out,
           router_w, router_b, moe_w_gate, moe_b_gate, moe_w_up, moe_b_up, moe_w_down, moe_b_down):
    x = jnp.concatenate([x_prompt.reshape(N_PROMPT, D_MODEL), x_sample.reshape(N_SAMPLE, D_MODEL)], axis=0)
    cond = jnp.concatenate([c_ctx[None, :], c, jnp.zeros((COND_ROWS - N_COND, D_MODEL), F32)], axis=0)
    mods_all = adaln_all(cond, w_ada, b_ada).reshape(DEPTH, COND_ROWS, 6, D_MODEL)
    lb = jax.nn.softmax(hgrn_lb.astype(F32), axis=1)
    lb = jnp.cumsum(lb, axis=1) - lb[:, :1]
    cos, sin = _rope_tables()
    new_k, new_v, new_s = [], [], []
    for layer in range(DEPTH):
        j = layer // 2
        mods = mods_all[layer]
        if layer % 2 == 0:
            q, k, v, qb, ib, ff, fb, go = inproj_even(x, mods, w_in_even[j].astype(BF16), b_in_even[j], cos, sin)
            new_k.append(k[:N_PROMPT].reshape(BATCH, SEQ, N_KV_A, HEAD_DIM))
            new_v.append(v[:N_PROMPT].reshape(BATCH, SEQ, N_KV_A, HEAD_DIM))
            attn_p = attn_context(attn_sink[j], q, k, v)
            attn_s = attn_latent(attn_sink[j], q, k, v,
                                 cache_k[:, j].reshape(DEC_BATCH, PAST_LEN, A_KV),
                                 cache_v[:, j].reshape(DEC_BATCH, PAST_LEN, A_KV))
            hg = (qb, ib, ff, fb, go, lb[0, j], lb[1, j], hgrn_norm[j])
            rec_p, states = hgrn_mixer(*hg, latent=False)
            rec_s, = hgrn_mixer(*hg, latent=True, state_in=state_hgrn, layer_j=j)
            new_s.append(states)
            x = outproj_even(x, mods, attn_p, attn_s, rec_p, rec_s, w_out_even[j].astype(BF16),
                             ln_g[layer, 0], ln_b[layer, 0])
        else:
            u = conv_in(x, mods, conv_w_in[j].astype(BF16), conv_b_in[j])
            x = conv_out(x, mods, u, conv_dw[j], conv_dw_b[j], conv_ln_g[j], conv_ln_b[j],
                         conv_w_out[j].astype(BF16), conv_b_out[j], ln_g[layer, 0], ln_b[layer, 0])
        x = moe_layer(x, mods, layer, router_w, router_b, moe_w_gate, moe_b_gate, moe_w_up, moe_b_up,
                      moe_w_down, moe_b_down, ln_g[layer, 1], ln_b[layer, 1])
    return (x[:N_PROMPT].reshape(BATCH, SEQ, D_MODEL),
            x[N_PROMPT:].reshape(DEC_BATCH, DEC_SEQ, D_MODEL),
            jnp.stack(new_k, axis=1), jnp.stack(new_v, axis=1), jnp.stack(new_s, axis=1))
```

```python
import functools

import jax
import jax.numpy as jnp
import numpy as np
from jax import lax
from jax.experimental import pallas as pl
from jax.experimental.pallas import tpu as pltpu

D_MODEL = 1024
BATCH = 16
SEQ = 256
DEPTH = 4
DEC_BATCH = 4
DEC_SEQ = 2048
PAST_LEN = 512
GRID_W = 64
N_EVEN = (DEPTH + 1) // 2
N_ODD = DEPTH // 2
HEAD_DIM = 64
N_HEADS_A = 8
N_KV_A = 2
GROUP_A = N_HEADS_A // N_KV_A
WINDOW = 128
ATTN_BLOCK = 128
SCALE_A = HEAD_DIM ** -0.5
ROPE_BASE = 10000.0
ROPE_PAIRS = HEAD_DIM // 4
N_HEADS_B = 4
HGRN_DK = 128
HGRN_DV = 128
CONV_WIDTH = 31
N_EXPERTS = 32
TOP_K = 4
D_EXPERT = D_MODEL
SWIGLU_LIMIT = 7.0
SWIGLU_ALPHA = 1.702
LN_EPS = 1e-5
RMS_EPS = 1e-6
MASK_VALUE = -1e9
LB_FLOOR = 1e-30
DN_ALPHA = (2 * DEPTH) ** 0.25
A_Q = N_HEADS_A * HEAD_DIM
A_KV = N_KV_A * HEAD_DIM
B_QK = N_HEADS_B * HGRN_DK
B_V = N_HEADS_B * HGRN_DV
IN_SIZES = (A_Q, A_KV, A_KV, B_QK, B_V, B_QK, B_QK, B_V)
D_IN_EVEN = sum(IN_SIZES)

N_PROMPT = BATCH * SEQ
N_SAMPLE = DEC_BATCH * DEC_SEQ
N_TOK = N_PROMPT + N_SAMPLE
N_COND = 1 + DEC_BATCH
COND_ROWS = 8

LANES = 128
SUBLANES = 8
VMEM_LIMIT = 56 * 1024 * 1024

ROW_BLOCK = 256
HG_BLOCK = 128
HG_SUB = 8
MOE_ROWS = 256
MOE_BLOCKS = (N_TOK * TOP_K + N_EXPERTS * (MOE_ROWS - 1)) // MOE_ROWS + 1
MOE_R = MOE_BLOCKS * MOE_ROWS
COMB_TOK = 128

F32 = jnp.float32
BF16 = jnp.bfloat16
HIGHEST = lax.Precision.HIGHEST


def _cond_of_block(i, rows):
    start = i * rows
    return jnp.where(start < N_PROMPT, 0, 1 + (start - N_PROMPT) // DEC_SEQ)


def _cparams(sem):
    return pltpu.CompilerParams(dimension_semantics=sem, vmem_limit_bytes=VMEM_LIMIT)


def _layer_norm(z, g, b):
    mu = jnp.mean(z, axis=-1, keepdims=True)
    zc = z - mu
    var = jnp.mean(zc * zc, axis=-1, keepdims=True)
    return zc * lax.rsqrt(var + LN_EPS) * g + b


def _dot(a, b):
    return jnp.dot(a, b, preferred_element_type=F32)


def _dot_nt(a, b):
    return lax.dot_general(a, b, (((1,), (1,)), ((), ())), preferred_element_type=F32)


ADA_TN = 1536


def _adaln_kernel(cond_ref, w_ref, b_ref, o_ref):
    c = cond_ref[...]
    s = c * jax.nn.sigmoid(c)
    o_ref[0] = jnp.dot(s, w_ref[0], precision=HIGHEST, preferred_element_type=F32) + b_ref[0]


def adaln_all(cond, w_ada, b_ada):
    n_out = 6 * D_MODEL
    return pl.pallas_call(
        _adaln_kernel,
        grid=(DEPTH, n_out // ADA_TN),
        in_specs=[
            pl.BlockSpec((COND_ROWS, D_MODEL), lambda l, n: (0, 0)),
            pl.BlockSpec((1, D_MODEL, ADA_TN), lambda l, n: (l, 0, n)),
            pl.BlockSpec((1, 1, ADA_TN), lambda l, n: (l, 0, n)),
        ],
        out_specs=pl.BlockSpec((1, COND_ROWS, ADA_TN), lambda l, n: (l, 0, n)),
        out_shape=jax.ShapeDtypeStruct((DEPTH, COND_ROWS, n_out), F32),
        compiler_params=_cparams(("parallel", "parallel")),
        name="adaln",
    )(cond, w_ada, b_ada.reshape(DEPTH, 1, n_out))


def _rope_tables():
    t = np.arange(DEC_SEQ)
    d = np.arange(LANES) % HEAD_DIM
    axis = d // (2 * ROPE_PAIRS)
    half = (d // ROPE_PAIRS) % 2
    pair = d % ROPE_PAIRS
    pos = jnp.where(axis[None, :] == 0, (t // GRID_W)[:, None], (t % GRID_W)[:, None]).astype(F32)
    inv_freq = ROPE_BASE ** (-jnp.arange(ROPE_PAIRS, dtype=F32) / ROPE_PAIRS)
    ang = pos * inv_freq[pair][None, :]
    sign = jnp.where(half[None, :] == 0, -1.0, 1.0).astype(F32)
    return jnp.cos(ang), jnp.sin(ang) * sign


def _rope(x, cos, sin_signed):
    lane = lax.broadcasted_iota(jnp.int32, x.shape, 1)
    first_half = (lane // ROPE_PAIRS) % 2 == 0
    partner = jnp.where(first_half, pltpu.roll(x, LANES - ROPE_PAIRS, 1), pltpu.roll(x, ROPE_PAIRS, 1))
    return x * cos + partner * sin_signed


def _inproj_kernel(x_ref, mod_ref, w_ref, b_ref, cos_ref, sin_ref,
                   q_ref, k_ref, v_ref, qb_ref, ib_ref, ff_ref, fb_ref, go_ref):
    i = pl.program_id(0)
    shift = mod_ref[0, 0:1, :]
    scale = mod_ref[0, 1:2, :]
    h = (x_ref[...] * (1.0 + scale) + shift).astype(BF16)
    y = _dot(h, w_ref[...]) + b_ref[...]
    offs = np.cumsum((0,) + IN_SIZES)
    q = y[:, offs[0]:offs[1]]
    k = y[:, offs[1]:offs[2]]
    v_ref[...] = y[:, offs[2]:offs[3]]
    qb_ref[...] = y[:, offs[3]:offs[4]]
    ib_ref[...] = y[:, offs[4]:offs[5]]
    ff_ref[...] = y[:, offs[5]:offs[6]]
    fb_ref[...] = y[:, offs[6]:offs[7]]
    go_ref[...] = y[:, offs[7]:offs[8]]
    is_latent = i * ROW_BLOCK >= N_PROMPT

    @pl.when(jnp.logical_not(is_latent))
    def _():
        q_ref[...] = q
        k_ref[...] = k

    @pl.when(is_latent)
    def _():
        cos = cos_ref[...]
        sin = sin_ref[...]
        for c in range(A_Q // LANES):
            q_ref[:, c * LANES:(c + 1) * LANES] = _rope(q[:, c * LANES:(c + 1) * LANES], cos, sin)
        k_ref[...] = _rope(k, cos, sin)


def inproj_even(x, mods, w_bf, b, cos, sin):
    nblk = N_TOK // ROW_BLOCK
    pos_blocks = DEC_SEQ // ROW_BLOCK

    def pos_map(i):
        return (jnp.maximum(i - N_PROMPT // ROW_BLOCK, 0) % pos_blocks, 0)

    row = lambda i: (i, 0)
    widths = (A_Q, A_KV, A_KV, B_QK, B_V, B_QK, B_QK, B_V)
    return pl.pallas_call(
        _inproj_kernel,
        grid=(nblk,),
        in_specs=[
            pl.BlockSpec((ROW_BLOCK, D_MODEL), row),
            pl.BlockSpec((1, 6, D_MODEL), lambda i: (_cond_of_block(i, ROW_BLOCK), 0, 0)),
            pl.BlockSpec((D_MODEL, D_IN_EVEN), lambda i: (0, 0)),
            pl.BlockSpec((1, D_IN_EVEN), lambda i: (0, 0)),
            pl.BlockSpec((ROW_BLOCK, LANES), pos_map),
            pl.BlockSpec((ROW_BLOCK, LANES), pos_map),
        ],
        out_specs=[pl.BlockSpec((ROW_BLOCK, w), row) for w in widths],
        out_shape=[jax.ShapeDtypeStruct((N_TOK, w), F32) for w in widths],
        compiler_params=_cparams(("parallel",)),
        name="inproj_even",
    )(x, mods, w_bf, b.reshape(1, D_IN_EVEN), cos, sin)


def _sink_attend(q, keys, vals, sink, masks):
    scores = []
    for kk, mask in zip(keys, masks):
        s = _dot_nt(q, kk) * SCALE_A
        if mask is not None:
            s = jnp.where(mask, s, MASK_VALUE)
        scores.append(s)
    m = sink
    for s in scores:
        m = jnp.maximum(m, jnp.max(s, axis=-1, keepdims=True))
    denom = jnp.exp(sink - m)
    acc = None
    for s, vv in zip(scores, vals):
        p = jnp.exp(s - m)
        denom = denom + jnp.sum(p, axis=-1, keepdims=True)
        pv = _dot(p.astype(BF16), vv)
        acc = pv if acc is None else acc + pv
    return acc / denom


def _attn_ctx_kernel(sink_ref, q_ref, k_ref, v_ref, o_ref):
    k = k_ref[...].astype(BF16)
    v = v_ref[...].astype(BF16)
    q = q_ref[...].astype(BF16)
    for h in range(N_HEADS_A):
        kv = h // GROUP_A
        qh = q[:, h * HEAD_DIM:(h + 1) * HEAD_DIM]
        kh = k[:, kv * HEAD_DIM:(kv + 1) * HEAD_DIM]
        vh = v[:, kv * HEAD_DIM:(kv + 1) * HEAD_DIM]
        o_ref[:, h * HEAD_DIM:(h + 1) * HEAD_DIM] = _sink_attend(qh, [kh], [vh], sink_ref[h], [None])


def attn_context(sink, q, k, v):
    row = lambda b: (b, 0)
    return pl.pallas_call(
        _attn_ctx_kernel,
        grid=(BATCH,),
        in_specs=[
            pl.BlockSpec(memory_space=pltpu.SMEM),
            pl.BlockSpec((SEQ, A_Q), row),
            pl.BlockSpec((SEQ, A_KV), row),
            pl.BlockSpec((SEQ, A_KV), row),
        ],
        out_specs=pl.BlockSpec((SEQ, A_Q), row),
        out_shape=jax.ShapeDtypeStruct((N_PROMPT, A_Q), F32),
        compiler_params=_cparams(("parallel",)),
        name="attn_context",
    )(sink, q, k, v)


def _attn_lat_kernel(sink_ref, q_ref, kp_ref, kc_ref, kn_ref, vp_ref, vc_ref, vn_ref, ck_ref, cv_ref, o_ref):
    n = pl.program_id(1)
    nb = DEC_SEQ // ATTN_BLOCK
    qi = lax.broadcasted_iota(jnp.int32, (ATTN_BLOCK, ATTN_BLOCK), 0)
    kj = lax.broadcasted_iota(jnp.int32, (ATTN_BLOCK, ATTN_BLOCK), 1)
    mask_prev = jnp.logical_and(kj - qi >= ATTN_BLOCK - WINDOW, n > 0)
    mask_next = jnp.logical_and(kj - qi <= WINDOW - ATTN_BLOCK, n < nb - 1)
    masks = [mask_prev, None, mask_next, None]
    q = q_ref[...].astype(BF16)
    kband = [r[...].astype(BF16) for r in (kp_ref, kc_ref, kn_ref)]
    vband = [r[...].astype(BF16) for r in (vp_ref, vc_ref, vn_ref)]
    ck = ck_ref[0].astype(BF16)
    cv = cv_ref[0].astype(BF16)
    for h in range(N_HEADS_A):
        kv = h // GROUP_A
        sl = slice(kv * HEAD_DIM, (kv + 1) * HEAD_DIM)
        qh = q[:, h * HEAD_DIM:(h + 1) * HEAD_DIM]
        keys = [kb[:, sl] for kb in kband] + [ck[:, sl]]
        vals = [vb[:, sl] for vb in vband] + [cv[:, sl]]
        o_ref[:, h * HEAD_DIM:(h + 1) * HEAD_DIM] = _sink_attend(qh, keys, vals, sink_ref[h], masks)


def attn_latent(sink, q, k, v, cache_k, cache_v):
    nb = DEC_SEQ // ATTN_BLOCK
    base = N_PROMPT // ATTN_BLOCK

    def blk(delta):
        return lambda b, n: (base + b * nb + jnp.clip(n + delta, 0, nb - 1), 0)

    kv_spec = lambda delta: pl.BlockSpec((ATTN_BLOCK, A_KV), blk(delta))
    cache_spec = pl.BlockSpec((1, PAST_LEN, A_KV), lambda b, n: (b, 0, 0))
    return pl.pallas_call(
        _attn_lat_kernel,
        grid=(DEC_BATCH, nb),
        in_specs=[
            pl.BlockSpec(memory_space=pltpu.SMEM),
            pl.BlockSpec((ATTN_BLOCK, A_Q), blk(0)),
            kv_spec(-1), kv_spec(0), kv_spec(1),
            kv_spec(-1), kv_spec(0), kv_spec(1),
            cache_spec, cache_spec,
        ],
        out_specs=pl.BlockSpec((ATTN_BLOCK, A_Q), lambda b, n: (b * nb + n, 0)),
        out_shape=jax.ShapeDtypeStruct((N_SAMPLE, A_Q), F32),
        compiler_params=_cparams(("parallel", "parallel")),
        name="attn_latent",
    )(sink, q, k, k, k, v, v, v, cache_k, cache_v)


def _hgrn_gate(f_pre, lb):
    log_f = jnp.logaddexp(jnp.log(jnp.maximum(lb, LB_FLOOR)), jnp.log1p(-lb) + jax.nn.log_sigmoid(f_pre))
    return 1.0 - jnp.exp(log_f), log_f


def _scan_rows(x, reverse):
    n = x.shape[0]
    row = lax.broadcasted_iota(jnp.int32, x.shape, 0)
    sh = 1
    while sh < n:
        if reverse:
            x = x + jnp.where(row < n - sh, pltpu.roll(x, n - sh, 0), 0.0)
        else:
            x = x + jnp.where(row >= sh, pltpu.roll(x, sh, 0), 0.0)
        sh *= 2
    return x


def _chunk_bcast(x, c, pick, shift):
    nc = HG_BLOCK // c
    rows = x.reshape(nc, c, x.shape[-1])[:, pick:pick + 1, :]
    zero = jnp.zeros((1, 1, x.shape[-1]), x.dtype)
    if shift == -1:
        rows = jnp.concatenate([zero, rows[:-1]], axis=0)
    elif shift == 1:
        rows = jnp.concatenate([rows[1:], zero], axis=0)
    return jnp.broadcast_to(rows, (nc, c, x.shape[-1])).reshape(HG_BLOCK, x.shape[-1])


def _hgrn_block(q, kk, v, log_f, st, reverse):
    cum = _scan_rows(log_f, reverse)
    tot = cum[0:1, :] if reverse else cum[HG_BLOCK - 1:HG_BLOCK, :]
    o = _dot_nt((q * jnp.exp(cum)).astype(BF16), st.astype(BF16))
    kd = (kk * jnp.exp(tot - cum)).astype(BF16)
    v_bf = v.astype(BF16)
    u_t = lax.dot_general(v_bf, kd, (((0,), (0,)), ((), ())), preferred_element_type=F32)
    st_new = st * jnp.exp(tot) + u_t
    ti = lax.broadcasted_iota(jnp.int32, (HG_BLOCK, HG_BLOCK), 0)
    si = lax.broadcasted_iota(jnp.int32, (HG_BLOCK, HG_BLOCK), 1)
    a = jnp.zeros((HG_BLOCK, HG_BLOCK), F32)
    c = HG_SUB
    while c < HG_BLOCK:
        if reverse:
            bound_t = _chunk_bcast(cum, c, 0, 1)
            bound_s = _chunk_bcast(cum, c, 0, 0)
            mask = jnp.logical_and((ti // c) % 2 == 0, si // c == ti // c + 1)
        else:
            bound_t = _chunk_bcast(cum, c, c - 1, -1)
            bound_s = _chunk_bcast(cum, c, c - 1, 0)
            mask = jnp.logical_and((ti // c) % 2 == 1, si // c == ti // c - 1)
        qc = (q * jnp.exp(cum - bound_t)).astype(BF16)
        kc = (kk * jnp.exp(bound_s - cum)).astype(BF16)
        a = a + jnp.where(mask, _dot_nt(qc, kc), 0.0)
        c *= 2
    o = o + _dot(a.astype(BF16), v_bf)
    nsub = HG_BLOCK // HG_SUB
    q3 = q.reshape(nsub, HG_SUB, HGRN_DK)
    k3 = kk.reshape(nsub, HG_SUB, HGRN_DK)
    v3 = v.reshape(nsub, HG_SUB, HGRN_DV)
    cum3 = cum.reshape(nsub, HG_SUB, HGRN_DK)
    t_off = lax.broadcasted_iota(jnp.int32, (1, HG_SUB, 1), 1)
    od = jnp.zeros((nsub, HG_SUB, HGRN_DV), F32)
    for s in range(HG_SUB):
        causal = (t_off <= s) if reverse else (t_off >= s)
        decay = jnp.exp(jnp.where(causal, cum3 - cum3[:, s:s + 1, :], MASK_VALUE))
        score = jnp.sum(q3 * decay * k3[:, s:s + 1, :], axis=-1, keepdims=True)
        od = od + score * v3[:, s:s + 1, :]
    return o + od.reshape(HG_BLOCK, HGRN_DV), st_new


def _hgrn_kernel(*refs, n_blocks, has_state_in, has_state_out):
    refs = list(refs)
    qb_ref, ib_ref, ff_ref, fb_ref, go_ref, lbf_ref, lbb_ref, nw_ref = refs[:8]
    refs = refs[8:]
    s0_ref = refs.pop(0) if has_state_in else None
    o_ref = refs.pop(0)
    so_ref = refs.pop(0) if has_state_out else None
    acc_ref = refs.pop(0)

    def run(reverse):
        f_ref, lb_ref = (fb_ref, lbb_ref) if reverse else (ff_ref, lbf_ref)
        lb = lb_ref[...]
        if has_state_in:
            st0 = s0_ref[0, 0, 1 if reverse else 0, 0].T
        else:
            st0 = jnp.zeros((HGRN_DV, HGRN_DK), F32)

        def body(it, st):
            blk = (n_blocks - 1 - it) if reverse else it
            rows = pl.ds(pl.multiple_of(blk * HG_BLOCK, HG_BLOCK), HG_BLOCK)
            qpre = qb_ref[rows, :]
            q = qpre * jax.nn.sigmoid(qpre)
            kk, log_f = _hgrn_gate(f_ref[rows, :], lb)
            o, st = _hgrn_block(q, kk, ib_ref[rows, :], log_f, st, reverse)
            if reverse:
                acc_ref[rows, :] = acc_ref[rows, :] + o
            else:
                acc_ref[rows, :] = o
            return st

        return lax.fori_loop(0, n_blocks, body, st0)

    st_f = run(False)
    st_b = run(True)
    if has_state_out:
        so_ref[0, 0, 0] = st_f.T
        so_ref[0, 1, 0] = st_b.T
    o = acc_ref[...]
    o = o * lax.rsqrt(jnp.mean(o * o, axis=-1, keepdims=True) + RMS_EPS) * nw_ref[...]
    g = go_ref[...]
    o_ref[...] = o * (g * jax.nn.sigmoid(g))


def hgrn_mixer(qb, ib, ff, fb, go, lb_f, lb_b, norm_w, *, latent, state_in=None, layer_j=0):
    t_len = DEC_SEQ if latent else SEQ
    n_seq = DEC_BATCH if latent else BATCH
    row0 = N_PROMPT // t_len if latent else 0
    tok = pl.BlockSpec((t_len, HGRN_DK), lambda b, h: (row0 + b, h))
    vec = pl.BlockSpec((1, HGRN_DK), lambda b, h: (0, h))
    in_specs = [tok] * 5 + [vec] * 3
    args = [qb, ib, ff, fb, go, lb_f.reshape(1, B_QK), lb_b.reshape(1, B_QK), norm_w.reshape(1, B_V)]
    if latent:
        in_specs.append(pl.BlockSpec((1, 1, 2, 1, HGRN_DK, HGRN_DV), lambda b, h: (b, layer_j, 0, h, 0, 0)))
        args.append(state_in)
    out_specs = [pl.BlockSpec((t_len, HGRN_DV), lambda b, h: (b, h))]
    out_shape = [jax.ShapeDtypeStruct((n_seq * t_len, B_V), F32)]
    if not latent:
        out_specs.append(pl.BlockSpec((1, 2, 1, HGRN_DK, HGRN_DV), lambda b, h: (b, 0, h, 0, 0)))
        out_shape.append(jax.ShapeDtypeStruct((BATCH, 2, N_HEADS_B, HGRN_DK, HGRN_DV), F32))
    kern = functools.partial(_hgrn_kernel, n_blocks=t_len // HG_BLOCK, has_state_in=latent, has_state_out=not latent)
    return pl.pallas_call(
        kern,
        grid=(n_seq, N_HEADS_B),
        in_specs=in_specs,
        out_specs=out_specs,
        out_shape=out_shape,
        scratch_shapes=[pltpu.VMEM((t_len, HGRN_DV), F32)],
        compiler_params=_cparams(("parallel", "parallel")),
        name="hgrn_latent" if latent else "hgrn_context",
    )(*args)


def _outproj_kernel(x_ref, mod_ref, ap_ref, as_ref, rp_ref, rs_ref, w_ref, g_ref, b_ref, o_ref):
    is_latent = pl.program_id(0) * ROW_BLOCK >= N_PROMPT
    attn = jnp.where(is_latent, as_ref[...], ap_ref[...]).astype(BF16)
    rec = jnp.where(is_latent, rs_ref[...], rp_ref[...]).astype(BF16)
    y = _dot(attn, w_ref[0:A_Q, :]) + _dot(rec, w_ref[A_Q:A_Q + B_V, :])
    z = DN_ALPHA * x_ref[...] + mod_ref[0, 2:3, :] * y
    o_ref[...] = _layer_norm(z, g_ref[...], b_ref[...])


def outproj_even(x, mods, attn_p, attn_s, rec_p, rec_s, w_bf, ln_g, ln_b):
    nblk = N_TOK // ROW_BLOCK
    npb = N_PROMPT // ROW_BLOCK
    row = lambda i: (i, 0)
    prow = lambda i: (jnp.minimum(i, npb - 1), 0)
    srow = lambda i: (jnp.maximum(i - npb, 0), 0)
    vec = pl.BlockSpec((1, D_MODEL), lambda i: (0, 0))
    return pl.pallas_call(
        _outproj_kernel,
        grid=(nblk,),
        in_specs=[
            pl.BlockSpec((ROW_BLOCK, D_MODEL), row),
            pl.BlockSpec((1, 6, D_MODEL), lambda i: (_cond_of_block(i, ROW_BLOCK), 0, 0)),
            pl.BlockSpec((ROW_BLOCK, A_Q), prow),
            pl.BlockSpec((ROW_BLOCK, A_Q), srow),
            pl.BlockSpec((ROW_BLOCK, B_V), prow),
            pl.BlockSpec((ROW_BLOCK, B_V), srow),
            pl.BlockSpec((A_Q + B_V, D_MODEL), lambda i: (0, 0)),
            vec, vec,
        ],
        out_specs=pl.BlockSpec((ROW_BLOCK, D_MODEL), row),
        out_shape=jax.ShapeDtypeStruct((N_TOK, D_MODEL), F32),
        compiler_params=_cparams(("parallel",)),
        name="outproj_even",
    )(x, mods, attn_p, attn_s, rec_p, rec_s, w_bf, ln_g.reshape(1, D_MODEL), ln_b.reshape(1, D_MODEL))


def _conv_in_kernel(x_ref, mod_ref, w_ref, b_ref, u_ref):
    h = (x_ref[...] * (1.0 + mod_ref[0, 1:2, :]) + mod_ref[0, 0:1, :]).astype(BF16)
    a = _dot(h, w_ref[:, 0:D_MODEL]) + b_ref[:, 0:D_MODEL]
    gt = _dot(h, w_ref[:, D_MODEL:2 * D_MODEL]) + b_ref[:, D_MODEL:2 * D_MODEL]
    u_ref[...] = a * jax.nn.sigmoid(gt)


def conv_in(x, mods, w_bf, b):
    row = lambda i: (i, 0)
    return pl.pallas_call(
        _conv_in_kernel,
        grid=(N_TOK // ROW_BLOCK,),
        in_specs=[
            pl.BlockSpec((ROW_BLOCK, D_MODEL), row),
            pl.BlockSpec((1, 6, D_MODEL), lambda i: (_cond_of_block(i, ROW_BLOCK), 0, 0)),
            pl.BlockSpec((D_MODEL, 2 * D_MODEL), lambda i: (0, 0)),
            pl.BlockSpec((1, 2 * D_MODEL), lambda i: (0, 0)),
        ],
        out_specs=pl.BlockSpec((ROW_BLOCK, D_MODEL), row),
        out_shape=jax.ShapeDtypeStruct((N_TOK, D_MODEL), F32),
        compiler_params=_cparams(("parallel",)),
        name="conv_in",
    )(x, mods, w_bf, b.reshape(1, 2 * D_MODEL))


CONV_HALO = 16
CONV_LANES = 256


CONV_SHIFT_ROWS = ROW_BLOCK + 2 * CONV_HALO - SUBLANES
CONV_ROW_CHUNK = 128


def _conv_out_kernel(x_ref, mod_ref, up_ref, uc_ref, un_ref, dw_ref, dwb_ref, cg_ref, cb_ref,
                     w_ref, b_ref, g_ref, bb_ref, o_ref, pad_ref, acc_ref, sh_ref):
    i = pl.program_id(0)
    blocks_per_seq = DEC_SEQ // ROW_BLOCK
    j = i - N_PROMPT // ROW_BLOCK
    is_latent = j >= 0
    has_prev = jnp.logical_and(is_latent, j % blocks_per_seq != 0)
    has_next = jnp.logical_and(is_latent, j % blocks_per_seq != blocks_per_seq - 1)
    pad_ref[0:CONV_HALO, :] = jnp.where(has_prev, up_ref[...], 0.0)
    pad_ref[CONV_HALO:CONV_HALO + ROW_BLOCK, :] = uc_ref[...]
    pad_ref[CONV_HALO + ROW_BLOCK:, :] = jnp.where(has_next, un_ref[...], 0.0)
    first = CONV_HALO - CONV_WIDTH // 2
    for c in range(D_MODEL // CONV_LANES):
        lanes = slice(c * CONV_LANES, (c + 1) * CONV_LANES)
        for s in range(SUBLANES):
            sh_ref[s] = pad_ref[s:s + CONV_SHIFT_ROWS, lanes]
        for r0 in range(0, ROW_BLOCK, CONV_ROW_CHUNK):
            acc = jnp.zeros((CONV_ROW_CHUNK, CONV_LANES), F32) + dwb_ref[:, lanes]
            for tap in range(CONV_WIDTH):
                whole, s = divmod(first + tap, SUBLANES)
                rows = slice(r0 + whole * SUBLANES, r0 + whole * SUBLANES + CONV_ROW_CHUNK)
                acc = acc + sh_ref[s, rows, :] * dw_ref[tap:tap + 1, lanes]
            acc_ref[r0:r0 + CONV_ROW_CHUNK, lanes] = acc
    u = _layer_norm(acc_ref[...], cg_ref[...], cb_ref[...])
    u = (u * jax.nn.sigmoid(u)).astype(BF16)
    y = _dot(u, w_ref[...]) + b_ref[...]
    z = DN_ALPHA * x_ref[...] + mod_ref[0, 2:3, :] * y
    o_ref[...] = _layer_norm(z, g_ref[...], bb_ref[...])


def conv_out(x, mods, u, dw, dw_b, cln_g, cln_b, w_bf, b_out, ln_g, ln_b):
    nblk = N_TOK // ROW_BLOCK
    ratio = ROW_BLOCK // CONV_HALO
    nhalo = N_TOK // CONV_HALO
    row = lambda i: (i, 0)
    vec = pl.BlockSpec((1, D_MODEL), lambda i: (0, 0))
    r1 = lambda a: a.reshape(1, D_MODEL)
    return pl.pallas_call(
        _conv_out_kernel,
        grid=(nblk,),
        in_specs=[
            pl.BlockSpec((ROW_BLOCK, D_MODEL), row),
            pl.BlockSpec((1, 6, D_MODEL), lambda i: (_cond_of_block(i, ROW_BLOCK), 0, 0)),
            pl.BlockSpec((CONV_HALO, D_MODEL), lambda i: (jnp.maximum(i * ratio - 1, 0), 0)),
            pl.BlockSpec((ROW_BLOCK, D_MODEL), row),
            pl.BlockSpec((CONV_HALO, D_MODEL), lambda i: (jnp.minimum((i + 1) * ratio, nhalo - 1), 0)),
            pl.BlockSpec((CONV_WIDTH, D_MODEL), lambda i: (0, 0)),
            vec, vec, vec,
            pl.BlockSpec((D_MODEL, D_MODEL), lambda i: (0, 0)),
            vec, vec, vec,
        ],
        out_specs=pl.BlockSpec((ROW_BLOCK, D_MODEL), row),
        out_shape=jax.ShapeDtypeStruct((N_TOK, D_MODEL), F32),
        scratch_shapes=[pltpu.VMEM((ROW_BLOCK + 2 * CONV_HALO, D_MODEL), F32),
                        pltpu.VMEM((ROW_BLOCK, D_MODEL), F32),
                        pltpu.VMEM((SUBLANES, CONV_SHIFT_ROWS, CONV_LANES), F32)],
        compiler_params=_cparams(("parallel",)),
        name="conv_out",
    )(x, mods, u, u, u, dw, r1(dw_b), r1(cln_g), r1(cln_b), w_bf, r1(b_out), r1(ln_g), r1(ln_b))


def _modulate2(x_ref, mod_ref):
    return x_ref[...] * (1.0 + mod_ref[0, 4:5, :]) + mod_ref[0, 3:4, :]


def _route_kernel(x_ref, mod_ref, wr_ref, br_ref, idx_ref, gate_ref, rank_ref, cnt_ref, carry_ref):
    i = pl.program_id(0)

    @pl.when(i == 0)
    def _():
        carry_ref[...] = jnp.zeros_like(carry_ref)

    h = _modulate2(x_ref, mod_ref)
    logits = lax.dot_general(wr_ref[...], h, (((1,), (1,)), ((), ())), precision=HIGHEST,
                             preferred_element_type=F32) + br_ref[...]
    eidx = lax.broadcasted_iota(jnp.int32, logits.shape, 0)
    vals = logits
    sels, tops = [], []
    for k in range(TOP_K):
        m = jnp.max(vals, axis=0, keepdims=True)
        idx = jnp.min(jnp.where(vals == m, eidx, N_EXPERTS), axis=0, keepdims=True)
        sel = eidx == idx
        idx_ref[k:k + 1, :] = idx
        sels.append(sel)
        tops.append(m)
        vals = jnp.where(sel, -jnp.inf, vals)
    exps = [jnp.exp(t - tops[0]) for t in tops]
    total = exps[0] + exps[1] + exps[2] + exps[3]
    for k in range(TOP_K):
        gate_ref[k:k + 1, :] = exps[k] / total
    onehot = jnp.zeros(logits.shape, F32)
    for sel in sels:
        onehot = onehot + sel.astype(F32)
    ta = lax.broadcasted_iota(jnp.int32, (ROW_BLOCK, ROW_BLOCK), 0)
    tb = lax.broadcasted_iota(jnp.int32, (ROW_BLOCK, ROW_BLOCK), 1)
    before = _dot(onehot.astype(BF16), (ta < tb).astype(BF16)) + carry_ref[:, 0:1]
    for k in range(TOP_K):
        rank = jnp.sum(jnp.where(sels[k], before, 0.0), axis=0, keepdims=True)
        rank_ref[k:k + 1, :] = rank.astype(jnp.int32)
    carry = carry_ref[...] + jnp.sum(onehot, axis=1, keepdims=True)
    carry_ref[...] = carry
    cnt_ref[...] = carry.astype(jnp.int32)


def moe_route(x, mods, wr_t, b_r):
    tok = pl.BlockSpec((TOP_K, ROW_BLOCK), lambda i: (0, i))
    return pl.pallas_call(
        _route_kernel,
        grid=(N_TOK // ROW_BLOCK,),
        in_specs=[
            pl.BlockSpec((ROW_BLOCK, D_MODEL), lambda i: (i, 0)),
            pl.BlockSpec((1, 6, D_MODEL), lambda i: (_cond_of_block(i, ROW_BLOCK), 0, 0)),
            pl.BlockSpec((N_EXPERTS, D_MODEL), lambda i: (0, 0)),
            pl.BlockSpec((N_EXPERTS, 1), lambda i: (0, 0)),
        ],
        out_specs=[tok, tok, tok, pl.BlockSpec((N_EXPERTS, LANES), lambda i: (0, 0))],
        out_shape=[jax.ShapeDtypeStruct((TOP_K, N_TOK), jnp.int32),
                   jax.ShapeDtypeStruct((TOP_K, N_TOK), F32),
                   jax.ShapeDtypeStruct((TOP_K, N_TOK), jnp.int32),
                   jax.ShapeDtypeStruct((N_EXPERTS, LANES), jnp.int32)],
        scratch_shapes=[pltpu.VMEM((N_EXPERTS, LANES), F32)],
        compiler_params=_cparams(("arbitrary",)),
        name="moe_route",
    )(x, mods, wr_t, b_r.reshape(N_EXPERTS, 1))


HALF_D = D_MODEL // 2
HI_MASK = 0xFFFF0000


def _pack_bf16_pairs(h):
    lo = lax.bitcast_convert_type(h[:, :HALF_D].astype(BF16).astype(F32), jnp.uint32)
    hi = lax.bitcast_convert_type(h[:, HALF_D:].astype(BF16).astype(F32), jnp.uint32)
    return (lo >> 16) | (hi & jnp.uint32(HI_MASK))


def _unpack_bf16_pairs(w):
    lo = lax.bitcast_convert_type(w << 16, F32).astype(BF16)
    hi = lax.bitcast_convert_type(w & jnp.uint32(HI_MASK), F32).astype(BF16)
    return lo, hi


def _scatter_kernel(dest_ref, zstart_ref, x_ref, mod_ref, xs_ref, pk_ref, zero_ref, sem, zsem):
    i = pl.program_id(0)

    def zero_copy(start):
        start = pl.multiple_of(start, MOE_ROWS)
        return pltpu.make_async_copy(zero_ref, xs_ref.at[pl.ds(start, MOE_ROWS), :], zsem)

    def zero_blocks(fn):
        for e in range(N_EXPERTS):
            @pl.when(zstart_ref[e] >= 0)
            def _():
                fn(zero_copy(jnp.maximum(zstart_ref[e], 0)))

        def unused(b, carry):
            fn(zero_copy(b * MOE_ROWS))
            return carry

        lax.fori_loop(zstart_ref[N_EXPERTS], MOE_BLOCKS, unused, 0)

    @pl.when(i == 0)
    def _():
        zero_ref[...] = jnp.zeros_like(zero_ref)
        zero_blocks(lambda cp: cp.start())
        zero_blocks(lambda cp: cp.wait())

    pk_ref[...] = _pack_bf16_pairs(_modulate2(x_ref, mod_ref))
    base = i * ROW_BLOCK

    for t in range(ROW_BLOCK):
        for k in range(TOP_K):
            row = dest_ref[k * N_TOK + base + t]
            pltpu.make_async_copy(pk_ref.at[pl.ds(t, 1), :], xs_ref.at[pl.ds(row, 1), :], sem).start(priority=k % 2)
    for k in range(TOP_K):
        pltpu.make_async_copy(pk_ref, xs_ref.at[pl.ds(0, ROW_BLOCK), :], sem).wait()


def moe_scatter(dest_flat, zstart, x, mods):
    return pl.pallas_call(
        _scatter_kernel,
        grid_spec=pltpu.PrefetchScalarGridSpec(
            num_scalar_prefetch=2,
            grid=(N_TOK // ROW_BLOCK,),
            in_specs=[
                pl.BlockSpec((ROW_BLOCK, D_MODEL), lambda i, d, z: (i, 0)),
                pl.BlockSpec((1, 6, D_MODEL), lambda i, d, z: (_cond_of_block(i, ROW_BLOCK), 0, 0)),
            ],
            out_specs=pl.BlockSpec(memory_space=pl.ANY),
            scratch_shapes=[pltpu.VMEM((ROW_BLOCK, HALF_D), jnp.uint32),
                            pltpu.VMEM((MOE_ROWS, HALF_D), jnp.uint32),
                            pltpu.SemaphoreType.DMA, pltpu.SemaphoreType.DMA],
        ),
        out_shape=jax.ShapeDtypeStruct((MOE_R, HALF_D), jnp.uint32),
        compiler_params=_cparams(("arbitrary",)),
        name="moe_scatter",
    )(dest_flat, zstart, x, mods)


def _expert_kernel(be_ref, nused_ref, xs_ref, wg_ref, wu_ref, wd_ref, bg_ref, bu_ref, bd_ref,
                   ys_ref, wg_bf, wu_bf, wd_bf):
    i = pl.program_id(0)
    new_expert = jnp.logical_or(i == 0, be_ref[i] != be_ref[jnp.maximum(i - 1, 0)])

    @pl.when(new_expert)
    def _():
        wg_bf[...] = wg_ref[0, 0].astype(BF16)
        wu_bf[...] = wu_ref[0, 0].astype(BF16)
        wd_bf[...] = wd_ref[0, 0].astype(BF16)

    @pl.when(i < nused_ref[0])
    def _():
        lo, hi = _unpack_bf16_pairs(xs_ref[...])
        gt = _dot(lo, wg_bf[0:HALF_D, :]) + _dot(hi, wg_bf[HALF_D:, :]) + bg_ref[0, 0]
        up = _dot(lo, wu_bf[0:HALF_D, :]) + _dot(hi, wu_bf[HALF_D:, :]) + bu_ref[0, 0]
        gt = jnp.minimum(gt, SWIGLU_LIMIT)
        up = jnp.clip(up, -SWIGLU_LIMIT, SWIGLU_LIMIT)
        act = ((up + 1.0) * gt * jax.nn.sigmoid(SWIGLU_ALPHA * gt)).astype(BF16)
        ys_ref[...] = _dot(act, wd_bf[...]) + bd_ref[0, 0]

    @pl.when(i >= nused_ref[0])
    def _():
        ys_ref[...] = jnp.zeros_like(ys_ref)


def moe_experts(block_e, n_used, xs, w_g, w_u, w_d, b_g, b_u, b_d, layer):
    rows = lambda i, be, nu: (i, 0)
    wspec = pl.BlockSpec((1, 1, D_MODEL, D_EXPERT), lambda i, be, nu: (layer, be[i], 0, 0))
    bspec = pl.BlockSpec((1, 1, 1, D_EXPERT), lambda i, be, nu: (layer, be[i], 0, 0))
    r4 = lambda b: b.reshape(DEPTH, N_EXPERTS, 1, D_EXPERT)
    return pl.pallas_call(
        _expert_kernel,
        grid_spec=pltpu.PrefetchScalarGridSpec(
            num_scalar_prefetch=2,
            grid=(MOE_BLOCKS,),
            in_specs=[pl.BlockSpec((MOE_ROWS, HALF_D), rows), wspec, wspec, wspec, bspec, bspec, bspec],
            out_specs=pl.BlockSpec((MOE_ROWS, D_MODEL), rows),
            scratch_shapes=[pltpu.VMEM((D_MODEL, D_EXPERT), BF16)] * 3,
        ),
        out_shape=jax.ShapeDtypeStruct((MOE_R, D_MODEL), F32),
        compiler_params=_cparams(("arbitrary",)),
        name="moe_experts",
    )(block_e, n_used, xs, w_g, w_u, w_d, r4(b_g), r4(b_u), r4(b_d))


def _combine_kernel(dest_ref, x_ref, mod_ref, gate_ref, ys_ref, g_ref, b_ref, o_ref, buf_ref, sem):
    i = pl.program_id(0)
    base = i * COMB_TOK

    for t in range(COMB_TOK):
        for k in range(TOP_K):
            row = dest_ref[k * N_TOK + base + t]
            pltpu.make_async_copy(ys_ref.at[pl.ds(row, 1), :], buf_ref.at[k, pl.ds(t, 1), :], sem).start(priority=k % 2)
    eye = (lax.broadcasted_iota(jnp.int32, (COMB_TOK, COMB_TOK), 0)
           == lax.broadcasted_iota(jnp.int32, (COMB_TOK, COMB_TOK), 1))
    gates = gate_ref[...]
    cols = [jnp.sum(jnp.where(eye, gates[k:k + 1, :], 0.0), axis=1, keepdims=True) for k in range(TOP_K)]
    for k in range(TOP_K):
        pltpu.make_async_copy(ys_ref.at[pl.ds(0, COMB_TOK), :], buf_ref.at[k], sem).wait()
    y = cols[0] * buf_ref[0]
    for k in range(1, TOP_K):
        y = y + cols[k] * buf_ref[k]
    z = DN_ALPHA * x_ref[...] + mod_ref[0, 5:6, :] * y
    o_ref[...] = _layer_norm(z, g_ref[...], b_ref[...])


def moe_combine(dest_flat, x, mods, gates_t, ys, ln_g, ln_b):
    vec = pl.BlockSpec((1, D_MODEL), lambda i, d: (0, 0))
    return pl.pallas_call(
        _combine_kernel,
        grid_spec=pltpu.PrefetchScalarGridSpec(
            num_scalar_prefetch=1,
            grid=(N_TOK // COMB_TOK,),
            in_specs=[
                pl.BlockSpec((COMB_TOK, D_MODEL), lambda i, d: (i, 0)),
                pl.BlockSpec((1, 6, D_MODEL), lambda i, d: (_cond_of_block(i, COMB_TOK), 0, 0)),
                pl.BlockSpec((TOP_K, COMB_TOK), lambda i, d: (0, i)),
                pl.BlockSpec(memory_space=pl.ANY),
                vec, vec,
            ],
            out_specs=pl.BlockSpec((COMB_TOK, D_MODEL), lambda i, d: (i, 0)),
            scratch_shapes=[pltpu.VMEM((TOP_K, COMB_TOK, D_MODEL), F32), pltpu.SemaphoreType.DMA],
        ),
        out_shape=jax.ShapeDtypeStruct((N_TOK, D_MODEL), F32),
        compiler_params=_cparams(("arbitrary",)),
        name="moe_combine",
    )(dest_flat, x, mods, gates_t, ys, ln_g.reshape(1, D_MODEL), ln_b.reshape(1, D_MODEL))


def moe_layer(x, mods, layer, router_w, router_b, w_g, b_g, w_u, b_u, w_d, b_d, ln_g, ln_b):
    idx_t, gates_t, rank_t, counts = moe_route(x, mods, router_w[layer].T, router_b[layer])
    counts = counts[:, 0]
    padded = (counts + MOE_ROWS - 1) // MOE_ROWS * MOE_ROWS
    ends = jnp.cumsum(padded)
    base = ends - padded
    n_used = (ends[-1] // MOE_ROWS).astype(jnp.int32)
    block_start = jnp.arange(MOE_BLOCKS, dtype=jnp.int32) * MOE_ROWS
    block_e = jnp.sum(block_start[:, None] >= ends[None, :], axis=1).astype(jnp.int32)
    block_e = jnp.minimum(block_e, block_e[jnp.maximum(n_used - 1, 0)])
    zstart = jnp.where(padded > 0, ends - MOE_ROWS, -1).astype(jnp.int32)
    zstart = jnp.concatenate([zstart, n_used.reshape(1)])
    onehot = idx_t[:, :, None] == jnp.arange(N_EXPERTS, dtype=jnp.int32)[None, None, :]
    dest = rank_t + jnp.sum(jnp.where(onehot, base[None, None, :], 0), axis=-1)
    dest_flat = dest.reshape(-1).astype(jnp.int32)
    xs = moe_scatter(dest_flat, zstart, x, mods)
    ys = moe_experts(block_e, n_used.reshape(1), xs, w_g, w_u, w_d, b_g, b_u, b_d, layer)
    return moe_combine(dest_flat, x, mods, gates_t, ys, ln_g, ln_b)


def kernel(x_prompt, x_sample, c, cache_k, cache_v, state_hgrn, c_ctx, w_ada, b_ada, ln_g, ln_b,
           w_in_even, b_in_even, attn_sink, hgrn_lb, hgrn_norm, w_out_even,
           conv_w_in, conv_b_in, conv_dw, conv_dw_b, conv_ln_g, conv_ln_b, conv_w_out, conv_b_out,
           router_w, router_b, moe_w_gate, moe_b_gate, moe_w_up, moe_b_up, moe_w_down, moe_b_down):
    x = jnp.concatenate([x_prompt.reshape(N_PROMPT, D_MODEL), x_sample.reshape(N_SAMPLE, D_MODEL)], axis=0)
    cond = jnp.concatenate([c_ctx[None, :], c, jnp.zeros((COND_ROWS - N_COND, D_MODEL), F32)], axis=0)
    mods_all = adaln_all(cond, w_ada, b_ada).reshape(DEPTH, COND_ROWS, 6, D_MODEL)
    lb = jax.nn.softmax(hgrn_lb.astype(F32), axis=1)
    lb = jnp.cumsum(lb, axis=1) - lb[:, :1]
    cos, sin = _rope_tables()
    new_k, new_v, new_s = [], [], []
    for layer in range(DEPTH):
        j = layer // 2
        mods = mods_all[layer]
        if layer % 2 == 0:
            q, k, v, qb, ib, ff, fb, go = inproj_even(x, mods, w_in_even[j].astype(BF16), b_in_even[j], cos, sin)
            new_k.append(k[:N_PROMPT].reshape(BATCH, SEQ, N_KV_A, HEAD_DIM))
            new_v.append(v[:N_PROMPT].reshape(BATCH, SEQ, N_KV_A, HEAD_DIM))
            attn_p = attn_context(attn_sink[j], q, k, v)
            attn_s = attn_latent(attn_sink[j], q, k, v,
                                 cache_k[:, j].reshape(DEC_BATCH, PAST_LEN, A_KV),
                                 cache_v[:, j].reshape(DEC_BATCH, PAST_LEN, A_KV))
            hg = (qb, ib, ff, fb, go, lb[0, j], lb[1, j], hgrn_norm[j])
            rec_p, states = hgrn_mixer(*hg, latent=False)
            rec_s, = hgrn_mixer(*hg, latent=True, state_in=state_hgrn, layer_j=j)
            new_s.append(states)
            x = outproj_even(x, mods, attn_p, attn_s, rec_p, rec_s, w_out_even[j].astype(BF16),
                             ln_g[layer, 0], ln_b[layer, 0])
        else:
            u = conv_in(x, mods, conv_w_in[j].astype(BF16), conv_b_in[j])
            x = conv_out(x, mods, u, conv_dw[j], conv_dw_b[j], conv_ln_g[j], conv_ln_b[j],
                         conv_w_out[j].astype(BF16), conv_b_out[j], ln_g[layer, 0], ln_b[layer, 0])
        x = moe_layer(x, mods, layer, router_w, router_b, moe_w_gate, moe_b_gate, moe_w_up, moe_b_up,
                      moe_w_down, moe_b_down, ln_g[layer, 1], ln_b[layer, 1])
    return (x[:N_PROMPT].reshape(BATCH, SEQ, D_MODEL),
            x[N_PROMPT:].reshape(DEC_BATCH, DEC_SEQ, D_MODEL),
            jnp.stack(new_k, axis=1), jnp.stack(new_v, axis=1), jnp.stack(new_s, axis=1))
```

```python
import functools

import jax
import jax.numpy as jnp
import numpy as np
from jax import lax
from jax.experimental import pallas as pl
from jax.experimental.pallas import tpu as pltpu

D_MODEL = 1024
BATCH = 16
SEQ = 256
DEPTH = 4
DEC_BATCH = 4
DEC_SEQ = 2048
PAST_LEN = 512
GRID_W = 64
N_EVEN = (DEPTH + 1) // 2
N_ODD = DEPTH // 2
HEAD_DIM = 64
N_HEADS_A = 8
N_KV_A = 2
GROUP_A = N_HEADS_A // N_KV_A
WINDOW = 128
ATTN_BLOCK = 128
SCALE_A = HEAD_DIM ** -0.5
ROPE_BASE = 10000.0
ROPE_PAIRS = HEAD_DIM // 4
N_HEADS_B = 4
HGRN_DK = 128
HGRN_DV = 128
CONV_WIDTH = 31
N_EXPERTS = 32
TOP_K = 4
D_EXPERT = D_MODEL
SWIGLU_LIMIT = 7.0
SWIGLU_ALPHA = 1.702
LN_EPS = 1e-5
RMS_EPS = 1e-6
MASK_VALUE = -1e9
LB_FLOOR = 1e-30
DN_ALPHA = (2 * DEPTH) ** 0.25
A_Q = N_HEADS_A * HEAD_DIM
A_KV = N_KV_A * HEAD_DIM
B_QK = N_HEADS_B * HGRN_DK
B_V = N_HEADS_B * HGRN_DV
IN_SIZES = (A_Q, A_KV, A_KV, B_QK, B_V, B_QK, B_QK, B_V)
D_IN_EVEN = sum(IN_SIZES)

N_PROMPT = BATCH * SEQ
N_SAMPLE = DEC_BATCH * DEC_SEQ
N_TOK = N_PROMPT + N_SAMPLE
N_COND = 1 + DEC_BATCH
COND_ROWS = 8

LANES = 128
SUBLANES = 8
VMEM_LIMIT = 56 * 1024 * 1024

ROW_BLOCK = 256
HG_BLOCK = 128
HG_SUB = 8
MOE_ROWS = 256
MOE_BLOCKS = (N_TOK * TOP_K + N_EXPERTS * (MOE_ROWS - 1)) // MOE_ROWS + 1
MOE_R = MOE_BLOCKS * MOE_ROWS
COMB_TOK = 128

F32 = jnp.float32
BF16 = jnp.bfloat16
HIGHEST = lax.Precision.HIGHEST


def _cond_of_block(i, rows):
    start = i * rows
    return jnp.where(start < N_PROMPT, 0, 1 + (start - N_PROMPT) // DEC_SEQ)


def _cparams(sem):
    return pltpu.CompilerParams(dimension_semantics=sem, vmem_limit_bytes=VMEM_LIMIT)


def _layer_norm(z, g, b):
    mu = jnp.mean(z, axis=-1, keepdims=True)
    zc = z - mu
    var = jnp.mean(zc * zc, axis=-1, keepdims=True)
    return zc * lax.rsqrt(var + LN_EPS) * g + b


def _dot(a, b):
    return jnp.dot(a, b, preferred_element_type=F32)


def _dot_nt(a, b):
    return lax.dot_general(a, b, (((1,), (1,)), ((), ())), preferred_element_type=F32)


ADA_TN = 1536


def _adaln_kernel(cond_ref, w_ref, b_ref, o_ref):
    c = cond_ref[...]
    s = c * jax.nn.sigmoid(c)
    o_ref[0] = jnp.dot(s, w_ref[0], precision=HIGHEST, preferred_element_type=F32) + b_ref[0]


def adaln_all(cond, w_ada, b_ada):
    n_out = 6 * D_MODEL
    return pl.pallas_call(
        _adaln_kernel,
        grid=(DEPTH, n_out // ADA_TN),
        in_specs=[
            pl.BlockSpec((COND_ROWS, D_MODEL), lambda l, n: (0, 0)),
            pl.BlockSpec((1, D_MODEL, ADA_TN), lambda l, n: (l, 0, n)),
            pl.BlockSpec((1, 1, ADA_TN), lambda l, n: (l, 0, n)),
        ],
        out_specs=pl.BlockSpec((1, COND_ROWS, ADA_TN), lambda l, n: (l, 0, n)),
        out_shape=jax.ShapeDtypeStruct((DEPTH, COND_ROWS, n_out), F32),
        compiler_params=_cparams(("parallel", "parallel")),
        name="adaln",
    )(cond, w_ada, b_ada.reshape(DEPTH, 1, n_out))


def _rope_tables():
    t = np.arange(DEC_SEQ)
    d = np.arange(LANES) % HEAD_DIM
    axis = d // (2 * ROPE_PAIRS)
    half = (d // ROPE_PAIRS) % 2
    pair = d % ROPE_PAIRS
    pos = jnp.where(axis[None, :] == 0, (t // GRID_W)[:, None], (t % GRID_W)[:, None]).astype(F32)
    inv_freq = ROPE_BASE ** (-jnp.arange(ROPE_PAIRS, dtype=F32) / ROPE_PAIRS)
    ang = pos * inv_freq[pair][None, :]
    sign = jnp.where(half[None, :] == 0, -1.0, 1.0).astype(F32)
    return jnp.cos(ang), jnp.sin(ang) * sign


def _rope(x, cos, sin_signed):
    lane = lax.broadcasted_iota(jnp.int32, x.shape, 1)
    first_half = (lane // ROPE_PAIRS) % 2 == 0
    partner = jnp.where(first_half, pltpu.roll(x, LANES - ROPE_PAIRS, 1), pltpu.roll(x, ROPE_PAIRS, 1))
    return x * cos + partner * sin_signed


def _inproj_kernel(x_ref, mod_ref, w_ref, b_ref, cos_ref, sin_ref,
                   q_ref, k_ref, v_ref, qb_ref, ib_ref, ff_ref, fb_ref, go_ref):
    i = pl.program_id(0)
    shift = mod_ref[0, 0:1, :]
    scale = mod_ref[0, 1:2, :]
    h = (x_ref[...] * (1.0 + scale) + shift).astype(BF16)
    y = _dot(h, w_ref[...]) + b_ref[...]
    offs = np.cumsum((0,) + IN_SIZES)
    q = y[:, offs[0]:offs[1]]
    k = y[:, offs[1]:offs[2]]
    v_ref[...] = y[:, offs[2]:offs[3]]
    qb_ref[...] = y[:, offs[3]:offs[4]]
    ib_ref[...] = y[:, offs[4]:offs[5]]
    ff_ref[...] = y[:, offs[5]:offs[6]]
    fb_ref[...] = y[:, offs[6]:offs[7]]
    go_ref[...] = y[:, offs[7]:offs[8]]
    is_latent = i * ROW_BLOCK >= N_PROMPT

    @pl.when(jnp.logical_not(is_latent))
    def _():
        q_ref[...] = q
        k_ref[...] = k

    @pl.when(is_latent)
    def _():
        cos = cos_ref[...]
        sin = sin_ref[...]
        for c in range(A_Q // LANES):
            q_ref[:, c * LANES:(c + 1) * LANES] = _rope(q[:, c * LANES:(c + 1) * LANES], cos, sin)
        k_ref[...] = _rope(k, cos, sin)


def inproj_even(x, mods, w_bf, b, cos, sin):
    nblk = N_TOK // ROW_BLOCK
    pos_blocks = DEC_SEQ // ROW_BLOCK

    def pos_map(i):
        return (jnp.maximum(i - N_PROMPT // ROW_BLOCK, 0) % pos_blocks, 0)

    row = lambda i: (i, 0)
    widths = (A_Q, A_KV, A_KV, B_QK, B_V, B_QK, B_QK, B_V)
    return pl.pallas_call(
        _inproj_kernel,
        grid=(nblk,),
        in_specs=[
            pl.BlockSpec((ROW_BLOCK, D_MODEL), row),
            pl.BlockSpec((1, 6, D_MODEL), lambda i: (_cond_of_block(i, ROW_BLOCK), 0, 0)),
            pl.BlockSpec((D_MODEL, D_IN_EVEN), lambda i: (0, 0)),
            pl.BlockSpec((1, D_IN_EVEN), lambda i: (0, 0)),
            pl.BlockSpec((ROW_BLOCK, LANES), pos_map),
            pl.BlockSpec((ROW_BLOCK, LANES), pos_map),
        ],
        out_specs=[pl.BlockSpec((ROW_BLOCK, w), row) for w in widths],
        out_shape=[jax.ShapeDtypeStruct((N_TOK, w), F32) for w in widths],
        compiler_params=_cparams(("parallel",)),
        name="inproj_even",
    )(x, mods, w_bf, b.reshape(1, D_IN_EVEN), cos, sin)


def _sink_attend(q, keys, vals, sink, masks):
    scores = []
    for kk, mask in zip(keys, masks):
        s = _dot_nt(q, kk) * SCALE_A
        if mask is not None:
            s = jnp.where(mask, s, MASK_VALUE)
        scores.append(s)
    m = sink
    for s in scores:
        m = jnp.maximum(m, jnp.max(s, axis=-1, keepdims=True))
    denom = jnp.exp(sink - m)
    acc = None
    for s, vv in zip(scores, vals):
        p = jnp.exp(s - m)
        denom = denom + jnp.sum(p, axis=-1, keepdims=True)
        pv = _dot(p.astype(BF16), vv)
        acc = pv if acc is None else acc + pv
    return acc / denom


def _attn_ctx_kernel(sink_ref, q_ref, k_ref, v_ref, o_ref):
    k = k_ref[...].astype(BF16)
    v = v_ref[...].astype(BF16)
    q = q_ref[...].astype(BF16)
    for h in range(N_HEADS_A):
        kv = h // GROUP_A
        qh = q[:, h * HEAD_DIM:(h + 1) * HEAD_DIM]
        kh = k[:, kv * HEAD_DIM:(kv + 1) * HEAD_DIM]
        vh = v[:, kv * HEAD_DIM:(kv + 1) * HEAD_DIM]
        o_ref[:, h * HEAD_DIM:(h + 1) * HEAD_DIM] = _sink_attend(qh, [kh], [vh], sink_ref[h], [None])


def attn_context(sink, q, k, v):
    row = lambda b: (b, 0)
    return pl.pallas_call(
        _attn_ctx_kernel,
        grid=(BATCH,),
        in_specs=[
            pl.BlockSpec(memory_space=pltpu.SMEM),
            pl.BlockSpec((SEQ, A_Q), row),
            pl.BlockSpec((SEQ, A_KV), row),
            pl.BlockSpec((SEQ, A_KV), row),
        ],
        out_specs=pl.BlockSpec((SEQ, A_Q), row),
        out_shape=jax.ShapeDtypeStruct((N_PROMPT, A_Q), F32),
        compiler_params=_cparams(("parallel",)),
        name="attn_context",
    )(sink, q, k, v)


def _attn_lat_kernel(sink_ref, q_ref, kp_ref, kc_ref, kn_ref, vp_ref, vc_ref, vn_ref, ck_ref, cv_ref, o_ref):
    n = pl.program_id(1)
    nb = DEC_SEQ // ATTN_BLOCK
    qi = lax.broadcasted_iota(jnp.int32, (ATTN_BLOCK, ATTN_BLOCK), 0)
    kj = lax.broadcasted_iota(jnp.int32, (ATTN_BLOCK, ATTN_BLOCK), 1)
    mask_prev = jnp.logical_and(kj - qi >= ATTN_BLOCK - WINDOW, n > 0)
    mask_next = jnp.logical_and(kj - qi <= WINDOW - ATTN_BLOCK, n < nb - 1)
    masks = [mask_prev, None, mask_next, None]
    q = q_ref[...].astype(BF16)
    kband = [r[...].astype(BF16) for r in (kp_ref, kc_ref, kn_ref)]
    vband = [r[...].astype(BF16) for r in (vp_ref, vc_ref, vn_ref)]
    ck = ck_ref[0].astype(BF16)
    cv = cv_ref[0].astype(BF16)
    for h in range(N_HEADS_A):
        kv = h // GROUP_A
        sl = slice(kv * HEAD_DIM, (kv + 1) * HEAD_DIM)
        qh = q[:, h * HEAD_DIM:(h + 1) * HEAD_DIM]
        keys = [kb[:, sl] for kb in kband] + [ck[:, sl]]
        vals = [vb[:, sl] for vb in vband] + [cv[:, sl]]
        o_ref[:, h * HEAD_DIM:(h + 1) * HEAD_DIM] = _sink_attend(qh, keys, vals, sink_ref[h], masks)


def attn_latent(sink, q, k, v, cache_k, cache_v):
    nb = DEC_SEQ // ATTN_BLOCK
    base = N_PROMPT // ATTN_BLOCK

    def blk(delta):
        return lambda b, n: (base + b * nb + jnp.clip(n + delta, 0, nb - 1), 0)

    kv_spec = lambda delta: pl.BlockSpec((ATTN_BLOCK, A_KV), blk(delta))
    cache_spec = pl.BlockSpec((1, PAST_LEN, A_KV), lambda b, n: (b, 0, 0))
    return pl.pallas_call(
        _attn_lat_kernel,
        grid=(DEC_BATCH, nb),
        in_specs=[
            pl.BlockSpec(memory_space=pltpu.SMEM),
            pl.BlockSpec((ATTN_BLOCK, A_Q), blk(0)),
            kv_spec(-1), kv_spec(0), kv_spec(1),
            kv_spec(-1), kv_spec(0), kv_spec(1),
            cache_spec, cache_spec,
        ],
        out_specs=pl.BlockSpec((ATTN_BLOCK, A_Q), lambda b, n: (b * nb + n, 0)),
        out_shape=jax.ShapeDtypeStruct((N_SAMPLE, A_Q), F32),
        compiler_params=_cparams(("parallel", "parallel")),
        name="attn_latent",
    )(sink, q, k, k, k, v, v, v, cache_k, cache_v)


def _hgrn_gate(f_pre, lb):
    log_f = jnp.logaddexp(jnp.log(jnp.maximum(lb, LB_FLOOR)), jnp.log1p(-lb) + jax.nn.log_sigmoid(f_pre))
    return 1.0 - jnp.exp(log_f), log_f


def _scan_rows(x, reverse):
    n = x.shape[0]
    row = lax.broadcasted_iota(jnp.int32, x.shape, 0)
    sh = 1
    while sh < n:
        if reverse:
            x = x + jnp.where(row < n - sh, pltpu.roll(x, n - sh, 0), 0.0)
        else:
            x = x + jnp.where(row >= sh, pltpu.roll(x, sh, 0), 0.0)
        sh *= 2
    return x


def _chunk_bcast(x, c, pick, shift):
    nc = HG_BLOCK // c
    rows = x.reshape(nc, c, x.shape[-1])[:, pick:pick + 1, :]
    zero = jnp.zeros((1, 1, x.shape[-1]), x.dtype)
    if shift == -1:
        rows = jnp.concatenate([zero, rows[:-1]], axis=0)
    elif shift == 1:
        rows = jnp.concatenate([rows[1:], zero], axis=0)
    return jnp.broadcast_to(rows, (nc, c, x.shape[-1])).reshape(HG_BLOCK, x.shape[-1])


def _hgrn_block(q, kk, v, log_f, st, reverse):
    cum = _scan_rows(log_f, reverse)
    tot = cum[0:1, :] if reverse else cum[HG_BLOCK - 1:HG_BLOCK, :]
    o = _dot_nt((q * jnp.exp(cum)).astype(BF16), st.astype(BF16))
    kd = (kk * jnp.exp(tot - cum)).astype(BF16)
    v_bf = v.astype(BF16)
    u_t = lax.dot_general(v_bf, kd, (((0,), (0,)), ((), ())), preferred_element_type=F32)
    st_new = st * jnp.exp(tot) + u_t
    ti = lax.broadcasted_iota(jnp.int32, (HG_BLOCK, HG_BLOCK), 0)
    si = lax.broadcasted_iota(jnp.int32, (HG_BLOCK, HG_BLOCK), 1)
    a = jnp.zeros((HG_BLOCK, HG_BLOCK), F32)
    c = HG_SUB
    while c < HG_BLOCK:
        if reverse:
            bound_t = _chunk_bcast(cum, c, 0, 1)
            bound_s = _chunk_bcast(cum, c, 0, 0)
            mask = jnp.logical_and((ti // c) % 2 == 0, si // c == ti // c + 1)
        else:
            bound_t = _chunk_bcast(cum, c, c - 1, -1)
            bound_s = _chunk_bcast(cum, c, c - 1, 0)
            mask = jnp.logical_and((ti // c) % 2 == 1, si // c == ti // c - 1)
        qc = (q * jnp.exp(cum - bound_t)).astype(BF16)
        kc = (kk * jnp.exp(bound_s - cum)).astype(BF16)
        a = a + jnp.where(mask, _dot_nt(qc, kc), 0.0)
        c *= 2
    o = o + _dot(a.astype(BF16), v_bf)
    nsub = HG_BLOCK // HG_SUB
    q3 = q.reshape(nsub, HG_SUB, HGRN_DK)
    k3 = kk.reshape(nsub, HG_SUB, HGRN_DK)
    v3 = v.reshape(nsub, HG_SUB, HGRN_DV)
    cum3 = cum.reshape(nsub, HG_SUB, HGRN_DK)
    t_off = lax.broadcasted_iota(jnp.int32, (1, HG_SUB, 1), 1)
    od = jnp.zeros((nsub, HG_SUB, HGRN_DV), F32)
    for s in range(HG_SUB):
        causal = (t_off <= s) if reverse else (t_off >= s)
        decay = jnp.exp(jnp.where(causal, cum3 - cum3[:, s:s + 1, :], MASK_VALUE))
        score = jnp.sum(q3 * decay * k3[:, s:s + 1, :], axis=-1, keepdims=True)
        od = od + score * v3[:, s:s + 1, :]
    return o + od.reshape(HG_BLOCK, HGRN_DV), st_new


def _hgrn_kernel(*refs, n_blocks, has_state_in, has_state_out):
    refs = list(refs)
    qb_ref, ib_ref, ff_ref, fb_ref, go_ref, lbf_ref, lbb_ref, nw_ref = refs[:8]
    refs = refs[8:]
    s0_ref = refs.pop(0) if has_state_in else None
    o_ref = refs.pop(0)
    so_ref = refs.pop(0) if has_state_out else None
    acc_ref = refs.pop(0)

    def run(reverse):
        f_ref, lb_ref = (fb_ref, lbb_ref) if reverse else (ff_ref, lbf_ref)
        lb = lb_ref[...]
        if has_state_in:
            st0 = s0_ref[0, 0, 1 if reverse else 0, 0].T
        else:
            st0 = jnp.zeros((HGRN_DV, HGRN_DK), F32)

        def body(it, st):
            blk = (n_blocks - 1 - it) if reverse else it
            rows = pl.ds(pl.multiple_of(blk * HG_BLOCK, HG_BLOCK), HG_BLOCK)
            qpre = qb_ref[rows, :]
            q = qpre * jax.nn.sigmoid(qpre)
            kk, log_f = _hgrn_gate(f_ref[rows, :], lb)
            o, st = _hgrn_block(q, kk, ib_ref[rows, :], log_f, st, reverse)
            if reverse:
                acc_ref[rows, :] = acc_ref[rows, :] + o
            else:
                acc_ref[rows, :] = o
            return st

        return lax.fori_loop(0, n_blocks, body, st0)

    st_f = run(False)
    st_b = run(True)
    if has_state_out:
        so_ref[0, 0, 0] = st_f.T
        so_ref[0, 1, 0] = st_b.T
    o = acc_ref[...]
    o = o * lax.rsqrt(jnp.mean(o * o, axis=-1, keepdims=True) + RMS_EPS) * nw_ref[...]
    g = go_ref[...]
    o_ref[...] = o * (g * jax.nn.sigmoid(g))


def hgrn_mixer(qb, ib, ff, fb, go, lb_f, lb_b, norm_w, *, latent, state_in=None, layer_j=0):
    t_len = DEC_SEQ if latent else SEQ
    n_seq = DEC_BATCH if latent else BATCH
    row0 = N_PROMPT // t_len if latent else 0
    tok = pl.BlockSpec((t_len, HGRN_DK), lambda b, h: (row0 + b, h))
    vec = pl.BlockSpec((1, HGRN_DK), lambda b, h: (0, h))
    in_specs = [tok] * 5 + [vec] * 3
    args = [qb, ib, ff, fb, go, lb_f.reshape(1, B_QK), lb_b.reshape(1, B_QK), norm_w.reshape(1, B_V)]
    if latent:
        in_specs.append(pl.BlockSpec((1, 1, 2, 1, HGRN_DK, HGRN_DV), lambda b, h: (b, layer_j, 0, h, 0, 0)))
        args.append(state_in)
    out_specs = [pl.BlockSpec((t_len, HGRN_DV), lambda b, h: (b, h))]
    out_shape = [jax.ShapeDtypeStruct((n_seq * t_len, B_V), F32)]
    if not latent:
        out_specs.append(pl.BlockSpec((1, 2, 1, HGRN_DK, HGRN_DV), lambda b, h: (b, 0, h, 0, 0)))
        out_shape.append(jax.ShapeDtypeStruct((BATCH, 2, N_HEADS_B, HGRN_DK, HGRN_DV), F32))
    kern = functools.partial(_hgrn_kernel, n_blocks=t_len // HG_BLOCK, has_state_in=latent, has_state_out=not latent)
    return pl.pallas_call(
        kern,
        grid=(n_seq, N_HEADS_B),
        in_specs=in_specs,
        out_specs=out_specs,
        out_shape=out_shape,
        scratch_shapes=[pltpu.VMEM((t_len, HGRN_DV), F32)],
        compiler_params=_cparams(("parallel", "parallel")),
        name="hgrn_latent" if latent else "hgrn_context",
    )(*args)


def _outproj_kernel(x_ref, mod_ref, ap_ref, as_ref, rp_ref, rs_ref, w_ref, g_ref, b_ref, o_ref):
    is_latent = pl.program_id(0) * ROW_BLOCK >= N_PROMPT
    attn = jnp.where(is_latent, as_ref[...], ap_ref[...]).astype(BF16)
    rec = jnp.where(is_latent, rs_ref[...], rp_ref[...]).astype(BF16)
    y = _dot(attn, w_ref[0:A_Q, :]) + _dot(rec, w_ref[A_Q:A_Q + B_V, :])
    z = DN_ALPHA * x_ref[...] + mod_ref[0, 2:3, :] * y
    o_ref[...] = _layer_norm(z, g_ref[...], b_ref[...])


def outproj_even(x, mods, attn_p, attn_s, rec_p, rec_s, w_bf, ln_g, ln_b):
    nblk = N_TOK // ROW_BLOCK
    npb = N_PROMPT // ROW_BLOCK
    row = lambda i: (i, 0)
    prow = lambda i: (jnp.minimum(i, npb - 1), 0)
    srow = lambda i: (jnp.maximum(i - npb, 0), 0)
    vec = pl.BlockSpec((1, D_MODEL), lambda i: (0, 0))
    return pl.pallas_call(
        _outproj_kernel,
        grid=(nblk,),
        in_specs=[
            pl.BlockSpec((ROW_BLOCK, D_MODEL), row),
            pl.BlockSpec((1, 6, D_MODEL), lambda i: (_cond_of_block(i, ROW_BLOCK), 0, 0)),
            pl.BlockSpec((ROW_BLOCK, A_Q), prow),
            pl.BlockSpec((ROW_BLOCK, A_Q), srow),
            pl.BlockSpec((ROW_BLOCK, B_V), prow),
            pl.BlockSpec((ROW_BLOCK, B_V), srow),
            pl.BlockSpec((A_Q + B_V, D_MODEL), lambda i: (0, 0)),
            vec, vec,
        ],
        out_specs=pl.BlockSpec((ROW_BLOCK, D_MODEL), row),
        out_shape=jax.ShapeDtypeStruct((N_TOK, D_MODEL), F32),
        compiler_params=_cparams(("parallel",)),
        name="outproj_even",
    )(x, mods, attn_p, attn_s, rec_p, rec_s, w_bf, ln_g.reshape(1, D_MODEL), ln_b.reshape(1, D_MODEL))


def _conv_in_kernel(x_ref, mod_ref, w_ref, b_ref, u_ref):
    h = (x_ref[...] * (1.0 + mod_ref[0, 1:2, :]) + mod_ref[0, 0:1, :]).astype(BF16)
    a = _dot(h, w_ref[:, 0:D_MODEL]) + b_ref[:, 0:D_MODEL]
    gt = _dot(h, w_ref[:, D_MODEL:2 * D_MODEL]) + b_ref[:, D_MODEL:2 * D_MODEL]
    u_ref[...] = a * jax.nn.sigmoid(gt)


def conv_in(x, mods, w_bf, b):
    row = lambda i: (i, 0)
    return pl.pallas_call(
        _conv_in_kernel,
        grid=(N_TOK // ROW_BLOCK,),
        in_specs=[
            pl.BlockSpec((ROW_BLOCK, D_MODEL), row),
            pl.BlockSpec((1, 6, D_MODEL), lambda i: (_cond_of_block(i, ROW_BLOCK), 0, 0)),
            pl.BlockSpec((D_MODEL, 2 * D_MODEL), lambda i: (0, 0)),
            pl.BlockSpec((1, 2 * D_MODEL), lambda i: (0, 0)),
        ],
        out_specs=pl.BlockSpec((ROW_BLOCK, D_MODEL), row),
        out_shape=jax.ShapeDtypeStruct((N_TOK, D_MODEL), F32),
        compiler_params=_cparams(("parallel",)),
        name="conv_in",
    )(x, mods, w_bf, b.reshape(1, 2 * D_MODEL))


CONV_HALO = 16
CONV_LANES = 256


CONV_SHIFT_ROWS = ROW_BLOCK + 2 * CONV_HALO - SUBLANES
CONV_ROW_CHUNK = 128


def _conv_out_kernel(x_ref, mod_ref, up_ref, uc_ref, un_ref, dw_ref, dwb_ref, cg_ref, cb_ref,
                     w_ref, b_ref, g_ref, bb_ref, o_ref, pad_ref, acc_ref, sh_ref):
    i = pl.program_id(0)
    blocks_per_seq = DEC_SEQ // ROW_BLOCK
    j = i - N_PROMPT // ROW_BLOCK
    is_latent = j >= 0
    has_prev = jnp.logical_and(is_latent, j % blocks_per_seq != 0)
    has_next = jnp.logical_and(is_latent, j % blocks_per_seq != blocks_per_seq - 1)
    pad_ref[0:CONV_HALO, :] = jnp.where(has_prev, up_ref[...], 0.0)
    pad_ref[CONV_HALO:CONV_HALO + ROW_BLOCK, :] = uc_ref[...]
    pad_ref[CONV_HALO + ROW_BLOCK:, :] = jnp.where(has_next, un_ref[...], 0.0)
    first = CONV_HALO - CONV_WIDTH // 2
    for c in range(D_MODEL // CONV_LANES):
        lanes = slice(c * CONV_LANES, (c + 1) * CONV_LANES)
        for s in range(SUBLANES):
            sh_ref[s] = pad_ref[s:s + CONV_SHIFT_ROWS, lanes]
        for r0 in range(0, ROW_BLOCK, CONV_ROW_CHUNK):
            acc = jnp.zeros((CONV_ROW_CHUNK, CONV_LANES), F32) + dwb_ref[:, lanes]
            for tap in range(CONV_WIDTH):
                whole, s = divmod(first + tap, SUBLANES)
                rows = slice(r0 + whole * SUBLANES, r0 + whole * SUBLANES + CONV_ROW_CHUNK)
                acc = acc + sh_ref[s, rows, :] * dw_ref[tap:tap + 1, lanes]
            acc_ref[r0:r0 + CONV_ROW_CHUNK, lanes] = acc
    u = _layer_norm(acc_ref[...], cg_ref[...], cb_ref[...])
    u = (u * jax.nn.sigmoid(u)).astype(BF16)
    y = _dot(u, w_ref[...]) + b_ref[...]
    z = DN_ALPHA * x_ref[...] + mod_ref[0, 2:3, :] * y
    o_ref[...] = _layer_norm(z, g_ref[...], bb_ref[...])


def conv_out(x, mods, u, dw, dw_b, cln_g, cln_b, w_bf, b_out, ln_g, ln_b):
    nblk = N_TOK // ROW_BLOCK
    ratio = ROW_BLOCK // CONV_HALO
    nhalo = N_TOK // CONV_HALO
    row = lambda i: (i, 0)
    vec = pl.BlockSpec((1, D_MODEL), lambda i: (0, 0))
    r1 = lambda a: a.reshape(1, D_MODEL)
    return pl.pallas_call(
        _conv_out_kernel,
        grid=(nblk,),
        in_specs=[
            pl.BlockSpec((ROW_BLOCK, D_MODEL), row),
            pl.BlockSpec((1, 6, D_MODEL), lambda i: (_cond_of_block(i, ROW_BLOCK), 0, 0)),
            pl.BlockSpec((CONV_HALO, D_MODEL), lambda i: (jnp.maximum(i * ratio - 1, 0), 0)),
            pl.BlockSpec((ROW_BLOCK, D_MODEL), row),
            pl.BlockSpec((CONV_HALO, D_MODEL), lambda i: (jnp.minimum((i + 1) * ratio, nhalo - 1), 0)),
            pl.BlockSpec((CONV_WIDTH, D_MODEL), lambda i: (0, 0)),
            vec, vec, vec,
            pl.BlockSpec((D_MODEL, D_MODEL), lambda i: (0, 0)),
            vec, vec, vec,
        ],
        out_specs=pl.BlockSpec((ROW_BLOCK, D_MODEL), row),
        out_shape=jax.ShapeDtypeStruct((N_TOK, D_MODEL), F32),
        scratch_shapes=[pltpu.VMEM((ROW_BLOCK + 2 * CONV_HALO, D_MODEL), F32),
                        pltpu.VMEM((ROW_BLOCK, D_MODEL), F32),
                        pltpu.VMEM((SUBLANES, CONV_SHIFT_ROWS, CONV_LANES), F32)],
        compiler_params=_cparams(("parallel",)),
        name="conv_out",
    )(x, mods, u, u, u, dw, r1(dw_b), r1(cln_g), r1(cln_b), w_bf, r1(b_out), r1(ln_g), r1(ln_b))


def _modulate2(x_ref, mod_ref):
    return x_ref[...] * (1.0 + mod_ref[0, 4:5, :]) + mod_ref[0, 3:4, :]


def _route_kernel(x_ref, mod_ref, wr_ref, br_ref, idx_ref, gate_ref, rank_ref, cnt_ref, carry_ref):
    i = pl.program_id(0)

    @pl.when(i == 0)
    def _():
        carry_ref[...] = jnp.zeros_like(carry_ref)

    h = _modulate2(x_ref, mod_ref)
    logits = lax.dot_general(wr_ref[...], h, (((1,), (1,)), ((), ())), precision=HIGHEST,
                             preferred_element_type=F32) + br_ref[...]
    eidx = lax.broadcasted_iota(jnp.int32, logits.shape, 0)
    vals = logits
    sels, tops = [], []
    for k in range(TOP_K):
        m = jnp.max(vals, axis=0, keepdims=True)
        idx = jnp.min(jnp.where(vals == m, eidx, N_EXPERTS), axis=0, keepdims=True)
        sel = eidx == idx
        idx_ref[k:k + 1, :] = idx
        sels.append(sel)
        tops.append(m)
        vals = jnp.where(sel, -jnp.inf, vals)
    exps = [jnp.exp(t - tops[0]) for t in tops]
    total = exps[0] + exps[1] + exps[2] + exps[3]
    for k in range(TOP_K):
        gate_ref[k:k + 1, :] = exps[k] / total
    onehot = jnp.zeros(logits.shape, F32)
    for sel in sels:
        onehot = onehot + sel.astype(F32)
    ta = lax.broadcasted_iota(jnp.int32, (ROW_BLOCK, ROW_BLOCK), 0)
    tb = lax.broadcasted_iota(jnp.int32, (ROW_BLOCK, ROW_BLOCK), 1)
    before = _dot(onehot.astype(BF16), (ta < tb).astype(BF16)) + carry_ref[:, 0:1]
    for k in range(TOP_K):
        rank = jnp.sum(jnp.where(sels[k], before, 0.0), axis=0, keepdims=True)
        rank_ref[k:k + 1, :] = rank.astype(jnp.int32)
    carry = carry_ref[...] + jnp.sum(onehot, axis=1, keepdims=True)
    carry_ref[...] = carry
    cnt_ref[...] = carry.astype(jnp.int32)


def moe_route(x, mods, wr_t, b_r):
    tok = pl.BlockSpec((TOP_K, ROW_BLOCK), lambda i: (0, i))
    return pl.pallas_call(
        _route_kernel,
        grid=(N_TOK // ROW_BLOCK,),
        in_specs=[
            pl.BlockSpec((ROW_BLOCK, D_MODEL), lambda i: (i, 0)),
            pl.BlockSpec((1, 6, D_MODEL), lambda i: (_cond_of_block(i, ROW_BLOCK), 0, 0)),
            pl.BlockSpec((N_EXPERTS, D_MODEL), lambda i: (0, 0)),
            pl.BlockSpec((N_EXPERTS, 1), lambda i: (0, 0)),
        ],
        out_specs=[tok, tok, tok, pl.BlockSpec((N_EXPERTS, LANES), lambda i: (0, 0))],
        out_shape=[jax.ShapeDtypeStruct((TOP_K, N_TOK), jnp.int32),
                   jax.ShapeDtypeStruct((TOP_K, N_TOK), F32),
                   jax.ShapeDtypeStruct((TOP_K, N_TOK), jnp.int32),
                   jax.ShapeDtypeStruct((N_EXPERTS, LANES), jnp.int32)],
        scratch_shapes=[pltpu.VMEM((N_EXPERTS, LANES), F32)],
        compiler_params=_cparams(("arbitrary",)),
        name="moe_route",
    )(x, mods, wr_t, b_r.reshape(N_EXPERTS, 1))


HALF_D = D_MODEL // 2
HI_MASK = 0xFFFF0000


def _pack_bf16_pairs(h):
    lo = lax.bitcast_convert_type(h[:, :HALF_D].astype(BF16).astype(F32), jnp.uint32)
    hi = lax.bitcast_convert_type(h[:, HALF_D:].astype(BF16).astype(F32), jnp.uint32)
    return (lo >> 16) | (hi & jnp.uint32(HI_MASK))


def _unpack_bf16_pairs(w):
    lo = lax.bitcast_convert_type(w << 16, F32).astype(BF16)
    hi = lax.bitcast_convert_type(w & jnp.uint32(HI_MASK), F32).astype(BF16)
    return lo, hi


def _scatter_kernel(dest_ref, zstart_ref, x_ref, mod_ref, xs_ref, pk_ref, zero_ref, sem, zsem):
    i = pl.program_id(0)

    def zero_copy(start):
        start = pl.multiple_of(start, MOE_ROWS)
        return pltpu.make_async_copy(zero_ref, xs_ref.at[pl.ds(start, MOE_ROWS), :], zsem)

    def zero_blocks(fn):
        for e in range(N_EXPERTS):
            @pl.when(zstart_ref[e] >= 0)
            def _():
                fn(zero_copy(jnp.maximum(zstart_ref[e], 0)))

        def unused(b, carry):
            fn(zero_copy(b * MOE_ROWS))
            return carry

        lax.fori_loop(zstart_ref[N_EXPERTS], MOE_BLOCKS, unused, 0)

    @pl.when(i == 0)
    def _():
        zero_ref[...] = jnp.zeros_like(zero_ref)
        zero_blocks(lambda cp: cp.start())
        zero_blocks(lambda cp: cp.wait())

    pk_ref[...] = _pack_bf16_pairs(_modulate2(x_ref, mod_ref))
    base = i * ROW_BLOCK

    for t in range(ROW_BLOCK):
        for k in range(TOP_K):
            row = dest_ref[k * N_TOK + base + t]
            pltpu.make_async_copy(pk_ref.at[pl.ds(t, 1), :], xs_ref.at[pl.ds(row, 1), :], sem).start(priority=k % 2)
    for k in range(TOP_K):
        pltpu.make_async_copy(pk_ref, xs_ref.at[pl.ds(0, ROW_BLOCK), :], sem).wait()


def moe_scatter(dest_flat, zstart, x, mods):
    return pl.pallas_call(
        _scatter_kernel,
        grid_spec=pltpu.PrefetchScalarGridSpec(
            num_scalar_prefetch=2,
            grid=(N_TOK // ROW_BLOCK,),
            in_specs=[
                pl.BlockSpec((ROW_BLOCK, D_MODEL), lambda i, d, z: (i, 0)),
                pl.BlockSpec((1, 6, D_MODEL), lambda i, d, z: (_cond_of_block(i, ROW_BLOCK), 0, 0)),
            ],
            out_specs=pl.BlockSpec(memory_space=pl.ANY),
            scratch_shapes=[pltpu.VMEM((ROW_BLOCK, HALF_D), jnp.uint32),
                            pltpu.VMEM((MOE_ROWS, HALF_D), jnp.uint32),
                            pltpu.SemaphoreType.DMA, pltpu.SemaphoreType.DMA],
        ),
        out_shape=jax.ShapeDtypeStruct((MOE_R, HALF_D), jnp.uint32),
        compiler_params=_cparams(("arbitrary",)),
        name="moe_scatter",
    )(dest_flat, zstart, x, mods)


def _expert_kernel(be_ref, nused_ref, nexte_ref, par_ref, xs_ref, wg_hbm, wu_hbm, wd_hbm, bg_ref, bu_ref, bd_ref,
                   ys_ref, wbuf, wg_bf, wu_bf, wd_bf, wsem, *, layer):
    i = pl.program_id(0)
    new_expert = jnp.logical_or(i == 0, be_ref[i] != be_ref[jnp.maximum(i - 1, 0)])
    slot = par_ref[i]

    def weight_copies(e, s):
        return [pltpu.make_async_copy(w.at[layer, e], wbuf.at[s, j], wsem.at[s])
                for j, w in enumerate((wg_hbm, wu_hbm, wd_hbm))]

    @pl.when(i == 0)
    def _():
        for cp in weight_copies(be_ref[0], 0):
            cp.start()

    @pl.when(jnp.logical_and(new_expert, i < nused_ref[0]))
    def _():
        for cp in weight_copies(be_ref[i], slot):
            cp.wait()

        @pl.when(nexte_ref[i] >= 0)
        def _():
            for cp in weight_copies(jnp.maximum(nexte_ref[i], 0), 1 - slot):
                cp.start()

        wg_bf[...] = wbuf[slot, 0].astype(BF16)
        wu_bf[...] = wbuf[slot, 1].astype(BF16)
        wd_bf[...] = wbuf[slot, 2].astype(BF16)

    @pl.when(i < nused_ref[0])
    def _():
        lo, hi = _unpack_bf16_pairs(xs_ref[...])
        x = jnp.concatenate([lo, hi], axis=1)
        gt = _dot(x, wg_bf[...]) + bg_ref[0, 0]
        up = _dot(x, wu_bf[...]) + bu_ref[0, 0]
        gt = jnp.minimum(gt, SWIGLU_LIMIT)
        up = jnp.clip(up, -SWIGLU_LIMIT, SWIGLU_LIMIT)
        act = ((up + 1.0) * gt * jax.nn.sigmoid(SWIGLU_ALPHA * gt)).astype(BF16)
        ys_ref[...] = _dot(act, wd_bf[...]) + bd_ref[0, 0]

    @pl.when(i >= nused_ref[0])
    def _():
        ys_ref[...] = jnp.zeros_like(ys_ref)


def moe_experts(block_e, n_used, next_e, parity, xs, w_g, w_u, w_d, b_g, b_u, b_d, layer):
    rows = lambda i, be, nu, ne, pa: (i, 0)
    wspec = pl.BlockSpec(memory_space=pl.ANY)
    bspec = pl.BlockSpec((1, 1, 1, D_EXPERT), lambda i, be, nu, ne, pa: (layer, be[i], 0, 0))
    r4 = lambda b: b.reshape(DEPTH, N_EXPERTS, 1, D_EXPERT)
    return pl.pallas_call(
        functools.partial(_expert_kernel, layer=layer),
        grid_spec=pltpu.PrefetchScalarGridSpec(
            num_scalar_prefetch=4,
            grid=(MOE_BLOCKS,),
            in_specs=[pl.BlockSpec((MOE_ROWS, HALF_D), rows), wspec, wspec, wspec, bspec, bspec, bspec],
            out_specs=pl.BlockSpec((MOE_ROWS, D_MODEL), rows),
            scratch_shapes=[pltpu.VMEM((2, 3, D_MODEL, D_EXPERT), F32)]
                           + [pltpu.VMEM((D_MODEL, D_EXPERT), BF16)] * 3
                           + [pltpu.SemaphoreType.DMA((2,))],
        ),
        out_shape=jax.ShapeDtypeStruct((MOE_R, D_MODEL), F32),
        compiler_params=_cparams(("arbitrary",)),
        name="moe_experts",
    )(block_e, n_used, next_e, parity, xs, w_g, w_u, w_d, r4(b_g), r4(b_u), r4(b_d))


def _combine_kernel(dest_ref, x_ref, mod_ref, gate_ref, ys_ref, g_ref, b_ref, o_ref, buf_ref, sem):
    i = pl.program_id(0)
    base = i * COMB_TOK

    for t in range(COMB_TOK):
        for k in range(TOP_K):
            row = dest_ref[k * N_TOK + base + t]
            pltpu.make_async_copy(ys_ref.at[pl.ds(row, 1), :], buf_ref.at[k, pl.ds(t, 1), :], sem).start(priority=k % 2)
    eye = (lax.broadcasted_iota(jnp.int32, (COMB_TOK, COMB_TOK), 0)
           == lax.broadcasted_iota(jnp.int32, (COMB_TOK, COMB_TOK), 1))
    gates = gate_ref[...]
    cols = [jnp.sum(jnp.where(eye, gates[k:k + 1, :], 0.0), axis=1, keepdims=True) for k in range(TOP_K)]
    for k in range(TOP_K):
        pltpu.make_async_copy(ys_ref.at[pl.ds(0, COMB_TOK), :], buf_ref.at[k], sem).wait()
    y = cols[0] * buf_ref[0]
    for k in range(1, TOP_K):
        y = y + cols[k] * buf_ref[k]
    z = DN_ALPHA * x_ref[...] + mod_ref[0, 5:6, :] * y
    o_ref[...] = _layer_norm(z, g_ref[...], b_ref[...])


def moe_combine(dest_flat, x, mods, gates_t, ys, ln_g, ln_b):
    vec = pl.BlockSpec((1, D_MODEL), lambda i, d: (0, 0))
    return pl.pallas_call(
        _combine_kernel,
        grid_spec=pltpu.PrefetchScalarGridSpec(
            num_scalar_prefetch=1,
            grid=(N_TOK // COMB_TOK,),
            in_specs=[
                pl.BlockSpec((COMB_TOK, D_MODEL), lambda i, d: (i, 0)),
                pl.BlockSpec((1, 6, D_MODEL), lambda i, d: (_cond_of_block(i, COMB_TOK), 0, 0)),
                pl.BlockSpec((TOP_K, COMB_TOK), lambda i, d: (0, i)),
                pl.BlockSpec(memory_space=pl.ANY),
                vec, vec,
            ],
            out_specs=pl.BlockSpec((COMB_TOK, D_MODEL), lambda i, d: (i, 0)),
            scratch_shapes=[pltpu.VMEM((TOP_K, COMB_TOK, D_MODEL), F32), pltpu.SemaphoreType.DMA],
        ),
        out_shape=jax.ShapeDtypeStruct((N_TOK, D_MODEL), F32),
        compiler_params=_cparams(("arbitrary",)),
        name="moe_combine",
    )(dest_flat, x, mods, gates_t, ys, ln_g.reshape(1, D_MODEL), ln_b.reshape(1, D_MODEL))


def moe_layer(x, mods, layer, router_w, router_b, w_g, b_g, w_u, b_u, w_d, b_d, ln_g, ln_b):
    idx_t, gates_t, rank_t, counts = moe_route(x, mods, router_w[layer].T, router_b[layer])
    counts = counts[:, 0]
    padded = (counts + MOE_ROWS - 1) // MOE_ROWS * MOE_ROWS
    ends = jnp.cumsum(padded)
    base = ends - padded
    n_used = (ends[-1] // MOE_ROWS).astype(jnp.int32)
    block_start = jnp.arange(MOE_BLOCKS, dtype=jnp.int32) * MOE_ROWS
    block_e = jnp.sum(block_start[:, None] >= ends[None, :], axis=1).astype(jnp.int32)
    block_e = jnp.minimum(block_e, block_e[jnp.maximum(n_used - 1, 0)])
    zstart = jnp.where(padded > 0, ends - MOE_ROWS, -1).astype(jnp.int32)
    zstart = jnp.concatenate([zstart, n_used.reshape(1)])
    onehot = idx_t[:, :, None] == jnp.arange(N_EXPERTS, dtype=jnp.int32)[None, None, :]
    dest = rank_t + jnp.sum(jnp.where(onehot, base[None, None, :], 0), axis=-1)
    dest_flat = dest.reshape(-1).astype(jnp.int32)
    xs = moe_scatter(dest_flat, zstart, x, mods)
    used = padded > 0
    experts = jnp.arange(N_EXPERTS, dtype=jnp.int32)
    later_used = jnp.logical_and(used[None, :], experts[None, :] > experts[:, None])
    next_used = jnp.min(jnp.where(later_used, experts[None, :], N_EXPERTS), axis=1)
    next_used = jnp.where(next_used == N_EXPERTS, -1, next_used).astype(jnp.int32)
    run_parity = ((jnp.cumsum(used.astype(jnp.int32)) - 1) % 2).astype(jnp.int32)
    ys = moe_experts(block_e, n_used.reshape(1), next_used[block_e], run_parity[block_e], xs,
                     w_g, w_u, w_d, b_g, b_u, b_d, layer)
    return moe_combine(dest_flat, x, mods, gates_t, ys, ln_g, ln_b)


def kernel(x_prompt, x_sample, c, cache_k, cache_v, state_hgrn, c_ctx, w_ada, b_ada, ln_g, ln_b,
           w_in_even, b_in_even, attn_sink, hgrn_lb, hgrn_norm, w_out_even,
           conv_w_in, conv_b_in, conv_dw, conv_dw_b, conv_ln_g, conv_ln_b, conv_w_out, conv_b_out,
           router_w, router_b, moe_w_gate, moe_b_gate, moe_w_up, moe_b_up, moe_w_down, moe_b_down):
    x = jnp.concatenate([x_prompt.reshape(N_PROMPT, D_MODEL), x_sample.reshape(N_SAMPLE, D_MODEL)], axis=0)
    cond = jnp.concatenate([c_ctx[None, :], c, jnp.zeros((COND_ROWS - N_COND, D_MODEL), F32)], axis=0)
    mods_all = adaln_all(cond, w_ada, b_ada).reshape(DEPTH, COND_ROWS, 6, D_MODEL)
    lb = jax.nn.softmax(hgrn_lb.astype(F32), axis=1)
    lb = jnp.cumsum(lb, axis=1) - lb[:, :1]
    cos, sin = _rope_tables()
    new_k, new_v, new_s = [], [], []
    for layer in range(DEPTH):
        j = layer // 2
        mods = mods_all[layer]
        if layer % 2 == 0:
            q, k, v, qb, ib, ff, fb, go = inproj_even(x, mods, w_in_even[j].astype(BF16), b_in_even[j], cos, sin)
            new_k.append(k[:N_PROMPT].reshape(BATCH, SEQ, N_KV_A, HEAD_DIM))
            new_v.append(v[:N_PROMPT].reshape(BATCH, SEQ, N_KV_A, HEAD_DIM))
            attn_p = attn_context(attn_sink[j], q, k, v)
            attn_s = attn_latent(attn_sink[j], q, k, v,
                                 cache_k[:, j].reshape(DEC_BATCH, PAST_LEN, A_KV),
                                 cache_v[:, j].reshape(DEC_BATCH, PAST_LEN, A_KV))
            hg = (qb, ib, ff, fb, go, lb[0, j], lb[1, j], hgrn_norm[j])
            rec_p, states = hgrn_mixer(*hg, latent=False)
            rec_s, = hgrn_mixer(*hg, latent=True, state_in=state_hgrn, layer_j=j)
            new_s.append(states)
            x = outproj_even(x, mods, attn_p, attn_s, rec_p, rec_s, w_out_even[j].astype(BF16),
                             ln_g[layer, 0], ln_b[layer, 0])
        else:
            u = conv_in(x, mods, conv_w_in[j].astype(BF16), conv_b_in[j])
            x = conv_out(x, mods, u, conv_dw[j], conv_dw_b[j], conv_ln_g[j], conv_ln_b[j],
                         conv_w_out[j].astype(BF16), conv_b_out[j], ln_g[layer, 0], ln_b[layer, 0])
        x = moe_layer(x, mods, layer, router_w, router_b, moe_w_gate, moe_b_gate, moe_w_up, moe_b_up,
                      moe_w_down, moe_b_down, ln_g[layer, 1], ln_b[layer, 1])
    return (x[:N_PROMPT].reshape(BATCH, SEQ, D_MODEL),
            x[N_PROMPT:].reshape(DEC_BATCH, DEC_SEQ, D_MODEL),
            jnp.stack(new_k, axis=1), jnp.stack(new_v, axis=1), jnp.stack(new_s, axis=1))
```

```python
import functools

import jax
import jax.numpy as jnp
import numpy as np
from jax import lax
from jax.experimental import pallas as pl
from jax.experimental.pallas import tpu as pltpu

D_MODEL = 1024
BATCH = 16
SEQ = 256
DEPTH = 4
DEC_BATCH = 4
DEC_SEQ = 2048
PAST_LEN = 512
GRID_W = 64
N_EVEN = (DEPTH + 1) // 2
N_ODD = DEPTH // 2
HEAD_DIM = 64
N_HEADS_A = 8
N_KV_A = 2
GROUP_A = N_HEADS_A // N_KV_A
WINDOW = 128
ATTN_BLOCK = 128
SCALE_A = HEAD_DIM ** -0.5
ROPE_BASE = 10000.0
ROPE_PAIRS = HEAD_DIM // 4
N_HEADS_B = 4
HGRN_DK = 128
HGRN_DV = 128
CONV_WIDTH = 31
N_EXPERTS = 32
TOP_K = 4
D_EXPERT = D_MODEL
SWIGLU_LIMIT = 7.0
SWIGLU_ALPHA = 1.702
LN_EPS = 1e-5
RMS_EPS = 1e-6
MASK_VALUE = -1e9
LB_FLOOR = 1e-30
DN_ALPHA = (2 * DEPTH) ** 0.25
A_Q = N_HEADS_A * HEAD_DIM
A_KV = N_KV_A * HEAD_DIM
B_QK = N_HEADS_B * HGRN_DK
B_V = N_HEADS_B * HGRN_DV
IN_SIZES = (A_Q, A_KV, A_KV, B_QK, B_V, B_QK, B_QK, B_V)
D_IN_EVEN = sum(IN_SIZES)

N_PROMPT = BATCH * SEQ
N_SAMPLE = DEC_BATCH * DEC_SEQ
N_TOK = N_PROMPT + N_SAMPLE
N_COND = 1 + DEC_BATCH
COND_ROWS = 8

LANES = 128
SUBLANES = 8
VMEM_LIMIT = 56 * 1024 * 1024

ROW_BLOCK = 256
HG_BLOCK = 128
HG_SUB = 8
MOE_ROWS = 256
MOE_BLOCKS = (N_TOK * TOP_K + N_EXPERTS * (MOE_ROWS - 1)) // MOE_ROWS + 1
MOE_R = MOE_BLOCKS * MOE_ROWS
COMB_TOK = 128

F32 = jnp.float32
BF16 = jnp.bfloat16
HIGHEST = lax.Precision.HIGHEST


def _cond_of_block(i, rows):
    start = i * rows
    return jnp.where(start < N_PROMPT, 0, 1 + (start - N_PROMPT) // DEC_SEQ)


def _cparams(sem):
    return pltpu.CompilerParams(dimension_semantics=sem, vmem_limit_bytes=VMEM_LIMIT)


def _layer_norm(z, g, b):
    mu = jnp.mean(z, axis=-1, keepdims=True)
    zc = z - mu
    var = jnp.mean(zc * zc, axis=-1, keepdims=True)
    return zc * lax.rsqrt(var + LN_EPS) * g + b


def _dot(a, b):
    return jnp.dot(a, b, preferred_element_type=F32)


def _dot_nt(a, b):
    return lax.dot_general(a, b, (((1,), (1,)), ((), ())), preferred_element_type=F32)


ADA_TN = 1536


def _adaln_kernel(cond_ref, w_ref, b_ref, o_ref):
    c = cond_ref[...]
    s = c * jax.nn.sigmoid(c)
    o_ref[0] = jnp.dot(s, w_ref[0], precision=HIGHEST, preferred_element_type=F32) + b_ref[0]


def adaln_all(cond, w_ada, b_ada):
    n_out = 6 * D_MODEL
    return pl.pallas_call(
        _adaln_kernel,
        grid=(DEPTH, n_out // ADA_TN),
        in_specs=[
            pl.BlockSpec((COND_ROWS, D_MODEL), lambda l, n: (0, 0)),
            pl.BlockSpec((1, D_MODEL, ADA_TN), lambda l, n: (l, 0, n)),
            pl.BlockSpec((1, 1, ADA_TN), lambda l, n: (l, 0, n)),
        ],
        out_specs=pl.BlockSpec((1, COND_ROWS, ADA_TN), lambda l, n: (l, 0, n)),
        out_shape=jax.ShapeDtypeStruct((DEPTH, COND_ROWS, n_out), F32),
        compiler_params=_cparams(("parallel", "parallel")),
        name="adaln",
    )(cond, w_ada, b_ada.reshape(DEPTH, 1, n_out))


def _rope_tables():
    t = np.arange(DEC_SEQ)
    d = np.arange(LANES) % HEAD_DIM
    axis = d // (2 * ROPE_PAIRS)
    half = (d // ROPE_PAIRS) % 2
    pair = d % ROPE_PAIRS
    pos = jnp.where(axis[None, :] == 0, (t // GRID_W)[:, None], (t % GRID_W)[:, None]).astype(F32)
    inv_freq = ROPE_BASE ** (-jnp.arange(ROPE_PAIRS, dtype=F32) / ROPE_PAIRS)
    ang = pos * inv_freq[pair][None, :]
    sign = jnp.where(half[None, :] == 0, -1.0, 1.0).astype(F32)
    return jnp.cos(ang), jnp.sin(ang) * sign


def _rope(x, cos, sin_signed):
    lane = lax.broadcasted_iota(jnp.int32, x.shape, 1)
    first_half = (lane // ROPE_PAIRS) % 2 == 0
    partner = jnp.where(first_half, pltpu.roll(x, LANES - ROPE_PAIRS, 1), pltpu.roll(x, ROPE_PAIRS, 1))
    return x * cos + partner * sin_signed


def _inproj_kernel(x_ref, mod_ref, w_ref, b_ref, cos_ref, sin_ref,
                   q_ref, k_ref, v_ref, qb_ref, ib_ref, ff_ref, fb_ref, go_ref):
    i = pl.program_id(0)
    shift = mod_ref[0, 0:1, :]
    scale = mod_ref[0, 1:2, :]
    h = (x_ref[...] * (1.0 + scale) + shift).astype(BF16)
    y = _dot(h, w_ref[...]) + b_ref[...]
    offs = np.cumsum((0,) + IN_SIZES)
    q = y[:, offs[0]:offs[1]]
    k = y[:, offs[1]:offs[2]]
    v_ref[...] = y[:, offs[2]:offs[3]]
    qb_ref[...] = y[:, offs[3]:offs[4]]
    ib_ref[...] = y[:, offs[4]:offs[5]]
    ff_ref[...] = y[:, offs[5]:offs[6]]
    fb_ref[...] = y[:, offs[6]:offs[7]]
    go_ref[...] = y[:, offs[7]:offs[8]]
    is_latent = i * ROW_BLOCK >= N_PROMPT

    @pl.when(jnp.logical_not(is_latent))
    def _():
        q_ref[...] = q
        k_ref[...] = k

    @pl.when(is_latent)
    def _():
        cos = cos_ref[...]
        sin = sin_ref[...]
        for c in range(A_Q // LANES):
            q_ref[:, c * LANES:(c + 1) * LANES] = _rope(q[:, c * LANES:(c + 1) * LANES], cos, sin)
        k_ref[...] = _rope(k, cos, sin)


def inproj_even(x, mods, w_bf, b, cos, sin):
    nblk = N_TOK // ROW_BLOCK
    pos_blocks = DEC_SEQ // ROW_BLOCK

    def pos_map(i):
        return (jnp.maximum(i - N_PROMPT // ROW_BLOCK, 0) % pos_blocks, 0)

    row = lambda i: (i, 0)
    widths = (A_Q, A_KV, A_KV, B_QK, B_V, B_QK, B_QK, B_V)
    return pl.pallas_call(
        _inproj_kernel,
        grid=(nblk,),
        in_specs=[
            pl.BlockSpec((ROW_BLOCK, D_MODEL), row),
            pl.BlockSpec((1, 6, D_MODEL), lambda i: (_cond_of_block(i, ROW_BLOCK), 0, 0)),
            pl.BlockSpec((D_MODEL, D_IN_EVEN), lambda i: (0, 0)),
            pl.BlockSpec((1, D_IN_EVEN), lambda i: (0, 0)),
            pl.BlockSpec((ROW_BLOCK, LANES), pos_map),
            pl.BlockSpec((ROW_BLOCK, LANES), pos_map),
        ],
        out_specs=[pl.BlockSpec((ROW_BLOCK, w), row) for w in widths],
        out_shape=[jax.ShapeDtypeStruct((N_TOK, w), F32) for w in widths],
        compiler_params=_cparams(("parallel",)),
        name="inproj_even",
    )(x, mods, w_bf, b.reshape(1, D_IN_EVEN), cos, sin)


def _sink_attend(q, keys, vals, sink, masks):
    scores = []
    for kk, mask in zip(keys, masks):
        s = _dot_nt(q, kk) * SCALE_A
        if mask is not None:
            s = jnp.where(mask, s, MASK_VALUE)
        scores.append(s)
    m = sink
    for s in scores:
        m = jnp.maximum(m, jnp.max(s, axis=-1, keepdims=True))
    denom = jnp.exp(sink - m)
    acc = None
    for s, vv in zip(scores, vals):
        p = jnp.exp(s - m)
        denom = denom + jnp.sum(p, axis=-1, keepdims=True)
        pv = _dot(p.astype(BF16), vv)
        acc = pv if acc is None else acc + pv
    return acc / denom


def _attn_ctx_kernel(sink_ref, q_ref, k_ref, v_ref, o_ref):
    k = k_ref[...].astype(BF16)
    v = v_ref[...].astype(BF16)
    q = q_ref[...].astype(BF16)
    for h in range(N_HEADS_A):
        kv = h // GROUP_A
        qh = q[:, h * HEAD_DIM:(h + 1) * HEAD_DIM]
        kh = k[:, kv * HEAD_DIM:(kv + 1) * HEAD_DIM]
        vh = v[:, kv * HEAD_DIM:(kv + 1) * HEAD_DIM]
        o_ref[:, h * HEAD_DIM:(h + 1) * HEAD_DIM] = _sink_attend(qh, [kh], [vh], sink_ref[h], [None])


def attn_context(sink, q, k, v):
    row = lambda b: (b, 0)
    return pl.pallas_call(
        _attn_ctx_kernel,
        grid=(BATCH,),
        in_specs=[
            pl.BlockSpec(memory_space=pltpu.SMEM),
            pl.BlockSpec((SEQ, A_Q), row),
            pl.BlockSpec((SEQ, A_KV), row),
            pl.BlockSpec((SEQ, A_KV), row),
        ],
        out_specs=pl.BlockSpec((SEQ, A_Q), row),
        out_shape=jax.ShapeDtypeStruct((N_PROMPT, A_Q), F32),
        compiler_params=_cparams(("parallel",)),
        name="attn_context",
    )(sink, q, k, v)


def _attn_lat_kernel(sink_ref, q_ref, kp_ref, kc_ref, kn_ref, vp_ref, vc_ref, vn_ref, ck_ref, cv_ref, o_ref):
    n = pl.program_id(1)
    nb = DEC_SEQ // ATTN_BLOCK
    qi = lax.broadcasted_iota(jnp.int32, (ATTN_BLOCK, ATTN_BLOCK), 0)
    kj = lax.broadcasted_iota(jnp.int32, (ATTN_BLOCK, ATTN_BLOCK), 1)
    mask_prev = jnp.logical_and(kj - qi >= ATTN_BLOCK - WINDOW, n > 0)
    mask_next = jnp.logical_and(kj - qi <= WINDOW - ATTN_BLOCK, n < nb - 1)
    masks = [mask_prev, None, mask_next, None]
    q = q_ref[...].astype(BF16)
    kband = [r[...].astype(BF16) for r in (kp_ref, kc_ref, kn_ref)]
    vband = [r[...].astype(BF16) for r in (vp_ref, vc_ref, vn_ref)]
    ck = ck_ref[0].astype(BF16)
    cv = cv_ref[0].astype(BF16)
    for h in range(N_HEADS_A):
        kv = h // GROUP_A
        sl = slice(kv * HEAD_DIM, (kv + 1) * HEAD_DIM)
        qh = q[:, h * HEAD_DIM:(h + 1) * HEAD_DIM]
        keys = [kb[:, sl] for kb in kband] + [ck[:, sl]]
        vals = [vb[:, sl] for vb in vband] + [cv[:, sl]]
        o_ref[:, h * HEAD_DIM:(h + 1) * HEAD_DIM] = _sink_attend(qh, keys, vals, sink_ref[h], masks)


def attn_latent(sink, q, k, v, cache_k, cache_v):
    nb = DEC_SEQ // ATTN_BLOCK
    base = N_PROMPT // ATTN_BLOCK

    def blk(delta):
        return lambda b, n: (base + b * nb + jnp.clip(n + delta, 0, nb - 1), 0)

    kv_spec = lambda delta: pl.BlockSpec((ATTN_BLOCK, A_KV), blk(delta))
    cache_spec = pl.BlockSpec((1, PAST_LEN, A_KV), lambda b, n: (b, 0, 0))
    return pl.pallas_call(
        _attn_lat_kernel,
        grid=(DEC_BATCH, nb),
        in_specs=[
            pl.BlockSpec(memory_space=pltpu.SMEM),
            pl.BlockSpec((ATTN_BLOCK, A_Q), blk(0)),
            kv_spec(-1), kv_spec(0), kv_spec(1),
            kv_spec(-1), kv_spec(0), kv_spec(1),
            cache_spec, cache_spec,
        ],
        out_specs=pl.BlockSpec((ATTN_BLOCK, A_Q), lambda b, n: (b * nb + n, 0)),
        out_shape=jax.ShapeDtypeStruct((N_SAMPLE, A_Q), F32),
        compiler_params=_cparams(("parallel", "parallel")),
        name="attn_latent",
    )(sink, q, k, k, k, v, v, v, cache_k, cache_v)


def _hgrn_gate(f_pre, lb):
    log_f = jnp.logaddexp(jnp.log(jnp.maximum(lb, LB_FLOOR)), jnp.log1p(-lb) + jax.nn.log_sigmoid(f_pre))
    return 1.0 - jnp.exp(log_f), log_f


def _scan_rows(x, reverse):
    n = x.shape[0]
    row = lax.broadcasted_iota(jnp.int32, x.shape, 0)
    sh = 1
    while sh < n:
        if reverse:
            x = x + jnp.where(row < n - sh, pltpu.roll(x, n - sh, 0), 0.0)
        else:
            x = x + jnp.where(row >= sh, pltpu.roll(x, sh, 0), 0.0)
        sh *= 2
    return x


def _chunk_bcast(x, c, pick, shift):
    nc = HG_BLOCK // c
    rows = x.reshape(nc, c, x.shape[-1])[:, pick:pick + 1, :]
    zero = jnp.zeros((1, 1, x.shape[-1]), x.dtype)
    if shift == -1:
        rows = jnp.concatenate([zero, rows[:-1]], axis=0)
    elif shift == 1:
        rows = jnp.concatenate([rows[1:], zero], axis=0)
    return jnp.broadcast_to(rows, (nc, c, x.shape[-1])).reshape(HG_BLOCK, x.shape[-1])


def _hgrn_block(q, kk, v, log_f, st, reverse):
    cum = _scan_rows(log_f, reverse)
    tot = cum[0:1, :] if reverse else cum[HG_BLOCK - 1:HG_BLOCK, :]
    o = _dot_nt((q * jnp.exp(cum)).astype(BF16), st.astype(BF16))
    kd = (kk * jnp.exp(tot - cum)).astype(BF16)
    v_bf = v.astype(BF16)
    u_t = lax.dot_general(v_bf, kd, (((0,), (0,)), ((), ())), preferred_element_type=F32)
    st_new = st * jnp.exp(tot) + u_t
    ti = lax.broadcasted_iota(jnp.int32, (HG_BLOCK, HG_BLOCK), 0)
    si = lax.broadcasted_iota(jnp.int32, (HG_BLOCK, HG_BLOCK), 1)
    a = jnp.zeros((HG_BLOCK, HG_BLOCK), F32)
    c = HG_SUB
    while c < HG_BLOCK:
        if reverse:
            bound_t = _chunk_bcast(cum, c, 0, 1)
            bound_s = _chunk_bcast(cum, c, 0, 0)
            mask = jnp.logical_and((ti // c) % 2 == 0, si // c == ti // c + 1)
        else:
            bound_t = _chunk_bcast(cum, c, c - 1, -1)
            bound_s = _chunk_bcast(cum, c, c - 1, 0)
            mask = jnp.logical_and((ti // c) % 2 == 1, si // c == ti // c - 1)
        qc = (q * jnp.exp(cum - bound_t)).astype(BF16)
        kc = (kk * jnp.exp(bound_s - cum)).astype(BF16)
        a = a + jnp.where(mask, _dot_nt(qc, kc), 0.0)
        c *= 2
    o = o + _dot(a.astype(BF16), v_bf)
    nsub = HG_BLOCK // HG_SUB
    q3 = q.reshape(nsub, HG_SUB, HGRN_DK)
    k3 = kk.reshape(nsub, HG_SUB, HGRN_DK)
    v3 = v.reshape(nsub, HG_SUB, HGRN_DV)
    cum3 = cum.reshape(nsub, HG_SUB, HGRN_DK)
    t_off = lax.broadcasted_iota(jnp.int32, (1, HG_SUB, 1), 1)
    od = jnp.zeros((nsub, HG_SUB, HGRN_DV), F32)
    for s in range(HG_SUB):
        causal = (t_off <= s) if reverse else (t_off >= s)
        decay = jnp.exp(jnp.where(causal, cum3 - cum3[:, s:s + 1, :], MASK_VALUE))
        score = jnp.sum(q3 * decay * k3[:, s:s + 1, :], axis=-1, keepdims=True)
        od = od + score * v3[:, s:s + 1, :]
    return o + od.reshape(HG_BLOCK, HGRN_DV), st_new


def _hgrn_kernel(*refs, n_blocks, has_state_in, has_state_out):
    refs = list(refs)
    qb_ref, ib_ref, ff_ref, fb_ref, go_ref, lbf_ref, lbb_ref, nw_ref = refs[:8]
    refs = refs[8:]
    s0_ref = refs.pop(0) if has_state_in else None
    o_ref = refs.pop(0)
    so_ref = refs.pop(0) if has_state_out else None
    acc_ref = refs.pop(0)

    def run(reverse):
        f_ref, lb_ref = (fb_ref, lbb_ref) if reverse else (ff_ref, lbf_ref)
        lb = lb_ref[...]
        if has_state_in:
            st0 = s0_ref[0, 0, 1 if reverse else 0, 0].T
        else:
            st0 = jnp.zeros((HGRN_DV, HGRN_DK), F32)

        def body(it, st):
            blk = (n_blocks - 1 - it) if reverse else it
            rows = pl.ds(pl.multiple_of(blk * HG_BLOCK, HG_BLOCK), HG_BLOCK)
            qpre = qb_ref[rows, :]
            q = qpre * jax.nn.sigmoid(qpre)
            kk, log_f = _hgrn_gate(f_ref[rows, :], lb)
            o, st = _hgrn_block(q, kk, ib_ref[rows, :], log_f, st, reverse)
            if reverse:
                acc_ref[rows, :] = acc_ref[rows, :] + o
            else:
                acc_ref[rows, :] = o
            return st

        return lax.fori_loop(0, n_blocks, body, st0)

    st_f = run(False)
    st_b = run(True)
    if has_state_out:
        so_ref[0, 0, 0] = st_f.T
        so_ref[0, 1, 0] = st_b.T
    o = acc_ref[...]
    o = o * lax.rsqrt(jnp.mean(o * o, axis=-1, keepdims=True) + RMS_EPS) * nw_ref[...]
    g = go_ref[...]
    o_ref[...] = o * (g * jax.nn.sigmoid(g))


def hgrn_mixer(qb, ib, ff, fb, go, lb_f, lb_b, norm_w, *, latent, state_in=None, layer_j=0):
    t_len = DEC_SEQ if latent else SEQ
    n_seq = DEC_BATCH if latent else BATCH
    row0 = N_PROMPT // t_len if latent else 0
    tok = pl.BlockSpec((t_len, HGRN_DK), lambda b, h: (row0 + b, h))
    vec = pl.BlockSpec((1, HGRN_DK), lambda b, h: (0, h))
    in_specs = [tok] * 5 + [vec] * 3
    args = [qb, ib, ff, fb, go, lb_f.reshape(1, B_QK), lb_b.reshape(1, B_QK), norm_w.reshape(1, B_V)]
    if latent:
        in_specs.append(pl.BlockSpec((1, 1, 2, 1, HGRN_DK, HGRN_DV), lambda b, h: (b, layer_j, 0, h, 0, 0)))
        args.append(state_in)
    out_specs = [pl.BlockSpec((t_len, HGRN_DV), lambda b, h: (b, h))]
    out_shape = [jax.ShapeDtypeStruct((n_seq * t_len, B_V), F32)]
    if not latent:
        out_specs.append(pl.BlockSpec((1, 2, 1, HGRN_DK, HGRN_DV), lambda b, h: (b, 0, h, 0, 0)))
        out_shape.append(jax.ShapeDtypeStruct((BATCH, 2, N_HEADS_B, HGRN_DK, HGRN_DV), F32))
    kern = functools.partial(_hgrn_kernel, n_blocks=t_len // HG_BLOCK, has_state_in=latent, has_state_out=not latent)
    return pl.pallas_call(
        kern,
        grid=(n_seq, N_HEADS_B),
        in_specs=in_specs,
        out_specs=out_specs,
        out_shape=out_shape,
        scratch_shapes=[pltpu.VMEM((t_len, HGRN_DV), F32)],
        compiler_params=_cparams(("parallel", "parallel")),
        name="hgrn_latent" if latent else "hgrn_context",
    )(*args)


def _outproj_kernel(x_ref, mod_ref, ap_ref, as_ref, rp_ref, rs_ref, w_ref, g_ref, b_ref, o_ref):
    is_latent = pl.program_id(0) * ROW_BLOCK >= N_PROMPT
    attn = jnp.where(is_latent, as_ref[...], ap_ref[...]).astype(BF16)
    rec = jnp.where(is_latent, rs_ref[...], rp_ref[...]).astype(BF16)
    y = _dot(attn, w_ref[0:A_Q, :]) + _dot(rec, w_ref[A_Q:A_Q + B_V, :])
    z = DN_ALPHA * x_ref[...] + mod_ref[0, 2:3, :] * y
    o_ref[...] = _layer_norm(z, g_ref[...], b_ref[...])


def outproj_even(x, mods, attn_p, attn_s, rec_p, rec_s, w_bf, ln_g, ln_b):
    nblk = N_TOK // ROW_BLOCK
    npb = N_PROMPT // ROW_BLOCK
    row = lambda i: (i, 0)
    prow = lambda i: (jnp.minimum(i, npb - 1), 0)
    srow = lambda i: (jnp.maximum(i - npb, 0), 0)
    vec = pl.BlockSpec((1, D_MODEL), lambda i: (0, 0))
    return pl.pallas_call(
        _outproj_kernel,
        grid=(nblk,),
        in_specs=[
            pl.BlockSpec((ROW_BLOCK, D_MODEL), row),
            pl.BlockSpec((1, 6, D_MODEL), lambda i: (_cond_of_block(i, ROW_BLOCK), 0, 0)),
            pl.BlockSpec((ROW_BLOCK, A_Q), prow),
            pl.BlockSpec((ROW_BLOCK, A_Q), srow),
            pl.BlockSpec((ROW_BLOCK, B_V), prow),
            pl.BlockSpec((ROW_BLOCK, B_V), srow),
            pl.BlockSpec((A_Q + B_V, D_MODEL), lambda i: (0, 0)),
            vec, vec,
        ],
        out_specs=pl.BlockSpec((ROW_BLOCK, D_MODEL), row),
        out_shape=jax.ShapeDtypeStruct((N_TOK, D_MODEL), F32),
        compiler_params=_cparams(("parallel",)),
        name="outproj_even",
    )(x, mods, attn_p, attn_s, rec_p, rec_s, w_bf, ln_g.reshape(1, D_MODEL), ln_b.reshape(1, D_MODEL))


def _conv_in_kernel(x_ref, mod_ref, w_ref, b_ref, u_ref):
    h = (x_ref[...] * (1.0 + mod_ref[0, 1:2, :]) + mod_ref[0, 0:1, :]).astype(BF16)
    a = _dot(h, w_ref[:, 0:D_MODEL]) + b_ref[:, 0:D_MODEL]
    gt = _dot(h, w_ref[:, D_MODEL:2 * D_MODEL]) + b_ref[:, D_MODEL:2 * D_MODEL]
    u_ref[...] = a * jax.nn.sigmoid(gt)


def conv_in(x, mods, w_bf, b):
    row = lambda i: (i, 0)
    return pl.pallas_call(
        _conv_in_kernel,
        grid=(N_TOK // ROW_BLOCK,),
        in_specs=[
            pl.BlockSpec((ROW_BLOCK, D_MODEL), row),
            pl.BlockSpec((1, 6, D_MODEL), lambda i: (_cond_of_block(i, ROW_BLOCK), 0, 0)),
            pl.BlockSpec((D_MODEL, 2 * D_MODEL), lambda i: (0, 0)),
            pl.BlockSpec((1, 2 * D_MODEL), lambda i: (0, 0)),
        ],
        out_specs=pl.BlockSpec((ROW_BLOCK, D_MODEL), row),
        out_shape=jax.ShapeDtypeStruct((N_TOK, D_MODEL), F32),
        compiler_params=_cparams(("parallel",)),
        name="conv_in",
    )(x, mods, w_bf, b.reshape(1, 2 * D_MODEL))


CONV_HALO = 16
CONV_LANES = 256


CONV_SHIFT_ROWS = ROW_BLOCK + 2 * CONV_HALO - SUBLANES
CONV_ROW_CHUNK = 128


def _conv_out_kernel(x_ref, mod_ref, up_ref, uc_ref, un_ref, dw_ref, dwb_ref, cg_ref, cb_ref,
                     w_ref, b_ref, g_ref, bb_ref, o_ref, pad_ref, acc_ref, sh_ref):
    i = pl.program_id(0)
    blocks_per_seq = DEC_SEQ // ROW_BLOCK
    j = i - N_PROMPT // ROW_BLOCK
    is_latent = j >= 0
    has_prev = jnp.logical_and(is_latent, j % blocks_per_seq != 0)
    has_next = jnp.logical_and(is_latent, j % blocks_per_seq != blocks_per_seq - 1)
    pad_ref[0:CONV_HALO, :] = jnp.where(has_prev, up_ref[...], 0.0)
    pad_ref[CONV_HALO:CONV_HALO + ROW_BLOCK, :] = uc_ref[...]
    pad_ref[CONV_HALO + ROW_BLOCK:, :] = jnp.where(has_next, un_ref[...], 0.0)
    first = CONV_HALO - CONV_WIDTH // 2
    for c in range(D_MODEL // CONV_LANES):
        lanes = slice(c * CONV_LANES, (c + 1) * CONV_LANES)
        for s in range(SUBLANES):
            sh_ref[s] = pad_ref[s:s + CONV_SHIFT_ROWS, lanes]
        for r0 in range(0, ROW_BLOCK, CONV_ROW_CHUNK):
            acc = jnp.zeros((CONV_ROW_CHUNK, CONV_LANES), F32) + dwb_ref[:, lanes]
            for tap in range(CONV_WIDTH):
                whole, s = divmod(first + tap, SUBLANES)
                rows = slice(r0 + whole * SUBLANES, r0 + whole * SUBLANES + CONV_ROW_CHUNK)
                acc = acc + sh_ref[s, rows, :] * dw_ref[tap:tap + 1, lanes]
            acc_ref[r0:r0 + CONV_ROW_CHUNK, lanes] = acc
    u = _layer_norm(acc_ref[...], cg_ref[...], cb_ref[...])
    u = (u * jax.nn.sigmoid(u)).astype(BF16)
    y = _dot(u, w_ref[...]) + b_ref[...]
    z = DN_ALPHA * x_ref[...] + mod_ref[0, 2:3, :] * y
    o_ref[...] = _layer_norm(z, g_ref[...], bb_ref[...])


def conv_out(x, mods, u, dw, dw_b, cln_g, cln_b, w_bf, b_out, ln_g, ln_b):
    nblk = N_TOK // ROW_BLOCK
    ratio = ROW_BLOCK // CONV_HALO
    nhalo = N_TOK // CONV_HALO
    row = lambda i: (i, 0)
    vec = pl.BlockSpec((1, D_MODEL), lambda i: (0, 0))
    r1 = lambda a: a.reshape(1, D_MODEL)
    return pl.pallas_call(
        _conv_out_kernel,
        grid=(nblk,),
        in_specs=[
            pl.BlockSpec((ROW_BLOCK, D_MODEL), row),
            pl.BlockSpec((1, 6, D_MODEL), lambda i: (_cond_of_block(i, ROW_BLOCK), 0, 0)),
            pl.BlockSpec((CONV_HALO, D_MODEL), lambda i: (jnp.maximum(i * ratio - 1, 0), 0)),
            pl.BlockSpec((ROW_BLOCK, D_MODEL), row),
            pl.BlockSpec((CONV_HALO, D_MODEL), lambda i: (jnp.minimum((i + 1) * ratio, nhalo - 1), 0)),
            pl.BlockSpec((CONV_WIDTH, D_MODEL), lambda i: (0, 0)),
            vec, vec, vec,
            pl.BlockSpec((D_MODEL, D_MODEL), lambda i: (0, 0)),
            vec, vec, vec,
        ],
        out_specs=pl.BlockSpec((ROW_BLOCK, D_MODEL), row),
        out_shape=jax.ShapeDtypeStruct((N_TOK, D_MODEL), F32),
        scratch_shapes=[pltpu.VMEM((ROW_BLOCK + 2 * CONV_HALO, D_MODEL), F32),
                        pltpu.VMEM((ROW_BLOCK, D_MODEL), F32),
                        pltpu.VMEM((SUBLANES, CONV_SHIFT_ROWS, CONV_LANES), F32)],
        compiler_params=_cparams(("parallel",)),
        name="conv_out",
    )(x, mods, u, u, u, dw, r1(dw_b), r1(cln_g), r1(cln_b), w_bf, r1(b_out), r1(ln_g), r1(ln_b))


def _modulate2(x_ref, mod_ref):
    return x_ref[...] * (1.0 + mod_ref[0, 4:5, :]) + mod_ref[0, 3:4, :]


def _route_kernel(x_ref, mod_ref, wr_ref, br_ref, idx_ref, gate_ref, rank_ref, cnt_ref, carry_ref):
    i = pl.program_id(0)

    @pl.when(i == 0)
    def _():
        carry_ref[...] = jnp.zeros_like(carry_ref)

    h = _modulate2(x_ref, mod_ref)
    logits = lax.dot_general(wr_ref[...], h, (((1,), (1,)), ((), ())), precision=HIGHEST,
                             preferred_element_type=F32) + br_ref[...]
    eidx = lax.broadcasted_iota(jnp.int32, logits.shape, 0)
    vals = logits
    sels, tops = [], []
    for k in range(TOP_K):
        m = jnp.max(vals, axis=0, keepdims=True)
        idx = jnp.min(jnp.where(vals == m, eidx, N_EXPERTS), axis=0, keepdims=True)
        sel = eidx == idx
        idx_ref[k:k + 1, :] = idx
        sels.append(sel)
        tops.append(m)
        vals = jnp.where(sel, -jnp.inf, vals)
    exps = [jnp.exp(t - tops[0]) for t in tops]
    total = exps[0] + exps[1] + exps[2] + exps[3]
    for k in range(TOP_K):
        gate_ref[k:k + 1, :] = exps[k] / total
    onehot = jnp.zeros(logits.shape, F32)
    for sel in sels:
        onehot = onehot + sel.astype(F32)
    ta = lax.broadcasted_iota(jnp.int32, (ROW_BLOCK, ROW_BLOCK), 0)
    tb = lax.broadcasted_iota(jnp.int32, (ROW_BLOCK, ROW_BLOCK), 1)
    before = _dot(onehot.astype(BF16), (ta < tb).astype(BF16)) + carry_ref[:, 0:1]
    for k in range(TOP_K):
        rank = jnp.sum(jnp.where(sels[k], before, 0.0), axis=0, keepdims=True)
        rank_ref[k:k + 1, :] = rank.astype(jnp.int32)
    carry = carry_ref[...] + jnp.sum(onehot, axis=1, keepdims=True)
    carry_ref[...] = carry
    cnt_ref[...] = carry.astype(jnp.int32)


def moe_route(x, mods, wr_t, b_r):
    tok = pl.BlockSpec((TOP_K, ROW_BLOCK), lambda i: (0, i))
    return pl.pallas_call(
        _route_kernel,
        grid=(N_TOK // ROW_BLOCK,),
        in_specs=[
            pl.BlockSpec((ROW_BLOCK, D_MODEL), lambda i: (i, 0)),
            pl.BlockSpec((1, 6, D_MODEL), lambda i: (_cond_of_block(i, ROW_BLOCK), 0, 0)),
            pl.BlockSpec((N_EXPERTS, D_MODEL), lambda i: (0, 0)),
            pl.BlockSpec((N_EXPERTS, 1), lambda i: (0, 0)),
        ],
        out_specs=[tok, tok, tok, pl.BlockSpec((N_EXPERTS, LANES), lambda i: (0, 0))],
        out_shape=[jax.ShapeDtypeStruct((TOP_K, N_TOK), jnp.int32),
                   jax.ShapeDtypeStruct((TOP_K, N_TOK), F32),
                   jax.ShapeDtypeStruct((TOP_K, N_TOK), jnp.int32),
                   jax.ShapeDtypeStruct((N_EXPERTS, LANES), jnp.int32)],
        scratch_shapes=[pltpu.VMEM((N_EXPERTS, LANES), F32)],
        compiler_params=_cparams(("arbitrary",)),
        name="moe_route",
    )(x, mods, wr_t, b_r.reshape(N_EXPERTS, 1))


HALF_D = D_MODEL // 2
HI_MASK = 0xFFFF0000


def _pack_bf16_pairs(h):
    half = h.shape[1] // 2
    lo = lax.bitcast_convert_type(h[:, :half].astype(BF16).astype(F32), jnp.uint32)
    hi = lax.bitcast_convert_type(h[:, half:].astype(BF16).astype(F32), jnp.uint32)
    return (lo >> 16) | (hi & jnp.uint32(HI_MASK))


def _unpack_bf16_pairs(w):
    lo = lax.bitcast_convert_type(w << 16, F32).astype(BF16)
    hi = lax.bitcast_convert_type(w & jnp.uint32(HI_MASK), F32).astype(BF16)
    return lo, hi


def _scatter_kernel(dest_ref, zstart_ref, x_ref, mod_ref, xs_ref, pk_ref, zero_ref, sem, zsem):
    i = pl.program_id(0)

    def zero_copy(start):
        start = pl.multiple_of(start, MOE_ROWS)
        return pltpu.make_async_copy(zero_ref, xs_ref.at[pl.ds(start, MOE_ROWS), :], zsem)

    def zero_blocks(fn):
        for e in range(N_EXPERTS):
            @pl.when(zstart_ref[e] >= 0)
            def _():
                fn(zero_copy(jnp.maximum(zstart_ref[e], 0)))

        def unused(b, carry):
            fn(zero_copy(b * MOE_ROWS))
            return carry

        lax.fori_loop(zstart_ref[N_EXPERTS], MOE_BLOCKS, unused, 0)

    @pl.when(i == 0)
    def _():
        zero_ref[...] = jnp.zeros_like(zero_ref)
        zero_blocks(lambda cp: cp.start())
        zero_blocks(lambda cp: cp.wait())

    pk_ref[...] = _pack_bf16_pairs(_modulate2(x_ref, mod_ref))
    base = i * ROW_BLOCK

    for t in range(ROW_BLOCK):
        for k in range(TOP_K):
            row = dest_ref[k * N_TOK + base + t]
            pltpu.make_async_copy(pk_ref.at[pl.ds(t, 1), :], xs_ref.at[pl.ds(row, 1), :], sem).start(priority=k % 2)
    for k in range(TOP_K):
        pltpu.make_async_copy(pk_ref, xs_ref.at[pl.ds(0, ROW_BLOCK), :], sem).wait()


def moe_scatter(dest_flat, zstart, x, mods):
    return pl.pallas_call(
        _scatter_kernel,
        grid_spec=pltpu.PrefetchScalarGridSpec(
            num_scalar_prefetch=2,
            grid=(N_TOK // ROW_BLOCK,),
            in_specs=[
                pl.BlockSpec((ROW_BLOCK, D_MODEL), lambda i, d, z: (i, 0)),
                pl.BlockSpec((1, 6, D_MODEL), lambda i, d, z: (_cond_of_block(i, ROW_BLOCK), 0, 0)),
            ],
            out_specs=pl.BlockSpec(memory_space=pl.ANY),
            scratch_shapes=[pltpu.VMEM((ROW_BLOCK, HALF_D), jnp.uint32),
                            pltpu.VMEM((MOE_ROWS, HALF_D), jnp.uint32),
                            pltpu.SemaphoreType.DMA, pltpu.SemaphoreType.DMA],
        ),
        out_shape=jax.ShapeDtypeStruct((MOE_R, HALF_D), jnp.uint32),
        compiler_params=_cparams(("arbitrary",)),
        name="moe_scatter",
    )(dest_flat, zstart, x, mods)


N_CHUNK = 256


def _expert_kernel(be_ref, nused_ref, nexte_ref, par_ref, xs_ref, wg_hbm, wu_hbm, wd_hbm, bg_ref, bu_ref, bd_ref,
                   ys_ref, wbuf, act_ref, wg_bf, wu_bf, wd_bf, wsem, *, layer):
    i = pl.program_id(0)
    new_expert = jnp.logical_or(i == 0, be_ref[i] != be_ref[jnp.maximum(i - 1, 0)])
    slot = par_ref[i]

    def weight_copies(e, s):
        return [pltpu.make_async_copy(w.at[layer, e], wbuf.at[s, j], wsem.at[s])
                for j, w in enumerate((wg_hbm, wu_hbm, wd_hbm))]

    @pl.when(i == 0)
    def _():
        for cp in weight_copies(be_ref[0], 0):
            cp.start()

    @pl.when(jnp.logical_and(new_expert, i < nused_ref[0]))
    def _():
        for cp in weight_copies(be_ref[i], slot):
            cp.wait()

        @pl.when(nexte_ref[i] >= 0)
        def _():
            for cp in weight_copies(jnp.maximum(nexte_ref[i], 0), 1 - slot):
                cp.start()

        wg_bf[...] = wbuf[slot, 0].astype(BF16)
        wu_bf[...] = wbuf[slot, 1].astype(BF16)
        wd_bf[...] = wbuf[slot, 2].astype(BF16)

    @pl.when(i < nused_ref[0])
    def _():
        lo, hi = _unpack_bf16_pairs(xs_ref[...])
        x = jnp.concatenate([lo, hi], axis=1)
        for n in range(D_EXPERT // N_CHUNK):
            cols = slice(n * N_CHUNK, (n + 1) * N_CHUNK)
            gt = _dot(x, wg_bf[:, cols]) + bg_ref[0, 0, :, cols]
            up = _dot(x, wu_bf[:, cols]) + bu_ref[0, 0, :, cols]
            gt = jnp.minimum(gt, SWIGLU_LIMIT)
            up = jnp.clip(up, -SWIGLU_LIMIT, SWIGLU_LIMIT)
            act_ref[:, cols] = ((up + 1.0) * gt * jax.nn.sigmoid(SWIGLU_ALPHA * gt)).astype(BF16)
        act = act_ref[...]
        for n in range(HALF_D // N_CHUNK):
            cols = slice(n * N_CHUNK, (n + 1) * N_CHUNK)
            cols_hi = slice(HALF_D + n * N_CHUNK, HALF_D + (n + 1) * N_CHUNK)
            y_lo = _dot(act, wd_bf[:, cols]) + bd_ref[0, 0, :, cols]
            y_hi = _dot(act, wd_bf[:, cols_hi]) + bd_ref[0, 0, :, cols_hi]
            ys_ref[:, cols] = _pack_bf16_pairs(jnp.concatenate([y_lo, y_hi], axis=1))

    @pl.when(i >= nused_ref[0])
    def _():
        ys_ref[...] = jnp.zeros_like(ys_ref)


def moe_experts(block_e, n_used, next_e, parity, xs, w_g, w_u, w_d, b_g, b_u, b_d, layer):
    rows = lambda i, be, nu, ne, pa: (i, 0)
    wspec = pl.BlockSpec(memory_space=pl.ANY)
    bspec = pl.BlockSpec((1, 1, 1, D_EXPERT), lambda i, be, nu, ne, pa: (layer, be[i], 0, 0))
    r4 = lambda b: b.reshape(DEPTH, N_EXPERTS, 1, D_EXPERT)
    return pl.pallas_call(
        functools.partial(_expert_kernel, layer=layer),
        grid_spec=pltpu.PrefetchScalarGridSpec(
            num_scalar_prefetch=4,
            grid=(MOE_BLOCKS,),
            in_specs=[pl.BlockSpec((MOE_ROWS, HALF_D), rows), wspec, wspec, wspec, bspec, bspec, bspec],
            out_specs=pl.BlockSpec((MOE_ROWS, HALF_D), rows),
            scratch_shapes=[pltpu.VMEM((2, 3, D_MODEL, D_EXPERT), F32),
                            pltpu.VMEM((MOE_ROWS, D_EXPERT), BF16)]
                           + [pltpu.VMEM((D_MODEL, D_EXPERT), BF16)] * 3
                           + [pltpu.SemaphoreType.DMA((2,))],
        ),
        out_shape=jax.ShapeDtypeStruct((MOE_R, HALF_D), jnp.uint32),
        compiler_params=_cparams(("arbitrary",)),
        name="moe_experts",
    )(block_e, n_used, next_e, parity, xs, w_g, w_u, w_d, r4(b_g), r4(b_u), r4(b_d))


def _combine_kernel(dest_ref, x_ref, mod_ref, gate_ref, ys_ref, g_ref, b_ref, o_ref, buf_ref, sem):
    i = pl.program_id(0)
    base = i * COMB_TOK

    for t in range(COMB_TOK):
        for k in range(TOP_K):
            row = dest_ref[k * N_TOK + base + t]
            pltpu.make_async_copy(ys_ref.at[pl.ds(row, 1), :], buf_ref.at[k, pl.ds(t, 1), :], sem).start(priority=k % 2)
    eye = (lax.broadcasted_iota(jnp.int32, (COMB_TOK, COMB_TOK), 0)
           == lax.broadcasted_iota(jnp.int32, (COMB_TOK, COMB_TOK), 1))
    gates = gate_ref[...]
    cols = [jnp.sum(jnp.where(eye, gates[k:k + 1, :], 0.0), axis=1, keepdims=True) for k in range(TOP_K)]
    for k in range(TOP_K):
        pltpu.make_async_copy(ys_ref.at[pl.ds(0, COMB_TOK), :], buf_ref.at[k], sem).wait()
    y_lo = y_hi = None
    for k in range(TOP_K):
        w = buf_ref[k]
        lo = cols[k] * lax.bitcast_convert_type(w << 16, F32)
        hi = cols[k] * lax.bitcast_convert_type(w & jnp.uint32(HI_MASK), F32)
        y_lo = lo if y_lo is None else y_lo + lo
        y_hi = hi if y_hi is None else y_hi + hi
    y = jnp.concatenate([y_lo, y_hi], axis=1)
    z = DN_ALPHA * x_ref[...] + mod_ref[0, 5:6, :] * y
    o_ref[...] = _layer_norm(z, g_ref[...], b_ref[...])


def moe_combine(dest_flat, x, mods, gates_t, ys, ln_g, ln_b):
    vec = pl.BlockSpec((1, D_MODEL), lambda i, d: (0, 0))
    return pl.pallas_call(
        _combine_kernel,
        grid_spec=pltpu.PrefetchScalarGridSpec(
            num_scalar_prefetch=1,
            grid=(N_TOK // COMB_TOK,),
            in_specs=[
                pl.BlockSpec((COMB_TOK, D_MODEL), lambda i, d: (i, 0)),
                pl.BlockSpec((1, 6, D_MODEL), lambda i, d: (_cond_of_block(i, COMB_TOK), 0, 0)),
                pl.BlockSpec((TOP_K, COMB_TOK), lambda i, d: (0, i)),
                pl.BlockSpec(memory_space=pl.ANY),
                vec, vec,
            ],
            out_specs=pl.BlockSpec((COMB_TOK, D_MODEL), lambda i, d: (i, 0)),
            scratch_shapes=[pltpu.VMEM((TOP_K, COMB_TOK, HALF_D), jnp.uint32), pltpu.SemaphoreType.DMA],
        ),
        out_shape=jax.ShapeDtypeStruct((N_TOK, D_MODEL), F32),
        compiler_params=_cparams(("arbitrary",)),
        name="moe_combine",
    )(dest_flat, x, mods, gates_t, ys, ln_g.reshape(1, D_MODEL), ln_b.reshape(1, D_MODEL))


def moe_layer(x, mods, layer, router_w, router_b, w_g, b_g, w_u, b_u, w_d, b_d, ln_g, ln_b):
    idx_t, gates_t, rank_t, counts = moe_route(x, mods, router_w[layer].T, router_b[layer])
    counts = counts[:, 0]
    padded = (counts + MOE_ROWS - 1) // MOE_ROWS * MOE_ROWS
    ends = jnp.cumsum(padded)
    base = ends - padded
    n_used = (ends[-1] // MOE_ROWS).astype(jnp.int32)
    block_start = jnp.arange(MOE_BLOCKS, dtype=jnp.int32) * MOE_ROWS
    block_e = jnp.sum(block_start[:, None] >= ends[None, :], axis=1).astype(jnp.int32)
    used = padded > 0
    experts = jnp.arange(N_EXPERTS, dtype=jnp.int32)
    block_e = jnp.minimum(block_e, jnp.max(jnp.where(used, experts, 0)))
    zstart = jnp.where(padded > 0, ends - MOE_ROWS, -1).astype(jnp.int32)
    zstart = jnp.concatenate([zstart, n_used.reshape(1)])
    onehot = idx_t[:, :, None] == jnp.arange(N_EXPERTS, dtype=jnp.int32)[None, None, :]
    dest = rank_t + jnp.sum(jnp.where(onehot, base[None, None, :], 0), axis=-1)
    dest_flat = dest.reshape(-1).astype(jnp.int32)
    xs = moe_scatter(dest_flat, zstart, x, mods)
    later_used = jnp.logical_and(used[None, :], experts[None, :] > experts[:, None])
    next_used = jnp.min(jnp.where(later_used, experts[None, :], N_EXPERTS), axis=1)
    next_used = jnp.where(next_used == N_EXPERTS, -1, next_used).astype(jnp.int32)
    run_parity = ((jnp.cumsum(used.astype(jnp.int32)) - 1) % 2).astype(jnp.int32)
    of_block = block_e[:, None] == experts[None, :]
    block_next = jnp.sum(jnp.where(of_block, next_used[None, :], 0), axis=1).astype(jnp.int32)
    block_parity = jnp.sum(jnp.where(of_block, run_parity[None, :], 0), axis=1).astype(jnp.int32)
    ys = moe_experts(block_e, n_used.reshape(1), block_next, block_parity, xs,
                     w_g, w_u, w_d, b_g, b_u, b_d, layer)
    return moe_combine(dest_flat, x, mods, gates_t, ys, ln_g, ln_b)


def kernel(x_prompt, x_sample, c, cache_k, cache_v, state_hgrn, c_ctx, w_ada, b_ada, ln_g, ln_b,
           w_in_even, b_in_even, attn_sink, hgrn_lb, hgrn_norm, w_out_even,
           conv_w_in, conv_b_in, conv_dw, conv_dw_b, conv_ln_g, conv_ln_b, conv_w_out, conv_b_out,
           router_w, router_b, moe_w_gate, moe_b_gate, moe_w_up, moe_b_up, moe_w_down, moe_b_down):
    x = jnp.concatenate([x_prompt.reshape(N_PROMPT, D_MODEL), x_sample.reshape(N_SAMPLE, D_MODEL)], axis=0)
    cond = jnp.concatenate([c_ctx[None, :], c, jnp.zeros((COND_ROWS - N_COND, D_MODEL), F32)], axis=0)
    mods_all = adaln_all(cond, w_ada, b_ada).reshape(DEPTH, COND_ROWS, 6, D_MODEL)
    lb = jax.nn.softmax(hgrn_lb.astype(F32), axis=1)
    lb = jnp.cumsum(lb, axis=1) - lb[:, :1]
    cos, sin = _rope_tables()
    new_k, new_v, new_s = [], [], []
    for layer in range(DEPTH):
        j = layer // 2
        mods = mods_all[layer]
        if layer % 2 == 0:
            q, k, v, qb, ib, ff, fb, go = inproj_even(x, mods, w_in_even[j].astype(BF16), b_in_even[j], cos, sin)
            new_k.append(k[:N_PROMPT].reshape(BATCH, SEQ, N_KV_A, HEAD_DIM))
            new_v.append(v[:N_PROMPT].reshape(BATCH, SEQ, N_KV_A, HEAD_DIM))
            attn_p = attn_context(attn_sink[j], q, k, v)
            attn_s = attn_latent(attn_sink[j], q, k, v,
                                 cache_k[:, j].reshape(DEC_BATCH, PAST_LEN, A_KV),
                                 cache_v[:, j].reshape(DEC_BATCH, PAST_LEN, A_KV))
            hg = (qb, ib, ff, fb, go, lb[0, j], lb[1, j], hgrn_norm[j])
            rec_p, states = hgrn_mixer(*hg, latent=False)
            rec_s, = hgrn_mixer(*hg, latent=True, state_in=state_hgrn, layer_j=j)
            new_s.append(states)
            x = outproj_even(x, mods, attn_p, attn_s, rec_p, rec_s, w_out_even[j].astype(BF16),
                             ln_g[layer, 0], ln_b[layer, 0])
        else:
            u = conv_in(x, mods, conv_w_in[j].astype(BF16), conv_b_in[j])
            x = conv_out(x, mods, u, conv_dw[j], conv_dw_b[j], conv_ln_g[j], conv_ln_b[j],
                         conv_w_out[j].astype(BF16), conv_b_out[j], ln_g[layer, 0], ln_b[layer, 0])
        x = moe_layer(x, mods, layer, router_w, router_b, moe_w_gate, moe_b_gate, moe_w_up, moe_b_up,
                      moe_w_down, moe_b_down, ln_g[layer, 1], ln_b[layer, 1])
    return (x[:N_PROMPT].reshape(BATCH, SEQ, D_MODEL),
            x[N_PROMPT:].reshape(DEC_BATCH, DEC_SEQ, D_MODEL),
            jnp.stack(new_k, axis=1), jnp.stack(new_v, axis=1), jnp.stack(new_s, axis=1))
```

```python
import functools

import jax
import jax.numpy as jnp
import numpy as np
from jax import lax
from jax.experimental import pallas as pl
from jax.experimental.pallas import tpu as pltpu

D_MODEL = 1024
BATCH = 16
SEQ = 256
DEPTH = 4
DEC_BATCH = 4
DEC_SEQ = 2048
PAST_LEN = 512
GRID_W = 64
N_EVEN = (DEPTH + 1) // 2
N_ODD = DEPTH // 2
HEAD_DIM = 64
N_HEADS_A = 8
N_KV_A = 2
GROUP_A = N_HEADS_A // N_KV_A
WINDOW = 128
ATTN_BLOCK = 128
SCALE_A = HEAD_DIM ** -0.5
ROPE_BASE = 10000.0
ROPE_PAIRS = HEAD_DIM // 4
N_HEADS_B = 4
HGRN_DK = 128
HGRN_DV = 128
CONV_WIDTH = 31
N_EXPERTS = 32
TOP_K = 4
D_EXPERT = D_MODEL
SWIGLU_LIMIT = 7.0
SWIGLU_ALPHA = 1.702
LN_EPS = 1e-5
RMS_EPS = 1e-6
MASK_VALUE = -1e9
LB_FLOOR = 1e-30
DN_ALPHA = (2 * DEPTH) ** 0.25
A_Q = N_HEADS_A * HEAD_DIM
A_KV = N_KV_A * HEAD_DIM
B_QK = N_HEADS_B * HGRN_DK
B_V = N_HEADS_B * HGRN_DV
IN_SIZES = (A_Q, A_KV, A_KV, B_QK, B_V, B_QK, B_QK, B_V)
D_IN_EVEN = sum(IN_SIZES)

N_PROMPT = BATCH * SEQ
N_SAMPLE = DEC_BATCH * DEC_SEQ
N_TOK = N_PROMPT + N_SAMPLE
N_COND = 1 + DEC_BATCH
COND_ROWS = 8

LANES = 128
SUBLANES = 8
VMEM_LIMIT = 56 * 1024 * 1024

ROW_BLOCK = 256
HG_BLOCK = 128
HG_SUB = 8
MOE_ROWS = 256
MOE_BLOCKS = (N_TOK * TOP_K + N_EXPERTS * (MOE_ROWS - 1)) // MOE_ROWS + 1
MOE_R = MOE_BLOCKS * MOE_ROWS
COMB_TOK = 128

F32 = jnp.float32
BF16 = jnp.bfloat16
HIGHEST = lax.Precision.HIGHEST


def _cond_of_block(i, rows):
    start = i * rows
    return jnp.where(start < N_PROMPT, 0, 1 + (start - N_PROMPT) // DEC_SEQ)


def _cparams(sem):
    return pltpu.CompilerParams(dimension_semantics=sem, vmem_limit_bytes=VMEM_LIMIT)


def _layer_norm(z, g, b):
    mu = jnp.mean(z, axis=-1, keepdims=True)
    zc = z - mu
    var = jnp.mean(zc * zc, axis=-1, keepdims=True)
    return zc * lax.rsqrt(var + LN_EPS) * g + b


def _dot(a, b):
    return jnp.dot(a, b, preferred_element_type=F32)


def _dot_nt(a, b):
    return lax.dot_general(a, b, (((1,), (1,)), ((), ())), preferred_element_type=F32)


ADA_TN = 1536


def _adaln_kernel(cond_ref, w_ref, b_ref, o_ref):
    c = cond_ref[...]
    s = c * jax.nn.sigmoid(c)
    o_ref[0] = jnp.dot(s, w_ref[0], precision=HIGHEST, preferred_element_type=F32) + b_ref[0]


def adaln_all(cond, w_ada, b_ada):
    n_out = 6 * D_MODEL
    return pl.pallas_call(
        _adaln_kernel,
        grid=(DEPTH, n_out // ADA_TN),
        in_specs=[
            pl.BlockSpec((COND_ROWS, D_MODEL), lambda l, n: (0, 0)),
            pl.BlockSpec((1, D_MODEL, ADA_TN), lambda l, n: (l, 0, n)),
            pl.BlockSpec((1, 1, ADA_TN), lambda l, n: (l, 0, n)),
        ],
        out_specs=pl.BlockSpec((1, COND_ROWS, ADA_TN), lambda l, n: (l, 0, n)),
        out_shape=jax.ShapeDtypeStruct((DEPTH, COND_ROWS, n_out), F32),
        compiler_params=_cparams(("parallel", "parallel")),
        name="adaln",
    )(cond, w_ada, b_ada.reshape(DEPTH, 1, n_out))


def _rope_tables():
    t = np.arange(DEC_SEQ)
    d = np.arange(LANES) % HEAD_DIM
    axis = d // (2 * ROPE_PAIRS)
    half = (d // ROPE_PAIRS) % 2
    pair = d % ROPE_PAIRS
    pos = jnp.where(axis[None, :] == 0, (t // GRID_W)[:, None], (t % GRID_W)[:, None]).astype(F32)
    inv_freq = ROPE_BASE ** (-jnp.arange(ROPE_PAIRS, dtype=F32) / ROPE_PAIRS)
    ang = pos * inv_freq[pair][None, :]
    sign = jnp.where(half[None, :] == 0, -1.0, 1.0).astype(F32)
    return jnp.cos(ang), jnp.sin(ang) * sign


def _rope(x, cos, sin_signed):
    lane = lax.broadcasted_iota(jnp.int32, x.shape, 1)
    first_half = (lane // ROPE_PAIRS) % 2 == 0
    partner = jnp.where(first_half, pltpu.roll(x, LANES - ROPE_PAIRS, 1), pltpu.roll(x, ROPE_PAIRS, 1))
    return x * cos + partner * sin_signed


def _inproj_kernel(x_ref, mod_ref, w_ref, b_ref, cos_ref, sin_ref,
                   q_ref, k_ref, v_ref, qb_ref, ib_ref, ff_ref, fb_ref, go_ref):
    i = pl.program_id(0)
    shift = mod_ref[0, 0:1, :]
    scale = mod_ref[0, 1:2, :]
    h = (x_ref[...] * (1.0 + scale) + shift).astype(BF16)
    y = _dot(h, w_ref[...]) + b_ref[...]
    offs = np.cumsum((0,) + IN_SIZES)
    q = y[:, offs[0]:offs[1]]
    k = y[:, offs[1]:offs[2]]
    v_ref[...] = y[:, offs[2]:offs[3]]
    qb_ref[...] = y[:, offs[3]:offs[4]]
    ib_ref[...] = y[:, offs[4]:offs[5]]
    ff_ref[...] = y[:, offs[5]:offs[6]]
    fb_ref[...] = y[:, offs[6]:offs[7]]
    go_ref[...] = y[:, offs[7]:offs[8]]
    is_latent = i * ROW_BLOCK >= N_PROMPT

    @pl.when(jnp.logical_not(is_latent))
    def _():
        q_ref[...] = q
        k_ref[...] = k

    @pl.when(is_latent)
    def _():
        cos = cos_ref[...]
        sin = sin_ref[...]
        for c in range(A_Q // LANES):
            q_ref[:, c * LANES:(c + 1) * LANES] = _rope(q[:, c * LANES:(c + 1) * LANES], cos, sin)
        k_ref[...] = _rope(k, cos, sin)


def inproj_even(x, mods, w_bf, b, cos, sin):
    nblk = N_TOK // ROW_BLOCK
    pos_blocks = DEC_SEQ // ROW_BLOCK

    def pos_map(i):
        return (jnp.maximum(i - N_PROMPT // ROW_BLOCK, 0) % pos_blocks, 0)

    row = lambda i: (i, 0)
    widths = (A_Q, A_KV, A_KV, B_QK, B_V, B_QK, B_QK, B_V)
    return pl.pallas_call(
        _inproj_kernel,
        grid=(nblk,),
        in_specs=[
            pl.BlockSpec((ROW_BLOCK, D_MODEL), row),
            pl.BlockSpec((1, 6, D_MODEL), lambda i: (_cond_of_block(i, ROW_BLOCK), 0, 0)),
            pl.BlockSpec((D_MODEL, D_IN_EVEN), lambda i: (0, 0)),
            pl.BlockSpec((1, D_IN_EVEN), lambda i: (0, 0)),
            pl.BlockSpec((ROW_BLOCK, LANES), pos_map),
            pl.BlockSpec((ROW_BLOCK, LANES), pos_map),
        ],
        out_specs=[pl.BlockSpec((ROW_BLOCK, w), row) for w in widths],
        out_shape=[jax.ShapeDtypeStruct((N_TOK, w), F32) for w in widths],
        compiler_params=_cparams(("parallel",)),
        name="inproj_even",
    )(x, mods, w_bf, b.reshape(1, D_IN_EVEN), cos, sin)


def _sink_attend(q, keys, vals, sink, masks):
    scores = []
    for kk, mask in zip(keys, masks):
        s = _dot_nt(q, kk) * SCALE_A
        if mask is not None:
            s = jnp.where(mask, s, MASK_VALUE)
        scores.append(s)
    m = sink
    for s in scores:
        m = jnp.maximum(m, jnp.max(s, axis=-1, keepdims=True))
    denom = jnp.exp(sink - m)
    acc = None
    for s, vv in zip(scores, vals):
        p = jnp.exp(s - m)
        denom = denom + jnp.sum(p, axis=-1, keepdims=True)
        pv = _dot(p.astype(BF16), vv)
        acc = pv if acc is None else acc + pv
    return acc / denom


def _attn_ctx_kernel(sink_ref, q_ref, k_ref, v_ref, o_ref):
    k = k_ref[...].astype(BF16)
    v = v_ref[...].astype(BF16)
    q = q_ref[...].astype(BF16)
    for h in range(N_HEADS_A):
        kv = h // GROUP_A
        qh = q[:, h * HEAD_DIM:(h + 1) * HEAD_DIM]
        kh = k[:, kv * HEAD_DIM:(kv + 1) * HEAD_DIM]
        vh = v[:, kv * HEAD_DIM:(kv + 1) * HEAD_DIM]
        o_ref[:, h * HEAD_DIM:(h + 1) * HEAD_DIM] = _sink_attend(qh, [kh], [vh], sink_ref[h], [None])


def attn_context(sink, q, k, v):
    row = lambda b: (b, 0)
    return pl.pallas_call(
        _attn_ctx_kernel,
        grid=(BATCH,),
        in_specs=[
            pl.BlockSpec(memory_space=pltpu.SMEM),
            pl.BlockSpec((SEQ, A_Q), row),
            pl.BlockSpec((SEQ, A_KV), row),
            pl.BlockSpec((SEQ, A_KV), row),
        ],
        out_specs=pl.BlockSpec((SEQ, A_Q), row),
        out_shape=jax.ShapeDtypeStruct((N_PROMPT, A_Q), F32),
        compiler_params=_cparams(("parallel",)),
        name="attn_context",
    )(sink, q, k, v)


def _attn_lat_kernel(sink_ref, q_ref, kp_ref, kc_ref, kn_ref, vp_ref, vc_ref, vn_ref, ck_ref, cv_ref, o_ref):
    n = pl.program_id(1)
    nb = DEC_SEQ // ATTN_BLOCK
    qi = lax.broadcasted_iota(jnp.int32, (ATTN_BLOCK, ATTN_BLOCK), 0)
    kj = lax.broadcasted_iota(jnp.int32, (ATTN_BLOCK, ATTN_BLOCK), 1)
    mask_prev = jnp.logical_and(kj - qi >= ATTN_BLOCK - WINDOW, n > 0)
    mask_next = jnp.logical_and(kj - qi <= WINDOW - ATTN_BLOCK, n < nb - 1)
    masks = [mask_prev, None, mask_next, None]
    q = q_ref[...].astype(BF16)
    kband = [r[...].astype(BF16) for r in (kp_ref, kc_ref, kn_ref)]
    vband = [r[...].astype(BF16) for r in (vp_ref, vc_ref, vn_ref)]
    ck = ck_ref[0].astype(BF16)
    cv = cv_ref[0].astype(BF16)
    for h in range(N_HEADS_A):
        kv = h // GROUP_A
        sl = slice(kv * HEAD_DIM, (kv + 1) * HEAD_DIM)
        qh = q[:, h * HEAD_DIM:(h + 1) * HEAD_DIM]
        keys = [kb[:, sl] for kb in kband] + [ck[:, sl]]
        vals = [vb[:, sl] for vb in vband] + [cv[:, sl]]
        o_ref[:, h * HEAD_DIM:(h + 1) * HEAD_DIM] = _sink_attend(qh, keys, vals, sink_ref[h], masks)


def attn_latent(sink, q, k, v, cache_k, cache_v):
    nb = DEC_SEQ // ATTN_BLOCK
    base = N_PROMPT // ATTN_BLOCK

    def blk(delta):
        return lambda b, n: (base + b * nb + jnp.clip(n + delta, 0, nb - 1), 0)

    kv_spec = lambda delta: pl.BlockSpec((ATTN_BLOCK, A_KV), blk(delta))
    cache_spec = pl.BlockSpec((1, PAST_LEN, A_KV), lambda b, n: (b, 0, 0))
    return pl.pallas_call(
        _attn_lat_kernel,
        grid=(DEC_BATCH, nb),
        in_specs=[
            pl.BlockSpec(memory_space=pltpu.SMEM),
            pl.BlockSpec((ATTN_BLOCK, A_Q), blk(0)),
            kv_spec(-1), kv_spec(0), kv_spec(1),
            kv_spec(-1), kv_spec(0), kv_spec(1),
            cache_spec, cache_spec,
        ],
        out_specs=pl.BlockSpec((ATTN_BLOCK, A_Q), lambda b, n: (b * nb + n, 0)),
        out_shape=jax.ShapeDtypeStruct((N_SAMPLE, A_Q), F32),
        compiler_params=_cparams(("parallel", "parallel")),
        name="attn_latent",
    )(sink, q, k, k, k, v, v, v, cache_k, cache_v)


def _hgrn_gate(f_pre, lb):
    log_f = jnp.logaddexp(jnp.log(jnp.maximum(lb, LB_FLOOR)), jnp.log1p(-lb) + jax.nn.log_sigmoid(f_pre))
    return 1.0 - jnp.exp(log_f), log_f


def _scan_rows(x, reverse):
    n = x.shape[0]
    row = lax.broadcasted_iota(jnp.int32, x.shape, 0)
    sh = 1
    while sh < n:
        if reverse:
            x = x + jnp.where(row < n - sh, pltpu.roll(x, n - sh, 0), 0.0)
        else:
            x = x + jnp.where(row >= sh, pltpu.roll(x, sh, 0), 0.0)
        sh *= 2
    return x


def _chunk_bcast(x, c, pick, shift):
    nc = HG_BLOCK // c
    rows = x.reshape(nc, c, x.shape[-1])[:, pick:pick + 1, :]
    zero = jnp.zeros((1, 1, x.shape[-1]), x.dtype)
    if shift == -1:
        rows = jnp.concatenate([zero, rows[:-1]], axis=0)
    elif shift == 1:
        rows = jnp.concatenate([rows[1:], zero], axis=0)
    return jnp.broadcast_to(rows, (nc, c, x.shape[-1])).reshape(HG_BLOCK, x.shape[-1])


def _hgrn_block(q, kk, v, log_f, st, reverse):
    cum = _scan_rows(log_f, reverse)
    tot = cum[0:1, :] if reverse else cum[HG_BLOCK - 1:HG_BLOCK, :]
    o = _dot_nt((q * jnp.exp(cum)).astype(BF16), st.astype(BF16))
    kd = (kk * jnp.exp(tot - cum)).astype(BF16)
    v_bf = v.astype(BF16)
    u_t = lax.dot_general(v_bf, kd, (((0,), (0,)), ((), ())), preferred_element_type=F32)
    st_new = st * jnp.exp(tot) + u_t
    ti = lax.broadcasted_iota(jnp.int32, (HG_BLOCK, HG_BLOCK), 0)
    si = lax.broadcasted_iota(jnp.int32, (HG_BLOCK, HG_BLOCK), 1)
    a = jnp.zeros((HG_BLOCK, HG_BLOCK), F32)
    c = HG_SUB
    while c < HG_BLOCK:
        if reverse:
            bound_t = _chunk_bcast(cum, c, 0, 1)
            bound_s = _chunk_bcast(cum, c, 0, 0)
            mask = jnp.logical_and((ti // c) % 2 == 0, si // c == ti // c + 1)
        else:
            bound_t = _chunk_bcast(cum, c, c - 1, -1)
            bound_s = _chunk_bcast(cum, c, c - 1, 0)
            mask = jnp.logical_and((ti // c) % 2 == 1, si // c == ti // c - 1)
        qc = (q * jnp.exp(cum - bound_t)).astype(BF16)
        kc = (kk * jnp.exp(bound_s - cum)).astype(BF16)
        a = a + jnp.where(mask, _dot_nt(qc, kc), 0.0)
        c *= 2
    o = o + _dot(a.astype(BF16), v_bf)
    nsub = HG_BLOCK // HG_SUB
    q3 = q.reshape(nsub, HG_SUB, HGRN_DK)
    k3 = kk.reshape(nsub, HG_SUB, HGRN_DK)
    v3 = v.reshape(nsub, HG_SUB, HGRN_DV)
    cum3 = cum.reshape(nsub, HG_SUB, HGRN_DK)
    t_off = lax.broadcasted_iota(jnp.int32, (1, HG_SUB, 1), 1)
    od = jnp.zeros((nsub, HG_SUB, HGRN_DV), F32)
    for s in range(HG_SUB):
        causal = (t_off <= s) if reverse else (t_off >= s)
        decay = jnp.exp(jnp.where(causal, cum3 - cum3[:, s:s + 1, :], MASK_VALUE))
        score = jnp.sum(q3 * decay * k3[:, s:s + 1, :], axis=-1, keepdims=True)
        od = od + score * v3[:, s:s + 1, :]
    return o + od.reshape(HG_BLOCK, HGRN_DV), st_new


def _hgrn_kernel(*refs, n_blocks, has_state_in, has_state_out):
    refs = list(refs)
    qb_ref, ib_ref, ff_ref, fb_ref, go_ref, lbf_ref, lbb_ref, nw_ref = refs[:8]
    refs = refs[8:]
    s0_ref = refs.pop(0) if has_state_in else None
    o_ref = refs.pop(0)
    so_ref = refs.pop(0) if has_state_out else None
    acc_ref = refs.pop(0)

    def run(reverse):
        f_ref, lb_ref = (fb_ref, lbb_ref) if reverse else (ff_ref, lbf_ref)
        lb = lb_ref[...]
        if has_state_in:
            st0 = s0_ref[0, 0, 1 if reverse else 0, 0].T
        else:
            st0 = jnp.zeros((HGRN_DV, HGRN_DK), F32)

        def body(it, st):
            blk = (n_blocks - 1 - it) if reverse else it
            rows = pl.ds(pl.multiple_of(blk * HG_BLOCK, HG_BLOCK), HG_BLOCK)
            qpre = qb_ref[rows, :]
            q = qpre * jax.nn.sigmoid(qpre)
            kk, log_f = _hgrn_gate(f_ref[rows, :], lb)
            o, st = _hgrn_block(q, kk, ib_ref[rows, :], log_f, st, reverse)
            if reverse:
                acc_ref[rows, :] = acc_ref[rows, :] + o
            else:
                acc_ref[rows, :] = o
            return st

        return lax.fori_loop(0, n_blocks, body, st0)

    st_f = run(False)
    st_b = run(True)
    if has_state_out:
        so_ref[0, 0, 0] = st_f.T
        so_ref[0, 1, 0] = st_b.T
    o = acc_ref[...]
    o = o * lax.rsqrt(jnp.mean(o * o, axis=-1, keepdims=True) + RMS_EPS) * nw_ref[...]
    g = go_ref[...]
    o_ref[...] = o * (g * jax.nn.sigmoid(g))


def hgrn_mixer(qb, ib, ff, fb, go, lb_f, lb_b, norm_w, *, latent, state_in=None, layer_j=0):
    t_len = DEC_SEQ if latent else SEQ
    n_seq = DEC_BATCH if latent else BATCH
    row0 = N_PROMPT // t_len if latent else 0
    tok = pl.BlockSpec((t_len, HGRN_DK), lambda b, h: (row0 + b, h))
    vec = pl.BlockSpec((1, HGRN_DK), lambda b, h: (0, h))
    in_specs = [tok] * 5 + [vec] * 3
    args = [qb, ib, ff, fb, go, lb_f.reshape(1, B_QK), lb_b.reshape(1, B_QK), norm_w.reshape(1, B_V)]
    if latent:
        in_specs.append(pl.BlockSpec((1, 1, 2, 1, HGRN_DK, HGRN_DV), lambda b, h: (b, layer_j, 0, h, 0, 0)))
        args.append(state_in)
    out_specs = [pl.BlockSpec((t_len, HGRN_DV), lambda b, h: (b, h))]
    out_shape = [jax.ShapeDtypeStruct((n_seq * t_len, B_V), F32)]
    if not latent:
        out_specs.append(pl.BlockSpec((1, 2, 1, HGRN_DK, HGRN_DV), lambda b, h: (b, 0, h, 0, 0)))
        out_shape.append(jax.ShapeDtypeStruct((BATCH, 2, N_HEADS_B, HGRN_DK, HGRN_DV), F32))
    kern = functools.partial(_hgrn_kernel, n_blocks=t_len // HG_BLOCK, has_state_in=latent, has_state_out=not latent)
    return pl.pallas_call(
        kern,
        grid=(n_seq, N_HEADS_B),
        in_specs=in_specs,
        out_specs=out_specs,
        out_shape=out_shape,
        scratch_shapes=[pltpu.VMEM((t_len, HGRN_DV), F32)],
        compiler_params=_cparams(("parallel", "parallel")),
        name="hgrn_latent" if latent else "hgrn_context",
    )(*args)


def _outproj_kernel(x_ref, mod_ref, ap_ref, as_ref, rp_ref, rs_ref, w_ref, g_ref, b_ref, o_ref):
    is_latent = pl.program_id(0) * ROW_BLOCK >= N_PROMPT
    attn = jnp.where(is_latent, as_ref[...], ap_ref[...]).astype(BF16)
    rec = jnp.where(is_latent, rs_ref[...], rp_ref[...]).astype(BF16)
    y = _dot(attn, w_ref[0:A_Q, :]) + _dot(rec, w_ref[A_Q:A_Q + B_V, :])
    z = DN_ALPHA * x_ref[...] + mod_ref[0, 2:3, :] * y
    o_ref[...] = _layer_norm(z, g_ref[...], b_ref[...])


def outproj_even(x, mods, attn_p, attn_s, rec_p, rec_s, w_bf, ln_g, ln_b):
    nblk = N_TOK // ROW_BLOCK
    npb = N_PROMPT // ROW_BLOCK
    row = lambda i: (i, 0)
    prow = lambda i: (jnp.minimum(i, npb - 1), 0)
    srow = lambda i: (jnp.maximum(i - npb, 0), 0)
    vec = pl.BlockSpec((1, D_MODEL), lambda i: (0, 0))
    return pl.pallas_call(
        _outproj_kernel,
        grid=(nblk,),
        in_specs=[
            pl.BlockSpec((ROW_BLOCK, D_MODEL), row),
            pl.BlockSpec((1, 6, D_MODEL), lambda i: (_cond_of_block(i, ROW_BLOCK), 0, 0)),
            pl.BlockSpec((ROW_BLOCK, A_Q), prow),
            pl.BlockSpec((ROW_BLOCK, A_Q), srow),
            pl.BlockSpec((ROW_BLOCK, B_V), prow),
            pl.BlockSpec((ROW_BLOCK, B_V), srow),
            pl.BlockSpec((A_Q + B_V, D_MODEL), lambda i: (0, 0)),
            vec, vec,
        ],
        out_specs=pl.BlockSpec((ROW_BLOCK, D_MODEL), row),
        out_shape=jax.ShapeDtypeStruct((N_TOK, D_MODEL), F32),
        compiler_params=_cparams(("parallel",)),
        name="outproj_even",
    )(x, mods, attn_p, attn_s, rec_p, rec_s, w_bf, ln_g.reshape(1, D_MODEL), ln_b.reshape(1, D_MODEL))


def _conv_in_kernel(x_ref, mod_ref, w_ref, b_ref, u_ref):
    h = (x_ref[...] * (1.0 + mod_ref[0, 1:2, :]) + mod_ref[0, 0:1, :]).astype(BF16)
    a = _dot(h, w_ref[:, 0:D_MODEL]) + b_ref[:, 0:D_MODEL]
    gt = _dot(h, w_ref[:, D_MODEL:2 * D_MODEL]) + b_ref[:, D_MODEL:2 * D_MODEL]
    u_ref[...] = a * jax.nn.sigmoid(gt)


def conv_in(x, mods, w_bf, b):
    row = lambda i: (i, 0)
    return pl.pallas_call(
        _conv_in_kernel,
        grid=(N_TOK // ROW_BLOCK,),
        in_specs=[
            pl.BlockSpec((ROW_BLOCK, D_MODEL), row),
            pl.BlockSpec((1, 6, D_MODEL), lambda i: (_cond_of_block(i, ROW_BLOCK), 0, 0)),
            pl.BlockSpec((D_MODEL, 2 * D_MODEL), lambda i: (0, 0)),
            pl.BlockSpec((1, 2 * D_MODEL), lambda i: (0, 0)),
        ],
        out_specs=pl.BlockSpec((ROW_BLOCK, D_MODEL), row),
        out_shape=jax.ShapeDtypeStruct((N_TOK, D_MODEL), F32),
        compiler_params=_cparams(("parallel",)),
        name="conv_in",
    )(x, mods, w_bf, b.reshape(1, 2 * D_MODEL))


CONV_HALO = 16
CONV_LANES = 256


CONV_SHIFT_ROWS = ROW_BLOCK + 2 * CONV_HALO - SUBLANES
CONV_ROW_CHUNK = 128


def _conv_out_kernel(x_ref, mod_ref, up_ref, uc_ref, un_ref, dw_ref, dwb_ref, cg_ref, cb_ref,
                     w_ref, b_ref, g_ref, bb_ref, o_ref, pad_ref, acc_ref, sh_ref):
    i = pl.program_id(0)
    blocks_per_seq = DEC_SEQ // ROW_BLOCK
    j = i - N_PROMPT // ROW_BLOCK
    is_latent = j >= 0
    has_prev = jnp.logical_and(is_latent, j % blocks_per_seq != 0)
    has_next = jnp.logical_and(is_latent, j % blocks_per_seq != blocks_per_seq - 1)
    pad_ref[0:CONV_HALO, :] = jnp.where(has_prev, up_ref[...], 0.0)
    pad_ref[CONV_HALO:CONV_HALO + ROW_BLOCK, :] = uc_ref[...]
    pad_ref[CONV_HALO + ROW_BLOCK:, :] = jnp.where(has_next, un_ref[...], 0.0)
    first = CONV_HALO - CONV_WIDTH // 2
    for c in range(D_MODEL // CONV_LANES):
        lanes = slice(c * CONV_LANES, (c + 1) * CONV_LANES)
        for s in range(SUBLANES):
            sh_ref[s] = pad_ref[s:s + CONV_SHIFT_ROWS, lanes]
        for r0 in range(0, ROW_BLOCK, CONV_ROW_CHUNK):
            acc = jnp.zeros((CONV_ROW_CHUNK, CONV_LANES), F32) + dwb_ref[:, lanes]
            for tap in range(CONV_WIDTH):
                whole, s = divmod(first + tap, SUBLANES)
                rows = slice(r0 + whole * SUBLANES, r0 + whole * SUBLANES + CONV_ROW_CHUNK)
                acc = acc + sh_ref[s, rows, :] * dw_ref[tap:tap + 1, lanes]
            acc_ref[r0:r0 + CONV_ROW_CHUNK, lanes] = acc
    u = _layer_norm(acc_ref[...], cg_ref[...], cb_ref[...])
    u = (u * jax.nn.sigmoid(u)).astype(BF16)
    y = _dot(u, w_ref[...]) + b_ref[...]
    z = DN_ALPHA * x_ref[...] + mod_ref[0, 2:3, :] * y
    o_ref[...] = _layer_norm(z, g_ref[...], bb_ref[...])


def conv_out(x, mods, u, dw, dw_b, cln_g, cln_b, w_bf, b_out, ln_g, ln_b):
    nblk = N_TOK // ROW_BLOCK
    ratio = ROW_BLOCK // CONV_HALO
    nhalo = N_TOK // CONV_HALO
    row = lambda i: (i, 0)
    vec = pl.BlockSpec((1, D_MODEL), lambda i: (0, 0))
    r1 = lambda a: a.reshape(1, D_MODEL)
    return pl.pallas_call(
        _conv_out_kernel,
        grid=(nblk,),
        in_specs=[
            pl.BlockSpec((ROW_BLOCK, D_MODEL), row),
            pl.BlockSpec((1, 6, D_MODEL), lambda i: (_cond_of_block(i, ROW_BLOCK), 0, 0)),
            pl.BlockSpec((CONV_HALO, D_MODEL), lambda i: (jnp.maximum(i * ratio - 1, 0), 0)),
            pl.BlockSpec((ROW_BLOCK, D_MODEL), row),
            pl.BlockSpec((CONV_HALO, D_MODEL), lambda i: (jnp.minimum((i + 1) * ratio, nhalo - 1), 0)),
            pl.BlockSpec((CONV_WIDTH, D_MODEL), lambda i: (0, 0)),
            vec, vec, vec,
            pl.BlockSpec((D_MODEL, D_MODEL), lambda i: (0, 0)),
            vec, vec, vec,
        ],
        out_specs=pl.BlockSpec((ROW_BLOCK, D_MODEL), row),
        out_shape=jax.ShapeDtypeStruct((N_TOK, D_MODEL), F32),
        scratch_shapes=[pltpu.VMEM((ROW_BLOCK + 2 * CONV_HALO, D_MODEL), F32),
                        pltpu.VMEM((ROW_BLOCK, D_MODEL), F32),
                        pltpu.VMEM((SUBLANES, CONV_SHIFT_ROWS, CONV_LANES), F32)],
        compiler_params=_cparams(("parallel",)),
        name="conv_out",
    )(x, mods, u, u, u, dw, r1(dw_b), r1(cln_g), r1(cln_b), w_bf, r1(b_out), r1(ln_g), r1(ln_b))


def _modulate2(x_ref, mod_ref):
    return x_ref[...] * (1.0 + mod_ref[0, 4:5, :]) + mod_ref[0, 3:4, :]


def _route_kernel(x_ref, mod_ref, wr_ref, br_ref, idx_ref, gate_ref, rank_ref, cnt_ref, carry_ref):
    i = pl.program_id(0)

    @pl.when(i == 0)
    def _():
        carry_ref[...] = jnp.zeros_like(carry_ref)

    h = _modulate2(x_ref, mod_ref)
    logits = lax.dot_general(wr_ref[...], h, (((1,), (1,)), ((), ())), precision=HIGHEST,
                             preferred_element_type=F32) + br_ref[...]
    eidx = lax.broadcasted_iota(jnp.int32, logits.shape, 0)
    vals = logits
    sels, tops = [], []
    for k in range(TOP_K):
        m = jnp.max(vals, axis=0, keepdims=True)
        idx = jnp.min(jnp.where(vals == m, eidx, N_EXPERTS), axis=0, keepdims=True)
        sel = eidx == idx
        idx_ref[k:k + 1, :] = idx
        sels.append(sel)
        tops.append(m)
        vals = jnp.where(sel, -jnp.inf, vals)
    exps = [jnp.exp(t - tops[0]) for t in tops]
    total = exps[0] + exps[1] + exps[2] + exps[3]
    for k in range(TOP_K):
        gate_ref[k:k + 1, :] = exps[k] / total
    onehot = jnp.zeros(logits.shape, F32)
    for sel in sels:
        onehot = onehot + sel.astype(F32)
    ta = lax.broadcasted_iota(jnp.int32, (ROW_BLOCK, ROW_BLOCK), 0)
    tb = lax.broadcasted_iota(jnp.int32, (ROW_BLOCK, ROW_BLOCK), 1)
    before = _dot(onehot.astype(BF16), (ta < tb).astype(BF16)) + carry_ref[:, 0:1]
    for k in range(TOP_K):
        rank = jnp.sum(jnp.where(sels[k], before, 0.0), axis=0, keepdims=True)
        rank_ref[k:k + 1, :] = rank.astype(jnp.int32)
    carry = carry_ref[...] + jnp.sum(onehot, axis=1, keepdims=True)
    carry_ref[...] = carry
    cnt_ref[...] = carry.astype(jnp.int32)


def moe_route(x, mods, wr_t, b_r):
    tok = pl.BlockSpec((TOP_K, ROW_BLOCK), lambda i: (0, i))
    return pl.pallas_call(
        _route_kernel,
        grid=(N_TOK // ROW_BLOCK,),
        in_specs=[
            pl.BlockSpec((ROW_BLOCK, D_MODEL), lambda i: (i, 0)),
            pl.BlockSpec((1, 6, D_MODEL), lambda i: (_cond_of_block(i, ROW_BLOCK), 0, 0)),
            pl.BlockSpec((N_EXPERTS, D_MODEL), lambda i: (0, 0)),
            pl.BlockSpec((N_EXPERTS, 1), lambda i: (0, 0)),
        ],
        out_specs=[tok, tok, tok, pl.BlockSpec((N_EXPERTS, LANES), lambda i: (0, 0))],
        out_shape=[jax.ShapeDtypeStruct((TOP_K, N_TOK), jnp.int32),
                   jax.ShapeDtypeStruct((TOP_K, N_TOK), F32),
                   jax.ShapeDtypeStruct((TOP_K, N_TOK), jnp.int32),
                   jax.ShapeDtypeStruct((N_EXPERTS, LANES), jnp.int32)],
        scratch_shapes=[pltpu.VMEM((N_EXPERTS, LANES), F32)],
        compiler_params=_cparams(("arbitrary",)),
        name="moe_route",
    )(x, mods, wr_t, b_r.reshape(N_EXPERTS, 1))


HALF_D = D_MODEL // 2
HI_MASK = 0xFFFF0000


def _pack_bf16_pairs(h):
    half = h.shape[1] // 2
    lo = lax.bitcast_convert_type(h[:, :half].astype(BF16).astype(F32), jnp.uint32)
    hi = lax.bitcast_convert_type(h[:, half:].astype(BF16).astype(F32), jnp.uint32)
    return (lo >> 16) | (hi & jnp.uint32(HI_MASK))


def _unpack_bf16_pairs(w):
    lo = lax.bitcast_convert_type(w << 16, F32).astype(BF16)
    hi = lax.bitcast_convert_type(w & jnp.uint32(HI_MASK), F32).astype(BF16)
    return lo, hi


def _scatter_kernel(dest_ref, zstart_ref, x_ref, mod_ref, xs_ref, pk_ref, zero_ref, sem, zsem):
    i = pl.program_id(0)

    def zero_copy(start):
        start = pl.multiple_of(start, MOE_ROWS)
        return pltpu.make_async_copy(zero_ref, xs_ref.at[pl.ds(start, MOE_ROWS), :], zsem)

    def zero_blocks(fn):
        for e in range(N_EXPERTS):
            @pl.when(zstart_ref[e] >= 0)
            def _():
                fn(zero_copy(jnp.maximum(zstart_ref[e], 0)))

        def unused(b, carry):
            fn(zero_copy(b * MOE_ROWS))
            return carry

        lax.fori_loop(zstart_ref[N_EXPERTS], MOE_BLOCKS, unused, 0)

    @pl.when(i == 0)
    def _():
        zero_ref[...] = jnp.zeros_like(zero_ref)
        zero_blocks(lambda cp: cp.start())
        zero_blocks(lambda cp: cp.wait())

    base = i * ROW_BLOCK
    last = pl.num_programs(0) - 1

    def drain(slot):
        for k in range(TOP_K):
            pltpu.make_async_copy(pk_ref.at[slot], xs_ref.at[pl.ds(0, ROW_BLOCK), :], sem.at[slot]).wait()

    for slot in range(2):
        @pl.when(i % 2 == slot)
        def _():
            @pl.when(i >= 2)
            def _():
                drain(slot)

            pk_ref[slot] = _pack_bf16_pairs(_modulate2(x_ref, mod_ref))
            for t in range(ROW_BLOCK):
                for k in range(TOP_K):
                    row = dest_ref[k * N_TOK + base + t]
                    pltpu.make_async_copy(pk_ref.at[slot, pl.ds(t, 1), :], xs_ref.at[pl.ds(row, 1), :],
                                          sem.at[slot]).start(priority=k % 2)

            @pl.when(i == last)
            def _():
                drain(slot)

                @pl.when(i >= 1)
                def _():
                    drain(1 - slot)


def moe_scatter(dest_flat, zstart, x, mods):
    return pl.pallas_call(
        _scatter_kernel,
        grid_spec=pltpu.PrefetchScalarGridSpec(
            num_scalar_prefetch=2,
            grid=(N_TOK // ROW_BLOCK,),
            in_specs=[
                pl.BlockSpec((ROW_BLOCK, D_MODEL), lambda i, d, z: (i, 0)),
                pl.BlockSpec((1, 6, D_MODEL), lambda i, d, z: (_cond_of_block(i, ROW_BLOCK), 0, 0)),
            ],
            out_specs=pl.BlockSpec(memory_space=pl.ANY),
            scratch_shapes=[pltpu.VMEM((2, ROW_BLOCK, HALF_D), jnp.uint32),
                            pltpu.VMEM((MOE_ROWS, HALF_D), jnp.uint32),
                            pltpu.SemaphoreType.DMA((2,)), pltpu.SemaphoreType.DMA],
        ),
        out_shape=jax.ShapeDtypeStruct((MOE_R, HALF_D), jnp.uint32),
        compiler_params=_cparams(("arbitrary",)),
        name="moe_scatter",
    )(dest_flat, zstart, x, mods)


N_CHUNK = 256


def _expert_kernel(be_ref, nused_ref, nexte_ref, par_ref, xs_ref, wg_hbm, wu_hbm, wd_hbm, bg_ref, bu_ref, bd_ref,
                   ys_ref, wbuf, act_ref, wg_bf, wu_bf, wd_bf, wsem, *, layer):
    i = pl.program_id(0)
    new_expert = jnp.logical_or(i == 0, be_ref[i] != be_ref[jnp.maximum(i - 1, 0)])
    slot = par_ref[i]

    def weight_copies(e, s):
        return [pltpu.make_async_copy(w.at[layer, e], wbuf.at[s, j], wsem.at[s])
                for j, w in enumerate((wg_hbm, wu_hbm, wd_hbm))]

    @pl.when(i == 0)
    def _():
        for cp in weight_copies(be_ref[0], 0):
            cp.start()

    @pl.when(jnp.logical_and(new_expert, i < nused_ref[0]))
    def _():
        for cp in weight_copies(be_ref[i], slot):
            cp.wait()

        @pl.when(nexte_ref[i] >= 0)
        def _():
            for cp in weight_copies(jnp.maximum(nexte_ref[i], 0), 1 - slot):
                cp.start()

        wg_bf[...] = wbuf[slot, 0].astype(BF16)
        wu_bf[...] = wbuf[slot, 1].astype(BF16)
        wd_bf[...] = wbuf[slot, 2].astype(BF16)

    @pl.when(i < nused_ref[0])
    def _():
        lo, hi = _unpack_bf16_pairs(xs_ref[...])
        x = jnp.concatenate([lo, hi], axis=1)
        for n in range(D_EXPERT // N_CHUNK):
            cols = slice(n * N_CHUNK, (n + 1) * N_CHUNK)
            gt = _dot(x, wg_bf[:, cols]) + bg_ref[0, 0, :, cols]
            up = _dot(x, wu_bf[:, cols]) + bu_ref[0, 0, :, cols]
            gt = jnp.minimum(gt, SWIGLU_LIMIT)
            up = jnp.clip(up, -SWIGLU_LIMIT, SWIGLU_LIMIT)
            act_ref[:, cols] = ((up + 1.0) * gt * jax.nn.sigmoid(SWIGLU_ALPHA * gt)).astype(BF16)
        act = act_ref[...]
        for n in range(HALF_D // N_CHUNK):
            cols = slice(n * N_CHUNK, (n + 1) * N_CHUNK)
            cols_hi = slice(HALF_D + n * N_CHUNK, HALF_D + (n + 1) * N_CHUNK)
            y_lo = _dot(act, wd_bf[:, cols]) + bd_ref[0, 0, :, cols]
            y_hi = _dot(act, wd_bf[:, cols_hi]) + bd_ref[0, 0, :, cols_hi]
            ys_ref[:, cols] = _pack_bf16_pairs(jnp.concatenate([y_lo, y_hi], axis=1))

    @pl.when(i >= nused_ref[0])
    def _():
        ys_ref[...] = jnp.zeros_like(ys_ref)


def moe_experts(block_e, n_used, next_e, parity, xs, w_g, w_u, w_d, b_g, b_u, b_d, layer):
    rows = lambda i, be, nu, ne, pa: (i, 0)
    wspec = pl.BlockSpec(memory_space=pl.ANY)
    bspec = pl.BlockSpec((1, 1, 1, D_EXPERT), lambda i, be, nu, ne, pa: (layer, be[i], 0, 0))
    r4 = lambda b: b.reshape(DEPTH, N_EXPERTS, 1, D_EXPERT)
    return pl.pallas_call(
        functools.partial(_expert_kernel, layer=layer),
        grid_spec=pltpu.PrefetchScalarGridSpec(
            num_scalar_prefetch=4,
            grid=(MOE_BLOCKS,),
            in_specs=[pl.BlockSpec((MOE_ROWS, HALF_D), rows), wspec, wspec, wspec, bspec, bspec, bspec],
            out_specs=pl.BlockSpec((MOE_ROWS, HALF_D), rows),
            scratch_shapes=[pltpu.VMEM((2, 3, D_MODEL, D_EXPERT), F32),
                            pltpu.VMEM((MOE_ROWS, D_EXPERT), BF16)]
                           + [pltpu.VMEM((D_MODEL, D_EXPERT), BF16)] * 3
                           + [pltpu.SemaphoreType.DMA((2,))],
        ),
        out_shape=jax.ShapeDtypeStruct((MOE_R, HALF_D), jnp.uint32),
        compiler_params=_cparams(("arbitrary",)),
        name="moe_experts",
    )(block_e, n_used, next_e, parity, xs, w_g, w_u, w_d, r4(b_g), r4(b_u), r4(b_d))


def _combine_kernel(dest_ref, x_ref, mod_ref, gate_ref, ys_ref, g_ref, b_ref, o_ref, buf_ref, sem):
    i = pl.program_id(0)
    last = pl.num_programs(0) - 1

    def gather(step, slot):
        base = step * COMB_TOK
        for t in range(COMB_TOK):
            for k in range(TOP_K):
                row = dest_ref[k * N_TOK + base + t]
                pltpu.make_async_copy(ys_ref.at[pl.ds(row, 1), :], buf_ref.at[slot, k, pl.ds(t, 1), :],
                                      sem.at[slot]).start(priority=k % 2)

    @pl.when(i == 0)
    def _():
        gather(0, 0)

    for slot in range(2):
        @pl.when(jnp.logical_and(i < last, i % 2 == slot))
        def _():
            gather(i + 1, 1 - slot)

    eye = (lax.broadcasted_iota(jnp.int32, (COMB_TOK, COMB_TOK), 0)
           == lax.broadcasted_iota(jnp.int32, (COMB_TOK, COMB_TOK), 1))
    gates = gate_ref[...]
    cols = [jnp.sum(jnp.where(eye, gates[k:k + 1, :], 0.0), axis=1, keepdims=True) for k in range(TOP_K)]
    cur = i % 2
    for k in range(TOP_K):
        pltpu.make_async_copy(ys_ref.at[pl.ds(0, COMB_TOK), :], buf_ref.at[cur, k], sem.at[cur]).wait()
    y_lo = y_hi = None
    for k in range(TOP_K):
        w = buf_ref[cur, k]
        lo = cols[k] * lax.bitcast_convert_type(w << 16, F32)
        hi = cols[k] * lax.bitcast_convert_type(w & jnp.uint32(HI_MASK), F32)
        y_lo = lo if y_lo is None else y_lo + lo
        y_hi = hi if y_hi is None else y_hi + hi
    y = jnp.concatenate([y_lo, y_hi], axis=1)
    z = DN_ALPHA * x_ref[...] + mod_ref[0, 5:6, :] * y
    o_ref[...] = _layer_norm(z, g_ref[...], b_ref[...])


def moe_combine(dest_flat, x, mods, gates_t, ys, ln_g, ln_b):
    vec = pl.BlockSpec((1, D_MODEL), lambda i, d: (0, 0))
    return pl.pallas_call(
        _combine_kernel,
        grid_spec=pltpu.PrefetchScalarGridSpec(
            num_scalar_prefetch=1,
            grid=(N_TOK // COMB_TOK,),
            in_specs=[
                pl.BlockSpec((COMB_TOK, D_MODEL), lambda i, d: (i, 0)),
                pl.BlockSpec((1, 6, D_MODEL), lambda i, d: (_cond_of_block(i, COMB_TOK), 0, 0)),
                pl.BlockSpec((TOP_K, COMB_TOK), lambda i, d: (0, i)),
                pl.BlockSpec(memory_space=pl.ANY),
                vec, vec,
            ],
            out_specs=pl.BlockSpec((COMB_TOK, D_MODEL), lambda i, d: (i, 0)),
            scratch_shapes=[pltpu.VMEM((2, TOP_K, COMB_TOK, HALF_D), jnp.uint32), pltpu.SemaphoreType.DMA((2,))],
        ),
        out_shape=jax.ShapeDtypeStruct((N_TOK, D_MODEL), F32),
        compiler_params=_cparams(("arbitrary",)),
        name="moe_combine",
    )(dest_flat, x, mods, gates_t, ys, ln_g.reshape(1, D_MODEL), ln_b.reshape(1, D_MODEL))


def moe_layer(x, mods, layer, router_w, router_b, w_g, b_g, w_u, b_u, w_d, b_d, ln_g, ln_b):
    idx_t, gates_t, rank_t, counts = moe_route(x, mods, router_w[layer].T, router_b[layer])
    counts = counts[:, 0]
    padded = (counts + MOE_ROWS - 1) // MOE_ROWS * MOE_ROWS
    ends = jnp.cumsum(padded)
    base = ends - padded
    n_used = (ends[-1] // MOE_ROWS).astype(jnp.int32)
    block_start = jnp.arange(MOE_BLOCKS, dtype=jnp.int32) * MOE_ROWS
    block_e = jnp.sum(block_start[:, None] >= ends[None, :], axis=1).astype(jnp.int32)
    used = padded > 0
    experts = jnp.arange(N_EXPERTS, dtype=jnp.int32)
    block_e = jnp.minimum(block_e, jnp.max(jnp.where(used, experts, 0)))
    zstart = jnp.where(padded > 0, ends - MOE_ROWS, -1).astype(jnp.int32)
    zstart = jnp.concatenate([zstart, n_used.reshape(1)])
    onehot = idx_t[:, :, None] == jnp.arange(N_EXPERTS, dtype=jnp.int32)[None, None, :]
    dest = rank_t + jnp.sum(jnp.where(onehot, base[None, None, :], 0), axis=-1)
    dest_flat = dest.reshape(-1).astype(jnp.int32)
    xs = moe_scatter(dest_flat, zstart, x, mods)
    later_used = jnp.logical_and(used[None, :], experts[None, :] > experts[:, None])
    next_used = jnp.min(jnp.where(later_used, experts[None, :], N_EXPERTS), axis=1)
    next_used = jnp.where(next_used == N_EXPERTS, -1, next_used).astype(jnp.int32)
    run_parity = ((jnp.cumsum(used.astype(jnp.int32)) - 1) % 2).astype(jnp.int32)
    of_block = block_e[:, None] == experts[None, :]
    block_next = jnp.sum(jnp.where(of_block, next_used[None, :], 0), axis=1).astype(jnp.int32)
    block_parity = jnp.sum(jnp.where(of_block, run_parity[None, :], 0), axis=1).astype(jnp.int32)
    ys = moe_experts(block_e, n_used.reshape(1), block_next, block_parity, xs,
                     w_g, w_u, w_d, b_g, b_u, b_d, layer)
    return moe_combine(dest_flat, x, mods, gates_t, ys, ln_g, ln_b)


def kernel(x_prompt, x_sample, c, cache_k, cache_v, state_hgrn, c_ctx, w_ada, b_ada, ln_g, ln_b,
           w_in_even, b_in_even, attn_sink, hgrn_lb, hgrn_norm, w_out_even,
           conv_w_in, conv_b_in, conv_dw, conv_dw_b, conv_ln_g, conv_ln_b, conv_w_out, conv_b_out,
           router_w, router_b, moe_w_gate, moe_b_gate, moe_w_up, moe_b_up, moe_w_down, moe_b_down):
    x = jnp.concatenate([x_prompt.reshape(N_PROMPT, D_MODEL), x_sample.reshape(N_SAMPLE, D_MODEL)], axis=0)
    cond = jnp.concatenate([c_ctx[None, :], c, jnp.zeros((COND_ROWS - N_COND, D_MODEL), F32)], axis=0)
    mods_all = adaln_all(cond, w_ada, b_ada).reshape(DEPTH, COND_ROWS, 6, D_MODEL)
    lb = jax.nn.softmax(hgrn_lb.astype(F32), axis=1)
    lb = jnp.cumsum(lb, axis=1) - lb[:, :1]
    cos, sin = _rope_tables()
    new_k, new_v, new_s = [], [], []
    for layer in range(DEPTH):
        j = layer // 2
        mods = mods_all[layer]
        if layer % 2 == 0:
            q, k, v, qb, ib, ff, fb, go = inproj_even(x, mods, w_in_even[j].astype(BF16), b_in_even[j], cos, sin)
            new_k.append(k[:N_PROMPT].reshape(BATCH, SEQ, N_KV_A, HEAD_DIM))
            new_v.append(v[:N_PROMPT].reshape(BATCH, SEQ, N_KV_A, HEAD_DIM))
            attn_p = attn_context(attn_sink[j], q, k, v)
            attn_s = attn_latent(attn_sink[j], q, k, v,
                                 cache_k[:, j].reshape(DEC_BATCH, PAST_LEN, A_KV),
                                 cache_v[:, j].reshape(DEC_BATCH, PAST_LEN, A_KV))
            hg = (qb, ib, ff, fb, go, lb[0, j], lb[1, j], hgrn_norm[j])
            rec_p, states = hgrn_mixer(*hg, latent=False)
            rec_s, = hgrn_mixer(*hg, latent=True, state_in=state_hgrn, layer_j=j)
            new_s.append(states)
            x = outproj_even(x, mods, attn_p, attn_s, rec_p, rec_s, w_out_even[j].astype(BF16),
                             ln_g[layer, 0], ln_b[layer, 0])
        else:
            u = conv_in(x, mods, conv_w_in[j].astype(BF16), conv_b_in[j])
            x = conv_out(x, mods, u, conv_dw[j], conv_dw_b[j], conv_ln_g[j], conv_ln_b[j],
                         conv_w_out[j].astype(BF16), conv_b_out[j], ln_g[layer, 0], ln_b[layer, 0])
        x = moe_layer(x, mods, layer, router_w, router_b, moe_w_gate, moe_b_gate, moe_w_up, moe_b_up,
                      moe_w_down, moe_b_down, ln_g[layer, 1], ln_b[layer, 1])
    return (x[:N_PROMPT].reshape(BATCH, SEQ, D_MODEL),
            x[N_PROMPT:].reshape(DEC_BATCH, DEC_SEQ, D_MODEL),
            jnp.stack(new_k, axis=1), jnp.stack(new_v, axis=1), jnp.stack(new_s, axis=1))
```

```python
import functools

import jax
import jax.numpy as jnp
import numpy as np
from jax import lax
from jax.experimental import pallas as pl
from jax.experimental.pallas import tpu as pltpu

D_MODEL = 1024
BATCH = 16
SEQ = 256
DEPTH = 4
DEC_BATCH = 4
DEC_SEQ = 2048
PAST_LEN = 512
GRID_W = 64
N_EVEN = (DEPTH + 1) // 2
N_ODD = DEPTH // 2
HEAD_DIM = 64
N_HEADS_A = 8
N_KV_A = 2
GROUP_A = N_HEADS_A // N_KV_A
WINDOW = 128
ATTN_BLOCK = 128
SCALE_A = HEAD_DIM ** -0.5
ROPE_BASE = 10000.0
ROPE_PAIRS = HEAD_DIM // 4
N_HEADS_B = 4
HGRN_DK = 128
HGRN_DV = 128
CONV_WIDTH = 31
N_EXPERTS = 32
TOP_K = 4
D_EXPERT = D_MODEL
SWIGLU_LIMIT = 7.0
SWIGLU_ALPHA = 1.702
LN_EPS = 1e-5
RMS_EPS = 1e-6
MASK_VALUE = -1e9
LB_FLOOR = 1e-30
DN_ALPHA = (2 * DEPTH) ** 0.25
A_Q = N_HEADS_A * HEAD_DIM
A_KV = N_KV_A * HEAD_DIM
B_QK = N_HEADS_B * HGRN_DK
B_V = N_HEADS_B * HGRN_DV
IN_SIZES = (A_Q, A_KV, A_KV, B_QK, B_V, B_QK, B_QK, B_V)
D_IN_EVEN = sum(IN_SIZES)

N_PROMPT = BATCH * SEQ
N_SAMPLE = DEC_BATCH * DEC_SEQ
N_TOK = N_PROMPT + N_SAMPLE
N_COND = 1 + DEC_BATCH
COND_ROWS = 8

LANES = 128
SUBLANES = 8
VMEM_LIMIT = 56 * 1024 * 1024

ROW_BLOCK = 256
HG_BLOCK = 128
HG_SUB = 8
MOE_ROWS = 256
MOE_BLOCKS = (N_TOK * TOP_K + N_EXPERTS * (MOE_ROWS - 1)) // MOE_ROWS + 1
MOE_R = MOE_BLOCKS * MOE_ROWS
COMB_TOK = 128

F32 = jnp.float32
BF16 = jnp.bfloat16
HIGHEST = lax.Precision.HIGHEST


def _cond_of_block(i, rows):
    start = i * rows
    return jnp.where(start < N_PROMPT, 0, 1 + (start - N_PROMPT) // DEC_SEQ)


def _cparams(sem):
    return pltpu.CompilerParams(dimension_semantics=sem, vmem_limit_bytes=VMEM_LIMIT)


def _layer_norm(z, g, b):
    mu = jnp.mean(z, axis=-1, keepdims=True)
    zc = z - mu
    var = jnp.mean(zc * zc, axis=-1, keepdims=True)
    return zc * lax.rsqrt(var + LN_EPS) * g + b


def _dot(a, b):
    return jnp.dot(a, b, preferred_element_type=F32)


def _dot_nt(a, b):
    return lax.dot_general(a, b, (((1,), (1,)), ((), ())), preferred_element_type=F32)


ADA_TN = 1536


def _adaln_kernel(cond_ref, w_ref, b_ref, o_ref):
    c = cond_ref[...]
    s = c * jax.nn.sigmoid(c)
    o_ref[0] = jnp.dot(s, w_ref[0], precision=HIGHEST, preferred_element_type=F32) + b_ref[0]


def adaln_all(cond, w_ada, b_ada):
    n_out = 6 * D_MODEL
    return pl.pallas_call(
        _adaln_kernel,
        grid=(DEPTH, n_out // ADA_TN),
        in_specs=[
            pl.BlockSpec((COND_ROWS, D_MODEL), lambda l, n: (0, 0)),
            pl.BlockSpec((1, D_MODEL, ADA_TN), lambda l, n: (l, 0, n)),
            pl.BlockSpec((1, 1, ADA_TN), lambda l, n: (l, 0, n)),
        ],
        out_specs=pl.BlockSpec((1, COND_ROWS, ADA_TN), lambda l, n: (l, 0, n)),
        out_shape=jax.ShapeDtypeStruct((DEPTH, COND_ROWS, n_out), F32),
        compiler_params=_cparams(("parallel", "parallel")),
        name="adaln",
    )(cond, w_ada, b_ada.reshape(DEPTH, 1, n_out))


def _rope_tables():
    t = np.arange(DEC_SEQ)
    d = np.arange(LANES) % HEAD_DIM
    axis = d // (2 * ROPE_PAIRS)
    half = (d // ROPE_PAIRS) % 2
    pair = d % ROPE_PAIRS
    pos = jnp.where(axis[None, :] == 0, (t // GRID_W)[:, None], (t % GRID_W)[:, None]).astype(F32)
    inv_freq = ROPE_BASE ** (-jnp.arange(ROPE_PAIRS, dtype=F32) / ROPE_PAIRS)
    ang = pos * inv_freq[pair][None, :]
    sign = jnp.where(half[None, :] == 0, -1.0, 1.0).astype(F32)
    return jnp.cos(ang), jnp.sin(ang) * sign


def _rope(x, cos, sin_signed):
    lane = lax.broadcasted_iota(jnp.int32, x.shape, 1)
    first_half = (lane // ROPE_PAIRS) % 2 == 0
    partner = jnp.where(first_half, pltpu.roll(x, LANES - ROPE_PAIRS, 1), pltpu.roll(x, ROPE_PAIRS, 1))
    return x * cos + partner * sin_signed


def _inproj_kernel(x_ref, mod_ref, w_ref, b_ref, cos_ref, sin_ref,
                   q_ref, k_ref, v_ref, qb_ref, ib_ref, ff_ref, fb_ref, go_ref):
    i = pl.program_id(0)
    shift = mod_ref[0, 0:1, :]
    scale = mod_ref[0, 1:2, :]
    h = (x_ref[...] * (1.0 + scale) + shift).astype(BF16)
    y = _dot(h, w_ref[...]) + b_ref[...]
    offs = np.cumsum((0,) + IN_SIZES)
    q = y[:, offs[0]:offs[1]]
    k = y[:, offs[1]:offs[2]]
    v_ref[...] = y[:, offs[2]:offs[3]]
    qb_ref[...] = y[:, offs[3]:offs[4]]
    ib_ref[...] = y[:, offs[4]:offs[5]]
    ff_ref[...] = y[:, offs[5]:offs[6]]
    fb_ref[...] = y[:, offs[6]:offs[7]]
    go_ref[...] = y[:, offs[7]:offs[8]]
    is_latent = i * ROW_BLOCK >= N_PROMPT

    @pl.when(jnp.logical_not(is_latent))
    def _():
        q_ref[...] = q
        k_ref[...] = k

    @pl.when(is_latent)
    def _():
        cos = cos_ref[...]
        sin = sin_ref[...]
        for c in range(A_Q // LANES):
            q_ref[:, c * LANES:(c + 1) * LANES] = _rope(q[:, c * LANES:(c + 1) * LANES], cos, sin)
        k_ref[...] = _rope(k, cos, sin)


def inproj_even(x, mods, w_bf, b, cos, sin):
    nblk = N_TOK // ROW_BLOCK
    pos_blocks = DEC_SEQ // ROW_BLOCK

    def pos_map(i):
        return (jnp.maximum(i - N_PROMPT // ROW_BLOCK, 0) % pos_blocks, 0)

    row = lambda i: (i, 0)
    widths = (A_Q, A_KV, A_KV, B_QK, B_V, B_QK, B_QK, B_V)
    return pl.pallas_call(
        _inproj_kernel,
        grid=(nblk,),
        in_specs=[
            pl.BlockSpec((ROW_BLOCK, D_MODEL), row),
            pl.BlockSpec((1, 6, D_MODEL), lambda i: (_cond_of_block(i, ROW_BLOCK), 0, 0)),
            pl.BlockSpec((D_MODEL, D_IN_EVEN), lambda i: (0, 0)),
            pl.BlockSpec((1, D_IN_EVEN), lambda i: (0, 0)),
            pl.BlockSpec((ROW_BLOCK, LANES), pos_map),
            pl.BlockSpec((ROW_BLOCK, LANES), pos_map),
        ],
        out_specs=[pl.BlockSpec((ROW_BLOCK, w), row) for w in widths],
        out_shape=[jax.ShapeDtypeStruct((N_TOK, w), F32) for w in widths],
        compiler_params=_cparams(("parallel",)),
        name="inproj_even",
    )(x, mods, w_bf, b.reshape(1, D_IN_EVEN), cos, sin)


def _sink_attend(q, keys, vals, sink, masks):
    scores = []
    for kk, mask in zip(keys, masks):
        s = _dot_nt(q, kk) * SCALE_A
        if mask is not None:
            s = jnp.where(mask, s, MASK_VALUE)
        scores.append(s)
    m = sink
    for s in scores:
        m = jnp.maximum(m, jnp.max(s, axis=-1, keepdims=True))
    denom = jnp.exp(sink - m)
    acc = None
    for s, vv in zip(scores, vals):
        p = jnp.exp(s - m)
        denom = denom + jnp.sum(p, axis=-1, keepdims=True)
        pv = _dot(p.astype(BF16), vv)
        acc = pv if acc is None else acc + pv
    return acc / denom


def _attn_ctx_kernel(sink_ref, q_ref, k_ref, v_ref, o_ref):
    k = k_ref[...].astype(BF16)
    v = v_ref[...].astype(BF16)
    q = q_ref[...].astype(BF16)
    for h in range(N_HEADS_A):
        kv = h // GROUP_A
        qh = q[:, h * HEAD_DIM:(h + 1) * HEAD_DIM]
        kh = k[:, kv * HEAD_DIM:(kv + 1) * HEAD_DIM]
        vh = v[:, kv * HEAD_DIM:(kv + 1) * HEAD_DIM]
        o_ref[:, h * HEAD_DIM:(h + 1) * HEAD_DIM] = _sink_attend(qh, [kh], [vh], sink_ref[h], [None])


def attn_context(sink, q, k, v):
    row = lambda b: (b, 0)
    return pl.pallas_call(
        _attn_ctx_kernel,
        grid=(BATCH,),
        in_specs=[
            pl.BlockSpec(memory_space=pltpu.SMEM),
            pl.BlockSpec((SEQ, A_Q), row),
            pl.BlockSpec((SEQ, A_KV), row),
            pl.BlockSpec((SEQ, A_KV), row),
        ],
        out_specs=pl.BlockSpec((SEQ, A_Q), row),
        out_shape=jax.ShapeDtypeStruct((N_PROMPT, A_Q), F32),
        compiler_params=_cparams(("parallel",)),
        name="attn_context",
    )(sink, q, k, v)


def _attn_lat_kernel(sink_ref, q_ref, kp_ref, kc_ref, kn_ref, vp_ref, vc_ref, vn_ref, ck_ref, cv_ref, o_ref):
    n = pl.program_id(1)
    nb = DEC_SEQ // ATTN_BLOCK
    qi = lax.broadcasted_iota(jnp.int32, (ATTN_BLOCK, ATTN_BLOCK), 0)
    kj = lax.broadcasted_iota(jnp.int32, (ATTN_BLOCK, ATTN_BLOCK), 1)
    mask_prev = jnp.logical_and(kj - qi >= ATTN_BLOCK - WINDOW, n > 0)
    mask_next = jnp.logical_and(kj - qi <= WINDOW - ATTN_BLOCK, n < nb - 1)
    masks = [mask_prev, None, mask_next, None]
    q = q_ref[...].astype(BF16)
    kband = [r[...].astype(BF16) for r in (kp_ref, kc_ref, kn_ref)]
    vband = [r[...].astype(BF16) for r in (vp_ref, vc_ref, vn_ref)]
    ck = ck_ref[0].astype(BF16)
    cv = cv_ref[0].astype(BF16)
    for h in range(N_HEADS_A):
        kv = h // GROUP_A
        sl = slice(kv * HEAD_DIM, (kv + 1) * HEAD_DIM)
        qh = q[:, h * HEAD_DIM:(h + 1) * HEAD_DIM]
        keys = [kb[:, sl] for kb in kband] + [ck[:, sl]]
        vals = [vb[:, sl] for vb in vband] + [cv[:, sl]]
        o_ref[:, h * HEAD_DIM:(h + 1) * HEAD_DIM] = _sink_attend(qh, keys, vals, sink_ref[h], masks)


def attn_latent(sink, q, k, v, cache_k, cache_v):
    nb = DEC_SEQ // ATTN_BLOCK
    base = N_PROMPT // ATTN_BLOCK

    def blk(delta):
        return lambda b, n: (base + b * nb + jnp.clip(n + delta, 0, nb - 1), 0)

    kv_spec = lambda delta: pl.BlockSpec((ATTN_BLOCK, A_KV), blk(delta))
    cache_spec = pl.BlockSpec((1, PAST_LEN, A_KV), lambda b, n: (b, 0, 0))
    return pl.pallas_call(
        _attn_lat_kernel,
        grid=(DEC_BATCH, nb),
        in_specs=[
            pl.BlockSpec(memory_space=pltpu.SMEM),
            pl.BlockSpec((ATTN_BLOCK, A_Q), blk(0)),
            kv_spec(-1), kv_spec(0), kv_spec(1),
            kv_spec(-1), kv_spec(0), kv_spec(1),
            cache_spec, cache_spec,
        ],
        out_specs=pl.BlockSpec((ATTN_BLOCK, A_Q), lambda b, n: (b * nb + n, 0)),
        out_shape=jax.ShapeDtypeStruct((N_SAMPLE, A_Q), F32),
        compiler_params=_cparams(("parallel", "parallel")),
        name="attn_latent",
    )(sink, q, k, k, k, v, v, v, cache_k, cache_v)


def _hgrn_gate(f_pre, lb):
    log_f = jnp.logaddexp(jnp.log(jnp.maximum(lb, LB_FLOOR)), jnp.log1p(-lb) + jax.nn.log_sigmoid(f_pre))
    return 1.0 - jnp.exp(log_f), log_f


def _scan_rows(x, reverse):
    n = x.shape[0]
    row = lax.broadcasted_iota(jnp.int32, x.shape, 0)
    sh = 1
    while sh < n:
        if reverse:
            x = x + jnp.where(row < n - sh, pltpu.roll(x, n - sh, 0), 0.0)
        else:
            x = x + jnp.where(row >= sh, pltpu.roll(x, sh, 0), 0.0)
        sh *= 2
    return x


def _chunk_bcast(x, c, pick, shift):
    nc = HG_BLOCK // c
    rows = x.reshape(nc, c, x.shape[-1])[:, pick:pick + 1, :]
    zero = jnp.zeros((1, 1, x.shape[-1]), x.dtype)
    if shift == -1:
        rows = jnp.concatenate([zero, rows[:-1]], axis=0)
    elif shift == 1:
        rows = jnp.concatenate([rows[1:], zero], axis=0)
    return jnp.broadcast_to(rows, (nc, c, x.shape[-1])).reshape(HG_BLOCK, x.shape[-1])


def _hgrn_block(q, kk, v, log_f, st, reverse):
    cum = _scan_rows(log_f, reverse)
    tot = cum[0:1, :] if reverse else cum[HG_BLOCK - 1:HG_BLOCK, :]
    o = _dot_nt((q * jnp.exp(cum)).astype(BF16), st.astype(BF16))
    kd = (kk * jnp.exp(tot - cum)).astype(BF16)
    v_bf = v.astype(BF16)
    u_t = lax.dot_general(v_bf, kd, (((0,), (0,)), ((), ())), preferred_element_type=F32)
    st_new = st * jnp.exp(tot) + u_t
    ti = lax.broadcasted_iota(jnp.int32, (HG_BLOCK, HG_BLOCK), 0)
    si = lax.broadcasted_iota(jnp.int32, (HG_BLOCK, HG_BLOCK), 1)
    a = jnp.zeros((HG_BLOCK, HG_BLOCK), F32)
    c = HG_SUB
    while c < HG_BLOCK:
        if reverse:
            bound_t = _chunk_bcast(cum, c, 0, 1)
            bound_s = _chunk_bcast(cum, c, 0, 0)
            mask = jnp.logical_and((ti // c) % 2 == 0, si // c == ti // c + 1)
        else:
            bound_t = _chunk_bcast(cum, c, c - 1, -1)
            bound_s = _chunk_bcast(cum, c, c - 1, 0)
            mask = jnp.logical_and((ti // c) % 2 == 1, si // c == ti // c - 1)
        qc = (q * jnp.exp(cum - bound_t)).astype(BF16)
        kc = (kk * jnp.exp(bound_s - cum)).astype(BF16)
        a = a + jnp.where(mask, _dot_nt(qc, kc), 0.0)
        c *= 2
    o = o + _dot(a.astype(BF16), v_bf)
    nsub = HG_BLOCK // HG_SUB
    q3 = q.reshape(nsub, HG_SUB, HGRN_DK)
    k3 = kk.reshape(nsub, HG_SUB, HGRN_DK)
    v3 = v.reshape(nsub, HG_SUB, HGRN_DV)
    cum3 = cum.reshape(nsub, HG_SUB, HGRN_DK)
    t_off = lax.broadcasted_iota(jnp.int32, (1, HG_SUB, 1), 1)
    od = jnp.zeros((nsub, HG_SUB, HGRN_DV), F32)
    for s in range(HG_SUB):
        causal = (t_off <= s) if reverse else (t_off >= s)
        decay = jnp.exp(jnp.where(causal, cum3 - cum3[:, s:s + 1, :], MASK_VALUE))
        score = jnp.sum(q3 * decay * k3[:, s:s + 1, :], axis=-1, keepdims=True)
        od = od + score * v3[:, s:s + 1, :]
    return o + od.reshape(HG_BLOCK, HGRN_DV), st_new


def _hgrn_kernel(*refs, n_blocks, has_state_in, has_state_out):
    refs = list(refs)
    qb_ref, ib_ref, ff_ref, fb_ref, go_ref, lbf_ref, lbb_ref, nw_ref = refs[:8]
    refs = refs[8:]
    s0_ref = refs.pop(0) if has_state_in else None
    o_ref = refs.pop(0)
    so_ref = refs.pop(0) if has_state_out else None
    acc_ref = refs.pop(0)

    def run(reverse):
        f_ref, lb_ref = (fb_ref, lbb_ref) if reverse else (ff_ref, lbf_ref)
        lb = lb_ref[...]
        if has_state_in:
            st0 = s0_ref[0, 0, 1 if reverse else 0, 0].T
        else:
            st0 = jnp.zeros((HGRN_DV, HGRN_DK), F32)

        def body(it, st):
            blk = (n_blocks - 1 - it) if reverse else it
            rows = pl.ds(pl.multiple_of(blk * HG_BLOCK, HG_BLOCK), HG_BLOCK)
            qpre = qb_ref[rows, :]
            q = qpre * jax.nn.sigmoid(qpre)
            kk, log_f = _hgrn_gate(f_ref[rows, :], lb)
            o, st = _hgrn_block(q, kk, ib_ref[rows, :], log_f, st, reverse)
            if reverse:
                acc_ref[rows, :] = acc_ref[rows, :] + o
            else:
                acc_ref[rows, :] = o
            return st

        return lax.fori_loop(0, n_blocks, body, st0)

    st_f = run(False)
    st_b = run(True)
    if has_state_out:
        so_ref[0, 0, 0] = st_f.T
        so_ref[0, 1, 0] = st_b.T
    o = acc_ref[...]
    o = o * lax.rsqrt(jnp.mean(o * o, axis=-1, keepdims=True) + RMS_EPS) * nw_ref[...]
    g = go_ref[...]
    o_ref[...] = o * (g * jax.nn.sigmoid(g))


def hgrn_mixer(qb, ib, ff, fb, go, lb_f, lb_b, norm_w, *, latent, state_in=None, layer_j=0):
    t_len = DEC_SEQ if latent else SEQ
    n_seq = DEC_BATCH if latent else BATCH
    row0 = N_PROMPT // t_len if latent else 0
    tok = pl.BlockSpec((t_len, HGRN_DK), lambda b, h: (row0 + b, h))
    vec = pl.BlockSpec((1, HGRN_DK), lambda b, h: (0, h))
    in_specs = [tok] * 5 + [vec] * 3
    args = [qb, ib, ff, fb, go, lb_f.reshape(1, B_QK), lb_b.reshape(1, B_QK), norm_w.reshape(1, B_V)]
    if latent:
        in_specs.append(pl.BlockSpec((1, 1, 2, 1, HGRN_DK, HGRN_DV), lambda b, h: (b, layer_j, 0, h, 0, 0)))
        args.append(state_in)
    out_specs = [pl.BlockSpec((t_len, HGRN_DV), lambda b, h: (b, h))]
    out_shape = [jax.ShapeDtypeStruct((n_seq * t_len, B_V), F32)]
    if not latent:
        out_specs.append(pl.BlockSpec((1, 2, 1, HGRN_DK, HGRN_DV), lambda b, h: (b, 0, h, 0, 0)))
        out_shape.append(jax.ShapeDtypeStruct((BATCH, 2, N_HEADS_B, HGRN_DK, HGRN_DV), F32))
    kern = functools.partial(_hgrn_kernel, n_blocks=t_len // HG_BLOCK, has_state_in=latent, has_state_out=not latent)
    return pl.pallas_call(
        kern,
        grid=(n_seq, N_HEADS_B),
        in_specs=in_specs,
        out_specs=out_specs,
        out_shape=out_shape,
        scratch_shapes=[pltpu.VMEM((t_len, HGRN_DV), F32)],
        compiler_params=_cparams(("parallel", "parallel")),
        name="hgrn_latent" if latent else "hgrn_context",
    )(*args)


def _outproj_kernel(x_ref, mod_ref, ap_ref, as_ref, rp_ref, rs_ref, w_ref, g_ref, b_ref, o_ref):
    is_latent = pl.program_id(0) * ROW_BLOCK >= N_PROMPT
    attn = jnp.where(is_latent, as_ref[...], ap_ref[...]).astype(BF16)
    rec = jnp.where(is_latent, rs_ref[...], rp_ref[...]).astype(BF16)
    y = _dot(attn, w_ref[0:A_Q, :]) + _dot(rec, w_ref[A_Q:A_Q + B_V, :])
    z = DN_ALPHA * x_ref[...] + mod_ref[0, 2:3, :] * y
    o_ref[...] = _layer_norm(z, g_ref[...], b_ref[...])


def outproj_even(x, mods, attn_p, attn_s, rec_p, rec_s, w_bf, ln_g, ln_b):
    nblk = N_TOK // ROW_BLOCK
    npb = N_PROMPT // ROW_BLOCK
    row = lambda i: (i, 0)
    prow = lambda i: (jnp.minimum(i, npb - 1), 0)
    srow = lambda i: (jnp.maximum(i - npb, 0), 0)
    vec = pl.BlockSpec((1, D_MODEL), lambda i: (0, 0))
    return pl.pallas_call(
        _outproj_kernel,
        grid=(nblk,),
        in_specs=[
            pl.BlockSpec((ROW_BLOCK, D_MODEL), row),
            pl.BlockSpec((1, 6, D_MODEL), lambda i: (_cond_of_block(i, ROW_BLOCK), 0, 0)),
            pl.BlockSpec((ROW_BLOCK, A_Q), prow),
            pl.BlockSpec((ROW_BLOCK, A_Q), srow),
            pl.BlockSpec((ROW_BLOCK, B_V), prow),
            pl.BlockSpec((ROW_BLOCK, B_V), srow),
            pl.BlockSpec((A_Q + B_V, D_MODEL), lambda i: (0, 0)),
            vec, vec,
        ],
        out_specs=pl.BlockSpec((ROW_BLOCK, D_MODEL), row),
        out_shape=jax.ShapeDtypeStruct((N_TOK, D_MODEL), F32),
        compiler_params=_cparams(("parallel",)),
        name="outproj_even",
    )(x, mods, attn_p, attn_s, rec_p, rec_s, w_bf, ln_g.reshape(1, D_MODEL), ln_b.reshape(1, D_MODEL))


def _conv_in_kernel(x_ref, mod_ref, w_ref, b_ref, u_ref):
    h = (x_ref[...] * (1.0 + mod_ref[0, 1:2, :]) + mod_ref[0, 0:1, :]).astype(BF16)
    a = _dot(h, w_ref[:, 0:D_MODEL]) + b_ref[:, 0:D_MODEL]
    gt = _dot(h, w_ref[:, D_MODEL:2 * D_MODEL]) + b_ref[:, D_MODEL:2 * D_MODEL]
    u_ref[...] = a * jax.nn.sigmoid(gt)


def conv_in(x, mods, w_bf, b):
    row = lambda i: (i, 0)
    return pl.pallas_call(
        _conv_in_kernel,
        grid=(N_TOK // ROW_BLOCK,),
        in_specs=[
            pl.BlockSpec((ROW_BLOCK, D_MODEL), row),
            pl.BlockSpec((1, 6, D_MODEL), lambda i: (_cond_of_block(i, ROW_BLOCK), 0, 0)),
            pl.BlockSpec((D_MODEL, 2 * D_MODEL), lambda i: (0, 0)),
            pl.BlockSpec((1, 2 * D_MODEL), lambda i: (0, 0)),
        ],
        out_specs=pl.BlockSpec((ROW_BLOCK, D_MODEL), row),
        out_shape=jax.ShapeDtypeStruct((N_TOK, D_MODEL), F32),
        compiler_params=_cparams(("parallel",)),
        name="conv_in",
    )(x, mods, w_bf, b.reshape(1, 2 * D_MODEL))


CONV_HALO = 16
CONV_LANES = 256


CONV_SHIFT_ROWS = ROW_BLOCK + 2 * CONV_HALO - SUBLANES
CONV_ROW_CHUNK = 128


def _conv_out_kernel(x_ref, mod_ref, up_ref, uc_ref, un_ref, dw_ref, dwb_ref, cg_ref, cb_ref,
                     w_ref, b_ref, g_ref, bb_ref, o_ref, pad_ref, acc_ref, sh_ref):
    i = pl.program_id(0)
    blocks_per_seq = DEC_SEQ // ROW_BLOCK
    j = i - N_PROMPT // ROW_BLOCK
    is_latent = j >= 0
    has_prev = jnp.logical_and(is_latent, j % blocks_per_seq != 0)
    has_next = jnp.logical_and(is_latent, j % blocks_per_seq != blocks_per_seq - 1)
    pad_ref[0:CONV_HALO, :] = jnp.where(has_prev, up_ref[...], 0.0)
    pad_ref[CONV_HALO:CONV_HALO + ROW_BLOCK, :] = uc_ref[...]
    pad_ref[CONV_HALO + ROW_BLOCK:, :] = jnp.where(has_next, un_ref[...], 0.0)
    first = CONV_HALO - CONV_WIDTH // 2
    for c in range(D_MODEL // CONV_LANES):
        lanes = slice(c * CONV_LANES, (c + 1) * CONV_LANES)
        for s in range(SUBLANES):
            sh_ref[s] = pad_ref[s:s + CONV_SHIFT_ROWS, lanes]
        for r0 in range(0, ROW_BLOCK, CONV_ROW_CHUNK):
            acc = jnp.zeros((CONV_ROW_CHUNK, CONV_LANES), F32) + dwb_ref[:, lanes]
            for tap in range(CONV_WIDTH):
                whole, s = divmod(first + tap, SUBLANES)
                rows = slice(r0 + whole * SUBLANES, r0 + whole * SUBLANES + CONV_ROW_CHUNK)
                acc = acc + sh_ref[s, rows, :] * dw_ref[tap:tap + 1, lanes]
            acc_ref[r0:r0 + CONV_ROW_CHUNK, lanes] = acc
    u = _layer_norm(acc_ref[...], cg_ref[...], cb_ref[...])
    u = (u * jax.nn.sigmoid(u)).astype(BF16)
    y = _dot(u, w_ref[...]) + b_ref[...]
    z = DN_ALPHA * x_ref[...] + mod_ref[0, 2:3, :] * y
    o_ref[...] = _layer_norm(z, g_ref[...], bb_ref[...])


def conv_out(x, mods, u, dw, dw_b, cln_g, cln_b, w_bf, b_out, ln_g, ln_b):
    nblk = N_TOK // ROW_BLOCK
    ratio = ROW_BLOCK // CONV_HALO
    nhalo = N_TOK // CONV_HALO
    row = lambda i: (i, 0)
    vec = pl.BlockSpec((1, D_MODEL), lambda i: (0, 0))
    r1 = lambda a: a.reshape(1, D_MODEL)
    return pl.pallas_call(
        _conv_out_kernel,
        grid=(nblk,),
        in_specs=[
            pl.BlockSpec((ROW_BLOCK, D_MODEL), row),
            pl.BlockSpec((1, 6, D_MODEL), lambda i: (_cond_of_block(i, ROW_BLOCK), 0, 0)),
            pl.BlockSpec((CONV_HALO, D_MODEL), lambda i: (jnp.maximum(i * ratio - 1, 0), 0)),
            pl.BlockSpec((ROW_BLOCK, D_MODEL), row),
            pl.BlockSpec((CONV_HALO, D_MODEL), lambda i: (jnp.minimum((i + 1) * ratio, nhalo - 1), 0)),
            pl.BlockSpec((CONV_WIDTH, D_MODEL), lambda i: (0, 0)),
            vec, vec, vec,
            pl.BlockSpec((D_MODEL, D_MODEL), lambda i: (0, 0)),
            vec, vec, vec,
        ],
        out_specs=pl.BlockSpec((ROW_BLOCK, D_MODEL), row),
        out_shape=jax.ShapeDtypeStruct((N_TOK, D_MODEL), F32),
        scratch_shapes=[pltpu.VMEM((ROW_BLOCK + 2 * CONV_HALO, D_MODEL), F32),
                        pltpu.VMEM((ROW_BLOCK, D_MODEL), F32),
                        pltpu.VMEM((SUBLANES, CONV_SHIFT_ROWS, CONV_LANES), F32)],
        compiler_params=_cparams(("parallel",)),
        name="conv_out",
    )(x, mods, u, u, u, dw, r1(dw_b), r1(cln_g), r1(cln_b), w_bf, r1(b_out), r1(ln_g), r1(ln_b))


def _modulate2(x_ref, mod_ref):
    return x_ref[...] * (1.0 + mod_ref[0, 4:5, :]) + mod_ref[0, 3:4, :]


def _route_kernel(x_ref, mod_ref, wr_ref, br_ref, idx_ref, gate_ref, rank_ref, cnt_ref, carry_ref):
    i = pl.program_id(0)

    @pl.when(i == 0)
    def _():
        carry_ref[...] = jnp.zeros_like(carry_ref)

    h = _modulate2(x_ref, mod_ref)
    logits = lax.dot_general(wr_ref[...], h, (((1,), (1,)), ((), ())), precision=HIGHEST,
                             preferred_element_type=F32) + br_ref[...]
    eidx = lax.broadcasted_iota(jnp.int32, logits.shape, 0)
    vals = logits
    sels, tops = [], []
    for k in range(TOP_K):
        m = jnp.max(vals, axis=0, keepdims=True)
        idx = jnp.min(jnp.where(vals == m, eidx, N_EXPERTS), axis=0, keepdims=True)
        sel = eidx == idx
        idx_ref[k:k + 1, :] = idx
        sels.append(sel)
        tops.append(m)
        vals = jnp.where(sel, -jnp.inf, vals)
    exps = [jnp.exp(t - tops[0]) for t in tops]
    total = exps[0] + exps[1] + exps[2] + exps[3]
    for k in range(TOP_K):
        gate_ref[k:k + 1, :] = exps[k] / total
    onehot = jnp.zeros(logits.shape, F32)
    for sel in sels:
        onehot = onehot + sel.astype(F32)
    ta = lax.broadcasted_iota(jnp.int32, (ROW_BLOCK, ROW_BLOCK), 0)
    tb = lax.broadcasted_iota(jnp.int32, (ROW_BLOCK, ROW_BLOCK), 1)
    before = _dot(onehot.astype(BF16), (ta < tb).astype(BF16)) + carry_ref[:, 0:1]
    for k in range(TOP_K):
        rank = jnp.sum(jnp.where(sels[k], before, 0.0), axis=0, keepdims=True)
        rank_ref[k:k + 1, :] = rank.astype(jnp.int32)
    carry = carry_ref[...] + jnp.sum(onehot, axis=1, keepdims=True)
    carry_ref[...] = carry
    cnt_ref[...] = carry.astype(jnp.int32)


def moe_route(x, mods, wr_t, b_r):
    tok = pl.BlockSpec((TOP_K, ROW_BLOCK), lambda i: (0, i))
    return pl.pallas_call(
        _route_kernel,
        grid=(N_TOK // ROW_BLOCK,),
        in_specs=[
            pl.BlockSpec((ROW_BLOCK, D_MODEL), lambda i: (i, 0)),
            pl.BlockSpec((1, 6, D_MODEL), lambda i: (_cond_of_block(i, ROW_BLOCK), 0, 0)),
            pl.BlockSpec((N_EXPERTS, D_MODEL), lambda i: (0, 0)),
            pl.BlockSpec((N_EXPERTS, 1), lambda i: (0, 0)),
        ],
        out_specs=[tok, tok, tok, pl.BlockSpec((N_EXPERTS, LANES), lambda i: (0, 0))],
        out_shape=[jax.ShapeDtypeStruct((TOP_K, N_TOK), jnp.int32),
                   jax.ShapeDtypeStruct((TOP_K, N_TOK), F32),
                   jax.ShapeDtypeStruct((TOP_K, N_TOK), jnp.int32),
                   jax.ShapeDtypeStruct((N_EXPERTS, LANES), jnp.int32)],
        scratch_shapes=[pltpu.VMEM((N_EXPERTS, LANES), F32)],
        compiler_params=_cparams(("arbitrary",)),
        name="moe_route",
    )(x, mods, wr_t, b_r.reshape(N_EXPERTS, 1))


HALF_D = D_MODEL // 2
HI_MASK = 0xFFFF0000


def _pack_bf16_pairs(h):
    half = h.shape[1] // 2
    lo = lax.bitcast_convert_type(h[:, :half].astype(BF16).astype(F32), jnp.uint32)
    hi = lax.bitcast_convert_type(h[:, half:].astype(BF16).astype(F32), jnp.uint32)
    return (lo >> 16) | (hi & jnp.uint32(HI_MASK))


def _unpack_bf16_pairs(w):
    lo = lax.bitcast_convert_type(w << 16, F32).astype(BF16)
    hi = lax.bitcast_convert_type(w & jnp.uint32(HI_MASK), F32).astype(BF16)
    return lo, hi


ROW_PARTS = HALF_D // LANES


def _rows_to_parts(ref_view, packed):
    for j in range(ROW_PARTS):
        ref_view[:, j, :] = packed[:, j * LANES:(j + 1) * LANES]


def _parts_to_rows(ref_view):
    return jnp.concatenate([ref_view[:, j, :] for j in range(ROW_PARTS)], axis=1)


def _scatter_kernel(dest_ref, zstart_ref, x_ref, mod_ref, xs_ref, pk_ref, zero_ref, sem, zsem):
    i = pl.program_id(0)

    def zero_copy(start):
        start = pl.multiple_of(start, MOE_ROWS)
        return pltpu.make_async_copy(zero_ref, xs_ref.at[pl.ds(start, MOE_ROWS)], zsem)

    def zero_blocks(fn):
        for e in range(N_EXPERTS):
            @pl.when(zstart_ref[e] >= 0)
            def _():
                fn(zero_copy(jnp.maximum(zstart_ref[e], 0)))

        def unused(b, carry):
            fn(zero_copy(b * MOE_ROWS))
            return carry

        lax.fori_loop(zstart_ref[N_EXPERTS], MOE_BLOCKS, unused, 0)

    @pl.when(i == 0)
    def _():
        zero_ref[...] = jnp.zeros_like(zero_ref)
        zero_blocks(lambda cp: cp.start())
        zero_blocks(lambda cp: cp.wait())

    base = i * ROW_BLOCK
    last = pl.num_programs(0) - 1

    def drain(slot):
        for k in range(TOP_K):
            pltpu.make_async_copy(pk_ref.at[slot], xs_ref.at[pl.ds(0, ROW_BLOCK)], sem.at[slot]).wait()

    for slot in range(2):
        @pl.when(i % 2 == slot)
        def _():
            @pl.when(i >= 2)
            def _():
                drain(slot)

            _rows_to_parts(pk_ref.at[slot], _pack_bf16_pairs(_modulate2(x_ref, mod_ref)))
            for t in range(ROW_BLOCK):
                for k in range(TOP_K):
                    row = dest_ref[k * N_TOK + base + t]
                    pltpu.make_async_copy(pk_ref.at[slot, t], xs_ref.at[row], sem.at[slot]).start(priority=k % 2)

            @pl.when(i == last)
            def _():
                drain(slot)

                @pl.when(i >= 1)
                def _():
                    drain(1 - slot)


def moe_scatter(dest_flat, zstart, x, mods):
    return pl.pallas_call(
        _scatter_kernel,
        grid_spec=pltpu.PrefetchScalarGridSpec(
            num_scalar_prefetch=2,
            grid=(N_TOK // ROW_BLOCK,),
            in_specs=[
                pl.BlockSpec((ROW_BLOCK, D_MODEL), lambda i, d, z: (i, 0)),
                pl.BlockSpec((1, 6, D_MODEL), lambda i, d, z: (_cond_of_block(i, ROW_BLOCK), 0, 0)),
            ],
            out_specs=pl.BlockSpec(memory_space=pl.ANY),
            scratch_shapes=[pltpu.VMEM((2, ROW_BLOCK, ROW_PARTS, LANES), jnp.uint32),
                            pltpu.VMEM((MOE_ROWS, ROW_PARTS, LANES), jnp.uint32),
                            pltpu.SemaphoreType.DMA((2,)), pltpu.SemaphoreType.DMA],
        ),
        out_shape=jax.ShapeDtypeStruct((MOE_R, ROW_PARTS, LANES), jnp.uint32),
        compiler_params=_cparams(("arbitrary",)),
        name="moe_scatter",
    )(dest_flat, zstart, x, mods)


N_CHUNK = 256


def _expert_kernel(be_ref, nused_ref, nexte_ref, par_ref, xs_ref, wg_hbm, wu_hbm, wd_hbm, bg_ref, bu_ref, bd_ref,
                   ys_ref, wbuf, act_ref, wg_bf, wu_bf, wd_bf, wsem, *, layer):
    i = pl.program_id(0)
    new_expert = jnp.logical_or(i == 0, be_ref[i] != be_ref[jnp.maximum(i - 1, 0)])
    slot = par_ref[i]

    def weight_copies(e, s):
        return [pltpu.make_async_copy(w.at[layer, e], wbuf.at[s, j], wsem.at[s])
                for j, w in enumerate((wg_hbm, wu_hbm, wd_hbm))]

    @pl.when(i == 0)
    def _():
        for cp in weight_copies(be_ref[0], 0):
            cp.start()

    @pl.when(jnp.logical_and(new_expert, i < nused_ref[0]))
    def _():
        for cp in weight_copies(be_ref[i], slot):
            cp.wait()

        @pl.when(nexte_ref[i] >= 0)
        def _():
            for cp in weight_copies(jnp.maximum(nexte_ref[i], 0), 1 - slot):
                cp.start()

        wg_bf[...] = wbuf[slot, 0].astype(BF16)
        wu_bf[...] = wbuf[slot, 1].astype(BF16)
        wd_bf[...] = wbuf[slot, 2].astype(BF16)

    @pl.when(i < nused_ref[0])
    def _():
        lo, hi = _unpack_bf16_pairs(_parts_to_rows(xs_ref))
        x = jnp.concatenate([lo, hi], axis=1)
        for n in range(D_EXPERT // N_CHUNK):
            cols = slice(n * N_CHUNK, (n + 1) * N_CHUNK)
            gt = _dot(x, wg_bf[:, cols]) + bg_ref[0, 0, :, cols]
            up = _dot(x, wu_bf[:, cols]) + bu_ref[0, 0, :, cols]
            gt = jnp.minimum(gt, SWIGLU_LIMIT)
            up = jnp.clip(up, -SWIGLU_LIMIT, SWIGLU_LIMIT)
            act_ref[:, cols] = ((up + 1.0) * gt * jax.nn.sigmoid(SWIGLU_ALPHA * gt)).astype(BF16)
        act = act_ref[...]
        for n in range(HALF_D // N_CHUNK):
            cols = slice(n * N_CHUNK, (n + 1) * N_CHUNK)
            cols_hi = slice(HALF_D + n * N_CHUNK, HALF_D + (n + 1) * N_CHUNK)
            y_lo = _dot(act, wd_bf[:, cols]) + bd_ref[0, 0, :, cols]
            y_hi = _dot(act, wd_bf[:, cols_hi]) + bd_ref[0, 0, :, cols_hi]
            packed = _pack_bf16_pairs(jnp.concatenate([y_lo, y_hi], axis=1))
            for p in range(N_CHUNK // LANES):
                ys_ref[:, n * (N_CHUNK // LANES) + p, :] = packed[:, p * LANES:(p + 1) * LANES]

    @pl.when(i >= nused_ref[0])
    def _():
        ys_ref[...] = jnp.zeros_like(ys_ref)


def moe_experts(block_e, n_used, next_e, parity, xs, w_g, w_u, w_d, b_g, b_u, b_d, layer):
    wspec = pl.BlockSpec(memory_space=pl.ANY)
    bspec = pl.BlockSpec((1, 1, 1, D_EXPERT), lambda i, be, nu, ne, pa: (layer, be[i], 0, 0))
    r4 = lambda b: b.reshape(DEPTH, N_EXPERTS, 1, D_EXPERT)
    return pl.pallas_call(
        functools.partial(_expert_kernel, layer=layer),
        grid_spec=pltpu.PrefetchScalarGridSpec(
            num_scalar_prefetch=4,
            grid=(MOE_BLOCKS,),
            in_specs=[pl.BlockSpec((MOE_ROWS, ROW_PARTS, LANES), lambda i, be, nu, ne, pa: (i, 0, 0)),
                      wspec, wspec, wspec, bspec, bspec, bspec],
            out_specs=pl.BlockSpec((MOE_ROWS, ROW_PARTS, LANES), lambda i, be, nu, ne, pa: (i, 0, 0)),
            scratch_shapes=[pltpu.VMEM((2, 3, D_MODEL, D_EXPERT), F32),
                            pltpu.VMEM((MOE_ROWS, D_EXPERT), BF16)]
                           + [pltpu.VMEM((D_MODEL, D_EXPERT), BF16)] * 3
                           + [pltpu.SemaphoreType.DMA((2,))],
        ),
        out_shape=jax.ShapeDtypeStruct((MOE_R, ROW_PARTS, LANES), jnp.uint32),
        compiler_params=_cparams(("arbitrary",)),
        name="moe_experts",
    )(block_e, n_used, next_e, parity, xs, w_g, w_u, w_d, r4(b_g), r4(b_u), r4(b_d))


def _combine_kernel(dest_ref, x_ref, mod_ref, gate_ref, ys_ref, g_ref, b_ref, o_ref, buf_ref, sem):
    i = pl.program_id(0)
    last = pl.num_programs(0) - 1

    def gather(step, slot):
        base = step * COMB_TOK
        for t in range(COMB_TOK):
            for k in range(TOP_K):
                row = dest_ref[k * N_TOK + base + t]
                pltpu.make_async_copy(ys_ref.at[row], buf_ref.at[slot, k, t], sem.at[slot]).start(priority=k % 2)

    @pl.when(i == 0)
    def _():
        gather(0, 0)

    for slot in range(2):
        @pl.when(jnp.logical_and(i < last, i % 2 == slot))
        def _():
            gather(i + 1, 1 - slot)

    eye = (lax.broadcasted_iota(jnp.int32, (COMB_TOK, COMB_TOK), 0)
           == lax.broadcasted_iota(jnp.int32, (COMB_TOK, COMB_TOK), 1))
    gates = gate_ref[...]
    cols = [jnp.sum(jnp.where(eye, gates[k:k + 1, :], 0.0), axis=1, keepdims=True) for k in range(TOP_K)]
    cur = i % 2
    for k in range(TOP_K):
        pltpu.make_async_copy(ys_ref.at[pl.ds(0, COMB_TOK)], buf_ref.at[cur, k], sem.at[cur]).wait()
    y_lo = y_hi = None
    for k in range(TOP_K):
        w = _parts_to_rows(buf_ref.at[cur, k])
        lo = cols[k] * lax.bitcast_convert_type(w << 16, F32)
        hi = cols[k] * lax.bitcast_convert_type(w & jnp.uint32(HI_MASK), F32)
        y_lo = lo if y_lo is None else y_lo + lo
        y_hi = hi if y_hi is None else y_hi + hi
    y = jnp.concatenate([y_lo, y_hi], axis=1)
    z = DN_ALPHA * x_ref[...] + mod_ref[0, 5:6, :] * y
    o_ref[...] = _layer_norm(z, g_ref[...], b_ref[...])


def moe_combine(dest_flat, x, mods, gates_t, ys, ln_g, ln_b):
    vec = pl.BlockSpec((1, D_MODEL), lambda i, d: (0, 0))
    return pl.pallas_call(
        _combine_kernel,
        grid_spec=pltpu.PrefetchScalarGridSpec(
            num_scalar_prefetch=1,
            grid=(N_TOK // COMB_TOK,),
            in_specs=[
                pl.BlockSpec((COMB_TOK, D_MODEL), lambda i, d: (i, 0)),
                pl.BlockSpec((1, 6, D_MODEL), lambda i, d: (_cond_of_block(i, COMB_TOK), 0, 0)),
                pl.BlockSpec((TOP_K, COMB_TOK), lambda i, d: (0, i)),
                pl.BlockSpec(memory_space=pl.ANY),
                vec, vec,
            ],
            out_specs=pl.BlockSpec((COMB_TOK, D_MODEL), lambda i, d: (i, 0)),
            scratch_shapes=[pltpu.VMEM((2, TOP_K, COMB_TOK, ROW_PARTS, LANES), jnp.uint32),
                            pltpu.SemaphoreType.DMA((2,))],
        ),
        out_shape=jax.ShapeDtypeStruct((N_TOK, D_MODEL), F32),
        compiler_params=_cparams(("arbitrary",)),
        name="moe_combine",
    )(dest_flat, x, mods, gates_t, ys, ln_g.reshape(1, D_MODEL), ln_b.reshape(1, D_MODEL))


def moe_layer(x, mods, layer, router_w, router_b, w_g, b_g, w_u, b_u, w_d, b_d, ln_g, ln_b):
    idx_t, gates_t, rank_t, counts = moe_route(x, mods, router_w[layer].T, router_b[layer])
    counts = counts[:, 0]
    padded = (counts + MOE_ROWS - 1) // MOE_ROWS * MOE_ROWS
    ends = jnp.cumsum(padded)
    base = ends - padded
    n_used = (ends[-1] // MOE_ROWS).astype(jnp.int32)
    block_start = jnp.arange(MOE_BLOCKS, dtype=jnp.int32) * MOE_ROWS
    block_e = jnp.sum(block_start[:, None] >= ends[None, :], axis=1).astype(jnp.int32)
    used = padded > 0
    experts = jnp.arange(N_EXPERTS, dtype=jnp.int32)
    block_e = jnp.minimum(block_e, jnp.max(jnp.where(used, experts, 0)))
    zstart = jnp.where(padded > 0, ends - MOE_ROWS, -1).astype(jnp.int32)
    zstart = jnp.concatenate([zstart, n_used.reshape(1)])
    onehot = idx_t[:, :, None] == jnp.arange(N_EXPERTS, dtype=jnp.int32)[None, None, :]
    dest = rank_t + jnp.sum(jnp.where(onehot, base[None, None, :], 0), axis=-1)
    dest_flat = dest.reshape(-1).astype(jnp.int32)
    xs = moe_scatter(dest_flat, zstart, x, mods)
    later_used = jnp.logical_and(used[None, :], experts[None, :] > experts[:, None])
    next_used = jnp.min(jnp.where(later_used, experts[None, :], N_EXPERTS), axis=1)
    next_used = jnp.where(next_used == N_EXPERTS, -1, next_used).astype(jnp.int32)
    run_parity = ((jnp.cumsum(used.astype(jnp.int32)) - 1) % 2).astype(jnp.int32)
    of_block = block_e[:, None] == experts[None, :]
    block_next = jnp.sum(jnp.where(of_block, next_used[None, :], 0), axis=1).astype(jnp.int32)
    block_parity = jnp.sum(jnp.where(of_block, run_parity[None, :], 0), axis=1).astype(jnp.int32)
    ys = moe_experts(block_e, n_used.reshape(1), block_next, block_parity, xs,
                     w_g, w_u, w_d, b_g, b_u, b_d, layer)
    return moe_combine(dest_flat, x, mods, gates_t, ys, ln_g, ln_b)


def kernel(x_prompt, x_sample, c, cache_k, cache_v, state_hgrn, c_ctx, w_ada, b_ada, ln_g, ln_b,
           w_in_even, b_in_even, attn_sink, hgrn_lb, hgrn_norm, w_out_even,
           conv_w_in, conv_b_in, conv_dw, conv_dw_b, conv_ln_g, conv_ln_b, conv_w_out, conv_b_out,
           router_w, router_b, moe_w_gate, moe_b_gate, moe_w_up, moe_b_up, moe_w_down, moe_b_down):
    x = jnp.concatenate([x_prompt.reshape(N_PROMPT, D_MODEL), x_sample.reshape(N_SAMPLE, D_MODEL)], axis=0)
    cond = jnp.concatenate([c_ctx[None, :], c, jnp.zeros((COND_ROWS - N_COND, D_MODEL), F32)], axis=0)
    mods_all = adaln_all(cond, w_ada, b_ada).reshape(DEPTH, COND_ROWS, 6, D_MODEL)
    lb = jax.nn.softmax(hgrn_lb.astype(F32), axis=1)
    lb = jnp.cumsum(lb, axis=1) - lb[:, :1]
    cos, sin = _rope_tables()
    new_k, new_v, new_s = [], [], []
    for layer in range(DEPTH):
        j = layer // 2
        mods = mods_all[layer]
        if layer % 2 == 0:
            q, k, v, qb, ib, ff, fb, go = inproj_even(x, mods, w_in_even[j].astype(BF16), b_in_even[j], cos, sin)
            new_k.append(k[:N_PROMPT].reshape(BATCH, SEQ, N_KV_A, HEAD_DIM))
            new_v.append(v[:N_PROMPT].reshape(BATCH, SEQ, N_KV_A, HEAD_DIM))
            attn_p = attn_context(attn_sink[j], q, k, v)
            attn_s = attn_latent(attn_sink[j], q, k, v,
                                 cache_k[:, j].reshape(DEC_BATCH, PAST_LEN, A_KV),
                                 cache_v[:, j].reshape(DEC_BATCH, PAST_LEN, A_KV))
            hg = (qb, ib, ff, fb, go, lb[0, j], lb[1, j], hgrn_norm[j])
            rec_p, states = hgrn_mixer(*hg, latent=False)
            rec_s, = hgrn_mixer(*hg, latent=True, state_in=state_hgrn, layer_j=j)
            new_s.append(states)
            x = outproj_even(x, mods, attn_p, attn_s, rec_p, rec_s, w_out_even[j].astype(BF16),
                             ln_g[layer, 0], ln_b[layer, 0])
        else:
            u = conv_in(x, mods, conv_w_in[j].astype(BF16), conv_b_in[j])
            x = conv_out(x, mods, u, conv_dw[j], conv_dw_b[j], conv_ln_g[j], conv_ln_b[j],
                         conv_w_out[j].astype(BF16), conv_b_out[j], ln_g[layer, 0], ln_b[layer, 0])
        x = moe_layer(x, mods, layer, router_w, router_b, moe_w_gate, moe_b_gate, moe_w_up, moe_b_up,
                      moe_w_down, moe_b_down, ln_g[layer, 1], ln_b[layer, 1])
    return (x[:N_PROMPT].reshape(BATCH, SEQ, D_MODEL),
            x[N_PROMPT:].reshape(DEC_BATCH, DEC_SEQ, D_MODEL),
            jnp.stack(new_k, axis=1), jnp.stack(new_v, axis=1), jnp.stack(new_s, axis=1))
```

```python
import functools

import jax
import jax.numpy as jnp
import numpy as np
from jax import lax
from jax.experimental import pallas as pl
from jax.experimental.pallas import tpu as pltpu

D_MODEL = 1024
BATCH = 16
SEQ = 256
DEPTH = 4
DEC_BATCH = 4
DEC_SEQ = 2048
PAST_LEN = 512
GRID_W = 64
N_EVEN = (DEPTH + 1) // 2
N_ODD = DEPTH // 2
HEAD_DIM = 64
N_HEADS_A = 8
N_KV_A = 2
GROUP_A = N_HEADS_A // N_KV_A
WINDOW = 128
ATTN_BLOCK = 128
SCALE_A = HEAD_DIM ** -0.5
ROPE_BASE = 10000.0
ROPE_PAIRS = HEAD_DIM // 4
N_HEADS_B = 4
HGRN_DK = 128
HGRN_DV = 128
CONV_WIDTH = 31
N_EXPERTS = 32
TOP_K = 4
D_EXPERT = D_MODEL
SWIGLU_LIMIT = 7.0
SWIGLU_ALPHA = 1.702
LN_EPS = 1e-5
RMS_EPS = 1e-6
MASK_VALUE = -1e9
LB_FLOOR = 1e-30
DN_ALPHA = (2 * DEPTH) ** 0.25
A_Q = N_HEADS_A * HEAD_DIM
A_KV = N_KV_A * HEAD_DIM
B_QK = N_HEADS_B * HGRN_DK
B_V = N_HEADS_B * HGRN_DV
IN_SIZES = (A_Q, A_KV, A_KV, B_QK, B_V, B_QK, B_QK, B_V)
D_IN_EVEN = sum(IN_SIZES)

N_PROMPT = BATCH * SEQ
N_SAMPLE = DEC_BATCH * DEC_SEQ
N_TOK = N_PROMPT + N_SAMPLE
N_COND = 1 + DEC_BATCH
COND_ROWS = 8

LANES = 128
SUBLANES = 8
VMEM_LIMIT = 56 * 1024 * 1024

ROW_BLOCK = 256
DENSE_ROWS = 512
HG_BLOCK = 128
HG_SUB = 8
MOE_ROWS = 256
MOE_BLOCKS = (N_TOK * TOP_K + N_EXPERTS * (MOE_ROWS - 1)) // MOE_ROWS + 1
MOE_R = MOE_BLOCKS * MOE_ROWS
COMB_TOK = 128

F32 = jnp.float32
BF16 = jnp.bfloat16
HIGHEST = lax.Precision.HIGHEST


def _cond_of_block(i, rows):
    start = i * rows
    return jnp.where(start < N_PROMPT, 0, 1 + (start - N_PROMPT) // DEC_SEQ)


def _cparams(sem):
    return pltpu.CompilerParams(dimension_semantics=sem, vmem_limit_bytes=VMEM_LIMIT)


def _layer_norm(z, g, b):
    mu = jnp.mean(z, axis=-1, keepdims=True)
    zc = z - mu
    var = jnp.mean(zc * zc, axis=-1, keepdims=True)
    return zc * lax.rsqrt(var + LN_EPS) * g + b


def _dot(a, b):
    return jnp.dot(a, b, preferred_element_type=F32)


def _dot_nt(a, b):
    return lax.dot_general(a, b, (((1,), (1,)), ((), ())), preferred_element_type=F32)


ADA_TN = 1536


def _adaln_kernel(cond_ref, w_ref, b_ref, o_ref):
    c = cond_ref[...]
    s = c * jax.nn.sigmoid(c)
    o_ref[0] = jnp.dot(s, w_ref[0], precision=HIGHEST, preferred_element_type=F32) + b_ref[0]


def adaln_all(cond, w_ada, b_ada):
    n_out = 6 * D_MODEL
    return pl.pallas_call(
        _adaln_kernel,
        grid=(DEPTH, n_out // ADA_TN),
        in_specs=[
            pl.BlockSpec((COND_ROWS, D_MODEL), lambda l, n: (0, 0)),
            pl.BlockSpec((1, D_MODEL, ADA_TN), lambda l, n: (l, 0, n)),
            pl.BlockSpec((1, 1, ADA_TN), lambda l, n: (l, 0, n)),
        ],
        out_specs=pl.BlockSpec((1, COND_ROWS, ADA_TN), lambda l, n: (l, 0, n)),
        out_shape=jax.ShapeDtypeStruct((DEPTH, COND_ROWS, n_out), F32),
        compiler_params=_cparams(("parallel", "parallel")),
        name="adaln",
    )(cond, w_ada, b_ada.reshape(DEPTH, 1, n_out))


def _rope_tables():
    t = np.arange(DEC_SEQ)
    d = np.arange(LANES) % HEAD_DIM
    axis = d // (2 * ROPE_PAIRS)
    half = (d // ROPE_PAIRS) % 2
    pair = d % ROPE_PAIRS
    pos = jnp.where(axis[None, :] == 0, (t // GRID_W)[:, None], (t % GRID_W)[:, None]).astype(F32)
    inv_freq = ROPE_BASE ** (-jnp.arange(ROPE_PAIRS, dtype=F32) / ROPE_PAIRS)
    ang = pos * inv_freq[pair][None, :]
    sign = jnp.where(half[None, :] == 0, -1.0, 1.0).astype(F32)
    return jnp.cos(ang), jnp.sin(ang) * sign


def _rope(x, cos, sin_signed):
    lane = lax.broadcasted_iota(jnp.int32, x.shape, 1)
    first_half = (lane // ROPE_PAIRS) % 2 == 0
    partner = jnp.where(first_half, pltpu.roll(x, LANES - ROPE_PAIRS, 1), pltpu.roll(x, ROPE_PAIRS, 1))
    return x * cos + partner * sin_signed


def _inproj_kernel(x_ref, mod_ref, w_ref, b_ref, cos_ref, sin_ref,
                   q_ref, k_ref, v_ref, qb_ref, ib_ref, ff_ref, fb_ref, go_ref):
    i = pl.program_id(0)
    shift = mod_ref[0, 0:1, :]
    scale = mod_ref[0, 1:2, :]
    h = (x_ref[...] * (1.0 + scale) + shift).astype(BF16)
    y = _dot(h, w_ref[...]) + b_ref[...]
    offs = np.cumsum((0,) + IN_SIZES)
    q = y[:, offs[0]:offs[1]]
    k = y[:, offs[1]:offs[2]]
    v_ref[...] = y[:, offs[2]:offs[3]]
    qb_ref[...] = y[:, offs[3]:offs[4]]
    ib_ref[...] = y[:, offs[4]:offs[5]]
    ff_ref[...] = y[:, offs[5]:offs[6]]
    fb_ref[...] = y[:, offs[6]:offs[7]]
    go_ref[...] = y[:, offs[7]:offs[8]]
    is_latent = i * DENSE_ROWS >= N_PROMPT

    @pl.when(jnp.logical_not(is_latent))
    def _():
        q_ref[...] = q
        k_ref[...] = k

    @pl.when(is_latent)
    def _():
        cos = cos_ref[...]
        sin = sin_ref[...]
        for c in range(A_Q // LANES):
            q_ref[:, c * LANES:(c + 1) * LANES] = _rope(q[:, c * LANES:(c + 1) * LANES], cos, sin)
        k_ref[...] = _rope(k, cos, sin)


def inproj_even(x, mods, w_bf, b, cos, sin):
    nblk = N_TOK // DENSE_ROWS
    pos_blocks = DEC_SEQ // DENSE_ROWS

    def pos_map(i):
        return (jnp.maximum(i - N_PROMPT // DENSE_ROWS, 0) % pos_blocks, 0)

    row = lambda i: (i, 0)
    widths = (A_Q, A_KV, A_KV, B_QK, B_V, B_QK, B_QK, B_V)
    return pl.pallas_call(
        _inproj_kernel,
        grid=(nblk,),
        in_specs=[
            pl.BlockSpec((DENSE_ROWS, D_MODEL), row),
            pl.BlockSpec((1, 6, D_MODEL), lambda i: (_cond_of_block(i, DENSE_ROWS), 0, 0)),
            pl.BlockSpec((D_MODEL, D_IN_EVEN), lambda i: (0, 0)),
            pl.BlockSpec((1, D_IN_EVEN), lambda i: (0, 0)),
            pl.BlockSpec((DENSE_ROWS, LANES), pos_map),
            pl.BlockSpec((DENSE_ROWS, LANES), pos_map),
        ],
        out_specs=[pl.BlockSpec((DENSE_ROWS, w), row) for w in widths],
        out_shape=[jax.ShapeDtypeStruct((N_TOK, w), F32) for w in widths],
        compiler_params=_cparams(("parallel",)),
        name="inproj_even",
    )(x, mods, w_bf, b.reshape(1, D_IN_EVEN), cos, sin)


def _sink_attend(q, keys, vals, sink, masks):
    scores = []
    for kk, mask in zip(keys, masks):
        s = _dot_nt(q, kk) * SCALE_A
        if mask is not None:
            s = jnp.where(mask, s, MASK_VALUE)
        scores.append(s)
    m = sink
    for s in scores:
        m = jnp.maximum(m, jnp.max(s, axis=-1, keepdims=True))
    denom = jnp.exp(sink - m)
    acc = None
    for s, vv in zip(scores, vals):
        p = jnp.exp(s - m)
        denom = denom + jnp.sum(p, axis=-1, keepdims=True)
        pv = _dot(p.astype(BF16), vv)
        acc = pv if acc is None else acc + pv
    return acc / denom


def _attn_ctx_kernel(sink_ref, q_ref, k_ref, v_ref, o_ref):
    k = k_ref[...].astype(BF16)
    v = v_ref[...].astype(BF16)
    q = q_ref[...].astype(BF16)
    for h in range(N_HEADS_A):
        kv = h // GROUP_A
        qh = q[:, h * HEAD_DIM:(h + 1) * HEAD_DIM]
        kh = k[:, kv * HEAD_DIM:(kv + 1) * HEAD_DIM]
        vh = v[:, kv * HEAD_DIM:(kv + 1) * HEAD_DIM]
        o_ref[:, h * HEAD_DIM:(h + 1) * HEAD_DIM] = _sink_attend(qh, [kh], [vh], sink_ref[h], [None])


def attn_context(sink, q, k, v):
    row = lambda b: (b, 0)
    return pl.pallas_call(
        _attn_ctx_kernel,
        grid=(BATCH,),
        in_specs=[
            pl.BlockSpec(memory_space=pltpu.SMEM),
            pl.BlockSpec((SEQ, A_Q), row),
            pl.BlockSpec((SEQ, A_KV), row),
            pl.BlockSpec((SEQ, A_KV), row),
        ],
        out_specs=pl.BlockSpec((SEQ, A_Q), row),
        out_shape=jax.ShapeDtypeStruct((N_PROMPT, A_Q), F32),
        compiler_params=_cparams(("parallel",)),
        name="attn_context",
    )(sink, q, k, v)


def _attn_lat_kernel(sink_ref, q_ref, kp_ref, kc_ref, kn_ref, vp_ref, vc_ref, vn_ref, ck_ref, cv_ref, o_ref):
    n = pl.program_id(1)
    nb = DEC_SEQ // ATTN_BLOCK
    qi = lax.broadcasted_iota(jnp.int32, (ATTN_BLOCK, ATTN_BLOCK), 0)
    kj = lax.broadcasted_iota(jnp.int32, (ATTN_BLOCK, ATTN_BLOCK), 1)
    mask_prev = jnp.logical_and(kj - qi >= ATTN_BLOCK - WINDOW, n > 0)
    mask_next = jnp.logical_and(kj - qi <= WINDOW - ATTN_BLOCK, n < nb - 1)
    masks = [mask_prev, None, mask_next, None]
    q = q_ref[...].astype(BF16)
    kband = [r[...].astype(BF16) for r in (kp_ref, kc_ref, kn_ref)]
    vband = [r[...].astype(BF16) for r in (vp_ref, vc_ref, vn_ref)]
    ck = ck_ref[0].astype(BF16)
    cv = cv_ref[0].astype(BF16)
    for h in range(N_HEADS_A):
        kv = h // GROUP_A
        sl = slice(kv * HEAD_DIM, (kv + 1) * HEAD_DIM)
        qh = q[:, h * HEAD_DIM:(h + 1) * HEAD_DIM]
        keys = [kb[:, sl] for kb in kband] + [ck[:, sl]]
        vals = [vb[:, sl] for vb in vband] + [cv[:, sl]]
        o_ref[:, h * HEAD_DIM:(h + 1) * HEAD_DIM] = _sink_attend(qh, keys, vals, sink_ref[h], masks)


def attn_latent(sink, q, k, v, cache_k, cache_v):
    nb = DEC_SEQ // ATTN_BLOCK
    base = N_PROMPT // ATTN_BLOCK

    def blk(delta):
        return lambda b, n: (base + b * nb + jnp.clip(n + delta, 0, nb - 1), 0)

    kv_spec = lambda delta: pl.BlockSpec((ATTN_BLOCK, A_KV), blk(delta))
    cache_spec = pl.BlockSpec((1, PAST_LEN, A_KV), lambda b, n: (b, 0, 0))
    return pl.pallas_call(
        _attn_lat_kernel,
        grid=(DEC_BATCH, nb),
        in_specs=[
            pl.BlockSpec(memory_space=pltpu.SMEM),
            pl.BlockSpec((ATTN_BLOCK, A_Q), blk(0)),
            kv_spec(-1), kv_spec(0), kv_spec(1),
            kv_spec(-1), kv_spec(0), kv_spec(1),
            cache_spec, cache_spec,
        ],
        out_specs=pl.BlockSpec((ATTN_BLOCK, A_Q), lambda b, n: (b * nb + n, 0)),
        out_shape=jax.ShapeDtypeStruct((N_SAMPLE, A_Q), F32),
        compiler_params=_cparams(("parallel", "parallel")),
        name="attn_latent",
    )(sink, q, k, k, k, v, v, v, cache_k, cache_v)


def _hgrn_gate(f_pre, lb):
    f = jnp.maximum(lb, LB_FLOOR) + (1.0 - lb) * jax.nn.sigmoid(f_pre)
    return 1.0 - f, jnp.log(f)


def _scan_rows(x, reverse):
    n = x.shape[0]
    row = lax.broadcasted_iota(jnp.int32, x.shape, 0)
    sh = 1
    while sh < n:
        if reverse:
            x = x + jnp.where(row < n - sh, pltpu.roll(x, n - sh, 0), 0.0)
        else:
            x = x + jnp.where(row >= sh, pltpu.roll(x, sh, 0), 0.0)
        sh *= 2
    return x


def _chunk_bcast(x, c, pick, shift):
    nc = HG_BLOCK // c
    rows = x.reshape(nc, c, x.shape[-1])[:, pick:pick + 1, :]
    zero = jnp.zeros((1, 1, x.shape[-1]), x.dtype)
    if shift == -1:
        rows = jnp.concatenate([zero, rows[:-1]], axis=0)
    elif shift == 1:
        rows = jnp.concatenate([rows[1:], zero], axis=0)
    return jnp.broadcast_to(rows, (nc, c, x.shape[-1])).reshape(HG_BLOCK, x.shape[-1])


def _hgrn_block(q, kk, v, log_f, st, reverse):
    cum = _scan_rows(log_f, reverse)
    tot = cum[0:1, :] if reverse else cum[HG_BLOCK - 1:HG_BLOCK, :]
    o = _dot_nt((q * jnp.exp(cum)).astype(BF16), st.astype(BF16))
    kd = (kk * jnp.exp(tot - cum)).astype(BF16)
    v_bf = v.astype(BF16)
    u_t = lax.dot_general(v_bf, kd, (((0,), (0,)), ((), ())), preferred_element_type=F32)
    st_new = st * jnp.exp(tot) + u_t
    ti = lax.broadcasted_iota(jnp.int32, (HG_BLOCK, HG_BLOCK), 0)
    si = lax.broadcasted_iota(jnp.int32, (HG_BLOCK, HG_BLOCK), 1)
    a = jnp.zeros((HG_BLOCK, HG_BLOCK), F32)
    c = HG_SUB
    while c < HG_BLOCK:
        if reverse:
            bound_t = _chunk_bcast(cum, c, 0, 1)
            bound_s = _chunk_bcast(cum, c, 0, 0)
            mask = jnp.logical_and((ti // c) % 2 == 0, si // c == ti // c + 1)
        else:
            bound_t = _chunk_bcast(cum, c, c - 1, -1)
            bound_s = _chunk_bcast(cum, c, c - 1, 0)
            mask = jnp.logical_and((ti // c) % 2 == 1, si // c == ti // c - 1)
        qc = (q * jnp.exp(cum - bound_t)).astype(BF16)
        kc = (kk * jnp.exp(bound_s - cum)).astype(BF16)
        a = a + jnp.where(mask, _dot_nt(qc, kc), 0.0)
        c *= 2
    o = o + _dot(a.astype(BF16), v_bf)
    nsub = HG_BLOCK // HG_SUB
    q3 = q.reshape(nsub, HG_SUB, HGRN_DK)
    k3 = kk.reshape(nsub, HG_SUB, HGRN_DK)
    v3 = v.reshape(nsub, HG_SUB, HGRN_DV)
    cum3 = cum.reshape(nsub, HG_SUB, HGRN_DK)
    t_off = lax.broadcasted_iota(jnp.int32, (1, HG_SUB, 1), 1)
    od = jnp.zeros((nsub, HG_SUB, HGRN_DV), F32)
    for s in range(HG_SUB):
        causal = (t_off <= s) if reverse else (t_off >= s)
        decay = jnp.exp(jnp.where(causal, cum3 - cum3[:, s:s + 1, :], MASK_VALUE))
        score = jnp.sum(q3 * decay * k3[:, s:s + 1, :], axis=-1, keepdims=True)
        od = od + score * v3[:, s:s + 1, :]
    return o + od.reshape(HG_BLOCK, HGRN_DV), st_new


def _hgrn_kernel(*refs, n_blocks, has_state_in, has_state_out):
    refs = list(refs)
    qb_ref, ib_ref, ff_ref, fb_ref, go_ref, lbf_ref, lbb_ref, nw_ref = refs[:8]
    refs = refs[8:]
    s0_ref = refs.pop(0) if has_state_in else None
    o_ref = refs.pop(0)
    so_ref = refs.pop(0) if has_state_out else None
    acc_ref = refs.pop(0)

    def run(reverse):
        f_ref, lb_ref = (fb_ref, lbb_ref) if reverse else (ff_ref, lbf_ref)
        lb = lb_ref[...]
        if has_state_in:
            st0 = s0_ref[0, 0, 1 if reverse else 0, 0].T
        else:
            st0 = jnp.zeros((HGRN_DV, HGRN_DK), F32)

        def body(it, st):
            blk = (n_blocks - 1 - it) if reverse else it
            rows = pl.ds(pl.multiple_of(blk * HG_BLOCK, HG_BLOCK), HG_BLOCK)
            qpre = qb_ref[rows, :]
            q = qpre * jax.nn.sigmoid(qpre)
            kk, log_f = _hgrn_gate(f_ref[rows, :], lb)
            o, st = _hgrn_block(q, kk, ib_ref[rows, :], log_f, st, reverse)
            if reverse:
                acc_ref[rows, :] = acc_ref[rows, :] + o
            else:
                acc_ref[rows, :] = o
            return st

        return lax.fori_loop(0, n_blocks, body, st0)

    st_f = run(False)
    st_b = run(True)
    if has_state_out:
        so_ref[0, 0, 0] = st_f.T
        so_ref[0, 1, 0] = st_b.T
    o = acc_ref[...]
    o = o * lax.rsqrt(jnp.mean(o * o, axis=-1, keepdims=True) + RMS_EPS) * nw_ref[...]
    g = go_ref[...]
    o_ref[...] = o * (g * jax.nn.sigmoid(g))


def hgrn_mixer(qb, ib, ff, fb, go, lb_f, lb_b, norm_w, *, latent, state_in=None, layer_j=0):
    t_len = DEC_SEQ if latent else SEQ
    n_seq = DEC_BATCH if latent else BATCH
    row0 = N_PROMPT // t_len if latent else 0
    tok = pl.BlockSpec((t_len, HGRN_DK), lambda b, h: (row0 + b, h))
    vec = pl.BlockSpec((1, HGRN_DK), lambda b, h: (0, h))
    in_specs = [tok] * 5 + [vec] * 3
    args = [qb, ib, ff, fb, go, lb_f.reshape(1, B_QK), lb_b.reshape(1, B_QK), norm_w.reshape(1, B_V)]
    if latent:
        in_specs.append(pl.BlockSpec((1, 1, 2, 1, HGRN_DK, HGRN_DV), lambda b, h: (b, layer_j, 0, h, 0, 0)))
        args.append(state_in)
    out_specs = [pl.BlockSpec((t_len, HGRN_DV), lambda b, h: (b, h))]
    out_shape = [jax.ShapeDtypeStruct((n_seq * t_len, B_V), F32)]
    if not latent:
        out_specs.append(pl.BlockSpec((1, 2, 1, HGRN_DK, HGRN_DV), lambda b, h: (b, 0, h, 0, 0)))
        out_shape.append(jax.ShapeDtypeStruct((BATCH, 2, N_HEADS_B, HGRN_DK, HGRN_DV), F32))
    kern = functools.partial(_hgrn_kernel, n_blocks=t_len // HG_BLOCK, has_state_in=latent, has_state_out=not latent)
    return pl.pallas_call(
        kern,
        grid=(n_seq, N_HEADS_B),
        in_specs=in_specs,
        out_specs=out_specs,
        out_shape=out_shape,
        scratch_shapes=[pltpu.VMEM((t_len, HGRN_DV), F32)],
        compiler_params=_cparams(("parallel", "parallel")),
        name="hgrn_latent" if latent else "hgrn_context",
    )(*args)


def _outproj_kernel(x_ref, mod_ref, ap_ref, as_ref, rp_ref, rs_ref, w_ref, g_ref, b_ref, o_ref):
    is_latent = pl.program_id(0) * DENSE_ROWS >= N_PROMPT
    attn = jnp.where(is_latent, as_ref[...], ap_ref[...]).astype(BF16)
    rec = jnp.where(is_latent, rs_ref[...], rp_ref[...]).astype(BF16)
    y = _dot(attn, w_ref[0:A_Q, :]) + _dot(rec, w_ref[A_Q:A_Q + B_V, :])
    z = DN_ALPHA * x_ref[...] + mod_ref[0, 2:3, :] * y
    o_ref[...] = _layer_norm(z, g_ref[...], b_ref[...])


def outproj_even(x, mods, attn_p, attn_s, rec_p, rec_s, w_bf, ln_g, ln_b):
    nblk = N_TOK // DENSE_ROWS
    npb = N_PROMPT // DENSE_ROWS
    row = lambda i: (i, 0)
    prow = lambda i: (jnp.minimum(i, npb - 1), 0)
    srow = lambda i: (jnp.maximum(i - npb, 0), 0)
    vec = pl.BlockSpec((1, D_MODEL), lambda i: (0, 0))
    return pl.pallas_call(
        _outproj_kernel,
        grid=(nblk,),
        in_specs=[
            pl.BlockSpec((DENSE_ROWS, D_MODEL), row),
            pl.BlockSpec((1, 6, D_MODEL), lambda i: (_cond_of_block(i, DENSE_ROWS), 0, 0)),
            pl.BlockSpec((DENSE_ROWS, A_Q), prow),
            pl.BlockSpec((DENSE_ROWS, A_Q), srow),
            pl.BlockSpec((DENSE_ROWS, B_V), prow),
            pl.BlockSpec((DENSE_ROWS, B_V), srow),
            pl.BlockSpec((A_Q + B_V, D_MODEL), lambda i: (0, 0)),
            vec, vec,
        ],
        out_specs=pl.BlockSpec((DENSE_ROWS, D_MODEL), row),
        out_shape=jax.ShapeDtypeStruct((N_TOK, D_MODEL), F32),
        compiler_params=_cparams(("parallel",)),
        name="outproj_even",
    )(x, mods, attn_p, attn_s, rec_p, rec_s, w_bf, ln_g.reshape(1, D_MODEL), ln_b.reshape(1, D_MODEL))


def _conv_in_kernel(x_ref, mod_ref, w_ref, b_ref, u_ref):
    h = (x_ref[...] * (1.0 + mod_ref[0, 1:2, :]) + mod_ref[0, 0:1, :]).astype(BF16)
    a = _dot(h, w_ref[:, 0:D_MODEL]) + b_ref[:, 0:D_MODEL]
    gt = _dot(h, w_ref[:, D_MODEL:2 * D_MODEL]) + b_ref[:, D_MODEL:2 * D_MODEL]
    u_ref[...] = a * jax.nn.sigmoid(gt)


def conv_in(x, mods, w_bf, b):
    row = lambda i: (i, 0)
    return pl.pallas_call(
        _conv_in_kernel,
        grid=(N_TOK // DENSE_ROWS,),
        in_specs=[
            pl.BlockSpec((DENSE_ROWS, D_MODEL), row),
            pl.BlockSpec((1, 6, D_MODEL), lambda i: (_cond_of_block(i, DENSE_ROWS), 0, 0)),
            pl.BlockSpec((D_MODEL, 2 * D_MODEL), lambda i: (0, 0)),
            pl.BlockSpec((1, 2 * D_MODEL), lambda i: (0, 0)),
        ],
        out_specs=pl.BlockSpec((DENSE_ROWS, D_MODEL), row),
        out_shape=jax.ShapeDtypeStruct((N_TOK, D_MODEL), F32),
        compiler_params=_cparams(("parallel",)),
        name="conv_in",
    )(x, mods, w_bf, b.reshape(1, 2 * D_MODEL))


CONV_HALO = 16
CONV_LANES = 256


CONV_SHIFT_ROWS = ROW_BLOCK + 2 * CONV_HALO - SUBLANES
CONV_ROW_CHUNK = 128


def _conv_out_kernel(x_ref, mod_ref, up_ref, uc_ref, un_ref, dw_ref, dwb_ref, cg_ref, cb_ref,
                     w_ref, b_ref, g_ref, bb_ref, o_ref, pad_ref, acc_ref, sh_ref):
    i = pl.program_id(0)
    blocks_per_seq = DEC_SEQ // ROW_BLOCK
    j = i - N_PROMPT // ROW_BLOCK
    is_latent = j >= 0
    has_prev = jnp.logical_and(is_latent, j % blocks_per_seq != 0)
    has_next = jnp.logical_and(is_latent, j % blocks_per_seq != blocks_per_seq - 1)
    pad_ref[0:CONV_HALO, :] = jnp.where(has_prev, up_ref[...], 0.0)
    pad_ref[CONV_HALO:CONV_HALO + ROW_BLOCK, :] = uc_ref[...]
    pad_ref[CONV_HALO + ROW_BLOCK:, :] = jnp.where(has_next, un_ref[...], 0.0)
    first = CONV_HALO - CONV_WIDTH // 2
    for c in range(D_MODEL // CONV_LANES):
        lanes = slice(c * CONV_LANES, (c + 1) * CONV_LANES)
        for s in range(SUBLANES):
            sh_ref[s] = pad_ref[s:s + CONV_SHIFT_ROWS, lanes]
        for r0 in range(0, ROW_BLOCK, CONV_ROW_CHUNK):
            acc = jnp.zeros((CONV_ROW_CHUNK, CONV_LANES), F32) + dwb_ref[:, lanes]
            for tap in range(CONV_WIDTH):
                whole, s = divmod(first + tap, SUBLANES)
                rows = slice(r0 + whole * SUBLANES, r0 + whole * SUBLANES + CONV_ROW_CHUNK)
                acc = acc + sh_ref[s, rows, :] * dw_ref[tap:tap + 1, lanes]
            acc_ref[r0:r0 + CONV_ROW_CHUNK, lanes] = acc
    u = _layer_norm(acc_ref[...], cg_ref[...], cb_ref[...])
    u = (u * jax.nn.sigmoid(u)).astype(BF16)
    y = _dot(u, w_ref[...]) + b_ref[...]
    z = DN_ALPHA * x_ref[...] + mod_ref[0, 2:3, :] * y
    o_ref[...] = _layer_norm(z, g_ref[...], bb_ref[...])


def conv_out(x, mods, u, dw, dw_b, cln_g, cln_b, w_bf, b_out, ln_g, ln_b):
    nblk = N_TOK // ROW_BLOCK
    ratio = ROW_BLOCK // CONV_HALO
    nhalo = N_TOK // CONV_HALO
    row = lambda i: (i, 0)
    vec = pl.BlockSpec((1, D_MODEL), lambda i: (0, 0))
    r1 = lambda a: a.reshape(1, D_MODEL)
    return pl.pallas_call(
        _conv_out_kernel,
        grid=(nblk,),
        in_specs=[
            pl.BlockSpec((ROW_BLOCK, D_MODEL), row),
            pl.BlockSpec((1, 6, D_MODEL), lambda i: (_cond_of_block(i, ROW_BLOCK), 0, 0)),
            pl.BlockSpec((CONV_HALO, D_MODEL), lambda i: (jnp.maximum(i * ratio - 1, 0), 0)),
            pl.BlockSpec((ROW_BLOCK, D_MODEL), row),
            pl.BlockSpec((CONV_HALO, D_MODEL), lambda i: (jnp.minimum((i + 1) * ratio, nhalo - 1), 0)),
            pl.BlockSpec((CONV_WIDTH, D_MODEL), lambda i: (0, 0)),
            vec, vec, vec,
            pl.BlockSpec((D_MODEL, D_MODEL), lambda i: (0, 0)),
            vec, vec, vec,
        ],
        out_specs=pl.BlockSpec((ROW_BLOCK, D_MODEL), row),
        out_shape=jax.ShapeDtypeStruct((N_TOK, D_MODEL), F32),
        scratch_shapes=[pltpu.VMEM((ROW_BLOCK + 2 * CONV_HALO, D_MODEL), F32),
                        pltpu.VMEM((ROW_BLOCK, D_MODEL), F32),
                        pltpu.VMEM((SUBLANES, CONV_SHIFT_ROWS, CONV_LANES), F32)],
        compiler_params=_cparams(("parallel",)),
        name="conv_out",
    )(x, mods, u, u, u, dw, r1(dw_b), r1(cln_g), r1(cln_b), w_bf, r1(b_out), r1(ln_g), r1(ln_b))


def _modulate2(x_ref, mod_ref):
    return x_ref[...] * (1.0 + mod_ref[0, 4:5, :]) + mod_ref[0, 3:4, :]


def _route_kernel(x_ref, mod_ref, wr_ref, br_ref, idx_ref, gate_ref, rank_ref, cnt_ref, carry_ref):
    i = pl.program_id(0)

    @pl.when(i == 0)
    def _():
        carry_ref[...] = jnp.zeros_like(carry_ref)

    h = _modulate2(x_ref, mod_ref)
    logits = lax.dot_general(wr_ref[...], h, (((1,), (1,)), ((), ())), precision=HIGHEST,
                             preferred_element_type=F32) + br_ref[...]
    eidx = lax.broadcasted_iota(jnp.int32, logits.shape, 0)
    vals = logits
    sels, tops = [], []
    for k in range(TOP_K):
        m = jnp.max(vals, axis=0, keepdims=True)
        idx = jnp.min(jnp.where(vals == m, eidx, N_EXPERTS), axis=0, keepdims=True)
        sel = eidx == idx
        idx_ref[k:k + 1, :] = idx
        sels.append(sel)
        tops.append(m)
        vals = jnp.where(sel, -jnp.inf, vals)
    exps = [jnp.exp(t - tops[0]) for t in tops]
    total = exps[0] + exps[1] + exps[2] + exps[3]
    for k in range(TOP_K):
        gate_ref[k:k + 1, :] = exps[k] / total
    onehot = jnp.zeros(logits.shape, F32)
    for sel in sels:
        onehot = onehot + sel.astype(F32)
    ta = lax.broadcasted_iota(jnp.int32, (ROW_BLOCK, ROW_BLOCK), 0)
    tb = lax.broadcasted_iota(jnp.int32, (ROW_BLOCK, ROW_BLOCK), 1)
    before = _dot(onehot.astype(BF16), (ta < tb).astype(BF16)) + carry_ref[:, 0:1]
    for k in range(TOP_K):
        rank = jnp.sum(jnp.where(sels[k], before, 0.0), axis=0, keepdims=True)
        rank_ref[k:k + 1, :] = rank.astype(jnp.int32)
    carry = carry_ref[...] + jnp.sum(onehot, axis=1, keepdims=True)
    carry_ref[...] = carry
    cnt_ref[...] = carry.astype(jnp.int32)


def moe_route(x, mods, wr_t, b_r):
    tok = pl.BlockSpec((TOP_K, ROW_BLOCK), lambda i: (0, i))
    return pl.pallas_call(
        _route_kernel,
        grid=(N_TOK // ROW_BLOCK,),
        in_specs=[
            pl.BlockSpec((ROW_BLOCK, D_MODEL), lambda i: (i, 0)),
            pl.BlockSpec((1, 6, D_MODEL), lambda i: (_cond_of_block(i, ROW_BLOCK), 0, 0)),
            pl.BlockSpec((N_EXPERTS, D_MODEL), lambda i: (0, 0)),
            pl.BlockSpec((N_EXPERTS, 1), lambda i: (0, 0)),
        ],
        out_specs=[tok, tok, tok, pl.BlockSpec((N_EXPERTS, LANES), lambda i: (0, 0))],
        out_shape=[jax.ShapeDtypeStruct((TOP_K, N_TOK), jnp.int32),
                   jax.ShapeDtypeStruct((TOP_K, N_TOK), F32),
                   jax.ShapeDtypeStruct((TOP_K, N_TOK), jnp.int32),
                   jax.ShapeDtypeStruct((N_EXPERTS, LANES), jnp.int32)],
        scratch_shapes=[pltpu.VMEM((N_EXPERTS, LANES), F32)],
        compiler_params=_cparams(("arbitrary",)),
        name="moe_route",
    )(x, mods, wr_t, b_r.reshape(N_EXPERTS, 1))


HALF_D = D_MODEL // 2
HI_MASK = 0xFFFF0000


def _pack_bf16_pairs(h):
    half = h.shape[1] // 2
    lo = lax.bitcast_convert_type(h[:, :half].astype(BF16).astype(F32), jnp.uint32)
    hi = lax.bitcast_convert_type(h[:, half:].astype(BF16).astype(F32), jnp.uint32)
    return (lo >> 16) | (hi & jnp.uint32(HI_MASK))


def _unpack_bf16_pairs(w):
    lo = lax.bitcast_convert_type(w << 16, F32).astype(BF16)
    hi = lax.bitcast_convert_type(w & jnp.uint32(HI_MASK), F32).astype(BF16)
    return lo, hi


def _scatter_kernel(dest_ref, zstart_ref, x_ref, mod_ref, xs_ref, pk_ref, zero_ref, sem, zsem):
    i = pl.program_id(0)

    def zero_copy(start):
        start = pl.multiple_of(start, MOE_ROWS)
        return pltpu.make_async_copy(zero_ref, xs_ref.at[pl.ds(start, MOE_ROWS), :], zsem)

    def zero_blocks(fn):
        for e in range(N_EXPERTS):
            @pl.when(zstart_ref[e] >= 0)
            def _():
                fn(zero_copy(jnp.maximum(zstart_ref[e], 0)))

        def unused(b, carry):
            fn(zero_copy(b * MOE_ROWS))
            return carry

        lax.fori_loop(zstart_ref[N_EXPERTS], MOE_BLOCKS, unused, 0)

    @pl.when(i == 0)
    def _():
        zero_ref[...] = jnp.zeros_like(zero_ref)
        zero_blocks(lambda cp: cp.start())
        zero_blocks(lambda cp: cp.wait())

    base = i * ROW_BLOCK
    last = pl.num_programs(0) - 1

    def drain(slot):
        for k in range(TOP_K):
            pltpu.make_async_copy(pk_ref.at[slot], xs_ref.at[pl.ds(0, ROW_BLOCK), :], sem.at[slot]).wait()

    for slot in range(2):
        @pl.when(i % 2 == slot)
        def _():
            @pl.when(i >= 2)
            def _():
                drain(slot)

            pk_ref[slot] = _pack_bf16_pairs(_modulate2(x_ref, mod_ref))
            for t in range(ROW_BLOCK):
                for k in range(TOP_K):
                    row = dest_ref[k * N_TOK + base + t]
                    pltpu.make_async_copy(pk_ref.at[slot, pl.ds(t, 1), :], xs_ref.at[pl.ds(row, 1), :],
                                          sem.at[slot]).start(priority=k % 2)

            @pl.when(i == last)
            def _():
                drain(slot)

                @pl.when(i >= 1)
                def _():
                    drain(1 - slot)


def moe_scatter(dest_flat, zstart, x, mods):
    return pl.pallas_call(
        _scatter_kernel,
        grid_spec=pltpu.PrefetchScalarGridSpec(
            num_scalar_prefetch=2,
            grid=(N_TOK // ROW_BLOCK,),
            in_specs=[
                pl.BlockSpec((ROW_BLOCK, D_MODEL), lambda i, d, z: (i, 0)),
                pl.BlockSpec((1, 6, D_MODEL), lambda i, d, z: (_cond_of_block(i, ROW_BLOCK), 0, 0)),
            ],
            out_specs=pl.BlockSpec(memory_space=pl.ANY),
            scratch_shapes=[pltpu.VMEM((2, ROW_BLOCK, HALF_D), jnp.uint32),
                            pltpu.VMEM((MOE_ROWS, HALF_D), jnp.uint32),
                            pltpu.SemaphoreType.DMA((2,)), pltpu.SemaphoreType.DMA],
        ),
        out_shape=jax.ShapeDtypeStruct((MOE_R, HALF_D), jnp.uint32),
        compiler_params=_cparams(("arbitrary",)),
        name="moe_scatter",
    )(dest_flat, zstart, x, mods)


N_CHUNK = 256


def _expert_kernel(be_ref, nused_ref, nexte_ref, par_ref, xs_ref, wg_hbm, wu_hbm, wd_hbm, bg_ref, bu_ref, bd_ref,
                   ys_ref, wbuf, act_ref, wg_bf, wu_bf, wd_bf, wsem, *, layer):
    i = pl.program_id(0)
    new_expert = jnp.logical_or(i == 0, be_ref[i] != be_ref[jnp.maximum(i - 1, 0)])
    slot = par_ref[i]

    def weight_copies(e, s):
        return [pltpu.make_async_copy(w.at[layer, e], wbuf.at[s, j], wsem.at[s])
                for j, w in enumerate((wg_hbm, wu_hbm, wd_hbm))]

    @pl.when(i == 0)
    def _():
        for cp in weight_copies(be_ref[0], 0):
            cp.start()

    @pl.when(jnp.logical_and(new_expert, i < nused_ref[0]))
    def _():
        for cp in weight_copies(be_ref[i], slot):
            cp.wait()

        @pl.when(nexte_ref[i] >= 0)
        def _():
            for cp in weight_copies(jnp.maximum(nexte_ref[i], 0), 1 - slot):
                cp.start()

        wg_bf[...] = wbuf[slot, 0].astype(BF16)
        wu_bf[...] = wbuf[slot, 1].astype(BF16)
        wd_bf[...] = wbuf[slot, 2].astype(BF16)

    @pl.when(i < nused_ref[0])
    def _():
        lo, hi = _unpack_bf16_pairs(xs_ref[...])
        x = jnp.concatenate([lo, hi], axis=1)
        for n in range(D_EXPERT // N_CHUNK):
            cols = slice(n * N_CHUNK, (n + 1) * N_CHUNK)
            gt = _dot(x, wg_bf[:, cols]) + bg_ref[0, 0, :, cols]
            up = _dot(x, wu_bf[:, cols]) + bu_ref[0, 0, :, cols]
            gt = jnp.minimum(gt, SWIGLU_LIMIT)
            up = jnp.clip(up, -SWIGLU_LIMIT, SWIGLU_LIMIT)
            act_ref[:, cols] = ((up + 1.0) * gt * jax.nn.sigmoid(SWIGLU_ALPHA * gt)).astype(BF16)
        act = act_ref[...]
        for n in range(HALF_D // N_CHUNK):
            cols = slice(n * N_CHUNK, (n + 1) * N_CHUNK)
            cols_hi = slice(HALF_D + n * N_CHUNK, HALF_D + (n + 1) * N_CHUNK)
            y_lo = _dot(act, wd_bf[:, cols]) + bd_ref[0, 0, :, cols]
            y_hi = _dot(act, wd_bf[:, cols_hi]) + bd_ref[0, 0, :, cols_hi]
            ys_ref[:, cols] = _pack_bf16_pairs(jnp.concatenate([y_lo, y_hi], axis=1))

    @pl.when(i >= nused_ref[0])
    def _():
        ys_ref[...] = jnp.zeros_like(ys_ref)


def moe_experts(block_e, n_used, next_e, parity, xs, w_g, w_u, w_d, b_g, b_u, b_d, layer):
    rows = lambda i, be, nu, ne, pa: (i, 0)
    wspec = pl.BlockSpec(memory_space=pl.ANY)
    bspec = pl.BlockSpec((1, 1, 1, D_EXPERT), lambda i, be, nu, ne, pa: (layer, be[i], 0, 0))
    r4 = lambda b: b.reshape(DEPTH, N_EXPERTS, 1, D_EXPERT)
    return pl.pallas_call(
        functools.partial(_expert_kernel, layer=layer),
        grid_spec=pltpu.PrefetchScalarGridSpec(
            num_scalar_prefetch=4,
            grid=(MOE_BLOCKS,),
            in_specs=[pl.BlockSpec((MOE_ROWS, HALF_D), rows), wspec, wspec, wspec, bspec, bspec, bspec],
            out_specs=pl.BlockSpec((MOE_ROWS, HALF_D), rows),
            scratch_shapes=[pltpu.VMEM((2, 3, D_MODEL, D_EXPERT), F32),
                            pltpu.VMEM((MOE_ROWS, D_EXPERT), BF16)]
                           + [pltpu.VMEM((D_MODEL, D_EXPERT), BF16)] * 3
                           + [pltpu.SemaphoreType.DMA((2,))],
        ),
        out_shape=jax.ShapeDtypeStruct((MOE_R, HALF_D), jnp.uint32),
        compiler_params=_cparams(("arbitrary",)),
        name="moe_experts",
    )(block_e, n_used, next_e, parity, xs, w_g, w_u, w_d, r4(b_g), r4(b_u), r4(b_d))


def _combine_kernel(dest_ref, x_ref, mod_ref, gate_ref, ys_ref, g_ref, b_ref, o_ref, buf_ref, sem):
    i = pl.program_id(0)
    last = pl.num_programs(0) - 1

    def gather(step, slot):
        base = step * COMB_TOK
        for t in range(COMB_TOK):
            for k in range(TOP_K):
                row = dest_ref[k * N_TOK + base + t]
                pltpu.make_async_copy(ys_ref.at[pl.ds(row, 1), :], buf_ref.at[slot, k, pl.ds(t, 1), :],
                                      sem.at[slot]).start(priority=k % 2)

    @pl.when(i == 0)
    def _():
        gather(0, 0)

    for slot in range(2):
        @pl.when(jnp.logical_and(i < last, i % 2 == slot))
        def _():
            gather(i + 1, 1 - slot)

    eye = (lax.broadcasted_iota(jnp.int32, (COMB_TOK, COMB_TOK), 0)
           == lax.broadcasted_iota(jnp.int32, (COMB_TOK, COMB_TOK), 1))
    gates = gate_ref[...]
    cols = [jnp.sum(jnp.where(eye, gates[k:k + 1, :], 0.0), axis=1, keepdims=True) for k in range(TOP_K)]
    cur = i % 2
    for k in range(TOP_K):
        pltpu.make_async_copy(ys_ref.at[pl.ds(0, COMB_TOK), :], buf_ref.at[cur, k], sem.at[cur]).wait()
    y_lo = y_hi = None
    for k in range(TOP_K):
        w = buf_ref[cur, k]
        lo = cols[k] * lax.bitcast_convert_type(w << 16, F32)
        hi = cols[k] * lax.bitcast_convert_type(w & jnp.uint32(HI_MASK), F32)
        y_lo = lo if y_lo is None else y_lo + lo
        y_hi = hi if y_hi is None else y_hi + hi
    y = jnp.concatenate([y_lo, y_hi], axis=1)
    z = DN_ALPHA * x_ref[...] + mod_ref[0, 5:6, :] * y
    o_ref[...] = _layer_norm(z, g_ref[...], b_ref[...])


def moe_combine(dest_flat, x, mods, gates_t, ys, ln_g, ln_b):
    vec = pl.BlockSpec((1, D_MODEL), lambda i, d: (0, 0))
    return pl.pallas_call(
        _combine_kernel,
        grid_spec=pltpu.PrefetchScalarGridSpec(
            num_scalar_prefetch=1,
            grid=(N_TOK // COMB_TOK,),
            in_specs=[
                pl.BlockSpec((COMB_TOK, D_MODEL), lambda i, d: (i, 0)),
                pl.BlockSpec((1, 6, D_MODEL), lambda i, d: (_cond_of_block(i, COMB_TOK), 0, 0)),
                pl.BlockSpec((TOP_K, COMB_TOK), lambda i, d: (0, i)),
                pl.BlockSpec(memory_space=pl.ANY),
                vec, vec,
            ],
            out_specs=pl.BlockSpec((COMB_TOK, D_MODEL), lambda i, d: (i, 0)),
            scratch_shapes=[pltpu.VMEM((2, TOP_K, COMB_TOK, HALF_D), jnp.uint32), pltpu.SemaphoreType.DMA((2,))],
        ),
        out_shape=jax.ShapeDtypeStruct((N_TOK, D_MODEL), F32),
        compiler_params=_cparams(("arbitrary",)),
        name="moe_combine",
    )(dest_flat, x, mods, gates_t, ys, ln_g.reshape(1, D_MODEL), ln_b.reshape(1, D_MODEL))


def moe_layer(x, mods, layer, router_w, router_b, w_g, b_g, w_u, b_u, w_d, b_d, ln_g, ln_b):
    idx_t, gates_t, rank_t, counts = moe_route(x, mods, router_w[layer].T, router_b[layer])
    counts = counts[:, 0]
    padded = (counts + MOE_ROWS - 1) // MOE_ROWS * MOE_ROWS
    ends = jnp.cumsum(padded)
    base = ends - padded
    n_used = (ends[-1] // MOE_ROWS).astype(jnp.int32)
    block_start = jnp.arange(MOE_BLOCKS, dtype=jnp.int32) * MOE_ROWS
    block_e = jnp.sum(block_start[:, None] >= ends[None, :], axis=1).astype(jnp.int32)
    used = padded > 0
    experts = jnp.arange(N_EXPERTS, dtype=jnp.int32)
    block_e = jnp.minimum(block_e, jnp.max(jnp.where(used, experts, 0)))
    zstart = jnp.where(padded > 0, ends - MOE_ROWS, -1).astype(jnp.int32)
    zstart = jnp.concatenate([zstart, n_used.reshape(1)])
    onehot = idx_t[:, :, None] == jnp.arange(N_EXPERTS, dtype=jnp.int32)[None, None, :]
    dest = rank_t + jnp.sum(jnp.where(onehot, base[None, None, :], 0), axis=-1)
    dest_flat = dest.reshape(-1).astype(jnp.int32)
    xs = moe_scatter(dest_flat, zstart, x, mods)
    later_used = jnp.logical_and(used[None, :], experts[None, :] > experts[:, None])
    next_used = jnp.min(jnp.where(later_used, experts[None, :], N_EXPERTS), axis=1)
    next_used = jnp.where(next_used == N_EXPERTS, -1, next_used).astype(jnp.int32)
    run_parity = ((jnp.cumsum(used.astype(jnp.int32)) - 1) % 2).astype(jnp.int32)
    of_block = block_e[:, None] == experts[None, :]
    block_next = jnp.sum(jnp.where(of_block, next_used[None, :], 0), axis=1).astype(jnp.int32)
    block_parity = jnp.sum(jnp.where(of_block, run_parity[None, :], 0), axis=1).astype(jnp.int32)
    ys = moe_experts(block_e, n_used.reshape(1), block_next, block_parity, xs,
                     w_g, w_u, w_d, b_g, b_u, b_d, layer)
    return moe_combine(dest_flat, x, mods, gates_t, ys, ln_g, ln_b)


def kernel(x_prompt, x_sample, c, cache_k, cache_v, state_hgrn, c_ctx, w_ada, b_ada, ln_g, ln_b,
           w_in_even, b_in_even, attn_sink, hgrn_lb, hgrn_norm, w_out_even,
           conv_w_in, conv_b_in, conv_dw, conv_dw_b, conv_ln_g, conv_ln_b, conv_w_out, conv_b_out,
           router_w, router_b, moe_w_gate, moe_b_gate, moe_w_up, moe_b_up, moe_w_down, moe_b_down):
    x = jnp.concatenate([x_prompt.reshape(N_PROMPT, D_MODEL), x_sample.reshape(N_SAMPLE, D_MODEL)], axis=0)
    cond = jnp.concatenate([c_ctx[None, :], c, jnp.zeros((COND_ROWS - N_COND, D_MODEL), F32)], axis=0)
    mods_all = adaln_all(cond, w_ada, b_ada).reshape(DEPTH, COND_ROWS, 6, D_MODEL)
    lb = jax.nn.softmax(hgrn_lb.astype(F32), axis=1)
    lb = jnp.cumsum(lb, axis=1) - lb[:, :1]
    cos, sin = _rope_tables()
    new_k, new_v, new_s = [], [], []
    for layer in range(DEPTH):
        j = layer // 2
        mods = mods_all[layer]
        if layer % 2 == 0:
            q, k, v, qb, ib, ff, fb, go = inproj_even(x, mods, w_in_even[j].astype(BF16), b_in_even[j], cos, sin)
            new_k.append(k[:N_PROMPT].reshape(BATCH, SEQ, N_KV_A, HEAD_DIM))
            new_v.append(v[:N_PROMPT].reshape(BATCH, SEQ, N_KV_A, HEAD_DIM))
            attn_p = attn_context(attn_sink[j], q, k, v)
            attn_s = attn_latent(attn_sink[j], q, k, v,
                                 cache_k[:, j].reshape(DEC_BATCH, PAST_LEN, A_KV),
                                 cache_v[:, j].reshape(DEC_BATCH, PAST_LEN, A_KV))
            hg = (qb, ib, ff, fb, go, lb[0, j], lb[1, j], hgrn_norm[j])
            rec_p, states = hgrn_mixer(*hg, latent=False)
            rec_s, = hgrn_mixer(*hg, latent=True, state_in=state_hgrn, layer_j=j)
            new_s.append(states)
            x = outproj_even(x, mods, attn_p, attn_s, rec_p, rec_s, w_out_even[j].astype(BF16),
                             ln_g[layer, 0], ln_b[layer, 0])
        else:
            u = conv_in(x, mods, conv_w_in[j].astype(BF16), conv_b_in[j])
            x = conv_out(x, mods, u, conv_dw[j], conv_dw_b[j], conv_ln_g[j], conv_ln_b[j],
                         conv_w_out[j].astype(BF16), conv_b_out[j], ln_g[layer, 0], ln_b[layer, 0])
        x = moe_layer(x, mods, layer, router_w, router_b, moe_w_gate, moe_b_gate, moe_w_up, moe_b_up,
                      moe_w_down, moe_b_down, ln_g[layer, 1], ln_b[layer, 1])
    return (x[:N_PROMPT].reshape(BATCH, SEQ, D_MODEL),
            x[N_PROMPT:].reshape(DEC_BATCH, DEC_SEQ, D_MODEL),
            jnp.stack(new_k, axis=1), jnp.stack(new_v, axis=1), jnp.stack(new_s, axis=1))
```

```python
import functools

import jax
import jax.numpy as jnp
import numpy as np
from jax import lax
from jax.experimental import pallas as pl
from jax.experimental.pallas import tpu as pltpu

D_MODEL = 1024
BATCH = 16
SEQ = 256
DEPTH = 4
DEC_BATCH = 4
DEC_SEQ = 2048
PAST_LEN = 512
GRID_W = 64
N_EVEN = (DEPTH + 1) // 2
N_ODD = DEPTH // 2
HEAD_DIM = 64
N_HEADS_A = 8
N_KV_A = 2
GROUP_A = N_HEADS_A // N_KV_A
WINDOW = 128
ATTN_BLOCK = 128
SCALE_A = HEAD_DIM ** -0.5
ROPE_BASE = 10000.0
ROPE_PAIRS = HEAD_DIM // 4
N_HEADS_B = 4
HGRN_DK = 128
HGRN_DV = 128
CONV_WIDTH = 31
N_EXPERTS = 32
TOP_K = 4
D_EXPERT = D_MODEL
SWIGLU_LIMIT = 7.0
SWIGLU_ALPHA = 1.702
LN_EPS = 1e-5
RMS_EPS = 1e-6
MASK_VALUE = -1e9
LB_FLOOR = 1e-30
DN_ALPHA = (2 * DEPTH) ** 0.25
A_Q = N_HEADS_A * HEAD_DIM
A_KV = N_KV_A * HEAD_DIM
B_QK = N_HEADS_B * HGRN_DK
B_V = N_HEADS_B * HGRN_DV
IN_SIZES = (A_Q, A_KV, A_KV, B_QK, B_V, B_QK, B_QK, B_V)
D_IN_EVEN = sum(IN_SIZES)

N_PROMPT = BATCH * SEQ
N_SAMPLE = DEC_BATCH * DEC_SEQ
N_TOK = N_PROMPT + N_SAMPLE
N_COND = 1 + DEC_BATCH
COND_ROWS = 8

LANES = 128
SUBLANES = 8
VMEM_LIMIT = 56 * 1024 * 1024

ROW_BLOCK = 256
DENSE_ROWS = 512
HG_BLOCK = 128
HG_SUB = 8
MOE_ROWS = 256
MOE_BLOCKS = (N_TOK * TOP_K + N_EXPERTS * (MOE_ROWS - 1)) // MOE_ROWS + 1
MOE_R = MOE_BLOCKS * MOE_ROWS
COMB_TOK = 128

F32 = jnp.float32
BF16 = jnp.bfloat16
HIGHEST = lax.Precision.HIGHEST


def _cond_of_block(i, rows):
    start = i * rows
    return jnp.where(start < N_PROMPT, 0, 1 + (start - N_PROMPT) // DEC_SEQ)


def _cparams(sem):
    return pltpu.CompilerParams(dimension_semantics=sem, vmem_limit_bytes=VMEM_LIMIT)


def _layer_norm(z, g, b):
    mu = jnp.mean(z, axis=-1, keepdims=True)
    zc = z - mu
    var = jnp.mean(zc * zc, axis=-1, keepdims=True)
    return zc * lax.rsqrt(var + LN_EPS) * g + b


def _dot(a, b):
    return jnp.dot(a, b, preferred_element_type=F32)


def _dot_nt(a, b):
    return lax.dot_general(a, b, (((1,), (1,)), ((), ())), preferred_element_type=F32)


ADA_TN = 3072


def _adaln_kernel(cond_ref, w_ref, b_ref, o_ref):
    c = cond_ref[...]
    s = c * jax.nn.sigmoid(c)
    o_ref[0] = jnp.dot(s, w_ref[0], precision=HIGHEST, preferred_element_type=F32) + b_ref[0]


def adaln_all(cond, w_ada, b_ada):
    n_out = 6 * D_MODEL
    return pl.pallas_call(
        _adaln_kernel,
        grid=(DEPTH, n_out // ADA_TN),
        in_specs=[
            pl.BlockSpec((COND_ROWS, D_MODEL), lambda l, n: (0, 0)),
            pl.BlockSpec((1, D_MODEL, ADA_TN), lambda l, n: (l, 0, n)),
            pl.BlockSpec((1, 1, ADA_TN), lambda l, n: (l, 0, n)),
        ],
        out_specs=pl.BlockSpec((1, COND_ROWS, ADA_TN), lambda l, n: (l, 0, n)),
        out_shape=jax.ShapeDtypeStruct((DEPTH, COND_ROWS, n_out), F32),
        compiler_params=_cparams(("parallel", "parallel")),
        name="adaln",
    )(cond, w_ada, b_ada.reshape(DEPTH, 1, n_out))


def _rope_tables():
    t = np.arange(DEC_SEQ)
    d = np.arange(LANES) % HEAD_DIM
    axis = d // (2 * ROPE_PAIRS)
    half = (d // ROPE_PAIRS) % 2
    pair = d % ROPE_PAIRS
    pos = jnp.where(axis[None, :] == 0, (t // GRID_W)[:, None], (t % GRID_W)[:, None]).astype(F32)
    inv_freq = ROPE_BASE ** (-jnp.arange(ROPE_PAIRS, dtype=F32) / ROPE_PAIRS)
    ang = pos * inv_freq[pair][None, :]
    sign = jnp.where(half[None, :] == 0, -1.0, 1.0).astype(F32)
    return jnp.cos(ang), jnp.sin(ang) * sign


def _rope(x, cos, sin_signed):
    lane = lax.broadcasted_iota(jnp.int32, x.shape, 1)
    first_half = (lane // ROPE_PAIRS) % 2 == 0
    partner = jnp.where(first_half, pltpu.roll(x, LANES - ROPE_PAIRS, 1), pltpu.roll(x, ROPE_PAIRS, 1))
    return x * cos + partner * sin_signed


def _inproj_kernel(x_ref, mod_ref, w_ref, b_ref, cos_ref, sin_ref,
                   q_ref, k_ref, v_ref, qb_ref, ib_ref, ff_ref, fb_ref, go_ref):
    i = pl.program_id(0)
    shift = mod_ref[0, 0:1, :]
    scale = mod_ref[0, 1:2, :]
    h = (x_ref[...] * (1.0 + scale) + shift).astype(BF16)
    y = _dot(h, w_ref[...]) + b_ref[...]
    offs = np.cumsum((0,) + IN_SIZES)
    q = y[:, offs[0]:offs[1]]
    k = y[:, offs[1]:offs[2]]
    v_ref[...] = y[:, offs[2]:offs[3]]
    qb_ref[...] = y[:, offs[3]:offs[4]]
    ib_ref[...] = y[:, offs[4]:offs[5]]
    ff_ref[...] = y[:, offs[5]:offs[6]]
    fb_ref[...] = y[:, offs[6]:offs[7]]
    go_ref[...] = y[:, offs[7]:offs[8]]
    is_latent = i * DENSE_ROWS >= N_PROMPT

    @pl.when(jnp.logical_not(is_latent))
    def _():
        q_ref[...] = q
        k_ref[...] = k

    @pl.when(is_latent)
    def _():
        cos = cos_ref[...]
        sin = sin_ref[...]
        for c in range(A_Q // LANES):
            q_ref[:, c * LANES:(c + 1) * LANES] = _rope(q[:, c * LANES:(c + 1) * LANES], cos, sin)
        k_ref[...] = _rope(k, cos, sin)


def inproj_even(x, mods, w_bf, b, cos, sin):
    nblk = N_TOK // DENSE_ROWS
    pos_blocks = DEC_SEQ // DENSE_ROWS

    def pos_map(i):
        return (jnp.maximum(i - N_PROMPT // DENSE_ROWS, 0) % pos_blocks, 0)

    row = lambda i: (i, 0)
    widths = (A_Q, A_KV, A_KV, B_QK, B_V, B_QK, B_QK, B_V)
    return pl.pallas_call(
        _inproj_kernel,
        grid=(nblk,),
        in_specs=[
            pl.BlockSpec((DENSE_ROWS, D_MODEL), row),
            pl.BlockSpec((1, 6, D_MODEL), lambda i: (_cond_of_block(i, DENSE_ROWS), 0, 0)),
            pl.BlockSpec((D_MODEL, D_IN_EVEN), lambda i: (0, 0)),
            pl.BlockSpec((1, D_IN_EVEN), lambda i: (0, 0)),
            pl.BlockSpec((DENSE_ROWS, LANES), pos_map),
            pl.BlockSpec((DENSE_ROWS, LANES), pos_map),
        ],
        out_specs=[pl.BlockSpec((DENSE_ROWS, w), row) for w in widths],
        out_shape=[jax.ShapeDtypeStruct((N_TOK, w), F32) for w in widths],
        compiler_params=_cparams(("parallel",)),
        name="inproj_even",
    )(x, mods, w_bf, b.reshape(1, D_IN_EVEN), cos, sin)


def _sink_attend(q, keys, vals, sink, masks):
    scores = []
    for kk, mask in zip(keys, masks):
        s = _dot_nt(q, kk) * SCALE_A
        if mask is not None:
            s = jnp.where(mask, s, MASK_VALUE)
        scores.append(s)
    m = sink
    for s in scores:
        m = jnp.maximum(m, jnp.max(s, axis=-1, keepdims=True))
    denom = jnp.exp(sink - m)
    acc = None
    for s, vv in zip(scores, vals):
        p = jnp.exp(s - m)
        denom = denom + jnp.sum(p, axis=-1, keepdims=True)
        pv = _dot(p.astype(BF16), vv)
        acc = pv if acc is None else acc + pv
    return acc / denom


def _attn_ctx_kernel(sink_ref, q_ref, k_ref, v_ref, o_ref):
    k = k_ref[...].astype(BF16)
    v = v_ref[...].astype(BF16)
    q = q_ref[...].astype(BF16)
    for h in range(N_HEADS_A):
        kv = h // GROUP_A
        qh = q[:, h * HEAD_DIM:(h + 1) * HEAD_DIM]
        kh = k[:, kv * HEAD_DIM:(kv + 1) * HEAD_DIM]
        vh = v[:, kv * HEAD_DIM:(kv + 1) * HEAD_DIM]
        o_ref[:, h * HEAD_DIM:(h + 1) * HEAD_DIM] = _sink_attend(qh, [kh], [vh], sink_ref[h], [None])


def attn_context(sink, q, k, v):
    row = lambda b: (b, 0)
    return pl.pallas_call(
        _attn_ctx_kernel,
        grid=(BATCH,),
        in_specs=[
            pl.BlockSpec(memory_space=pltpu.SMEM),
            pl.BlockSpec((SEQ, A_Q), row),
            pl.BlockSpec((SEQ, A_KV), row),
            pl.BlockSpec((SEQ, A_KV), row),
        ],
        out_specs=pl.BlockSpec((SEQ, A_Q), row),
        out_shape=jax.ShapeDtypeStruct((N_PROMPT, A_Q), F32),
        compiler_params=_cparams(("parallel",)),
        name="attn_context",
    )(sink, q, k, v)


def _attn_lat_kernel(sink_ref, q_ref, kp_ref, kc_ref, kn_ref, vp_ref, vc_ref, vn_ref, ck_ref, cv_ref, o_ref):
    n = pl.program_id(1)
    nb = DEC_SEQ // ATTN_BLOCK
    qi = lax.broadcasted_iota(jnp.int32, (ATTN_BLOCK, ATTN_BLOCK), 0)
    kj = lax.broadcasted_iota(jnp.int32, (ATTN_BLOCK, ATTN_BLOCK), 1)
    mask_prev = jnp.logical_and(kj - qi >= ATTN_BLOCK - WINDOW, n > 0)
    mask_next = jnp.logical_and(kj - qi <= WINDOW - ATTN_BLOCK, n < nb - 1)
    masks = [mask_prev, None, mask_next, None]
    q = q_ref[...].astype(BF16)
    kband = [r[...].astype(BF16) for r in (kp_ref, kc_ref, kn_ref)]
    vband = [r[...].astype(BF16) for r in (vp_ref, vc_ref, vn_ref)]
    ck = ck_ref[0].astype(BF16)
    cv = cv_ref[0].astype(BF16)
    for h in range(N_HEADS_A):
        kv = h // GROUP_A
        sl = slice(kv * HEAD_DIM, (kv + 1) * HEAD_DIM)
        qh = q[:, h * HEAD_DIM:(h + 1) * HEAD_DIM]
        keys = [kb[:, sl] for kb in kband] + [ck[:, sl]]
        vals = [vb[:, sl] for vb in vband] + [cv[:, sl]]
        o_ref[:, h * HEAD_DIM:(h + 1) * HEAD_DIM] = _sink_attend(qh, keys, vals, sink_ref[h], masks)


def attn_latent(sink, q, k, v, cache_k, cache_v):
    nb = DEC_SEQ // ATTN_BLOCK
    base = N_PROMPT // ATTN_BLOCK

    def blk(delta):
        return lambda b, n: (base + b * nb + jnp.clip(n + delta, 0, nb - 1), 0)

    kv_spec = lambda delta: pl.BlockSpec((ATTN_BLOCK, A_KV), blk(delta))
    cache_spec = pl.BlockSpec((1, PAST_LEN, A_KV), lambda b, n: (b, 0, 0))
    return pl.pallas_call(
        _attn_lat_kernel,
        grid=(DEC_BATCH, nb),
        in_specs=[
            pl.BlockSpec(memory_space=pltpu.SMEM),
            pl.BlockSpec((ATTN_BLOCK, A_Q), blk(0)),
            kv_spec(-1), kv_spec(0), kv_spec(1),
            kv_spec(-1), kv_spec(0), kv_spec(1),
            cache_spec, cache_spec,
        ],
        out_specs=pl.BlockSpec((ATTN_BLOCK, A_Q), lambda b, n: (b * nb + n, 0)),
        out_shape=jax.ShapeDtypeStruct((N_SAMPLE, A_Q), F32),
        compiler_params=_cparams(("parallel", "parallel")),
        name="attn_latent",
    )(sink, q, k, k, k, v, v, v, cache_k, cache_v)


def _hgrn_gate(f_pre, lb):
    f = jnp.maximum(lb, LB_FLOOR) + (1.0 - lb) * jax.nn.sigmoid(f_pre)
    return 1.0 - f, jnp.log(f)


def _scan_rows(x, reverse):
    n = x.shape[0]
    row = lax.broadcasted_iota(jnp.int32, x.shape, 0)
    sh = 1
    while sh < n:
        if reverse:
            x = x + jnp.where(row < n - sh, pltpu.roll(x, n - sh, 0), 0.0)
        else:
            x = x + jnp.where(row >= sh, pltpu.roll(x, sh, 0), 0.0)
        sh *= 2
    return x


def _chunk_bcast(x, c, pick, shift):
    nc = HG_BLOCK // c
    rows = x.reshape(nc, c, x.shape[-1])[:, pick:pick + 1, :]
    zero = jnp.zeros((1, 1, x.shape[-1]), x.dtype)
    if shift == -1:
        rows = jnp.concatenate([zero, rows[:-1]], axis=0)
    elif shift == 1:
        rows = jnp.concatenate([rows[1:], zero], axis=0)
    return jnp.broadcast_to(rows, (nc, c, x.shape[-1])).reshape(HG_BLOCK, x.shape[-1])


def _hgrn_block(q, kk, v, log_f, st, reverse):
    cum = _scan_rows(log_f, reverse)
    tot = cum[0:1, :] if reverse else cum[HG_BLOCK - 1:HG_BLOCK, :]
    o = _dot_nt((q * jnp.exp(cum)).astype(BF16), st.astype(BF16))
    kd = (kk * jnp.exp(tot - cum)).astype(BF16)
    v_bf = v.astype(BF16)
    u_t = lax.dot_general(v_bf, kd, (((0,), (0,)), ((), ())), preferred_element_type=F32)
    st_new = st * jnp.exp(tot) + u_t
    ti = lax.broadcasted_iota(jnp.int32, (HG_BLOCK, HG_BLOCK), 0)
    si = lax.broadcasted_iota(jnp.int32, (HG_BLOCK, HG_BLOCK), 1)
    a = jnp.zeros((HG_BLOCK, HG_BLOCK), F32)
    c = HG_SUB
    while c < HG_BLOCK:
        if reverse:
            bound_t = _chunk_bcast(cum, c, 0, 1)
            bound_s = _chunk_bcast(cum, c, 0, 0)
            mask = jnp.logical_and((ti // c) % 2 == 0, si // c == ti // c + 1)
        else:
            bound_t = _chunk_bcast(cum, c, c - 1, -1)
            bound_s = _chunk_bcast(cum, c, c - 1, 0)
            mask = jnp.logical_and((ti // c) % 2 == 1, si // c == ti // c - 1)
        qc = (q * jnp.exp(cum - bound_t)).astype(BF16)
        kc = (kk * jnp.exp(bound_s - cum)).astype(BF16)
        a = a + jnp.where(mask, _dot_nt(qc, kc), 0.0)
        c *= 2
    o = o + _dot(a.astype(BF16), v_bf)
    nsub = HG_BLOCK // HG_SUB
    q3 = q.reshape(nsub, HG_SUB, HGRN_DK)
    k3 = kk.reshape(nsub, HG_SUB, HGRN_DK)
    v3 = v.reshape(nsub, HG_SUB, HGRN_DV)
    cum3 = cum.reshape(nsub, HG_SUB, HGRN_DK)
    t_off = lax.broadcasted_iota(jnp.int32, (1, HG_SUB, 1), 1)
    od = jnp.zeros((nsub, HG_SUB, HGRN_DV), F32)
    for s in range(HG_SUB):
        causal = (t_off <= s) if reverse else (t_off >= s)
        decay = jnp.exp(jnp.where(causal, cum3 - cum3[:, s:s + 1, :], MASK_VALUE))
        score = jnp.sum(q3 * decay * k3[:, s:s + 1, :], axis=-1, keepdims=True)
        od = od + score * v3[:, s:s + 1, :]
    return o + od.reshape(HG_BLOCK, HGRN_DV), st_new


def _hgrn_kernel(*refs, n_blocks, has_state_in, has_state_out):
    refs = list(refs)
    qb_ref, ib_ref, ff_ref, fb_ref, go_ref, lbf_ref, lbb_ref, nw_ref = refs[:8]
    refs = refs[8:]
    s0_ref = refs.pop(0) if has_state_in else None
    o_ref = refs.pop(0)
    so_ref = refs.pop(0) if has_state_out else None
    acc_ref = refs.pop(0)

    def run(reverse):
        f_ref, lb_ref = (fb_ref, lbb_ref) if reverse else (ff_ref, lbf_ref)
        lb = lb_ref[...]
        if has_state_in:
            st0 = s0_ref[0, 0, 1 if reverse else 0, 0].T
        else:
            st0 = jnp.zeros((HGRN_DV, HGRN_DK), F32)

        def body(it, st):
            blk = (n_blocks - 1 - it) if reverse else it
            rows = pl.ds(pl.multiple_of(blk * HG_BLOCK, HG_BLOCK), HG_BLOCK)
            qpre = qb_ref[rows, :]
            q = qpre * jax.nn.sigmoid(qpre)
            kk, log_f = _hgrn_gate(f_ref[rows, :], lb)
            o, st = _hgrn_block(q, kk, ib_ref[rows, :], log_f, st, reverse)
            if reverse:
                acc_ref[rows, :] = acc_ref[rows, :] + o
            else:
                acc_ref[rows, :] = o
            return st

        return lax.fori_loop(0, n_blocks, body, st0)

    st_f = run(False)
    st_b = run(True)
    if has_state_out:
        so_ref[0, 0, 0] = st_f.T
        so_ref[0, 1, 0] = st_b.T
    o = acc_ref[...]
    o = o * lax.rsqrt(jnp.mean(o * o, axis=-1, keepdims=True) + RMS_EPS) * nw_ref[...]
    g = go_ref[...]
    o_ref[...] = o * (g * jax.nn.sigmoid(g))


def hgrn_mixer(qb, ib, ff, fb, go, lb_f, lb_b, norm_w, *, latent, state_in=None, layer_j=0):
    t_len = DEC_SEQ if latent else SEQ
    n_seq = DEC_BATCH if latent else BATCH
    row0 = N_PROMPT // t_len if latent else 0
    tok = pl.BlockSpec((t_len, HGRN_DK), lambda b, h: (row0 + b, h))
    vec = pl.BlockSpec((1, HGRN_DK), lambda b, h: (0, h))
    in_specs = [tok] * 5 + [vec] * 3
    args = [qb, ib, ff, fb, go, lb_f.reshape(1, B_QK), lb_b.reshape(1, B_QK), norm_w.reshape(1, B_V)]
    if latent:
        in_specs.append(pl.BlockSpec((1, 1, 2, 1, HGRN_DK, HGRN_DV), lambda b, h: (b, layer_j, 0, h, 0, 0)))
        args.append(state_in)
    out_specs = [pl.BlockSpec((t_len, HGRN_DV), lambda b, h: (b, h))]
    out_shape = [jax.ShapeDtypeStruct((n_seq * t_len, B_V), F32)]
    if not latent:
        out_specs.append(pl.BlockSpec((1, 2, 1, HGRN_DK, HGRN_DV), lambda b, h: (b, 0, h, 0, 0)))
        out_shape.append(jax.ShapeDtypeStruct((BATCH, 2, N_HEADS_B, HGRN_DK, HGRN_DV), F32))
    kern = functools.partial(_hgrn_kernel, n_blocks=t_len // HG_BLOCK, has_state_in=latent, has_state_out=not latent)
    return pl.pallas_call(
        kern,
        grid=(n_seq, N_HEADS_B),
        in_specs=in_specs,
        out_specs=out_specs,
        out_shape=out_shape,
        scratch_shapes=[pltpu.VMEM((t_len, HGRN_DV), F32)],
        compiler_params=_cparams(("parallel", "parallel")),
        name="hgrn_latent" if latent else "hgrn_context",
    )(*args)


def _outproj_kernel(x_ref, mod_ref, ap_ref, as_ref, rp_ref, rs_ref, w_ref, g_ref, b_ref, o_ref):
    is_latent = pl.program_id(0) * DENSE_ROWS >= N_PROMPT
    attn = jnp.where(is_latent, as_ref[...], ap_ref[...]).astype(BF16)
    rec = jnp.where(is_latent, rs_ref[...], rp_ref[...]).astype(BF16)
    y = _dot(attn, w_ref[0:A_Q, :]) + _dot(rec, w_ref[A_Q:A_Q + B_V, :])
    z = DN_ALPHA * x_ref[...] + mod_ref[0, 2:3, :] * y
    o_ref[...] = _layer_norm(z, g_ref[...], b_ref[...])


def outproj_even(x, mods, attn_p, attn_s, rec_p, rec_s, w_bf, ln_g, ln_b):
    nblk = N_TOK // DENSE_ROWS
    npb = N_PROMPT // DENSE_ROWS
    row = lambda i: (i, 0)
    prow = lambda i: (jnp.minimum(i, npb - 1), 0)
    srow = lambda i: (jnp.maximum(i - npb, 0), 0)
    vec = pl.BlockSpec((1, D_MODEL), lambda i: (0, 0))
    return pl.pallas_call(
        _outproj_kernel,
        grid=(nblk,),
        in_specs=[
            pl.BlockSpec((DENSE_ROWS, D_MODEL), row),
            pl.BlockSpec((1, 6, D_MODEL), lambda i: (_cond_of_block(i, DENSE_ROWS), 0, 0)),
            pl.BlockSpec((DENSE_ROWS, A_Q), prow),
            pl.BlockSpec((DENSE_ROWS, A_Q), srow),
            pl.BlockSpec((DENSE_ROWS, B_V), prow),
            pl.BlockSpec((DENSE_ROWS, B_V), srow),
            pl.BlockSpec((A_Q + B_V, D_MODEL), lambda i: (0, 0)),
            vec, vec,
        ],
        out_specs=pl.BlockSpec((DENSE_ROWS, D_MODEL), row),
        out_shape=jax.ShapeDtypeStruct((N_TOK, D_MODEL), F32),
        compiler_params=_cparams(("parallel",)),
        name="outproj_even",
    )(x, mods, attn_p, attn_s, rec_p, rec_s, w_bf, ln_g.reshape(1, D_MODEL), ln_b.reshape(1, D_MODEL))


def _conv_in_kernel(x_ref, mod_ref, w_ref, b_ref, u_ref):
    h = (x_ref[...] * (1.0 + mod_ref[0, 1:2, :]) + mod_ref[0, 0:1, :]).astype(BF16)
    a = _dot(h, w_ref[:, 0:D_MODEL]) + b_ref[:, 0:D_MODEL]
    gt = _dot(h, w_ref[:, D_MODEL:2 * D_MODEL]) + b_ref[:, D_MODEL:2 * D_MODEL]
    u_ref[...] = a * jax.nn.sigmoid(gt)


def conv_in(x, mods, w_bf, b):
    row = lambda i: (i, 0)
    return pl.pallas_call(
        _conv_in_kernel,
        grid=(N_TOK // DENSE_ROWS,),
        in_specs=[
            pl.BlockSpec((DENSE_ROWS, D_MODEL), row),
            pl.BlockSpec((1, 6, D_MODEL), lambda i: (_cond_of_block(i, DENSE_ROWS), 0, 0)),
            pl.BlockSpec((D_MODEL, 2 * D_MODEL), lambda i: (0, 0)),
            pl.BlockSpec((1, 2 * D_MODEL), lambda i: (0, 0)),
        ],
        out_specs=pl.BlockSpec((DENSE_ROWS, D_MODEL), row),
        out_shape=jax.ShapeDtypeStruct((N_TOK, D_MODEL), F32),
        compiler_params=_cparams(("parallel",)),
        name="conv_in",
    )(x, mods, w_bf, b.reshape(1, 2 * D_MODEL))


CONV_HALO = 16
CONV_LANES = 256


CONV_SHIFT_ROWS = ROW_BLOCK + 2 * CONV_HALO - SUBLANES
CONV_ROW_CHUNK = 128


def _conv_out_kernel(x_ref, mod_ref, up_ref, uc_ref, un_ref, dw_ref, dwb_ref, cg_ref, cb_ref,
                     w_ref, b_ref, g_ref, bb_ref, o_ref, pad_ref, acc_ref, sh_ref):
    i = pl.program_id(0)
    blocks_per_seq = DEC_SEQ // ROW_BLOCK
    j = i - N_PROMPT // ROW_BLOCK
    is_latent = j >= 0
    has_prev = jnp.logical_and(is_latent, j % blocks_per_seq != 0)
    has_next = jnp.logical_and(is_latent, j % blocks_per_seq != blocks_per_seq - 1)
    pad_ref[0:CONV_HALO, :] = jnp.where(has_prev, up_ref[...], 0.0)
    pad_ref[CONV_HALO:CONV_HALO + ROW_BLOCK, :] = uc_ref[...]
    pad_ref[CONV_HALO + ROW_BLOCK:, :] = jnp.where(has_next, un_ref[...], 0.0)
    first = CONV_HALO - CONV_WIDTH // 2
    for c in range(D_MODEL // CONV_LANES):
        lanes = slice(c * CONV_LANES, (c + 1) * CONV_LANES)
        for s in range(SUBLANES):
            sh_ref[s] = pad_ref[s:s + CONV_SHIFT_ROWS, lanes]
        for r0 in range(0, ROW_BLOCK, CONV_ROW_CHUNK):
            acc = jnp.zeros((CONV_ROW_CHUNK, CONV_LANES), F32) + dwb_ref[:, lanes]
            for tap in range(CONV_WIDTH):
                whole, s = divmod(first + tap, SUBLANES)
                rows = slice(r0 + whole * SUBLANES, r0 + whole * SUBLANES + CONV_ROW_CHUNK)
                acc = acc + sh_ref[s, rows, :] * dw_ref[tap:tap + 1, lanes]
            acc_ref[r0:r0 + CONV_ROW_CHUNK, lanes] = acc
    u = _layer_norm(acc_ref[...], cg_ref[...], cb_ref[...])
    u = (u * jax.nn.sigmoid(u)).astype(BF16)
    y = _dot(u, w_ref[...]) + b_ref[...]
    z = DN_ALPHA * x_ref[...] + mod_ref[0, 2:3, :] * y
    o_ref[...] = _layer_norm(z, g_ref[...], bb_ref[...])


def conv_out(x, mods, u, dw, dw_b, cln_g, cln_b, w_bf, b_out, ln_g, ln_b):
    nblk = N_TOK // ROW_BLOCK
    ratio = ROW_BLOCK // CONV_HALO
    nhalo = N_TOK // CONV_HALO
    row = lambda i: (i, 0)
    vec = pl.BlockSpec((1, D_MODEL), lambda i: (0, 0))
    r1 = lambda a: a.reshape(1, D_MODEL)
    return pl.pallas_call(
        _conv_out_kernel,
        grid=(nblk,),
        in_specs=[
            pl.BlockSpec((ROW_BLOCK, D_MODEL), row),
            pl.BlockSpec((1, 6, D_MODEL), lambda i: (_cond_of_block(i, ROW_BLOCK), 0, 0)),
            pl.BlockSpec((CONV_HALO, D_MODEL), lambda i: (jnp.maximum(i * ratio - 1, 0), 0)),
            pl.BlockSpec((ROW_BLOCK, D_MODEL), row),
            pl.BlockSpec((CONV_HALO, D_MODEL), lambda i: (jnp.minimum((i + 1) * ratio, nhalo - 1), 0)),
            pl.BlockSpec((CONV_WIDTH, D_MODEL), lambda i: (0, 0)),
            vec, vec, vec,
            pl.BlockSpec((D_MODEL, D_MODEL), lambda i: (0, 0)),
            vec, vec, vec,
        ],
        out_specs=pl.BlockSpec((ROW_BLOCK, D_MODEL), row),
        out_shape=jax.ShapeDtypeStruct((N_TOK, D_MODEL), F32),
        scratch_shapes=[pltpu.VMEM((ROW_BLOCK + 2 * CONV_HALO, D_MODEL), F32),
                        pltpu.VMEM((ROW_BLOCK, D_MODEL), F32),
                        pltpu.VMEM((SUBLANES, CONV_SHIFT_ROWS, CONV_LANES), F32)],
        compiler_params=_cparams(("parallel",)),
        name="conv_out",
    )(x, mods, u, u, u, dw, r1(dw_b), r1(cln_g), r1(cln_b), w_bf, r1(b_out), r1(ln_g), r1(ln_b))


def _modulate2(x_ref, mod_ref):
    return x_ref[...] * (1.0 + mod_ref[0, 4:5, :]) + mod_ref[0, 3:4, :]


def _route_kernel(x_ref, mod_ref, wr_ref, br_ref, idx_ref, gate_ref, rank_ref, cnt_ref, carry_ref):
    i = pl.program_id(0)

    @pl.when(i == 0)
    def _():
        carry_ref[...] = jnp.zeros_like(carry_ref)

    h = _modulate2(x_ref, mod_ref)
    h_hi = h.astype(BF16)
    h_lo = (h - h_hi.astype(F32)).astype(BF16)
    w = wr_ref[...]
    w_hi = w.astype(BF16)
    w_lo = (w - w_hi.astype(F32)).astype(BF16)
    logits = _dot_nt(w_hi, h_hi) + (_dot_nt(w_hi, h_lo) + _dot_nt(w_lo, h_hi)) + br_ref[...]
    eidx = lax.broadcasted_iota(jnp.int32, logits.shape, 0)
    vals = logits
    sels, tops = [], []
    for k in range(TOP_K):
        m = jnp.max(vals, axis=0, keepdims=True)
        idx = jnp.min(jnp.where(vals == m, eidx, N_EXPERTS), axis=0, keepdims=True)
        sel = eidx == idx
        idx_ref[k:k + 1, :] = idx
        sels.append(sel)
        tops.append(m)
        vals = jnp.where(sel, -jnp.inf, vals)
    exps = [jnp.exp(t - tops[0]) for t in tops]
    total = exps[0] + exps[1] + exps[2] + exps[3]
    for k in range(TOP_K):
        gate_ref[k:k + 1, :] = exps[k] / total
    onehot = jnp.zeros(logits.shape, F32)
    for sel in sels:
        onehot = onehot + sel.astype(F32)
    ta = lax.broadcasted_iota(jnp.int32, (ROW_BLOCK, ROW_BLOCK), 0)
    tb = lax.broadcasted_iota(jnp.int32, (ROW_BLOCK, ROW_BLOCK), 1)
    before = _dot(onehot.astype(BF16), (ta < tb).astype(BF16)) + carry_ref[:, 0:1]
    for k in range(TOP_K):
        rank = jnp.sum(jnp.where(sels[k], before, 0.0), axis=0, keepdims=True)
        rank_ref[k:k + 1, :] = rank.astype(jnp.int32)
    carry = carry_ref[...] + jnp.sum(onehot, axis=1, keepdims=True)
    carry_ref[...] = carry
    cnt_ref[...] = carry.astype(jnp.int32)


def moe_route(x, mods, wr_t, b_r):
    tok = pl.BlockSpec((TOP_K, ROW_BLOCK), lambda i: (0, i))
    return pl.pallas_call(
        _route_kernel,
        grid=(N_TOK // ROW_BLOCK,),
        in_specs=[
            pl.BlockSpec((ROW_BLOCK, D_MODEL), lambda i: (i, 0)),
            pl.BlockSpec((1, 6, D_MODEL), lambda i: (_cond_of_block(i, ROW_BLOCK), 0, 0)),
            pl.BlockSpec((N_EXPERTS, D_MODEL), lambda i: (0, 0)),
            pl.BlockSpec((N_EXPERTS, 1), lambda i: (0, 0)),
        ],
        out_specs=[tok, tok, tok, pl.BlockSpec((N_EXPERTS, LANES), lambda i: (0, 0))],
        out_shape=[jax.ShapeDtypeStruct((TOP_K, N_TOK), jnp.int32),
                   jax.ShapeDtypeStruct((TOP_K, N_TOK), F32),
                   jax.ShapeDtypeStruct((TOP_K, N_TOK), jnp.int32),
                   jax.ShapeDtypeStruct((N_EXPERTS, LANES), jnp.int32)],
        scratch_shapes=[pltpu.VMEM((N_EXPERTS, LANES), F32)],
        compiler_params=_cparams(("arbitrary",)),
        name="moe_route",
    )(x, mods, wr_t, b_r.reshape(N_EXPERTS, 1))


HALF_D = D_MODEL // 2
HI_MASK = 0xFFFF0000


def _pack_bf16_pairs(h):
    half = h.shape[1] // 2
    lo = lax.bitcast_convert_type(h[:, :half].astype(BF16).astype(F32), jnp.uint32)
    hi = lax.bitcast_convert_type(h[:, half:].astype(BF16).astype(F32), jnp.uint32)
    return (lo >> 16) | (hi & jnp.uint32(HI_MASK))


def _unpack_bf16_pairs(w):
    lo = lax.bitcast_convert_type(w << 16, F32).astype(BF16)
    hi = lax.bitcast_convert_type(w & jnp.uint32(HI_MASK), F32).astype(BF16)
    return lo, hi


def _scatter_kernel(dest_ref, zstart_ref, x_ref, mod_ref, xs_ref, pk_ref, zero_ref, sem, zsem):
    i = pl.program_id(0)

    def zero_copy(start):
        start = pl.multiple_of(start, MOE_ROWS)
        return pltpu.make_async_copy(zero_ref, xs_ref.at[pl.ds(start, MOE_ROWS), :], zsem)

    def zero_blocks(fn):
        for e in range(N_EXPERTS):
            @pl.when(zstart_ref[e] >= 0)
            def _():
                fn(zero_copy(jnp.maximum(zstart_ref[e], 0)))

        def unused(b, carry):
            fn(zero_copy(b * MOE_ROWS))
            return carry

        lax.fori_loop(zstart_ref[N_EXPERTS], MOE_BLOCKS, unused, 0)

    @pl.when(i == 0)
    def _():
        zero_ref[...] = jnp.zeros_like(zero_ref)
        zero_blocks(lambda cp: cp.start())
        zero_blocks(lambda cp: cp.wait())

    base = i * ROW_BLOCK
    last = pl.num_programs(0) - 1

    def drain(slot):
        for k in range(TOP_K):
            pltpu.make_async_copy(pk_ref.at[slot], xs_ref.at[pl.ds(0, ROW_BLOCK), :], sem.at[slot]).wait()

    for slot in range(2):
        @pl.when(i % 2 == slot)
        def _():
            @pl.when(i >= 2)
            def _():
                drain(slot)

            pk_ref[slot] = _pack_bf16_pairs(_modulate2(x_ref, mod_ref))
            for t in range(ROW_BLOCK):
                for k in range(TOP_K):
                    row = dest_ref[k * N_TOK + base + t]
                    pltpu.make_async_copy(pk_ref.at[slot, pl.ds(t, 1), :], xs_ref.at[pl.ds(row, 1), :],
                                          sem.at[slot]).start(priority=k % 2)

            @pl.when(i == last)
            def _():
                drain(slot)

                @pl.when(i >= 1)
                def _():
                    drain(1 - slot)


def moe_scatter(dest_flat, zstart, x, mods):
    return pl.pallas_call(
        _scatter_kernel,
        grid_spec=pltpu.PrefetchScalarGridSpec(
            num_scalar_prefetch=2,
            grid=(N_TOK // ROW_BLOCK,),
            in_specs=[
                pl.BlockSpec((ROW_BLOCK, D_MODEL), lambda i, d, z: (i, 0)),
                pl.BlockSpec((1, 6, D_MODEL), lambda i, d, z: (_cond_of_block(i, ROW_BLOCK), 0, 0)),
            ],
            out_specs=pl.BlockSpec(memory_space=pl.ANY),
            scratch_shapes=[pltpu.VMEM((2, ROW_BLOCK, HALF_D), jnp.uint32),
                            pltpu.VMEM((MOE_ROWS, HALF_D), jnp.uint32),
                            pltpu.SemaphoreType.DMA((2,)), pltpu.SemaphoreType.DMA],
        ),
        out_shape=jax.ShapeDtypeStruct((MOE_R, HALF_D), jnp.uint32),
        compiler_params=_cparams(("arbitrary",)),
        name="moe_scatter",
    )(dest_flat, zstart, x, mods)


N_CHUNK = 256


def _expert_kernel(be_ref, nused_ref, nexte_ref, par_ref, xs_ref, wg_hbm, wu_hbm, wd_hbm, bg_ref, bu_ref, bd_ref,
                   ys_ref, wbuf, act_ref, wg_bf, wu_bf, wd_bf, wsem, *, layer):
    i = pl.program_id(0)
    new_expert = jnp.logical_or(i == 0, be_ref[i] != be_ref[jnp.maximum(i - 1, 0)])
    slot = par_ref[i]

    def weight_copies(e, s):
        return [pltpu.make_async_copy(w.at[layer, e], wbuf.at[s, j], wsem.at[s])
                for j, w in enumerate((wg_hbm, wu_hbm, wd_hbm))]

    @pl.when(i == 0)
    def _():
        for cp in weight_copies(be_ref[0], 0):
            cp.start()

    @pl.when(jnp.logical_and(new_expert, i < nused_ref[0]))
    def _():
        for cp in weight_copies(be_ref[i], slot):
            cp.wait()

        @pl.when(nexte_ref[i] >= 0)
        def _():
            for cp in weight_copies(jnp.maximum(nexte_ref[i], 0), 1 - slot):
                cp.start()

        wg_bf[...] = wbuf[slot, 0].astype(BF16)
        wu_bf[...] = wbuf[slot, 1].astype(BF16)
        wd_bf[...] = wbuf[slot, 2].astype(BF16)

    @pl.when(i < nused_ref[0])
    def _():
        lo, hi = _unpack_bf16_pairs(xs_ref[...])
        x = jnp.concatenate([lo, hi], axis=1)
        for n in range(D_EXPERT // N_CHUNK):
            cols = slice(n * N_CHUNK, (n + 1) * N_CHUNK)
            gt = _dot(x, wg_bf[:, cols]) + bg_ref[0, 0, :, cols]
            up = _dot(x, wu_bf[:, cols]) + bu_ref[0, 0, :, cols]
            gt = jnp.minimum(gt, SWIGLU_LIMIT)
            up = jnp.clip(up, -SWIGLU_LIMIT, SWIGLU_LIMIT)
            act_ref[:, cols] = ((up + 1.0) * gt * jax.nn.sigmoid(SWIGLU_ALPHA * gt)).astype(BF16)
        act = act_ref[...]
        for n in range(HALF_D // N_CHUNK):
            cols = slice(n * N_CHUNK, (n + 1) * N_CHUNK)
            cols_hi = slice(HALF_D + n * N_CHUNK, HALF_D + (n + 1) * N_CHUNK)
            y_lo = _dot(act, wd_bf[:, cols]) + bd_ref[0, 0, :, cols]
            y_hi = _dot(act, wd_bf[:, cols_hi]) + bd_ref[0, 0, :, cols_hi]
            ys_ref[:, cols] = _pack_bf16_pairs(jnp.concatenate([y_lo, y_hi], axis=1))

    @pl.when(i >= nused_ref[0])
    def _():
        ys_ref[...] = jnp.zeros_like(ys_ref)


def moe_experts(block_e, n_used, next_e, parity, xs, w_g, w_u, w_d, b_g, b_u, b_d, layer):
    rows = lambda i, be, nu, ne, pa: (i, 0)
    wspec = pl.BlockSpec(memory_space=pl.ANY)
    bspec = pl.BlockSpec((1, 1, 1, D_EXPERT), lambda i, be, nu, ne, pa: (layer, be[i], 0, 0))
    r4 = lambda b: b.reshape(DEPTH, N_EXPERTS, 1, D_EXPERT)
    return pl.pallas_call(
        functools.partial(_expert_kernel, layer=layer),
        grid_spec=pltpu.PrefetchScalarGridSpec(
            num_scalar_prefetch=4,
            grid=(MOE_BLOCKS,),
            in_specs=[pl.BlockSpec((MOE_ROWS, HALF_D), rows), wspec, wspec, wspec, bspec, bspec, bspec],
            out_specs=pl.BlockSpec((MOE_ROWS, HALF_D), rows),
            scratch_shapes=[pltpu.VMEM((2, 3, D_MODEL, D_EXPERT), F32),
                            pltpu.VMEM((MOE_ROWS, D_EXPERT), BF16)]
                           + [pltpu.VMEM((D_MODEL, D_EXPERT), BF16)] * 3
                           + [pltpu.SemaphoreType.DMA((2,))],
        ),
        out_shape=jax.ShapeDtypeStruct((MOE_R, HALF_D), jnp.uint32),
        compiler_params=_cparams(("arbitrary",)),
        name="moe_experts",
    )(block_e, n_used, next_e, parity, xs, w_g, w_u, w_d, r4(b_g), r4(b_u), r4(b_d))


def _combine_kernel(dest_ref, x_ref, mod_ref, gate_ref, ys_ref, g_ref, b_ref, *rest, split_out):
    out_refs, (buf_ref, sem) = rest[:-2], rest[-2:]
    i = pl.program_id(0)
    last = pl.num_programs(0) - 1

    def gather(step, slot):
        base = step * COMB_TOK
        for t in range(COMB_TOK):
            for k in range(TOP_K):
                row = dest_ref[k * N_TOK + base + t]
                pltpu.make_async_copy(ys_ref.at[pl.ds(row, 1), :], buf_ref.at[slot, k, pl.ds(t, 1), :],
                                      sem.at[slot]).start(priority=k % 2)

    @pl.when(i == 0)
    def _():
        gather(0, 0)

    for slot in range(2):
        @pl.when(jnp.logical_and(i < last, i % 2 == slot))
        def _():
            gather(i + 1, 1 - slot)

    eye = (lax.broadcasted_iota(jnp.int32, (COMB_TOK, COMB_TOK), 0)
           == lax.broadcasted_iota(jnp.int32, (COMB_TOK, COMB_TOK), 1))
    gates = gate_ref[...]
    cols = [jnp.sum(jnp.where(eye, gates[k:k + 1, :], 0.0), axis=1, keepdims=True) for k in range(TOP_K)]
    cur = i % 2
    for k in range(TOP_K):
        pltpu.make_async_copy(ys_ref.at[pl.ds(0, COMB_TOK), :], buf_ref.at[cur, k], sem.at[cur]).wait()
    y_lo = y_hi = None
    for k in range(TOP_K):
        w = buf_ref[cur, k]
        lo = cols[k] * lax.bitcast_convert_type(w << 16, F32)
        hi = cols[k] * lax.bitcast_convert_type(w & jnp.uint32(HI_MASK), F32)
        y_lo = lo if y_lo is None else y_lo + lo
        y_hi = hi if y_hi is None else y_hi + hi
    y = jnp.concatenate([y_lo, y_hi], axis=1)
    z = DN_ALPHA * x_ref[...] + mod_ref[0, 5:6, :] * y
    res = _layer_norm(z, g_ref[...], b_ref[...])
    if split_out:
        is_prompt = i < N_PROMPT // COMB_TOK

        @pl.when(is_prompt)
        def _():
            out_refs[0][...] = res

        @pl.when(jnp.logical_not(is_prompt))
        def _():
            out_refs[1][...] = res
    else:
        out_refs[0][...] = res


def moe_combine(dest_flat, x, mods, gates_t, ys, ln_g, ln_b, split_out=False):
    vec = pl.BlockSpec((1, D_MODEL), lambda i, d: (0, 0))
    npb = N_PROMPT // COMB_TOK
    if split_out:
        out_specs = [pl.BlockSpec((COMB_TOK, D_MODEL), lambda i, d: (jnp.minimum(i, npb - 1), 0)),
                     pl.BlockSpec((COMB_TOK, D_MODEL), lambda i, d: (jnp.maximum(i - npb, 0), 0))]
        out_shape = [jax.ShapeDtypeStruct((N_PROMPT, D_MODEL), F32), jax.ShapeDtypeStruct((N_SAMPLE, D_MODEL), F32)]
    else:
        out_specs = pl.BlockSpec((COMB_TOK, D_MODEL), lambda i, d: (i, 0))
        out_shape = jax.ShapeDtypeStruct((N_TOK, D_MODEL), F32)
    return pl.pallas_call(
        functools.partial(_combine_kernel, split_out=split_out),
        grid_spec=pltpu.PrefetchScalarGridSpec(
            num_scalar_prefetch=1,
            grid=(N_TOK // COMB_TOK,),
            in_specs=[
                pl.BlockSpec((COMB_TOK, D_MODEL), lambda i, d: (i, 0)),
                pl.BlockSpec((1, 6, D_MODEL), lambda i, d: (_cond_of_block(i, COMB_TOK), 0, 0)),
                pl.BlockSpec((TOP_K, COMB_TOK), lambda i, d: (0, i)),
                pl.BlockSpec(memory_space=pl.ANY),
                vec, vec,
            ],
            out_specs=out_specs,
            scratch_shapes=[pltpu.VMEM((2, TOP_K, COMB_TOK, HALF_D), jnp.uint32), pltpu.SemaphoreType.DMA((2,))],
        ),
        out_shape=out_shape,
        compiler_params=_cparams(("arbitrary",)),
        name="moe_combine",
    )(dest_flat, x, mods, gates_t, ys, ln_g.reshape(1, D_MODEL), ln_b.reshape(1, D_MODEL))


def moe_layer(x, mods, layer, router_w, router_b, w_g, b_g, w_u, b_u, w_d, b_d, ln_g, ln_b, split_out=False):
    idx_t, gates_t, rank_t, counts = moe_route(x, mods, router_w[layer].T, router_b[layer])
    counts = counts[:, 0]
    padded = (counts + MOE_ROWS - 1) // MOE_ROWS * MOE_ROWS
    ends = jnp.cumsum(padded)
    base = ends - padded
    n_used = (ends[-1] // MOE_ROWS).astype(jnp.int32)
    block_start = jnp.arange(MOE_BLOCKS, dtype=jnp.int32) * MOE_ROWS
    block_e = jnp.sum(block_start[:, None] >= ends[None, :], axis=1).astype(jnp.int32)
    used = padded > 0
    experts = jnp.arange(N_EXPERTS, dtype=jnp.int32)
    block_e = jnp.minimum(block_e, jnp.max(jnp.where(used, experts, 0)))
    zstart = jnp.where(padded > 0, ends - MOE_ROWS, -1).astype(jnp.int32)
    zstart = jnp.concatenate([zstart, n_used.reshape(1)])
    onehot = idx_t[:, :, None] == jnp.arange(N_EXPERTS, dtype=jnp.int32)[None, None, :]
    dest = rank_t + jnp.sum(jnp.where(onehot, base[None, None, :], 0), axis=-1)
    dest_flat = dest.reshape(-1).astype(jnp.int32)
    xs = moe_scatter(dest_flat, zstart, x, mods)
    later_used = jnp.logical_and(used[None, :], experts[None, :] > experts[:, None])
    next_used = jnp.min(jnp.where(later_used, experts[None, :], N_EXPERTS), axis=1)
    next_used = jnp.where(next_used == N_EXPERTS, -1, next_used).astype(jnp.int32)
    run_parity = ((jnp.cumsum(used.astype(jnp.int32)) - 1) % 2).astype(jnp.int32)
    of_block = block_e[:, None] == experts[None, :]
    block_next = jnp.sum(jnp.where(of_block, next_used[None, :], 0), axis=1).astype(jnp.int32)
    block_parity = jnp.sum(jnp.where(of_block, run_parity[None, :], 0), axis=1).astype(jnp.int32)
    ys = moe_experts(block_e, n_used.reshape(1), block_next, block_parity, xs,
                     w_g, w_u, w_d, b_g, b_u, b_d, layer)
    return moe_combine(dest_flat, x, mods, gates_t, ys, ln_g, ln_b, split_out=split_out)


def kernel(x_prompt, x_sample, c, cache_k, cache_v, state_hgrn, c_ctx, w_ada, b_ada, ln_g, ln_b,
           w_in_even, b_in_even, attn_sink, hgrn_lb, hgrn_norm, w_out_even,
           conv_w_in, conv_b_in, conv_dw, conv_dw_b, conv_ln_g, conv_ln_b, conv_w_out, conv_b_out,
           router_w, router_b, moe_w_gate, moe_b_gate, moe_w_up, moe_b_up, moe_w_down, moe_b_down):
    x = jnp.concatenate([x_prompt.reshape(N_PROMPT, D_MODEL), x_sample.reshape(N_SAMPLE, D_MODEL)], axis=0)
    cond = jnp.concatenate([c_ctx[None, :], c, jnp.zeros((COND_ROWS - N_COND, D_MODEL), F32)], axis=0)
    mods_all = adaln_all(cond, w_ada, b_ada).reshape(DEPTH, COND_ROWS, 6, D_MODEL)
    lb = jax.nn.softmax(hgrn_lb.astype(F32), axis=1)
    lb = jnp.cumsum(lb, axis=1) - lb[:, :1]
    cos, sin = _rope_tables()
    new_k, new_v, new_s = [], [], []
    for layer in range(DEPTH):
        j = layer // 2
        mods = mods_all[layer]
        if layer % 2 == 0:
            q, k, v, qb, ib, ff, fb, go = inproj_even(x, mods, w_in_even[j].astype(BF16), b_in_even[j], cos, sin)
            new_k.append(k[:N_PROMPT].reshape(BATCH, SEQ, N_KV_A, HEAD_DIM))
            new_v.append(v[:N_PROMPT].reshape(BATCH, SEQ, N_KV_A, HEAD_DIM))
            attn_p = attn_context(attn_sink[j], q, k, v)
            attn_s = attn_latent(attn_sink[j], q, k, v,
                                 cache_k[:, j].reshape(DEC_BATCH, PAST_LEN, A_KV),
                                 cache_v[:, j].reshape(DEC_BATCH, PAST_LEN, A_KV))
            hg = (qb, ib, ff, fb, go, lb[0, j], lb[1, j], hgrn_norm[j])
            rec_p, states = hgrn_mixer(*hg, latent=False)
            rec_s, = hgrn_mixer(*hg, latent=True, state_in=state_hgrn, layer_j=j)
            new_s.append(states)
            x = outproj_even(x, mods, attn_p, attn_s, rec_p, rec_s, w_out_even[j].astype(BF16),
                             ln_g[layer, 0], ln_b[layer, 0])
        else:
            u = conv_in(x, mods, conv_w_in[j].astype(BF16), conv_b_in[j])
            x = conv_out(x, mods, u, conv_dw[j], conv_dw_b[j], conv_ln_g[j], conv_ln_b[j],
                         conv_w_out[j].astype(BF16), conv_b_out[j], ln_g[layer, 0], ln_b[layer, 0])
        x = moe_layer(x, mods, layer, router_w, router_b, moe_w_gate, moe_b_gate, moe_w_up, moe_b_up,
                      moe_w_down, moe_b_down, ln_g[layer, 1], ln_b[layer, 1], split_out=layer == DEPTH - 1)
    y_prompt, y_sample = x
    return (y_prompt.reshape(BATCH, SEQ, D_MODEL),
            y_sample.reshape(DEC_BATCH, DEC_SEQ, D_MODEL),
            jnp.stack(new_k, axis=1), jnp.stack(new_v, axis=1), jnp.stack(new_s, axis=1))
```

```python
import functools

import jax
import jax.numpy as jnp
import numpy as np
from jax import lax
from jax.experimental import pallas as pl
from jax.experimental.pallas import tpu as pltpu

D_MODEL = 1024
BATCH = 16
SEQ = 256
DEPTH = 4
DEC_BATCH = 4
DEC_SEQ = 2048
PAST_LEN = 512
GRID_W = 64
N_EVEN = (DEPTH + 1) // 2
N_ODD = DEPTH // 2
HEAD_DIM = 64
N_HEADS_A = 8
N_KV_A = 2
GROUP_A = N_HEADS_A // N_KV_A
WINDOW = 128
ATTN_BLOCK = 128
SCALE_A = HEAD_DIM ** -0.5
ROPE_BASE = 10000.0
ROPE_PAIRS = HEAD_DIM // 4
N_HEADS_B = 4
HGRN_DK = 128
HGRN_DV = 128
CONV_WIDTH = 31
N_EXPERTS = 32
TOP_K = 4
D_EXPERT = D_MODEL
SWIGLU_LIMIT = 7.0
SWIGLU_ALPHA = 1.702
LN_EPS = 1e-5
RMS_EPS = 1e-6
MASK_VALUE = -1e9
LB_FLOOR = 1e-30
DN_ALPHA = (2 * DEPTH) ** 0.25
A_Q = N_HEADS_A * HEAD_DIM
A_KV = N_KV_A * HEAD_DIM
B_QK = N_HEADS_B * HGRN_DK
B_V = N_HEADS_B * HGRN_DV
IN_SIZES = (A_Q, A_KV, A_KV, B_QK, B_V, B_QK, B_QK, B_V)
D_IN_EVEN = sum(IN_SIZES)

N_PROMPT = BATCH * SEQ
N_SAMPLE = DEC_BATCH * DEC_SEQ
N_TOK = N_PROMPT + N_SAMPLE
N_COND = 1 + DEC_BATCH
COND_ROWS = 8

LANES = 128
SUBLANES = 8
VMEM_LIMIT = 56 * 1024 * 1024

ROW_BLOCK = 256
DENSE_ROWS = 512
HG_BLOCK = 128
HG_SUB = 8
MOE_ROWS = 512
MOE_BLOCKS = (N_TOK * TOP_K + N_EXPERTS * (MOE_ROWS - 1)) // MOE_ROWS + 1
MOE_R = MOE_BLOCKS * MOE_ROWS
COMB_TOK = 128

F32 = jnp.float32
BF16 = jnp.bfloat16
HIGHEST = lax.Precision.HIGHEST


def _cond_of_block(i, rows):
    start = i * rows
    return jnp.where(start < N_PROMPT, 0, 1 + (start - N_PROMPT) // DEC_SEQ)


def _cparams(sem):
    return pltpu.CompilerParams(dimension_semantics=sem, vmem_limit_bytes=VMEM_LIMIT)


def _layer_norm(z, g, b):
    mu = jnp.mean(z, axis=-1, keepdims=True)
    zc = z - mu
    var = jnp.mean(zc * zc, axis=-1, keepdims=True)
    return zc * lax.rsqrt(var + LN_EPS) * g + b


def _dot(a, b):
    return jnp.dot(a, b, preferred_element_type=F32)


def _dot_nt(a, b):
    return lax.dot_general(a, b, (((1,), (1,)), ((), ())), preferred_element_type=F32)


ADA_TN = 3072


def _adaln_kernel(cond_ref, w_ref, b_ref, o_ref):
    c = cond_ref[...]
    s = c * jax.nn.sigmoid(c)
    o_ref[0] = jnp.dot(s, w_ref[0], precision=HIGHEST, preferred_element_type=F32) + b_ref[0]


def adaln_all(cond, w_ada, b_ada):
    n_out = 6 * D_MODEL
    return pl.pallas_call(
        _adaln_kernel,
        grid=(DEPTH, n_out // ADA_TN),
        in_specs=[
            pl.BlockSpec((COND_ROWS, D_MODEL), lambda l, n: (0, 0)),
            pl.BlockSpec((1, D_MODEL, ADA_TN), lambda l, n: (l, 0, n)),
            pl.BlockSpec((1, 1, ADA_TN), lambda l, n: (l, 0, n)),
        ],
        out_specs=pl.BlockSpec((1, COND_ROWS, ADA_TN), lambda l, n: (l, 0, n)),
        out_shape=jax.ShapeDtypeStruct((DEPTH, COND_ROWS, n_out), F32),
        compiler_params=_cparams(("parallel", "parallel")),
        name="adaln",
    )(cond, w_ada, b_ada.reshape(DEPTH, 1, n_out))


def _rope_tables():
    t = np.arange(DEC_SEQ)
    d = np.arange(LANES) % HEAD_DIM
    axis = d // (2 * ROPE_PAIRS)
    half = (d // ROPE_PAIRS) % 2
    pair = d % ROPE_PAIRS
    pos = jnp.where(axis[None, :] == 0, (t // GRID_W)[:, None], (t % GRID_W)[:, None]).astype(F32)
    inv_freq = ROPE_BASE ** (-jnp.arange(ROPE_PAIRS, dtype=F32) / ROPE_PAIRS)
    ang = pos * inv_freq[pair][None, :]
    sign = jnp.where(half[None, :] == 0, -1.0, 1.0).astype(F32)
    return jnp.cos(ang), jnp.sin(ang) * sign


def _rope(x, cos, sin_signed):
    lane = lax.broadcasted_iota(jnp.int32, x.shape, 1)
    first_half = (lane // ROPE_PAIRS) % 2 == 0
    partner = jnp.where(first_half, pltpu.roll(x, LANES - ROPE_PAIRS, 1), pltpu.roll(x, ROPE_PAIRS, 1))
    return x * cos + partner * sin_signed


def _inproj_kernel(x_ref, mod_ref, w_ref, b_ref, cos_ref, sin_ref,
                   q_ref, k_ref, v_ref, qb_ref, ib_ref, ff_ref, fb_ref, go_ref):
    i = pl.program_id(0)
    shift = mod_ref[0, 0:1, :]
    scale = mod_ref[0, 1:2, :]
    h = (x_ref[...] * (1.0 + scale) + shift).astype(BF16)
    y = _dot(h, w_ref[...]) + b_ref[...]
    offs = np.cumsum((0,) + IN_SIZES)
    q = y[:, offs[0]:offs[1]]
    k = y[:, offs[1]:offs[2]]
    v_ref[...] = y[:, offs[2]:offs[3]]
    qb_ref[...] = y[:, offs[3]:offs[4]]
    ib_ref[...] = y[:, offs[4]:offs[5]]
    ff_ref[...] = y[:, offs[5]:offs[6]]
    fb_ref[...] = y[:, offs[6]:offs[7]]
    go_ref[...] = y[:, offs[7]:offs[8]]
    is_latent = i * DENSE_ROWS >= N_PROMPT

    @pl.when(jnp.logical_not(is_latent))
    def _():
        q_ref[...] = q
        k_ref[...] = k

    @pl.when(is_latent)
    def _():
        cos = cos_ref[...]
        sin = sin_ref[...]
        for c in range(A_Q // LANES):
            q_ref[:, c * LANES:(c + 1) * LANES] = _rope(q[:, c * LANES:(c + 1) * LANES], cos, sin)
        k_ref[...] = _rope(k, cos, sin)


def inproj_even(x, mods, w_bf, b, cos, sin):
    nblk = N_TOK // DENSE_ROWS
    pos_blocks = DEC_SEQ // DENSE_ROWS

    def pos_map(i):
        return (jnp.maximum(i - N_PROMPT // DENSE_ROWS, 0) % pos_blocks, 0)

    row = lambda i: (i, 0)
    widths = (A_Q, A_KV, A_KV, B_QK, B_V, B_QK, B_QK, B_V)
    return pl.pallas_call(
        _inproj_kernel,
        grid=(nblk,),
        in_specs=[
            pl.BlockSpec((DENSE_ROWS, D_MODEL), row),
            pl.BlockSpec((1, 6, D_MODEL), lambda i: (_cond_of_block(i, DENSE_ROWS), 0, 0)),
            pl.BlockSpec((D_MODEL, D_IN_EVEN), lambda i: (0, 0)),
            pl.BlockSpec((1, D_IN_EVEN), lambda i: (0, 0)),
            pl.BlockSpec((DENSE_ROWS, LANES), pos_map),
            pl.BlockSpec((DENSE_ROWS, LANES), pos_map),
        ],
        out_specs=[pl.BlockSpec((DENSE_ROWS, w), row) for w in widths],
        out_shape=[jax.ShapeDtypeStruct((N_TOK, w), F32) for w in widths],
        compiler_params=_cparams(("parallel",)),
        name="inproj_even",
    )(x, mods, w_bf, b.reshape(1, D_IN_EVEN), cos, sin)


def _sink_attend(q, keys, vals, sink, masks):
    scores = []
    for kk, mask in zip(keys, masks):
        s = _dot_nt(q, kk) * SCALE_A
        if mask is not None:
            s = jnp.where(mask, s, MASK_VALUE)
        scores.append(s)
    m = sink
    for s in scores:
        m = jnp.maximum(m, jnp.max(s, axis=-1, keepdims=True))
    denom = jnp.exp(sink - m)
    acc = None
    for s, vv in zip(scores, vals):
        p = jnp.exp(s - m)
        denom = denom + jnp.sum(p, axis=-1, keepdims=True)
        pv = _dot(p.astype(BF16), vv)
        acc = pv if acc is None else acc + pv
    return acc / denom


def _attn_ctx_kernel(sink_ref, q_ref, k_ref, v_ref, o_ref):
    k = k_ref[...].astype(BF16)
    v = v_ref[...].astype(BF16)
    q = q_ref[...].astype(BF16)
    for h in range(N_HEADS_A):
        kv = h // GROUP_A
        qh = q[:, h * HEAD_DIM:(h + 1) * HEAD_DIM]
        kh = k[:, kv * HEAD_DIM:(kv + 1) * HEAD_DIM]
        vh = v[:, kv * HEAD_DIM:(kv + 1) * HEAD_DIM]
        o_ref[:, h * HEAD_DIM:(h + 1) * HEAD_DIM] = _sink_attend(qh, [kh], [vh], sink_ref[h], [None])


def attn_context(sink, q, k, v):
    row = lambda b: (b, 0)
    return pl.pallas_call(
        _attn_ctx_kernel,
        grid=(BATCH,),
        in_specs=[
            pl.BlockSpec(memory_space=pltpu.SMEM),
            pl.BlockSpec((SEQ, A_Q), row),
            pl.BlockSpec((SEQ, A_KV), row),
            pl.BlockSpec((SEQ, A_KV), row),
        ],
        out_specs=pl.BlockSpec((SEQ, A_Q), row),
        out_shape=jax.ShapeDtypeStruct((N_PROMPT, A_Q), F32),
        compiler_params=_cparams(("parallel",)),
        name="attn_context",
    )(sink, q, k, v)


def _attn_lat_kernel(sink_ref, q_ref, kp_ref, kc_ref, kn_ref, vp_ref, vc_ref, vn_ref, ck_ref, cv_ref, o_ref):
    n = pl.program_id(1)
    nb = DEC_SEQ // ATTN_BLOCK
    qi = lax.broadcasted_iota(jnp.int32, (ATTN_BLOCK, ATTN_BLOCK), 0)
    kj = lax.broadcasted_iota(jnp.int32, (ATTN_BLOCK, ATTN_BLOCK), 1)
    mask_prev = jnp.logical_and(kj - qi >= ATTN_BLOCK - WINDOW, n > 0)
    mask_next = jnp.logical_and(kj - qi <= WINDOW - ATTN_BLOCK, n < nb - 1)
    masks = [mask_prev, None, mask_next, None]
    q = q_ref[...].astype(BF16)
    kband = [r[...].astype(BF16) for r in (kp_ref, kc_ref, kn_ref)]
    vband = [r[...].astype(BF16) for r in (vp_ref, vc_ref, vn_ref)]
    ck = ck_ref[0].astype(BF16)
    cv = cv_ref[0].astype(BF16)
    for h in range(N_HEADS_A):
        kv = h // GROUP_A
        sl = slice(kv * HEAD_DIM, (kv + 1) * HEAD_DIM)
        qh = q[:, h * HEAD_DIM:(h + 1) * HEAD_DIM]
        keys = [kb[:, sl] for kb in kband] + [ck[:, sl]]
        vals = [vb[:, sl] for vb in vband] + [cv[:, sl]]
        o_ref[:, h * HEAD_DIM:(h + 1) * HEAD_DIM] = _sink_attend(qh, keys, vals, sink_ref[h], masks)


def attn_latent(sink, q, k, v, cache_k, cache_v):
    nb = DEC_SEQ // ATTN_BLOCK
    base = N_PROMPT // ATTN_BLOCK

    def blk(delta):
        return lambda b, n: (base + b * nb + jnp.clip(n + delta, 0, nb - 1), 0)

    kv_spec = lambda delta: pl.BlockSpec((ATTN_BLOCK, A_KV), blk(delta))
    cache_spec = pl.BlockSpec((1, PAST_LEN, A_KV), lambda b, n: (b, 0, 0))
    return pl.pallas_call(
        _attn_lat_kernel,
        grid=(DEC_BATCH, nb),
        in_specs=[
            pl.BlockSpec(memory_space=pltpu.SMEM),
            pl.BlockSpec((ATTN_BLOCK, A_Q), blk(0)),
            kv_spec(-1), kv_spec(0), kv_spec(1),
            kv_spec(-1), kv_spec(0), kv_spec(1),
            cache_spec, cache_spec,
        ],
        out_specs=pl.BlockSpec((ATTN_BLOCK, A_Q), lambda b, n: (b * nb + n, 0)),
        out_shape=jax.ShapeDtypeStruct((N_SAMPLE, A_Q), F32),
        compiler_params=_cparams(("parallel", "parallel")),
        name="attn_latent",
    )(sink, q, k, k, k, v, v, v, cache_k, cache_v)


def _hgrn_gate(f_pre, lb):
    f = jnp.maximum(lb, LB_FLOOR) + (1.0 - lb) * jax.nn.sigmoid(f_pre)
    return 1.0 - f, jnp.log(f)


def _scan_rows(x, reverse):
    n = x.shape[0]
    row = lax.broadcasted_iota(jnp.int32, x.shape, 0)
    sh = 1
    while sh < n:
        if reverse:
            x = x + jnp.where(row < n - sh, pltpu.roll(x, n - sh, 0), 0.0)
        else:
            x = x + jnp.where(row >= sh, pltpu.roll(x, sh, 0), 0.0)
        sh *= 2
    return x


def _chunk_bcast(x, c, pick, shift):
    nc = HG_BLOCK // c
    rows = x.reshape(nc, c, x.shape[-1])[:, pick:pick + 1, :]
    zero = jnp.zeros((1, 1, x.shape[-1]), x.dtype)
    if shift == -1:
        rows = jnp.concatenate([zero, rows[:-1]], axis=0)
    elif shift == 1:
        rows = jnp.concatenate([rows[1:], zero], axis=0)
    return jnp.broadcast_to(rows, (nc, c, x.shape[-1])).reshape(HG_BLOCK, x.shape[-1])


def _hgrn_block(q, kk, v, log_f, st, reverse):
    cum = _scan_rows(log_f, reverse)
    tot = cum[0:1, :] if reverse else cum[HG_BLOCK - 1:HG_BLOCK, :]
    o = _dot_nt((q * jnp.exp(cum)).astype(BF16), st.astype(BF16))
    kd = (kk * jnp.exp(tot - cum)).astype(BF16)
    v_bf = v.astype(BF16)
    u_t = lax.dot_general(v_bf, kd, (((0,), (0,)), ((), ())), preferred_element_type=F32)
    st_new = st * jnp.exp(tot) + u_t
    ti = lax.broadcasted_iota(jnp.int32, (HG_BLOCK, HG_BLOCK), 0)
    si = lax.broadcasted_iota(jnp.int32, (HG_BLOCK, HG_BLOCK), 1)
    a = jnp.zeros((HG_BLOCK, HG_BLOCK), F32)
    c = HG_SUB
    while c < HG_BLOCK:
        if reverse:
            bound_t = _chunk_bcast(cum, c, 0, 1)
            bound_s = _chunk_bcast(cum, c, 0, 0)
            mask = jnp.logical_and((ti // c) % 2 == 0, si // c == ti // c + 1)
        else:
            bound_t = _chunk_bcast(cum, c, c - 1, -1)
            bound_s = _chunk_bcast(cum, c, c - 1, 0)
            mask = jnp.logical_and((ti // c) % 2 == 1, si // c == ti // c - 1)
        qc = (q * jnp.exp(cum - bound_t)).astype(BF16)
        kc = (kk * jnp.exp(bound_s - cum)).astype(BF16)
        a = a + jnp.where(mask, _dot_nt(qc, kc), 0.0)
        c *= 2
    o = o + _dot(a.astype(BF16), v_bf)
    nsub = HG_BLOCK // HG_SUB
    q3 = q.reshape(nsub, HG_SUB, HGRN_DK)
    k3 = kk.reshape(nsub, HG_SUB, HGRN_DK)
    v3 = v.reshape(nsub, HG_SUB, HGRN_DV)
    cum3 = cum.reshape(nsub, HG_SUB, HGRN_DK)
    t_off = lax.broadcasted_iota(jnp.int32, (1, HG_SUB, 1), 1)
    od = jnp.zeros((nsub, HG_SUB, HGRN_DV), F32)
    for s in range(HG_SUB):
        causal = (t_off <= s) if reverse else (t_off >= s)
        decay = jnp.exp(jnp.where(causal, cum3 - cum3[:, s:s + 1, :], MASK_VALUE))
        score = jnp.sum(q3 * decay * k3[:, s:s + 1, :], axis=-1, keepdims=True)
        od = od + score * v3[:, s:s + 1, :]
    return o + od.reshape(HG_BLOCK, HGRN_DV), st_new


def _hgrn_kernel(*refs, n_blocks, has_state_in, has_state_out):
    refs = list(refs)
    qb_ref, ib_ref, ff_ref, fb_ref, go_ref, lbf_ref, lbb_ref, nw_ref = refs[:8]
    refs = refs[8:]
    s0_ref = refs.pop(0) if has_state_in else None
    o_ref = refs.pop(0)
    so_ref = refs.pop(0) if has_state_out else None
    acc_ref = refs.pop(0)

    def run(reverse):
        f_ref, lb_ref = (fb_ref, lbb_ref) if reverse else (ff_ref, lbf_ref)
        lb = lb_ref[...]
        if has_state_in:
            st0 = s0_ref[0, 0, 1 if reverse else 0, 0].T
        else:
            st0 = jnp.zeros((HGRN_DV, HGRN_DK), F32)

        def body(it, st):
            blk = (n_blocks - 1 - it) if reverse else it
            rows = pl.ds(pl.multiple_of(blk * HG_BLOCK, HG_BLOCK), HG_BLOCK)
            qpre = qb_ref[rows, :]
            q = qpre * jax.nn.sigmoid(qpre)
            kk, log_f = _hgrn_gate(f_ref[rows, :], lb)
            o, st = _hgrn_block(q, kk, ib_ref[rows, :], log_f, st, reverse)
            if reverse:
                acc_ref[rows, :] = acc_ref[rows, :] + o
            else:
                acc_ref[rows, :] = o
            return st

        return lax.fori_loop(0, n_blocks, body, st0)

    st_f = run(False)
    st_b = run(True)
    if has_state_out:
        so_ref[0, 0, 0] = st_f.T
        so_ref[0, 1, 0] = st_b.T
    o = acc_ref[...]
    o = o * lax.rsqrt(jnp.mean(o * o, axis=-1, keepdims=True) + RMS_EPS) * nw_ref[...]
    g = go_ref[...]
    o_ref[...] = o * (g * jax.nn.sigmoid(g))


def hgrn_mixer(qb, ib, ff, fb, go, lb_f, lb_b, norm_w, *, latent, state_in=None, layer_j=0):
    t_len = DEC_SEQ if latent else SEQ
    n_seq = DEC_BATCH if latent else BATCH
    row0 = N_PROMPT // t_len if latent else 0
    tok = pl.BlockSpec((t_len, HGRN_DK), lambda b, h: (row0 + b, h))
    vec = pl.BlockSpec((1, HGRN_DK), lambda b, h: (0, h))
    in_specs = [tok] * 5 + [vec] * 3
    args = [qb, ib, ff, fb, go, lb_f.reshape(1, B_QK), lb_b.reshape(1, B_QK), norm_w.reshape(1, B_V)]
    if latent:
        in_specs.append(pl.BlockSpec((1, 1, 2, 1, HGRN_DK, HGRN_DV), lambda b, h: (b, layer_j, 0, h, 0, 0)))
        args.append(state_in)
    out_specs = [pl.BlockSpec((t_len, HGRN_DV), lambda b, h: (b, h))]
    out_shape = [jax.ShapeDtypeStruct((n_seq * t_len, B_V), F32)]
    if not latent:
        out_specs.append(pl.BlockSpec((1, 2, 1, HGRN_DK, HGRN_DV), lambda b, h: (b, 0, h, 0, 0)))
        out_shape.append(jax.ShapeDtypeStruct((BATCH, 2, N_HEADS_B, HGRN_DK, HGRN_DV), F32))
    kern = functools.partial(_hgrn_kernel, n_blocks=t_len // HG_BLOCK, has_state_in=latent, has_state_out=not latent)
    return pl.pallas_call(
        kern,
        grid=(n_seq, N_HEADS_B),
        in_specs=in_specs,
        out_specs=out_specs,
        out_shape=out_shape,
        scratch_shapes=[pltpu.VMEM((t_len, HGRN_DV), F32)],
        compiler_params=_cparams(("parallel", "parallel")),
        name="hgrn_latent" if latent else "hgrn_context",
    )(*args)


def _outproj_kernel(x_ref, mod_ref, ap_ref, as_ref, rp_ref, rs_ref, w_ref, g_ref, b_ref, o_ref):
    is_latent = pl.program_id(0) * DENSE_ROWS >= N_PROMPT
    attn = jnp.where(is_latent, as_ref[...], ap_ref[...]).astype(BF16)
    rec = jnp.where(is_latent, rs_ref[...], rp_ref[...]).astype(BF16)
    y = _dot(attn, w_ref[0:A_Q, :]) + _dot(rec, w_ref[A_Q:A_Q + B_V, :])
    z = DN_ALPHA * x_ref[...] + mod_ref[0, 2:3, :] * y
    o_ref[...] = _layer_norm(z, g_ref[...], b_ref[...])


def outproj_even(x, mods, attn_p, attn_s, rec_p, rec_s, w_bf, ln_g, ln_b):
    nblk = N_TOK // DENSE_ROWS
    npb = N_PROMPT // DENSE_ROWS
    row = lambda i: (i, 0)
    prow = lambda i: (jnp.minimum(i, npb - 1), 0)
    srow = lambda i: (jnp.maximum(i - npb, 0), 0)
    vec = pl.BlockSpec((1, D_MODEL), lambda i: (0, 0))
    return pl.pallas_call(
        _outproj_kernel,
        grid=(nblk,),
        in_specs=[
            pl.BlockSpec((DENSE_ROWS, D_MODEL), row),
            pl.BlockSpec((1, 6, D_MODEL), lambda i: (_cond_of_block(i, DENSE_ROWS), 0, 0)),
            pl.BlockSpec((DENSE_ROWS, A_Q), prow),
            pl.BlockSpec((DENSE_ROWS, A_Q), srow),
            pl.BlockSpec((DENSE_ROWS, B_V), prow),
            pl.BlockSpec((DENSE_ROWS, B_V), srow),
            pl.BlockSpec((A_Q + B_V, D_MODEL), lambda i: (0, 0)),
            vec, vec,
        ],
        out_specs=pl.BlockSpec((DENSE_ROWS, D_MODEL), row),
        out_shape=jax.ShapeDtypeStruct((N_TOK, D_MODEL), F32),
        compiler_params=_cparams(("parallel",)),
        name="outproj_even",
    )(x, mods, attn_p, attn_s, rec_p, rec_s, w_bf, ln_g.reshape(1, D_MODEL), ln_b.reshape(1, D_MODEL))


def _conv_in_kernel(x_ref, mod_ref, w_ref, b_ref, u_ref):
    h = (x_ref[...] * (1.0 + mod_ref[0, 1:2, :]) + mod_ref[0, 0:1, :]).astype(BF16)
    a = _dot(h, w_ref[:, 0:D_MODEL]) + b_ref[:, 0:D_MODEL]
    gt = _dot(h, w_ref[:, D_MODEL:2 * D_MODEL]) + b_ref[:, D_MODEL:2 * D_MODEL]
    u_ref[...] = a * jax.nn.sigmoid(gt)


def conv_in(x, mods, w_bf, b):
    row = lambda i: (i, 0)
    return pl.pallas_call(
        _conv_in_kernel,
        grid=(N_TOK // DENSE_ROWS,),
        in_specs=[
            pl.BlockSpec((DENSE_ROWS, D_MODEL), row),
            pl.BlockSpec((1, 6, D_MODEL), lambda i: (_cond_of_block(i, DENSE_ROWS), 0, 0)),
            pl.BlockSpec((D_MODEL, 2 * D_MODEL), lambda i: (0, 0)),
            pl.BlockSpec((1, 2 * D_MODEL), lambda i: (0, 0)),
        ],
        out_specs=pl.BlockSpec((DENSE_ROWS, D_MODEL), row),
        out_shape=jax.ShapeDtypeStruct((N_TOK, D_MODEL), F32),
        compiler_params=_cparams(("parallel",)),
        name="conv_in",
    )(x, mods, w_bf, b.reshape(1, 2 * D_MODEL))


CONV_HALO = 16
CONV_LANES = 256


CONV_SHIFT_ROWS = ROW_BLOCK + 2 * CONV_HALO - SUBLANES
CONV_ROW_CHUNK = 128


def _conv_out_kernel(x_ref, mod_ref, up_ref, uc_ref, un_ref, dw_ref, dwb_ref, cg_ref, cb_ref,
                     w_ref, b_ref, g_ref, bb_ref, o_ref, pad_ref, acc_ref, sh_ref):
    i = pl.program_id(0)
    blocks_per_seq = DEC_SEQ // ROW_BLOCK
    j = i - N_PROMPT // ROW_BLOCK
    is_latent = j >= 0
    has_prev = jnp.logical_and(is_latent, j % blocks_per_seq != 0)
    has_next = jnp.logical_and(is_latent, j % blocks_per_seq != blocks_per_seq - 1)
    pad_ref[0:CONV_HALO, :] = jnp.where(has_prev, up_ref[...], 0.0)
    pad_ref[CONV_HALO:CONV_HALO + ROW_BLOCK, :] = uc_ref[...]
    pad_ref[CONV_HALO + ROW_BLOCK:, :] = jnp.where(has_next, un_ref[...], 0.0)
    first = CONV_HALO - CONV_WIDTH // 2
    for c in range(D_MODEL // CONV_LANES):
        lanes = slice(c * CONV_LANES, (c + 1) * CONV_LANES)
        for s in range(SUBLANES):
            sh_ref[s] = pad_ref[s:s + CONV_SHIFT_ROWS, lanes]
        for r0 in range(0, ROW_BLOCK, CONV_ROW_CHUNK):
            acc = jnp.zeros((CONV_ROW_CHUNK, CONV_LANES), F32) + dwb_ref[:, lanes]
            for tap in range(CONV_WIDTH):
                whole, s = divmod(first + tap, SUBLANES)
                rows = slice(r0 + whole * SUBLANES, r0 + whole * SUBLANES + CONV_ROW_CHUNK)
                acc = acc + sh_ref[s, rows, :] * dw_ref[tap:tap + 1, lanes]
            acc_ref[r0:r0 + CONV_ROW_CHUNK, lanes] = acc
    u = _layer_norm(acc_ref[...], cg_ref[...], cb_ref[...])
    u = (u * jax.nn.sigmoid(u)).astype(BF16)
    y = _dot(u, w_ref[...]) + b_ref[...]
    z = DN_ALPHA * x_ref[...] + mod_ref[0, 2:3, :] * y
    o_ref[...] = _layer_norm(z, g_ref[...], bb_ref[...])


def conv_out(x, mods, u, dw, dw_b, cln_g, cln_b, w_bf, b_out, ln_g, ln_b):
    nblk = N_TOK // ROW_BLOCK
    ratio = ROW_BLOCK // CONV_HALO
    nhalo = N_TOK // CONV_HALO
    row = lambda i: (i, 0)
    vec = pl.BlockSpec((1, D_MODEL), lambda i: (0, 0))
    r1 = lambda a: a.reshape(1, D_MODEL)
    return pl.pallas_call(
        _conv_out_kernel,
        grid=(nblk,),
        in_specs=[
            pl.BlockSpec((ROW_BLOCK, D_MODEL), row),
            pl.BlockSpec((1, 6, D_MODEL), lambda i: (_cond_of_block(i, ROW_BLOCK), 0, 0)),
            pl.BlockSpec((CONV_HALO, D_MODEL), lambda i: (jnp.maximum(i * ratio - 1, 0), 0)),
            pl.BlockSpec((ROW_BLOCK, D_MODEL), row),
            pl.BlockSpec((CONV_HALO, D_MODEL), lambda i: (jnp.minimum((i + 1) * ratio, nhalo - 1), 0)),
            pl.BlockSpec((CONV_WIDTH, D_MODEL), lambda i: (0, 0)),
            vec, vec, vec,
            pl.BlockSpec((D_MODEL, D_MODEL), lambda i: (0, 0)),
            vec, vec, vec,
        ],
        out_specs=pl.BlockSpec((ROW_BLOCK, D_MODEL), row),
        out_shape=jax.ShapeDtypeStruct((N_TOK, D_MODEL), F32),
        scratch_shapes=[pltpu.VMEM((ROW_BLOCK + 2 * CONV_HALO, D_MODEL), F32),
                        pltpu.VMEM((ROW_BLOCK, D_MODEL), F32),
                        pltpu.VMEM((SUBLANES, CONV_SHIFT_ROWS, CONV_LANES), F32)],
        compiler_params=_cparams(("parallel",)),
        name="conv_out",
    )(x, mods, u, u, u, dw, r1(dw_b), r1(cln_g), r1(cln_b), w_bf, r1(b_out), r1(ln_g), r1(ln_b))


def _modulate2(x_ref, mod_ref):
    return x_ref[...] * (1.0 + mod_ref[0, 4:5, :]) + mod_ref[0, 3:4, :]


def _route_kernel(x_ref, mod_ref, wr_ref, br_ref, idx_ref, gate_ref, rank_ref, cnt_ref, carry_ref):
    i = pl.program_id(0)

    @pl.when(i == 0)
    def _():
        carry_ref[...] = jnp.zeros_like(carry_ref)

    h = _modulate2(x_ref, mod_ref)
    h_hi = h.astype(BF16)
    h_lo = (h - h_hi.astype(F32)).astype(BF16)
    w = wr_ref[...]
    w_hi = w.astype(BF16)
    w_lo = (w - w_hi.astype(F32)).astype(BF16)
    logits = _dot_nt(w_hi, h_hi) + (_dot_nt(w_hi, h_lo) + _dot_nt(w_lo, h_hi)) + br_ref[...]
    eidx = lax.broadcasted_iota(jnp.int32, logits.shape, 0)
    vals = logits
    sels, tops = [], []
    for k in range(TOP_K):
        m = jnp.max(vals, axis=0, keepdims=True)
        idx = jnp.min(jnp.where(vals == m, eidx, N_EXPERTS), axis=0, keepdims=True)
        sel = eidx == idx
        idx_ref[k:k + 1, :] = idx
        sels.append(sel)
        tops.append(m)
        vals = jnp.where(sel, -jnp.inf, vals)
    exps = [jnp.exp(t - tops[0]) for t in tops]
    total = exps[0] + exps[1] + exps[2] + exps[3]
    for k in range(TOP_K):
        gate_ref[k:k + 1, :] = exps[k] / total
    onehot = jnp.zeros(logits.shape, F32)
    for sel in sels:
        onehot = onehot + sel.astype(F32)
    ta = lax.broadcasted_iota(jnp.int32, (ROW_BLOCK, ROW_BLOCK), 0)
    tb = lax.broadcasted_iota(jnp.int32, (ROW_BLOCK, ROW_BLOCK), 1)
    before = _dot(onehot.astype(BF16), (ta < tb).astype(BF16)) + carry_ref[:, 0:1]
    for k in range(TOP_K):
        rank = jnp.sum(jnp.where(sels[k], before, 0.0), axis=0, keepdims=True)
        rank_ref[k:k + 1, :] = rank.astype(jnp.int32)
    carry = carry_ref[...] + jnp.sum(onehot, axis=1, keepdims=True)
    carry_ref[...] = carry
    cnt_ref[...] = carry.astype(jnp.int32)


def moe_route(x, mods, wr_t, b_r):
    tok = pl.BlockSpec((TOP_K, ROW_BLOCK), lambda i: (0, i))
    return pl.pallas_call(
        _route_kernel,
        grid=(N_TOK // ROW_BLOCK,),
        in_specs=[
            pl.BlockSpec((ROW_BLOCK, D_MODEL), lambda i: (i, 0)),
            pl.BlockSpec((1, 6, D_MODEL), lambda i: (_cond_of_block(i, ROW_BLOCK), 0, 0)),
            pl.BlockSpec((N_EXPERTS, D_MODEL), lambda i: (0, 0)),
            pl.BlockSpec((N_EXPERTS, 1), lambda i: (0, 0)),
        ],
        out_specs=[tok, tok, tok, pl.BlockSpec((N_EXPERTS, LANES), lambda i: (0, 0))],
        out_shape=[jax.ShapeDtypeStruct((TOP_K, N_TOK), jnp.int32),
                   jax.ShapeDtypeStruct((TOP_K, N_TOK), F32),
                   jax.ShapeDtypeStruct((TOP_K, N_TOK), jnp.int32),
                   jax.ShapeDtypeStruct((N_EXPERTS, LANES), jnp.int32)],
        scratch_shapes=[pltpu.VMEM((N_EXPERTS, LANES), F32)],
        compiler_params=_cparams(("arbitrary",)),
        name="moe_route",
    )(x, mods, wr_t, b_r.reshape(N_EXPERTS, 1))


HALF_D = D_MODEL // 2
HI_MASK = 0xFFFF0000


def _pack_bf16_pairs(h):
    half = h.shape[1] // 2
    lo = lax.bitcast_convert_type(h[:, :half].astype(BF16).astype(F32), jnp.uint32)
    hi = lax.bitcast_convert_type(h[:, half:].astype(BF16).astype(F32), jnp.uint32)
    return (lo >> 16) | (hi & jnp.uint32(HI_MASK))


def _unpack_bf16_pairs(w):
    lo = lax.bitcast_convert_type(w << 16, F32).astype(BF16)
    hi = lax.bitcast_convert_type(w & jnp.uint32(HI_MASK), F32).astype(BF16)
    return lo, hi


def _scatter_kernel(dest_ref, zstart_ref, x_ref, mod_ref, xs_ref, pk_ref, zero_ref, sem, zsem):
    i = pl.program_id(0)

    def zero_copy(start):
        start = pl.multiple_of(start, MOE_ROWS)
        return pltpu.make_async_copy(zero_ref, xs_ref.at[pl.ds(start, MOE_ROWS), :], zsem)

    def zero_blocks(fn):
        for e in range(N_EXPERTS):
            @pl.when(zstart_ref[e] >= 0)
            def _():
                fn(zero_copy(jnp.maximum(zstart_ref[e], 0)))

        def unused(b, carry):
            fn(zero_copy(b * MOE_ROWS))
            return carry

        lax.fori_loop(zstart_ref[N_EXPERTS], MOE_BLOCKS, unused, 0)

    @pl.when(i == 0)
    def _():
        zero_ref[...] = jnp.zeros_like(zero_ref)
        zero_blocks(lambda cp: cp.start())
        zero_blocks(lambda cp: cp.wait())

    base = i * ROW_BLOCK
    last = pl.num_programs(0) - 1

    def drain(slot):
        for k in range(TOP_K):
            pltpu.make_async_copy(pk_ref.at[slot], xs_ref.at[pl.ds(0, ROW_BLOCK), :], sem.at[slot]).wait()

    for slot in range(2):
        @pl.when(i % 2 == slot)
        def _():
            @pl.when(i >= 2)
            def _():
                drain(slot)

            pk_ref[slot] = _pack_bf16_pairs(_modulate2(x_ref, mod_ref))
            for t in range(ROW_BLOCK):
                for k in range(TOP_K):
                    row = dest_ref[k * N_TOK + base + t]
                    pltpu.make_async_copy(pk_ref.at[slot, pl.ds(t, 1), :], xs_ref.at[pl.ds(row, 1), :],
                                          sem.at[slot]).start(priority=k % 2)

            @pl.when(i == last)
            def _():
                drain(slot)

                @pl.when(i >= 1)
                def _():
                    drain(1 - slot)


def moe_scatter(dest_flat, zstart, x, mods):
    return pl.pallas_call(
        _scatter_kernel,
        grid_spec=pltpu.PrefetchScalarGridSpec(
            num_scalar_prefetch=2,
            grid=(N_TOK // ROW_BLOCK,),
            in_specs=[
                pl.BlockSpec((ROW_BLOCK, D_MODEL), lambda i, d, z: (i, 0)),
                pl.BlockSpec((1, 6, D_MODEL), lambda i, d, z: (_cond_of_block(i, ROW_BLOCK), 0, 0)),
            ],
            out_specs=pl.BlockSpec(memory_space=pl.ANY),
            scratch_shapes=[pltpu.VMEM((2, ROW_BLOCK, HALF_D), jnp.uint32),
                            pltpu.VMEM((MOE_ROWS, HALF_D), jnp.uint32),
                            pltpu.SemaphoreType.DMA((2,)), pltpu.SemaphoreType.DMA],
        ),
        out_shape=jax.ShapeDtypeStruct((MOE_R, HALF_D), jnp.uint32),
        compiler_params=_cparams(("arbitrary",)),
        name="moe_scatter",
    )(dest_flat, zstart, x, mods)


N_CHUNK = 256


def _expert_kernel(be_ref, nused_ref, nexte_ref, par_ref, xs_ref, wg_hbm, wu_hbm, wd_hbm, bg_ref, bu_ref, bd_ref,
                   ys_ref, wbuf, act_ref, wg_bf, wu_bf, wd_bf, wsem, *, layer):
    i = pl.program_id(0)
    new_expert = jnp.logical_or(i == 0, be_ref[i] != be_ref[jnp.maximum(i - 1, 0)])
    slot = par_ref[i]

    def weight_copies(e, s):
        return [pltpu.make_async_copy(w.at[layer, e], wbuf.at[s, j], wsem.at[s])
                for j, w in enumerate((wg_hbm, wu_hbm, wd_hbm))]

    @pl.when(i == 0)
    def _():
        for cp in weight_copies(be_ref[0], 0):
            cp.start()

    @pl.when(jnp.logical_and(new_expert, i < nused_ref[0]))
    def _():
        for cp in weight_copies(be_ref[i], slot):
            cp.wait()

        @pl.when(nexte_ref[i] >= 0)
        def _():
            for cp in weight_copies(jnp.maximum(nexte_ref[i], 0), 1 - slot):
                cp.start()

        wg_bf[...] = wbuf[slot, 0].astype(BF16)
        wu_bf[...] = wbuf[slot, 1].astype(BF16)
        wd_bf[...] = wbuf[slot, 2].astype(BF16)

    @pl.when(i < nused_ref[0])
    def _():
        lo, hi = _unpack_bf16_pairs(xs_ref[...])
        x = jnp.concatenate([lo, hi], axis=1)
        for n in range(D_EXPERT // N_CHUNK):
            cols = slice(n * N_CHUNK, (n + 1) * N_CHUNK)
            gt = _dot(x, wg_bf[:, cols]) + bg_ref[0, 0, :, cols]
            up = _dot(x, wu_bf[:, cols]) + bu_ref[0, 0, :, cols]
            gt = jnp.minimum(gt, SWIGLU_LIMIT)
            up = jnp.clip(up, -SWIGLU_LIMIT, SWIGLU_LIMIT)
            act_ref[:, cols] = ((up + 1.0) * gt * jax.nn.sigmoid(SWIGLU_ALPHA * gt)).astype(BF16)
        act = act_ref[...]
        for n in range(HALF_D // N_CHUNK):
            cols = slice(n * N_CHUNK, (n + 1) * N_CHUNK)
            cols_hi = slice(HALF_D + n * N_CHUNK, HALF_D + (n + 1) * N_CHUNK)
            y_lo = _dot(act, wd_bf[:, cols]) + bd_ref[0, 0, :, cols]
            y_hi = _dot(act, wd_bf[:, cols_hi]) + bd_ref[0, 0, :, cols_hi]
            ys_ref[:, cols] = _pack_bf16_pairs(jnp.concatenate([y_lo, y_hi], axis=1))

    @pl.when(i >= nused_ref[0])
    def _():
        ys_ref[...] = jnp.zeros_like(ys_ref)


def moe_experts(block_e, n_used, next_e, parity, xs, w_g, w_u, w_d, b_g, b_u, b_d, layer):
    rows = lambda i, be, nu, ne, pa: (i, 0)
    wspec = pl.BlockSpec(memory_space=pl.ANY)
    bspec = pl.BlockSpec((1, 1, 1, D_EXPERT), lambda i, be, nu, ne, pa: (layer, be[i], 0, 0))
    r4 = lambda b: b.reshape(DEPTH, N_EXPERTS, 1, D_EXPERT)
    return pl.pallas_call(
        functools.partial(_expert_kernel, layer=layer),
        grid_spec=pltpu.PrefetchScalarGridSpec(
            num_scalar_prefetch=4,
            grid=(MOE_BLOCKS,),
            in_specs=[pl.BlockSpec((MOE_ROWS, HALF_D), rows), wspec, wspec, wspec, bspec, bspec, bspec],
            out_specs=pl.BlockSpec((MOE_ROWS, HALF_D), rows),
            scratch_shapes=[pltpu.VMEM((2, 3, D_MODEL, D_EXPERT), F32),
                            pltpu.VMEM((MOE_ROWS, D_EXPERT), BF16)]
                           + [pltpu.VMEM((D_MODEL, D_EXPERT), BF16)] * 3
                           + [pltpu.SemaphoreType.DMA((2,))],
        ),
        out_shape=jax.ShapeDtypeStruct((MOE_R, HALF_D), jnp.uint32),
        compiler_params=_cparams(("arbitrary",)),
        name="moe_experts",
    )(block_e, n_used, next_e, parity, xs, w_g, w_u, w_d, r4(b_g), r4(b_u), r4(b_d))


def _combine_kernel(dest_ref, x_ref, mod_ref, gate_ref, ys_ref, g_ref, b_ref, *rest, split_out):
    out_refs, (buf_ref, sem) = rest[:-2], rest[-2:]
    i = pl.program_id(0)
    last = pl.num_programs(0) - 1

    def gather(step, slot):
        base = step * COMB_TOK
        for t in range(COMB_TOK):
            for k in range(TOP_K):
                row = dest_ref[k * N_TOK + base + t]
                pltpu.make_async_copy(ys_ref.at[pl.ds(row, 1), :], buf_ref.at[slot, k, pl.ds(t, 1), :],
                                      sem.at[slot]).start(priority=k % 2)

    @pl.when(i == 0)
    def _():
        gather(0, 0)

    for slot in range(2):
        @pl.when(jnp.logical_and(i < last, i % 2 == slot))
        def _():
            gather(i + 1, 1 - slot)

    eye = (lax.broadcasted_iota(jnp.int32, (COMB_TOK, COMB_TOK), 0)
           == lax.broadcasted_iota(jnp.int32, (COMB_TOK, COMB_TOK), 1))
    gates = gate_ref[...]
    cols = [jnp.sum(jnp.where(eye, gates[k:k + 1, :], 0.0), axis=1, keepdims=True) for k in range(TOP_K)]
    cur = i % 2
    for k in range(TOP_K):
        pltpu.make_async_copy(ys_ref.at[pl.ds(0, COMB_TOK), :], buf_ref.at[cur, k], sem.at[cur]).wait()
    y_lo = y_hi = None
    for k in range(TOP_K):
        w = buf_ref[cur, k]
        lo = cols[k] * lax.bitcast_convert_type(w << 16, F32)
        hi = cols[k] * lax.bitcast_convert_type(w & jnp.uint32(HI_MASK), F32)
        y_lo = lo if y_lo is None else y_lo + lo
        y_hi = hi if y_hi is None else y_hi + hi
    y = jnp.concatenate([y_lo, y_hi], axis=1)
    z = DN_ALPHA * x_ref[...] + mod_ref[0, 5:6, :] * y
    res = _layer_norm(z, g_ref[...], b_ref[...])
    if split_out:
        is_prompt = i < N_PROMPT // COMB_TOK

        @pl.when(is_prompt)
        def _():
            out_refs[0][...] = res

        @pl.when(jnp.logical_not(is_prompt))
        def _():
            out_refs[1][...] = res
    else:
        out_refs[0][...] = res


def moe_combine(dest_flat, x, mods, gates_t, ys, ln_g, ln_b, split_out=False):
    vec = pl.BlockSpec((1, D_MODEL), lambda i, d: (0, 0))
    npb = N_PROMPT // COMB_TOK
    if split_out:
        out_specs = [pl.BlockSpec((COMB_TOK, D_MODEL), lambda i, d: (jnp.minimum(i, npb - 1), 0)),
                     pl.BlockSpec((COMB_TOK, D_MODEL), lambda i, d: (jnp.maximum(i - npb, 0), 0))]
        out_shape = [jax.ShapeDtypeStruct((N_PROMPT, D_MODEL), F32), jax.ShapeDtypeStruct((N_SAMPLE, D_MODEL), F32)]
    else:
        out_specs = pl.BlockSpec((COMB_TOK, D_MODEL), lambda i, d: (i, 0))
        out_shape = jax.ShapeDtypeStruct((N_TOK, D_MODEL), F32)
    return pl.pallas_call(
        functools.partial(_combine_kernel, split_out=split_out),
        grid_spec=pltpu.PrefetchScalarGridSpec(
            num_scalar_prefetch=1,
            grid=(N_TOK // COMB_TOK,),
            in_specs=[
                pl.BlockSpec((COMB_TOK, D_MODEL), lambda i, d: (i, 0)),
                pl.BlockSpec((1, 6, D_MODEL), lambda i, d: (_cond_of_block(i, COMB_TOK), 0, 0)),
                pl.BlockSpec((TOP_K, COMB_TOK), lambda i, d: (0, i)),
                pl.BlockSpec(memory_space=pl.ANY),
                vec, vec,
            ],
            out_specs=out_specs,
            scratch_shapes=[pltpu.VMEM((2, TOP_K, COMB_TOK, HALF_D), jnp.uint32), pltpu.SemaphoreType.DMA((2,))],
        ),
        out_shape=out_shape,
        compiler_params=_cparams(("arbitrary",)),
        name="moe_combine",
    )(dest_flat, x, mods, gates_t, ys, ln_g.reshape(1, D_MODEL), ln_b.reshape(1, D_MODEL))


def moe_layer(x, mods, layer, router_w, router_b, w_g, b_g, w_u, b_u, w_d, b_d, ln_g, ln_b, split_out=False):
    idx_t, gates_t, rank_t, counts = moe_route(x, mods, router_w[layer].T, router_b[layer])
    counts = counts[:, 0]
    padded = (counts + MOE_ROWS - 1) // MOE_ROWS * MOE_ROWS
    ends = jnp.cumsum(padded)
    base = ends - padded
    n_used = (ends[-1] // MOE_ROWS).astype(jnp.int32)
    block_start = jnp.arange(MOE_BLOCKS, dtype=jnp.int32) * MOE_ROWS
    block_e = jnp.sum(block_start[:, None] >= ends[None, :], axis=1).astype(jnp.int32)
    used = padded > 0
    experts = jnp.arange(N_EXPERTS, dtype=jnp.int32)
    block_e = jnp.minimum(block_e, jnp.max(jnp.where(used, experts, 0)))
    zstart = jnp.where(padded > 0, ends - MOE_ROWS, -1).astype(jnp.int32)
    zstart = jnp.concatenate([zstart, n_used.reshape(1)])
    onehot = idx_t[:, :, None] == jnp.arange(N_EXPERTS, dtype=jnp.int32)[None, None, :]
    dest = rank_t + jnp.sum(jnp.where(onehot, base[None, None, :], 0), axis=-1)
    dest_flat = dest.reshape(-1).astype(jnp.int32)
    xs = moe_scatter(dest_flat, zstart, x, mods)
    later_used = jnp.logical_and(used[None, :], experts[None, :] > experts[:, None])
    next_used = jnp.min(jnp.where(later_used, experts[None, :], N_EXPERTS), axis=1)
    next_used = jnp.where(next_used == N_EXPERTS, -1, next_used).astype(jnp.int32)
    run_parity = ((jnp.cumsum(used.astype(jnp.int32)) - 1) % 2).astype(jnp.int32)
    of_block = block_e[:, None] == experts[None, :]
    block_next = jnp.sum(jnp.where(of_block, next_used[None, :], 0), axis=1).astype(jnp.int32)
    block_parity = jnp.sum(jnp.where(of_block, run_parity[None, :], 0), axis=1).astype(jnp.int32)
    ys = moe_experts(block_e, n_used.reshape(1), block_next, block_parity, xs,
                     w_g, w_u, w_d, b_g, b_u, b_d, layer)
    return moe_combine(dest_flat, x, mods, gates_t, ys, ln_g, ln_b, split_out=split_out)


def kernel(x_prompt, x_sample, c, cache_k, cache_v, state_hgrn, c_ctx, w_ada, b_ada, ln_g, ln_b,
           w_in_even, b_in_even, attn_sink, hgrn_lb, hgrn_norm, w_out_even,
           conv_w_in, conv_b_in, conv_dw, conv_dw_b, conv_ln_g, conv_ln_b, conv_w_out, conv_b_out,
           router_w, router_b, moe_w_gate, moe_b_gate, moe_w_up, moe_b_up, moe_w_down, moe_b_down):
    x = jnp.concatenate([x_prompt.reshape(N_PROMPT, D_MODEL), x_sample.reshape(N_SAMPLE, D_MODEL)], axis=0)
    cond = jnp.concatenate([c_ctx[None, :], c, jnp.zeros((COND_ROWS - N_COND, D_MODEL), F32)], axis=0)
    mods_all = adaln_all(cond, w_ada, b_ada).reshape(DEPTH, COND_ROWS, 6, D_MODEL)
    lb = jax.nn.softmax(hgrn_lb.astype(F32), axis=1)
    lb = jnp.cumsum(lb, axis=1) - lb[:, :1]
    cos, sin = _rope_tables()
    new_k, new_v, new_s = [], [], []
    for layer in range(DEPTH):
        j = layer // 2
        mods = mods_all[layer]
        if layer % 2 == 0:
            q, k, v, qb, ib, ff, fb, go = inproj_even(x, mods, w_in_even[j].astype(BF16), b_in_even[j], cos, sin)
            new_k.append(k[:N_PROMPT].reshape(BATCH, SEQ, N_KV_A, HEAD_DIM))
            new_v.append(v[:N_PROMPT].reshape(BATCH, SEQ, N_KV_A, HEAD_DIM))
            attn_p = attn_context(attn_sink[j], q, k, v)
            attn_s = attn_latent(attn_sink[j], q, k, v,
                                 cache_k[:, j].reshape(DEC_BATCH, PAST_LEN, A_KV),
                                 cache_v[:, j].reshape(DEC_BATCH, PAST_LEN, A_KV))
            hg = (qb, ib, ff, fb, go, lb[0, j], lb[1, j], hgrn_norm[j])
            rec_p, states = hgrn_mixer(*hg, latent=False)
            rec_s, = hgrn_mixer(*hg, latent=True, state_in=state_hgrn, layer_j=j)
            new_s.append(states)
            x = outproj_even(x, mods, attn_p, attn_s, rec_p, rec_s, w_out_even[j].astype(BF16),
                             ln_g[layer, 0], ln_b[layer, 0])
        else:
            u = conv_in(x, mods, conv_w_in[j].astype(BF16), conv_b_in[j])
            x = conv_out(x, mods, u, conv_dw[j], conv_dw_b[j], conv_ln_g[j], conv_ln_b[j],
                         conv_w_out[j].astype(BF16), conv_b_out[j], ln_g[layer, 0], ln_b[layer, 0])
        x = moe_layer(x, mods, layer, router_w, router_b, moe_w_gate, moe_b_gate, moe_w_up, moe_b_up,
                      moe_w_down, moe_b_down, ln_g[layer, 1], ln_b[layer, 1], split_out=layer == DEPTH - 1)
    y_prompt, y_sample = x
    return (y_prompt.reshape(BATCH, SEQ, D_MODEL),
            y_sample.reshape(DEC_BATCH, DEC_SEQ, D_MODEL),
            jnp.stack(new_k, axis=1), jnp.stack(new_v, axis=1), jnp.stack(new_s, axis=1))
```

```python
import functools

import jax
import jax.numpy as jnp
import numpy as np
from jax import lax
from jax.experimental import pallas as pl
from jax.experimental.pallas import tpu as pltpu

D_MODEL = 1024
BATCH = 16
SEQ = 256
DEPTH = 4
DEC_BATCH = 4
DEC_SEQ = 2048
PAST_LEN = 512
GRID_W = 64
N_EVEN = (DEPTH + 1) // 2
N_ODD = DEPTH // 2
HEAD_DIM = 64
N_HEADS_A = 8
N_KV_A = 2
GROUP_A = N_HEADS_A // N_KV_A
WINDOW = 128
ATTN_BLOCK = 128
SCALE_A = HEAD_DIM ** -0.5
ROPE_BASE = 10000.0
ROPE_PAIRS = HEAD_DIM // 4
N_HEADS_B = 4
HGRN_DK = 128
HGRN_DV = 128
CONV_WIDTH = 31
N_EXPERTS = 32
TOP_K = 4
D_EXPERT = D_MODEL
SWIGLU_LIMIT = 7.0
SWIGLU_ALPHA = 1.702
LN_EPS = 1e-5
RMS_EPS = 1e-6
MASK_VALUE = -1e9
LB_FLOOR = 1e-30
DN_ALPHA = (2 * DEPTH) ** 0.25
A_Q = N_HEADS_A * HEAD_DIM
A_KV = N_KV_A * HEAD_DIM
B_QK = N_HEADS_B * HGRN_DK
B_V = N_HEADS_B * HGRN_DV
IN_SIZES = (A_Q, A_KV, A_KV, B_QK, B_V, B_QK, B_QK, B_V)
D_IN_EVEN = sum(IN_SIZES)

N_PROMPT = BATCH * SEQ
N_SAMPLE = DEC_BATCH * DEC_SEQ
N_TOK = N_PROMPT + N_SAMPLE
N_COND = 1 + DEC_BATCH
COND_ROWS = 8

LANES = 128
SUBLANES = 8
VMEM_LIMIT = 56 * 1024 * 1024

ROW_BLOCK = 256
DENSE_ROWS = 512
HG_BLOCK = 128
HG_SUB = 8
MOE_ROWS = 512
MOE_BLOCKS = (N_TOK * TOP_K + N_EXPERTS * (MOE_ROWS - 1)) // MOE_ROWS + 1
MOE_R = MOE_BLOCKS * MOE_ROWS
COMB_TOK = 256

F32 = jnp.float32
BF16 = jnp.bfloat16
HIGHEST = lax.Precision.HIGHEST


def _cond_of_block(i, rows):
    start = i * rows
    return jnp.where(start < N_PROMPT, 0, 1 + (start - N_PROMPT) // DEC_SEQ)


def _cparams(sem):
    return pltpu.CompilerParams(dimension_semantics=sem, vmem_limit_bytes=VMEM_LIMIT)


def _layer_norm(z, g, b):
    mu = jnp.mean(z, axis=-1, keepdims=True)
    zc = z - mu
    var = jnp.mean(zc * zc, axis=-1, keepdims=True)
    return zc * lax.rsqrt(var + LN_EPS) * g + b


def _dot(a, b):
    return jnp.dot(a, b, preferred_element_type=F32)


def _dot_nt(a, b):
    return lax.dot_general(a, b, (((1,), (1,)), ((), ())), preferred_element_type=F32)


ADA_TN = 3072


def _adaln_kernel(cond_ref, w_ref, b_ref, o_ref):
    c = cond_ref[...]
    s = c * jax.nn.sigmoid(c)
    o_ref[0] = jnp.dot(s, w_ref[0], precision=HIGHEST, preferred_element_type=F32) + b_ref[0]


def adaln_all(cond, w_ada, b_ada):
    n_out = 6 * D_MODEL
    return pl.pallas_call(
        _adaln_kernel,
        grid=(DEPTH, n_out // ADA_TN),
        in_specs=[
            pl.BlockSpec((COND_ROWS, D_MODEL), lambda l, n: (0, 0)),
            pl.BlockSpec((1, D_MODEL, ADA_TN), lambda l, n: (l, 0, n)),
            pl.BlockSpec((1, 1, ADA_TN), lambda l, n: (l, 0, n)),
        ],
        out_specs=pl.BlockSpec((1, COND_ROWS, ADA_TN), lambda l, n: (l, 0, n)),
        out_shape=jax.ShapeDtypeStruct((DEPTH, COND_ROWS, n_out), F32),
        compiler_params=_cparams(("parallel", "parallel")),
        name="adaln",
    )(cond, w_ada, b_ada.reshape(DEPTH, 1, n_out))


def _rope_tables():
    t = np.arange(DEC_SEQ)
    d = np.arange(LANES) % HEAD_DIM
    axis = d // (2 * ROPE_PAIRS)
    half = (d // ROPE_PAIRS) % 2
    pair = d % ROPE_PAIRS
    pos = jnp.where(axis[None, :] == 0, (t // GRID_W)[:, None], (t % GRID_W)[:, None]).astype(F32)
    inv_freq = ROPE_BASE ** (-jnp.arange(ROPE_PAIRS, dtype=F32) / ROPE_PAIRS)
    ang = pos * inv_freq[pair][None, :]
    sign = jnp.where(half[None, :] == 0, -1.0, 1.0).astype(F32)
    return jnp.cos(ang), jnp.sin(ang) * sign


def _rope(x, cos, sin_signed):
    lane = lax.broadcasted_iota(jnp.int32, x.shape, 1)
    first_half = (lane // ROPE_PAIRS) % 2 == 0
    partner = jnp.where(first_half, pltpu.roll(x, LANES - ROPE_PAIRS, 1), pltpu.roll(x, ROPE_PAIRS, 1))
    return x * cos + partner * sin_signed


def _inproj_kernel(x_ref, mod_ref, w_ref, b_ref, cos_ref, sin_ref,
                   q_ref, k_ref, v_ref, qb_ref, ib_ref, ff_ref, fb_ref, go_ref):
    i = pl.program_id(0)
    shift = mod_ref[0, 0:1, :]
    scale = mod_ref[0, 1:2, :]
    h = (x_ref[...] * (1.0 + scale) + shift).astype(BF16)
    y = _dot(h, w_ref[...]) + b_ref[...]
    offs = np.cumsum((0,) + IN_SIZES)
    q = y[:, offs[0]:offs[1]]
    k = y[:, offs[1]:offs[2]]
    v_ref[...] = y[:, offs[2]:offs[3]]
    qb_ref[...] = y[:, offs[3]:offs[4]]
    ib_ref[...] = y[:, offs[4]:offs[5]]
    ff_ref[...] = y[:, offs[5]:offs[6]]
    fb_ref[...] = y[:, offs[6]:offs[7]]
    go_ref[...] = y[:, offs[7]:offs[8]]
    is_latent = i * DENSE_ROWS >= N_PROMPT

    @pl.when(jnp.logical_not(is_latent))
    def _():
        q_ref[...] = q
        k_ref[...] = k

    @pl.when(is_latent)
    def _():
        cos = cos_ref[...]
        sin = sin_ref[...]
        for c in range(A_Q // LANES):
            q_ref[:, c * LANES:(c + 1) * LANES] = _rope(q[:, c * LANES:(c + 1) * LANES], cos, sin)
        k_ref[...] = _rope(k, cos, sin)


def inproj_even(x, mods, w_bf, b, cos, sin):
    nblk = N_TOK // DENSE_ROWS
    pos_blocks = DEC_SEQ // DENSE_ROWS

    def pos_map(i):
        return (jnp.maximum(i - N_PROMPT // DENSE_ROWS, 0) % pos_blocks, 0)

    row = lambda i: (i, 0)
    widths = (A_Q, A_KV, A_KV, B_QK, B_V, B_QK, B_QK, B_V)
    return pl.pallas_call(
        _inproj_kernel,
        grid=(nblk,),
        in_specs=[
            pl.BlockSpec((DENSE_ROWS, D_MODEL), row),
            pl.BlockSpec((1, 6, D_MODEL), lambda i: (_cond_of_block(i, DENSE_ROWS), 0, 0)),
            pl.BlockSpec((D_MODEL, D_IN_EVEN), lambda i: (0, 0)),
            pl.BlockSpec((1, D_IN_EVEN), lambda i: (0, 0)),
            pl.BlockSpec((DENSE_ROWS, LANES), pos_map),
            pl.BlockSpec((DENSE_ROWS, LANES), pos_map),
        ],
        out_specs=[pl.BlockSpec((DENSE_ROWS, w), row) for w in widths],
        out_shape=[jax.ShapeDtypeStruct((N_TOK, w), F32) for w in widths],
        compiler_params=_cparams(("parallel",)),
        name="inproj_even",
    )(x, mods, w_bf, b.reshape(1, D_IN_EVEN), cos, sin)


def _sink_attend(q, keys, vals, sink, masks):
    scores = []
    for kk, mask in zip(keys, masks):
        s = _dot_nt(q, kk) * SCALE_A
        if mask is not None:
            s = jnp.where(mask, s, MASK_VALUE)
        scores.append(s)
    m = sink
    for s in scores:
        m = jnp.maximum(m, jnp.max(s, axis=-1, keepdims=True))
    denom = jnp.exp(sink - m)
    acc = None
    for s, vv in zip(scores, vals):
        p = jnp.exp(s - m)
        denom = denom + jnp.sum(p, axis=-1, keepdims=True)
        pv = _dot(p.astype(BF16), vv)
        acc = pv if acc is None else acc + pv
    return acc / denom


def _attn_ctx_kernel(sink_ref, q_ref, k_ref, v_ref, o_ref):
    k = k_ref[...].astype(BF16)
    v = v_ref[...].astype(BF16)
    q = q_ref[...].astype(BF16)
    for h in range(N_HEADS_A):
        kv = h // GROUP_A
        qh = q[:, h * HEAD_DIM:(h + 1) * HEAD_DIM]
        kh = k[:, kv * HEAD_DIM:(kv + 1) * HEAD_DIM]
        vh = v[:, kv * HEAD_DIM:(kv + 1) * HEAD_DIM]
        o_ref[:, h * HEAD_DIM:(h + 1) * HEAD_DIM] = _sink_attend(qh, [kh], [vh], sink_ref[h], [None])


def attn_context(sink, q, k, v):
    row = lambda b: (b, 0)
    return pl.pallas_call(
        _attn_ctx_kernel,
        grid=(BATCH,),
        in_specs=[
            pl.BlockSpec(memory_space=pltpu.SMEM),
            pl.BlockSpec((SEQ, A_Q), row),
            pl.BlockSpec((SEQ, A_KV), row),
            pl.BlockSpec((SEQ, A_KV), row),
        ],
        out_specs=pl.BlockSpec((SEQ, A_Q), row),
        out_shape=jax.ShapeDtypeStruct((N_PROMPT, A_Q), F32),
        compiler_params=_cparams(("parallel",)),
        name="attn_context",
    )(sink, q, k, v)


def _attn_lat_kernel(sink_ref, q_ref, kp_ref, kc_ref, kn_ref, vp_ref, vc_ref, vn_ref, ck_ref, cv_ref, o_ref):
    n = pl.program_id(1)
    nb = DEC_SEQ // ATTN_BLOCK
    qi = lax.broadcasted_iota(jnp.int32, (ATTN_BLOCK, ATTN_BLOCK), 0)
    kj = lax.broadcasted_iota(jnp.int32, (ATTN_BLOCK, ATTN_BLOCK), 1)
    mask_prev = jnp.logical_and(kj - qi >= ATTN_BLOCK - WINDOW, n > 0)
    mask_next = jnp.logical_and(kj - qi <= WINDOW - ATTN_BLOCK, n < nb - 1)
    masks = [mask_prev, None, mask_next, None]
    q = q_ref[...].astype(BF16)
    kband = [r[...].astype(BF16) for r in (kp_ref, kc_ref, kn_ref)]
    vband = [r[...].astype(BF16) for r in (vp_ref, vc_ref, vn_ref)]
    ck = ck_ref[0].astype(BF16)
    cv = cv_ref[0].astype(BF16)
    for h in range(N_HEADS_A):
        kv = h // GROUP_A
        sl = slice(kv * HEAD_DIM, (kv + 1) * HEAD_DIM)
        qh = q[:, h * HEAD_DIM:(h + 1) * HEAD_DIM]
        keys = [kb[:, sl] for kb in kband] + [ck[:, sl]]
        vals = [vb[:, sl] for vb in vband] + [cv[:, sl]]
        o_ref[:, h * HEAD_DIM:(h + 1) * HEAD_DIM] = _sink_attend(qh, keys, vals, sink_ref[h], masks)


def attn_latent(sink, q, k, v, cache_k, cache_v):
    nb = DEC_SEQ // ATTN_BLOCK
    base = N_PROMPT // ATTN_BLOCK

    def blk(delta):
        return lambda b, n: (base + b * nb + jnp.clip(n + delta, 0, nb - 1), 0)

    kv_spec = lambda delta: pl.BlockSpec((ATTN_BLOCK, A_KV), blk(delta))
    cache_spec = pl.BlockSpec((1, PAST_LEN, A_KV), lambda b, n: (b, 0, 0))
    return pl.pallas_call(
        _attn_lat_kernel,
        grid=(DEC_BATCH, nb),
        in_specs=[
            pl.BlockSpec(memory_space=pltpu.SMEM),
            pl.BlockSpec((ATTN_BLOCK, A_Q), blk(0)),
            kv_spec(-1), kv_spec(0), kv_spec(1),
            kv_spec(-1), kv_spec(0), kv_spec(1),
            cache_spec, cache_spec,
        ],
        out_specs=pl.BlockSpec((ATTN_BLOCK, A_Q), lambda b, n: (b * nb + n, 0)),
        out_shape=jax.ShapeDtypeStruct((N_SAMPLE, A_Q), F32),
        compiler_params=_cparams(("parallel", "parallel")),
        name="attn_latent",
    )(sink, q, k, k, k, v, v, v, cache_k, cache_v)


def _hgrn_gate(f_pre, lb):
    f = jnp.maximum(lb, LB_FLOOR) + (1.0 - lb) * jax.nn.sigmoid(f_pre)
    return 1.0 - f, jnp.log(f)


def _scan_rows(x, reverse):
    n = x.shape[0]
    row = lax.broadcasted_iota(jnp.int32, x.shape, 0)
    sh = 1
    while sh < n:
        if reverse:
            x = x + jnp.where(row < n - sh, pltpu.roll(x, n - sh, 0), 0.0)
        else:
            x = x + jnp.where(row >= sh, pltpu.roll(x, sh, 0), 0.0)
        sh *= 2
    return x


def _chunk_bcast(x, c, pick, shift):
    nc = HG_BLOCK // c
    rows = x.reshape(nc, c, x.shape[-1])[:, pick:pick + 1, :]
    zero = jnp.zeros((1, 1, x.shape[-1]), x.dtype)
    if shift == -1:
        rows = jnp.concatenate([zero, rows[:-1]], axis=0)
    elif shift == 1:
        rows = jnp.concatenate([rows[1:], zero], axis=0)
    return jnp.broadcast_to(rows, (nc, c, x.shape[-1])).reshape(HG_BLOCK, x.shape[-1])


def _hgrn_block(q, kk, v, log_f, st, reverse):
    cum = _scan_rows(log_f, reverse)
    tot = cum[0:1, :] if reverse else cum[HG_BLOCK - 1:HG_BLOCK, :]
    o = _dot_nt((q * jnp.exp(cum)).astype(BF16), st.astype(BF16))
    kd = (kk * jnp.exp(tot - cum)).astype(BF16)
    v_bf = v.astype(BF16)
    u_t = lax.dot_general(v_bf, kd, (((0,), (0,)), ((), ())), preferred_element_type=F32)
    st_new = st * jnp.exp(tot) + u_t
    ti = lax.broadcasted_iota(jnp.int32, (HG_BLOCK, HG_BLOCK), 0)
    si = lax.broadcasted_iota(jnp.int32, (HG_BLOCK, HG_BLOCK), 1)
    a = jnp.zeros((HG_BLOCK, HG_BLOCK), F32)
    c = HG_SUB
    while c < HG_BLOCK:
        if reverse:
            bound_t = _chunk_bcast(cum, c, 0, 1)
            bound_s = _chunk_bcast(cum, c, 0, 0)
            mask = jnp.logical_and((ti // c) % 2 == 0, si // c == ti // c + 1)
        else:
            bound_t = _chunk_bcast(cum, c, c - 1, -1)
            bound_s = _chunk_bcast(cum, c, c - 1, 0)
            mask = jnp.logical_and((ti // c) % 2 == 1, si // c == ti // c - 1)
        qc = (q * jnp.exp(cum - bound_t)).astype(BF16)
        kc = (kk * jnp.exp(bound_s - cum)).astype(BF16)
        a = a + jnp.where(mask, _dot_nt(qc, kc), 0.0)
        c *= 2
    o = o + _dot(a.astype(BF16), v_bf)
    nsub = HG_BLOCK // HG_SUB
    q3 = q.reshape(nsub, HG_SUB, HGRN_DK)
    k3 = kk.reshape(nsub, HG_SUB, HGRN_DK)
    v3 = v.reshape(nsub, HG_SUB, HGRN_DV)
    cum3 = cum.reshape(nsub, HG_SUB, HGRN_DK)
    t_off = lax.broadcasted_iota(jnp.int32, (1, HG_SUB, 1), 1)
    od = jnp.zeros((nsub, HG_SUB, HGRN_DV), F32)
    for s in range(HG_SUB):
        causal = (t_off <= s) if reverse else (t_off >= s)
        decay = jnp.exp(jnp.where(causal, cum3 - cum3[:, s:s + 1, :], MASK_VALUE))
        score = jnp.sum(q3 * decay * k3[:, s:s + 1, :], axis=-1, keepdims=True)
        od = od + score * v3[:, s:s + 1, :]
    return o + od.reshape(HG_BLOCK, HGRN_DV), st_new


def _hgrn_kernel(*refs, n_blocks, has_state_in, has_state_out):
    refs = list(refs)
    qb_ref, ib_ref, ff_ref, fb_ref, go_ref, lbf_ref, lbb_ref, nw_ref = refs[:8]
    refs = refs[8:]
    s0_ref = refs.pop(0) if has_state_in else None
    o_ref = refs.pop(0)
    so_ref = refs.pop(0) if has_state_out else None
    acc_ref = refs.pop(0)

    def run(reverse):
        f_ref, lb_ref = (fb_ref, lbb_ref) if reverse else (ff_ref, lbf_ref)
        lb = lb_ref[...]
        if has_state_in:
            st0 = s0_ref[0, 0, 1 if reverse else 0, 0].T
        else:
            st0 = jnp.zeros((HGRN_DV, HGRN_DK), F32)

        def body(it, st):
            blk = (n_blocks - 1 - it) if reverse else it
            rows = pl.ds(pl.multiple_of(blk * HG_BLOCK, HG_BLOCK), HG_BLOCK)
            qpre = qb_ref[rows, :]
            q = qpre * jax.nn.sigmoid(qpre)
            kk, log_f = _hgrn_gate(f_ref[rows, :], lb)
            o, st = _hgrn_block(q, kk, ib_ref[rows, :], log_f, st, reverse)
            if reverse:
                acc_ref[rows, :] = acc_ref[rows, :] + o
            else:
                acc_ref[rows, :] = o
            return st

        return lax.fori_loop(0, n_blocks, body, st0)

    st_f = run(False)
    st_b = run(True)
    if has_state_out:
        so_ref[0, 0, 0] = st_f.T
        so_ref[0, 1, 0] = st_b.T
    o = acc_ref[...]
    o = o * lax.rsqrt(jnp.mean(o * o, axis=-1, keepdims=True) + RMS_EPS) * nw_ref[...]
    g = go_ref[...]
    o_ref[...] = o * (g * jax.nn.sigmoid(g))


def hgrn_mixer(qb, ib, ff, fb, go, lb_f, lb_b, norm_w, *, latent, state_in=None, layer_j=0):
    t_len = DEC_SEQ if latent else SEQ
    n_seq = DEC_BATCH if latent else BATCH
    row0 = N_PROMPT // t_len if latent else 0
    tok = pl.BlockSpec((t_len, HGRN_DK), lambda b, h: (row0 + b, h))
    vec = pl.BlockSpec((1, HGRN_DK), lambda b, h: (0, h))
    in_specs = [tok] * 5 + [vec] * 3
    args = [qb, ib, ff, fb, go, lb_f.reshape(1, B_QK), lb_b.reshape(1, B_QK), norm_w.reshape(1, B_V)]
    if latent:
        in_specs.append(pl.BlockSpec((1, 1, 2, 1, HGRN_DK, HGRN_DV), lambda b, h: (b, layer_j, 0, h, 0, 0)))
        args.append(state_in)
    out_specs = [pl.BlockSpec((t_len, HGRN_DV), lambda b, h: (b, h))]
    out_shape = [jax.ShapeDtypeStruct((n_seq * t_len, B_V), F32)]
    if not latent:
        out_specs.append(pl.BlockSpec((1, 2, 1, HGRN_DK, HGRN_DV), lambda b, h: (b, 0, h, 0, 0)))
        out_shape.append(jax.ShapeDtypeStruct((BATCH, 2, N_HEADS_B, HGRN_DK, HGRN_DV), F32))
    kern = functools.partial(_hgrn_kernel, n_blocks=t_len // HG_BLOCK, has_state_in=latent, has_state_out=not latent)
    return pl.pallas_call(
        kern,
        grid=(n_seq, N_HEADS_B),
        in_specs=in_specs,
        out_specs=out_specs,
        out_shape=out_shape,
        scratch_shapes=[pltpu.VMEM((t_len, HGRN_DV), F32)],
        compiler_params=_cparams(("parallel", "parallel")),
        name="hgrn_latent" if latent else "hgrn_context",
    )(*args)


def _outproj_kernel(x_ref, mod_ref, ap_ref, as_ref, rp_ref, rs_ref, w_ref, g_ref, b_ref, o_ref):
    is_latent = pl.program_id(0) * DENSE_ROWS >= N_PROMPT
    attn = jnp.where(is_latent, as_ref[...], ap_ref[...]).astype(BF16)
    rec = jnp.where(is_latent, rs_ref[...], rp_ref[...]).astype(BF16)
    y = _dot(attn, w_ref[0:A_Q, :]) + _dot(rec, w_ref[A_Q:A_Q + B_V, :])
    z = DN_ALPHA * x_ref[...] + mod_ref[0, 2:3, :] * y
    o_ref[...] = _layer_norm(z, g_ref[...], b_ref[...])


def outproj_even(x, mods, attn_p, attn_s, rec_p, rec_s, w_bf, ln_g, ln_b):
    nblk = N_TOK // DENSE_ROWS
    npb = N_PROMPT // DENSE_ROWS
    row = lambda i: (i, 0)
    prow = lambda i: (jnp.minimum(i, npb - 1), 0)
    srow = lambda i: (jnp.maximum(i - npb, 0), 0)
    vec = pl.BlockSpec((1, D_MODEL), lambda i: (0, 0))
    return pl.pallas_call(
        _outproj_kernel,
        grid=(nblk,),
        in_specs=[
            pl.BlockSpec((DENSE_ROWS, D_MODEL), row),
            pl.BlockSpec((1, 6, D_MODEL), lambda i: (_cond_of_block(i, DENSE_ROWS), 0, 0)),
            pl.BlockSpec((DENSE_ROWS, A_Q), prow),
            pl.BlockSpec((DENSE_ROWS, A_Q), srow),
            pl.BlockSpec((DENSE_ROWS, B_V), prow),
            pl.BlockSpec((DENSE_ROWS, B_V), srow),
            pl.BlockSpec((A_Q + B_V, D_MODEL), lambda i: (0, 0)),
            vec, vec,
        ],
        out_specs=pl.BlockSpec((DENSE_ROWS, D_MODEL), row),
        out_shape=jax.ShapeDtypeStruct((N_TOK, D_MODEL), F32),
        compiler_params=_cparams(("parallel",)),
        name="outproj_even",
    )(x, mods, attn_p, attn_s, rec_p, rec_s, w_bf, ln_g.reshape(1, D_MODEL), ln_b.reshape(1, D_MODEL))


def _conv_in_kernel(x_ref, mod_ref, w_ref, b_ref, u_ref):
    h = (x_ref[...] * (1.0 + mod_ref[0, 1:2, :]) + mod_ref[0, 0:1, :]).astype(BF16)
    a = _dot(h, w_ref[:, 0:D_MODEL]) + b_ref[:, 0:D_MODEL]
    gt = _dot(h, w_ref[:, D_MODEL:2 * D_MODEL]) + b_ref[:, D_MODEL:2 * D_MODEL]
    u_ref[...] = a * jax.nn.sigmoid(gt)


def conv_in(x, mods, w_bf, b):
    row = lambda i: (i, 0)
    return pl.pallas_call(
        _conv_in_kernel,
        grid=(N_TOK // DENSE_ROWS,),
        in_specs=[
            pl.BlockSpec((DENSE_ROWS, D_MODEL), row),
            pl.BlockSpec((1, 6, D_MODEL), lambda i: (_cond_of_block(i, DENSE_ROWS), 0, 0)),
            pl.BlockSpec((D_MODEL, 2 * D_MODEL), lambda i: (0, 0)),
            pl.BlockSpec((1, 2 * D_MODEL), lambda i: (0, 0)),
        ],
        out_specs=pl.BlockSpec((DENSE_ROWS, D_MODEL), row),
        out_shape=jax.ShapeDtypeStruct((N_TOK, D_MODEL), F32),
        compiler_params=_cparams(("parallel",)),
        name="conv_in",
    )(x, mods, w_bf, b.reshape(1, 2 * D_MODEL))


CONV_HALO = 16
CONV_LANES = 256


CONV_SHIFT_ROWS = ROW_BLOCK + 2 * CONV_HALO - SUBLANES
CONV_ROW_CHUNK = 128


def _conv_out_kernel(x_ref, mod_ref, up_ref, uc_ref, un_ref, dw_ref, dwb_ref, cg_ref, cb_ref,
                     w_ref, b_ref, g_ref, bb_ref, o_ref, pad_ref, acc_ref, sh_ref):
    i = pl.program_id(0)
    blocks_per_seq = DEC_SEQ // ROW_BLOCK
    j = i - N_PROMPT // ROW_BLOCK
    is_latent = j >= 0
    has_prev = jnp.logical_and(is_latent, j % blocks_per_seq != 0)
    has_next = jnp.logical_and(is_latent, j % blocks_per_seq != blocks_per_seq - 1)
    pad_ref[0:CONV_HALO, :] = jnp.where(has_prev, up_ref[...], 0.0)
    pad_ref[CONV_HALO:CONV_HALO + ROW_BLOCK, :] = uc_ref[...]
    pad_ref[CONV_HALO + ROW_BLOCK:, :] = jnp.where(has_next, un_ref[...], 0.0)
    first = CONV_HALO - CONV_WIDTH // 2
    for c in range(D_MODEL // CONV_LANES):
        lanes = slice(c * CONV_LANES, (c + 1) * CONV_LANES)
        for s in range(SUBLANES):
            sh_ref[s] = pad_ref[s:s + CONV_SHIFT_ROWS, lanes]
        for r0 in range(0, ROW_BLOCK, CONV_ROW_CHUNK):
            acc = jnp.zeros((CONV_ROW_CHUNK, CONV_LANES), F32) + dwb_ref[:, lanes]
            for tap in range(CONV_WIDTH):
                whole, s = divmod(first + tap, SUBLANES)
                rows = slice(r0 + whole * SUBLANES, r0 + whole * SUBLANES + CONV_ROW_CHUNK)
                acc = acc + sh_ref[s, rows, :] * dw_ref[tap:tap + 1, lanes]
            acc_ref[r0:r0 + CONV_ROW_CHUNK, lanes] = acc
    u = _layer_norm(acc_ref[...], cg_ref[...], cb_ref[...])
    u = (u * jax.nn.sigmoid(u)).astype(BF16)
    y = _dot(u, w_ref[...]) + b_ref[...]
    z = DN_ALPHA * x_ref[...] + mod_ref[0, 2:3, :] * y
    o_ref[...] = _layer_norm(z, g_ref[...], bb_ref[...])


def conv_out(x, mods, u, dw, dw_b, cln_g, cln_b, w_bf, b_out, ln_g, ln_b):
    nblk = N_TOK // ROW_BLOCK
    ratio = ROW_BLOCK // CONV_HALO
    nhalo = N_TOK // CONV_HALO
    row = lambda i: (i, 0)
    vec = pl.BlockSpec((1, D_MODEL), lambda i: (0, 0))
    r1 = lambda a: a.reshape(1, D_MODEL)
    return pl.pallas_call(
        _conv_out_kernel,
        grid=(nblk,),
        in_specs=[
            pl.BlockSpec((ROW_BLOCK, D_MODEL), row),
            pl.BlockSpec((1, 6, D_MODEL), lambda i: (_cond_of_block(i, ROW_BLOCK), 0, 0)),
            pl.BlockSpec((CONV_HALO, D_MODEL), lambda i: (jnp.maximum(i * ratio - 1, 0), 0)),
            pl.BlockSpec((ROW_BLOCK, D_MODEL), row),
            pl.BlockSpec((CONV_HALO, D_MODEL), lambda i: (jnp.minimum((i + 1) * ratio, nhalo - 1), 0)),
            pl.BlockSpec((CONV_WIDTH, D_MODEL), lambda i: (0, 0)),
            vec, vec, vec,
            pl.BlockSpec((D_MODEL, D_MODEL), lambda i: (0, 0)),
            vec, vec, vec,
        ],
        out_specs=pl.BlockSpec((ROW_BLOCK, D_MODEL), row),
        out_shape=jax.ShapeDtypeStruct((N_TOK, D_MODEL), F32),
        scratch_shapes=[pltpu.VMEM((ROW_BLOCK + 2 * CONV_HALO, D_MODEL), F32),
                        pltpu.VMEM((ROW_BLOCK, D_MODEL), F32),
                        pltpu.VMEM((SUBLANES, CONV_SHIFT_ROWS, CONV_LANES), F32)],
        compiler_params=_cparams(("parallel",)),
        name="conv_out",
    )(x, mods, u, u, u, dw, r1(dw_b), r1(cln_g), r1(cln_b), w_bf, r1(b_out), r1(ln_g), r1(ln_b))


def _modulate2(x_ref, mod_ref):
    return x_ref[...] * (1.0 + mod_ref[0, 4:5, :]) + mod_ref[0, 3:4, :]


def _route_kernel(x_ref, mod_ref, wr_ref, br_ref, idx_ref, gate_ref, rank_ref, cnt_ref, carry_ref):
    i = pl.program_id(0)

    @pl.when(i == 0)
    def _():
        carry_ref[...] = jnp.zeros_like(carry_ref)

    h = _modulate2(x_ref, mod_ref)
    h_hi = h.astype(BF16)
    h_lo = (h - h_hi.astype(F32)).astype(BF16)
    w = wr_ref[...]
    w_hi = w.astype(BF16)
    w_lo = (w - w_hi.astype(F32)).astype(BF16)
    logits = _dot_nt(w_hi, h_hi) + (_dot_nt(w_hi, h_lo) + _dot_nt(w_lo, h_hi)) + br_ref[...]
    eidx = lax.broadcasted_iota(jnp.int32, logits.shape, 0)
    vals = logits
    sels, tops = [], []
    for k in range(TOP_K):
        m = jnp.max(vals, axis=0, keepdims=True)
        idx = jnp.min(jnp.where(vals == m, eidx, N_EXPERTS), axis=0, keepdims=True)
        sel = eidx == idx
        idx_ref[k:k + 1, :] = idx
        sels.append(sel)
        tops.append(m)
        vals = jnp.where(sel, -jnp.inf, vals)
    exps = [jnp.exp(t - tops[0]) for t in tops]
    total = exps[0] + exps[1] + exps[2] + exps[3]
    for k in range(TOP_K):
        gate_ref[k:k + 1, :] = exps[k] / total
    onehot = jnp.zeros(logits.shape, F32)
    for sel in sels:
        onehot = onehot + sel.astype(F32)
    ta = lax.broadcasted_iota(jnp.int32, (ROW_BLOCK, ROW_BLOCK), 0)
    tb = lax.broadcasted_iota(jnp.int32, (ROW_BLOCK, ROW_BLOCK), 1)
    before = _dot(onehot.astype(BF16), (ta < tb).astype(BF16)) + carry_ref[:, 0:1]
    for k in range(TOP_K):
        rank = jnp.sum(jnp.where(sels[k], before, 0.0), axis=0, keepdims=True)
        rank_ref[k:k + 1, :] = rank.astype(jnp.int32)
    carry = carry_ref[...] + jnp.sum(onehot, axis=1, keepdims=True)
    carry_ref[...] = carry
    cnt_ref[...] = carry.astype(jnp.int32)


def moe_route(x, mods, wr_t, b_r):
    tok = pl.BlockSpec((TOP_K, ROW_BLOCK), lambda i: (0, i))
    return pl.pallas_call(
        _route_kernel,
        grid=(N_TOK // ROW_BLOCK,),
        in_specs=[
            pl.BlockSpec((ROW_BLOCK, D_MODEL), lambda i: (i, 0)),
            pl.BlockSpec((1, 6, D_MODEL), lambda i: (_cond_of_block(i, ROW_BLOCK), 0, 0)),
            pl.BlockSpec((N_EXPERTS, D_MODEL), lambda i: (0, 0)),
            pl.BlockSpec((N_EXPERTS, 1), lambda i: (0, 0)),
        ],
        out_specs=[tok, tok, tok, pl.BlockSpec((N_EXPERTS, LANES), lambda i: (0, 0))],
        out_shape=[jax.ShapeDtypeStruct((TOP_K, N_TOK), jnp.int32),
                   jax.ShapeDtypeStruct((TOP_K, N_TOK), F32),
                   jax.ShapeDtypeStruct((TOP_K, N_TOK), jnp.int32),
                   jax.ShapeDtypeStruct((N_EXPERTS, LANES), jnp.int32)],
        scratch_shapes=[pltpu.VMEM((N_EXPERTS, LANES), F32)],
        compiler_params=_cparams(("arbitrary",)),
        name="moe_route",
    )(x, mods, wr_t, b_r.reshape(N_EXPERTS, 1))


HALF_D = D_MODEL // 2
HI_MASK = 0xFFFF0000


def _pack_bf16_pairs(h):
    half = h.shape[1] // 2
    lo = lax.bitcast_convert_type(h[:, :half].astype(BF16).astype(F32), jnp.uint32)
    hi = lax.bitcast_convert_type(h[:, half:].astype(BF16).astype(F32), jnp.uint32)
    return (lo >> 16) | (hi & jnp.uint32(HI_MASK))


def _unpack_bf16_pairs(w):
    lo = lax.bitcast_convert_type(w << 16, F32).astype(BF16)
    hi = lax.bitcast_convert_type(w & jnp.uint32(HI_MASK), F32).astype(BF16)
    return lo, hi


def _scatter_kernel(dest_ref, zstart_ref, x_ref, mod_ref, xs_ref, pk_ref, zero_ref, sem, zsem):
    i = pl.program_id(0)

    def zero_copy(start):
        start = pl.multiple_of(start, MOE_ROWS)
        return pltpu.make_async_copy(zero_ref, xs_ref.at[pl.ds(start, MOE_ROWS), :], zsem)

    def zero_blocks(fn):
        for e in range(N_EXPERTS):
            @pl.when(zstart_ref[e] >= 0)
            def _():
                fn(zero_copy(jnp.maximum(zstart_ref[e], 0)))

        def unused(b, carry):
            fn(zero_copy(b * MOE_ROWS))
            return carry

        lax.fori_loop(zstart_ref[N_EXPERTS], MOE_BLOCKS, unused, 0)

    @pl.when(i == 0)
    def _():
        zero_ref[...] = jnp.zeros_like(zero_ref)
        zero_blocks(lambda cp: cp.start())
        zero_blocks(lambda cp: cp.wait())

    base = i * ROW_BLOCK
    last = pl.num_programs(0) - 1

    def drain(slot):
        for k in range(TOP_K):
            pltpu.make_async_copy(pk_ref.at[slot], xs_ref.at[pl.ds(0, ROW_BLOCK), :], sem.at[slot]).wait()

    for slot in range(2):
        @pl.when(i % 2 == slot)
        def _():
            @pl.when(i >= 2)
            def _():
                drain(slot)

            pk_ref[slot] = _pack_bf16_pairs(_modulate2(x_ref, mod_ref))
            for t in range(ROW_BLOCK):
                for k in range(TOP_K):
                    row = dest_ref[k * N_TOK + base + t]
                    pltpu.make_async_copy(pk_ref.at[slot, pl.ds(t, 1), :], xs_ref.at[pl.ds(row, 1), :],
                                          sem.at[slot]).start(priority=k % 2)

            @pl.when(i == last)
            def _():
                drain(slot)

                @pl.when(i >= 1)
                def _():
                    drain(1 - slot)


def moe_scatter(dest_flat, zstart, x, mods):
    return pl.pallas_call(
        _scatter_kernel,
        grid_spec=pltpu.PrefetchScalarGridSpec(
            num_scalar_prefetch=2,
            grid=(N_TOK // ROW_BLOCK,),
            in_specs=[
                pl.BlockSpec((ROW_BLOCK, D_MODEL), lambda i, d, z: (i, 0)),
                pl.BlockSpec((1, 6, D_MODEL), lambda i, d, z: (_cond_of_block(i, ROW_BLOCK), 0, 0)),
            ],
            out_specs=pl.BlockSpec(memory_space=pl.ANY),
            scratch_shapes=[pltpu.VMEM((2, ROW_BLOCK, HALF_D), jnp.uint32),
                            pltpu.VMEM((MOE_ROWS, HALF_D), jnp.uint32),
                            pltpu.SemaphoreType.DMA((2,)), pltpu.SemaphoreType.DMA],
        ),
        out_shape=jax.ShapeDtypeStruct((MOE_R, HALF_D), jnp.uint32),
        compiler_params=_cparams(("arbitrary",)),
        name="moe_scatter",
    )(dest_flat, zstart, x, mods)


N_CHUNK = 256


def _expert_kernel(be_ref, nused_ref, nexte_ref, par_ref, xs_ref, wg_hbm, wu_hbm, wd_hbm, bg_ref, bu_ref, bd_ref,
                   ys_ref, wbuf, act_ref, wg_bf, wu_bf, wd_bf, wsem, *, layer):
    i = pl.program_id(0)
    new_expert = jnp.logical_or(i == 0, be_ref[i] != be_ref[jnp.maximum(i - 1, 0)])
    slot = par_ref[i]

    def weight_copies(e, s):
        return [pltpu.make_async_copy(w.at[layer, e], wbuf.at[s, j], wsem.at[s])
                for j, w in enumerate((wg_hbm, wu_hbm, wd_hbm))]

    @pl.when(i == 0)
    def _():
        for cp in weight_copies(be_ref[0], 0):
            cp.start()

    @pl.when(jnp.logical_and(new_expert, i < nused_ref[0]))
    def _():
        for cp in weight_copies(be_ref[i], slot):
            cp.wait()

        @pl.when(nexte_ref[i] >= 0)
        def _():
            for cp in weight_copies(jnp.maximum(nexte_ref[i], 0), 1 - slot):
                cp.start()

        wg_bf[...] = wbuf[slot, 0].astype(BF16)
        wu_bf[...] = wbuf[slot, 1].astype(BF16)
        wd_bf[...] = wbuf[slot, 2].astype(BF16)

    @pl.when(i < nused_ref[0])
    def _():
        lo, hi = _unpack_bf16_pairs(xs_ref[...])
        x = jnp.concatenate([lo, hi], axis=1)
        for n in range(D_EXPERT // N_CHUNK):
            cols = slice(n * N_CHUNK, (n + 1) * N_CHUNK)
            gt = _dot(x, wg_bf[:, cols]) + bg_ref[0, 0, :, cols]
            up = _dot(x, wu_bf[:, cols]) + bu_ref[0, 0, :, cols]
            gt = jnp.minimum(gt, SWIGLU_LIMIT)
            up = jnp.clip(up, -SWIGLU_LIMIT, SWIGLU_LIMIT)
            act_ref[:, cols] = ((up + 1.0) * gt * jax.nn.sigmoid(SWIGLU_ALPHA * gt)).astype(BF16)
        act = act_ref[...]
        for n in range(HALF_D // N_CHUNK):
            cols = slice(n * N_CHUNK, (n + 1) * N_CHUNK)
            cols_hi = slice(HALF_D + n * N_CHUNK, HALF_D + (n + 1) * N_CHUNK)
            y_lo = _dot(act, wd_bf[:, cols]) + bd_ref[0, 0, :, cols]
            y_hi = _dot(act, wd_bf[:, cols_hi]) + bd_ref[0, 0, :, cols_hi]
            ys_ref[:, cols] = _pack_bf16_pairs(jnp.concatenate([y_lo, y_hi], axis=1))

    @pl.when(i >= nused_ref[0])
    def _():
        ys_ref[...] = jnp.zeros_like(ys_ref)


def moe_experts(block_e, n_used, next_e, parity, xs, w_g, w_u, w_d, b_g, b_u, b_d, layer):
    rows = lambda i, be, nu, ne, pa: (i, 0)
    wspec = pl.BlockSpec(memory_space=pl.ANY)
    bspec = pl.BlockSpec((1, 1, 1, D_EXPERT), lambda i, be, nu, ne, pa: (layer, be[i], 0, 0))
    r4 = lambda b: b.reshape(DEPTH, N_EXPERTS, 1, D_EXPERT)
    return pl.pallas_call(
        functools.partial(_expert_kernel, layer=layer),
        grid_spec=pltpu.PrefetchScalarGridSpec(
            num_scalar_prefetch=4,
            grid=(MOE_BLOCKS,),
            in_specs=[pl.BlockSpec((MOE_ROWS, HALF_D), rows), wspec, wspec, wspec, bspec, bspec, bspec],
            out_specs=pl.BlockSpec((MOE_ROWS, HALF_D), rows),
            scratch_shapes=[pltpu.VMEM((2, 3, D_MODEL, D_EXPERT), F32),
                            pltpu.VMEM((MOE_ROWS, D_EXPERT), BF16)]
                           + [pltpu.VMEM((D_MODEL, D_EXPERT), BF16)] * 3
                           + [pltpu.SemaphoreType.DMA((2,))],
        ),
        out_shape=jax.ShapeDtypeStruct((MOE_R, HALF_D), jnp.uint32),
        compiler_params=_cparams(("arbitrary",)),
        name="moe_experts",
    )(block_e, n_used, next_e, parity, xs, w_g, w_u, w_d, r4(b_g), r4(b_u), r4(b_d))


def _combine_kernel(dest_ref, x_ref, mod_ref, gate_ref, ys_ref, g_ref, b_ref, *rest, split_out):
    out_refs, (buf_ref, sem) = rest[:-2], rest[-2:]
    i = pl.program_id(0)
    last = pl.num_programs(0) - 1

    def gather(step, slot):
        base = step * COMB_TOK
        for t in range(COMB_TOK):
            for k in range(TOP_K):
                row = dest_ref[k * N_TOK + base + t]
                pltpu.make_async_copy(ys_ref.at[pl.ds(row, 1), :], buf_ref.at[slot, k, pl.ds(t, 1), :],
                                      sem.at[slot]).start(priority=k % 2)

    @pl.when(i == 0)
    def _():
        gather(0, 0)

    for slot in range(2):
        @pl.when(jnp.logical_and(i < last, i % 2 == slot))
        def _():
            gather(i + 1, 1 - slot)

    eye = (lax.broadcasted_iota(jnp.int32, (COMB_TOK, COMB_TOK), 0)
           == lax.broadcasted_iota(jnp.int32, (COMB_TOK, COMB_TOK), 1))
    gates = gate_ref[...]
    cols = [jnp.sum(jnp.where(eye, gates[k:k + 1, :], 0.0), axis=1, keepdims=True) for k in range(TOP_K)]
    cur = i % 2
    for k in range(TOP_K):
        pltpu.make_async_copy(ys_ref.at[pl.ds(0, COMB_TOK), :], buf_ref.at[cur, k], sem.at[cur]).wait()
    y_lo = y_hi = None
    for k in range(TOP_K):
        w = buf_ref[cur, k]
        lo = cols[k] * lax.bitcast_convert_type(w << 16, F32)
        hi = cols[k] * lax.bitcast_convert_type(w & jnp.uint32(HI_MASK), F32)
        y_lo = lo if y_lo is None else y_lo + lo
        y_hi = hi if y_hi is None else y_hi + hi
    y = jnp.concatenate([y_lo, y_hi], axis=1)
    z = DN_ALPHA * x_ref[...] + mod_ref[0, 5:6, :] * y
    res = _layer_norm(z, g_ref[...], b_ref[...])
    if split_out:
        is_prompt = i < N_PROMPT // COMB_TOK

        @pl.when(is_prompt)
        def _():
            out_refs[0][...] = res

        @pl.when(jnp.logical_not(is_prompt))
        def _():
            out_refs[1][...] = res
    else:
        out_refs[0][...] = res


def moe_combine(dest_flat, x, mods, gates_t, ys, ln_g, ln_b, split_out=False):
    vec = pl.BlockSpec((1, D_MODEL), lambda i, d: (0, 0))
    npb = N_PROMPT // COMB_TOK
    if split_out:
        out_specs = [pl.BlockSpec((COMB_TOK, D_MODEL), lambda i, d: (jnp.minimum(i, npb - 1), 0)),
                     pl.BlockSpec((COMB_TOK, D_MODEL), lambda i, d: (jnp.maximum(i - npb, 0), 0))]
        out_shape = [jax.ShapeDtypeStruct((N_PROMPT, D_MODEL), F32), jax.ShapeDtypeStruct((N_SAMPLE, D_MODEL), F32)]
    else:
        out_specs = pl.BlockSpec((COMB_TOK, D_MODEL), lambda i, d: (i, 0))
        out_shape = jax.ShapeDtypeStruct((N_TOK, D_MODEL), F32)
    return pl.pallas_call(
        functools.partial(_combine_kernel, split_out=split_out),
        grid_spec=pltpu.PrefetchScalarGridSpec(
            num_scalar_prefetch=1,
            grid=(N_TOK // COMB_TOK,),
            in_specs=[
                pl.BlockSpec((COMB_TOK, D_MODEL), lambda i, d: (i, 0)),
                pl.BlockSpec((1, 6, D_MODEL), lambda i, d: (_cond_of_block(i, COMB_TOK), 0, 0)),
                pl.BlockSpec((TOP_K, COMB_TOK), lambda i, d: (0, i)),
                pl.BlockSpec(memory_space=pl.ANY),
                vec, vec,
            ],
            out_specs=out_specs,
            scratch_shapes=[pltpu.VMEM((2, TOP_K, COMB_TOK, HALF_D), jnp.uint32), pltpu.SemaphoreType.DMA((2,))],
        ),
        out_shape=out_shape,
        compiler_params=_cparams(("arbitrary",)),
        name="moe_combine",
    )(dest_flat, x, mods, gates_t, ys, ln_g.reshape(1, D_MODEL), ln_b.reshape(1, D_MODEL))


def moe_layer(x, mods, layer, router_w, router_b, w_g, b_g, w_u, b_u, w_d, b_d, ln_g, ln_b, split_out=False):
    idx_t, gates_t, rank_t, counts = moe_route(x, mods, router_w[layer].T, router_b[layer])
    counts = counts[:, 0]
    padded = (counts + MOE_ROWS - 1) // MOE_ROWS * MOE_ROWS
    ends = jnp.cumsum(padded)
    base = ends - padded
    n_used = (ends[-1] // MOE_ROWS).astype(jnp.int32)
    block_start = jnp.arange(MOE_BLOCKS, dtype=jnp.int32) * MOE_ROWS
    block_e = jnp.sum(block_start[:, None] >= ends[None, :], axis=1).astype(jnp.int32)
    used = padded > 0
    experts = jnp.arange(N_EXPERTS, dtype=jnp.int32)
    block_e = jnp.minimum(block_e, jnp.max(jnp.where(used, experts, 0)))
    zstart = jnp.where(padded > 0, ends - MOE_ROWS, -1).astype(jnp.int32)
    zstart = jnp.concatenate([zstart, n_used.reshape(1)])
    onehot = idx_t[:, :, None] == jnp.arange(N_EXPERTS, dtype=jnp.int32)[None, None, :]
    dest = rank_t + jnp.sum(jnp.where(onehot, base[None, None, :], 0), axis=-1)
    dest_flat = dest.reshape(-1).astype(jnp.int32)
    xs = moe_scatter(dest_flat, zstart, x, mods)
    later_used = jnp.logical_and(used[None, :], experts[None, :] > experts[:, None])
    next_used = jnp.min(jnp.where(later_used, experts[None, :], N_EXPERTS), axis=1)
    next_used = jnp.where(next_used == N_EXPERTS, -1, next_used).astype(jnp.int32)
    run_parity = ((jnp.cumsum(used.astype(jnp.int32)) - 1) % 2).astype(jnp.int32)
    of_block = block_e[:, None] == experts[None, :]
    block_next = jnp.sum(jnp.where(of_block, next_used[None, :], 0), axis=1).astype(jnp.int32)
    block_parity = jnp.sum(jnp.where(of_block, run_parity[None, :], 0), axis=1).astype(jnp.int32)
    ys = moe_experts(block_e, n_used.reshape(1), block_next, block_parity, xs,
                     w_g, w_u, w_d, b_g, b_u, b_d, layer)
    return moe_combine(dest_flat, x, mods, gates_t, ys, ln_g, ln_b, split_out=split_out)


def kernel(x_prompt, x_sample, c, cache_k, cache_v, state_hgrn, c_ctx, w_ada, b_ada, ln_g, ln_b,
           w_in_even, b_in_even, attn_sink, hgrn_lb, hgrn_norm, w_out_even,
           conv_w_in, conv_b_in, conv_dw, conv_dw_b, conv_ln_g, conv_ln_b, conv_w_out, conv_b_out,
           router_w, router_b, moe_w_gate, moe_b_gate, moe_w_up, moe_b_up, moe_w_down, moe_b_down):
    x = jnp.concatenate([x_prompt.reshape(N_PROMPT, D_MODEL), x_sample.reshape(N_SAMPLE, D_MODEL)], axis=0)
    cond = jnp.concatenate([c_ctx[None, :], c, jnp.zeros((COND_ROWS - N_COND, D_MODEL), F32)], axis=0)
    mods_all = adaln_all(cond, w_ada, b_ada).reshape(DEPTH, COND_ROWS, 6, D_MODEL)
    lb = jax.nn.softmax(hgrn_lb.astype(F32), axis=1)
    lb = jnp.cumsum(lb, axis=1) - lb[:, :1]
    cos, sin = _rope_tables()
    new_k, new_v, new_s = [], [], []
    for layer in range(DEPTH):
        j = layer // 2
        mods = mods_all[layer]
        if layer % 2 == 0:
            q, k, v, qb, ib, ff, fb, go = inproj_even(x, mods, w_in_even[j].astype(BF16), b_in_even[j], cos, sin)
            new_k.append(k[:N_PROMPT].reshape(BATCH, SEQ, N_KV_A, HEAD_DIM))
            new_v.append(v[:N_PROMPT].reshape(BATCH, SEQ, N_KV_A, HEAD_DIM))
            attn_p = attn_context(attn_sink[j], q, k, v)
            attn_s = attn_latent(attn_sink[j], q, k, v,
                                 cache_k[:, j].reshape(DEC_BATCH, PAST_LEN, A_KV),
                                 cache_v[:, j].reshape(DEC_BATCH, PAST_LEN, A_KV))
            hg = (qb, ib, ff, fb, go, lb[0, j], lb[1, j], hgrn_norm[j])
            rec_p, states = hgrn_mixer(*hg, latent=False)
            rec_s, = hgrn_mixer(*hg, latent=True, state_in=state_hgrn, layer_j=j)
            new_s.append(states)
            x = outproj_even(x, mods, attn_p, attn_s, rec_p, rec_s, w_out_even[j].astype(BF16),
                             ln_g[layer, 0], ln_b[layer, 0])
        else:
            u = conv_in(x, mods, conv_w_in[j].astype(BF16), conv_b_in[j])
            x = conv_out(x, mods, u, conv_dw[j], conv_dw_b[j], conv_ln_g[j], conv_ln_b[j],
                         conv_w_out[j].astype(BF16), conv_b_out[j], ln_g[layer, 0], ln_b[layer, 0])
        x = moe_layer(x, mods, layer, router_w, router_b, moe_w_gate, moe_b_gate, moe_w_up, moe_b_up,
                      moe_w_down, moe_b_down, ln_g[layer, 1], ln_b[layer, 1], split_out=layer == DEPTH - 1)
    y_prompt, y_sample = x
    return (y_prompt.reshape(BATCH, SEQ, D_MODEL),
            y_sample.reshape(DEC_BATCH, DEC_SEQ, D_MODEL),
            jnp.stack(new_k, axis=1), jnp.stack(new_v, axis=1), jnp.stack(new_s, axis=1))
```

```python
import functools

import jax
import jax.numpy as jnp
import numpy as np
from jax import lax
from jax.experimental import pallas as pl
from jax.experimental.pallas import tpu as pltpu

D_MODEL = 1024
BATCH = 16
SEQ = 256
DEPTH = 4
DEC_BATCH = 4
DEC_SEQ = 2048
PAST_LEN = 512
GRID_W = 64
N_EVEN = (DEPTH + 1) // 2
N_ODD = DEPTH // 2
HEAD_DIM = 64
N_HEADS_A = 8
N_KV_A = 2
GROUP_A = N_HEADS_A // N_KV_A
WINDOW = 128
ATTN_BLOCK = 128
SCALE_A = HEAD_DIM ** -0.5
ROPE_BASE = 10000.0
ROPE_PAIRS = HEAD_DIM // 4
N_HEADS_B = 4
HGRN_DK = 128
HGRN_DV = 128
CONV_WIDTH = 31
N_EXPERTS = 32
TOP_K = 4
D_EXPERT = D_MODEL
SWIGLU_LIMIT = 7.0
SWIGLU_ALPHA = 1.702
LN_EPS = 1e-5
RMS_EPS = 1e-6
MASK_VALUE = -1e9
LB_FLOOR = 1e-30
DN_ALPHA = (2 * DEPTH) ** 0.25
A_Q = N_HEADS_A * HEAD_DIM
A_KV = N_KV_A * HEAD_DIM
B_QK = N_HEADS_B * HGRN_DK
B_V = N_HEADS_B * HGRN_DV
IN_SIZES = (A_Q, A_KV, A_KV, B_QK, B_V, B_QK, B_QK, B_V)
D_IN_EVEN = sum(IN_SIZES)

N_PROMPT = BATCH * SEQ
N_SAMPLE = DEC_BATCH * DEC_SEQ
N_TOK = N_PROMPT + N_SAMPLE
N_COND = 1 + DEC_BATCH
COND_ROWS = 8

LANES = 128
SUBLANES = 8
VMEM_LIMIT = 56 * 1024 * 1024

ROW_BLOCK = 256
DENSE_ROWS = 512
HG_BLOCK = 128
HG_SUB = 8
MOE_ROWS = 512
MOE_BLOCKS = (N_TOK * TOP_K + N_EXPERTS * (MOE_ROWS - 1)) // MOE_ROWS + 1
MOE_R = MOE_BLOCKS * MOE_ROWS
COMB_TOK = 256

F32 = jnp.float32
BF16 = jnp.bfloat16
HIGHEST = lax.Precision.HIGHEST


def _cond_of_block(i, rows):
    start = i * rows
    return jnp.where(start < N_PROMPT, 0, 1 + (start - N_PROMPT) // DEC_SEQ)


def _cparams(sem):
    return pltpu.CompilerParams(dimension_semantics=sem, vmem_limit_bytes=VMEM_LIMIT)


def _layer_norm(z, g, b):
    mu = jnp.mean(z, axis=-1, keepdims=True)
    zc = z - mu
    var = jnp.mean(zc * zc, axis=-1, keepdims=True)
    return zc * lax.rsqrt(var + LN_EPS) * g + b


def _dot(a, b):
    return jnp.dot(a, b, preferred_element_type=F32)


def _dot_nt(a, b):
    return lax.dot_general(a, b, (((1,), (1,)), ((), ())), preferred_element_type=F32)


ADA_TN = 3072


def _adaln_kernel(cond_ref, w_ref, b_ref, o_ref):
    c = cond_ref[...]
    s = c * jax.nn.sigmoid(c)
    o_ref[0] = jnp.dot(s, w_ref[0], precision=HIGHEST, preferred_element_type=F32) + b_ref[0]


def adaln_all(cond, w_ada, b_ada):
    n_out = 6 * D_MODEL
    return pl.pallas_call(
        _adaln_kernel,
        grid=(DEPTH, n_out // ADA_TN),
        in_specs=[
            pl.BlockSpec((COND_ROWS, D_MODEL), lambda l, n: (0, 0)),
            pl.BlockSpec((1, D_MODEL, ADA_TN), lambda l, n: (l, 0, n)),
            pl.BlockSpec((1, 1, ADA_TN), lambda l, n: (l, 0, n)),
        ],
        out_specs=pl.BlockSpec((1, COND_ROWS, ADA_TN), lambda l, n: (l, 0, n)),
        out_shape=jax.ShapeDtypeStruct((DEPTH, COND_ROWS, n_out), F32),
        compiler_params=_cparams(("parallel", "parallel")),
        name="adaln",
    )(cond, w_ada, b_ada.reshape(DEPTH, 1, n_out))


def _rope_tables():
    t = np.arange(DEC_SEQ)
    d = np.arange(LANES) % HEAD_DIM
    axis = d // (2 * ROPE_PAIRS)
    half = (d // ROPE_PAIRS) % 2
    pair = d % ROPE_PAIRS
    pos = jnp.where(axis[None, :] == 0, (t // GRID_W)[:, None], (t % GRID_W)[:, None]).astype(F32)
    inv_freq = ROPE_BASE ** (-jnp.arange(ROPE_PAIRS, dtype=F32) / ROPE_PAIRS)
    ang = pos * inv_freq[pair][None, :]
    sign = jnp.where(half[None, :] == 0, -1.0, 1.0).astype(F32)
    return jnp.cos(ang), jnp.sin(ang) * sign


def _rope(x, cos, sin_signed):
    lane = lax.broadcasted_iota(jnp.int32, x.shape, 1)
    first_half = (lane // ROPE_PAIRS) % 2 == 0
    partner = jnp.where(first_half, pltpu.roll(x, LANES - ROPE_PAIRS, 1), pltpu.roll(x, ROPE_PAIRS, 1))
    return x * cos + partner * sin_signed


def _token_rows(x, rows):
    npb = N_PROMPT // rows
    spec = lambda fn: pl.BlockSpec((rows, D_MODEL), fn)
    if isinstance(x, tuple):
        return [spec(lambda i: (jnp.minimum(i, npb - 1), 0)), spec(lambda i: (jnp.maximum(i - npb, 0), 0))], list(x)
    return [spec(lambda i: (jnp.minimum(i, npb - 1), 0)), spec(lambda i: (jnp.maximum(i, npb), 0))], [x, x]


def _inproj_kernel(xp_ref, xs_ref, mod_ref, w_ref, b_ref, cos_ref, sin_ref,
                   q_ref, k_ref, v_ref, qb_ref, ib_ref, ff_ref, fb_ref, go_ref):
    i = pl.program_id(0)
    shift = mod_ref[0, 0:1, :]
    scale = mod_ref[0, 1:2, :]
    x = jnp.where(i * DENSE_ROWS >= N_PROMPT, xs_ref[...], xp_ref[...])
    h = (x * (1.0 + scale) + shift).astype(BF16)
    y = _dot(h, w_ref[...]) + b_ref[...]
    offs = np.cumsum((0,) + IN_SIZES)
    q = y[:, offs[0]:offs[1]]
    k = y[:, offs[1]:offs[2]]
    v_ref[...] = y[:, offs[2]:offs[3]]
    qb_ref[...] = y[:, offs[3]:offs[4]]
    ib_ref[...] = y[:, offs[4]:offs[5]]
    ff_ref[...] = y[:, offs[5]:offs[6]]
    fb_ref[...] = y[:, offs[6]:offs[7]]
    go_ref[...] = y[:, offs[7]:offs[8]]
    is_latent = i * DENSE_ROWS >= N_PROMPT

    @pl.when(jnp.logical_not(is_latent))
    def _():
        q_ref[...] = q
        k_ref[...] = k

    @pl.when(is_latent)
    def _():
        cos = cos_ref[...]
        sin = sin_ref[...]
        for c in range(A_Q // LANES):
            q_ref[:, c * LANES:(c + 1) * LANES] = _rope(q[:, c * LANES:(c + 1) * LANES], cos, sin)
        k_ref[...] = _rope(k, cos, sin)


def inproj_even(x, mods, w_bf, b, cos, sin):
    nblk = N_TOK // DENSE_ROWS
    pos_blocks = DEC_SEQ // DENSE_ROWS

    def pos_map(i):
        return (jnp.maximum(i - N_PROMPT // DENSE_ROWS, 0) % pos_blocks, 0)

    row = lambda i: (i, 0)
    widths = (A_Q, A_KV, A_KV, B_QK, B_V, B_QK, B_QK, B_V)
    x_specs, x_args = _token_rows(x, DENSE_ROWS)
    return pl.pallas_call(
        _inproj_kernel,
        grid=(nblk,),
        in_specs=x_specs + [
            pl.BlockSpec((1, 6, D_MODEL), lambda i: (_cond_of_block(i, DENSE_ROWS), 0, 0)),
            pl.BlockSpec((D_MODEL, D_IN_EVEN), lambda i: (0, 0)),
            pl.BlockSpec((1, D_IN_EVEN), lambda i: (0, 0)),
            pl.BlockSpec((DENSE_ROWS, LANES), pos_map),
            pl.BlockSpec((DENSE_ROWS, LANES), pos_map),
        ],
        out_specs=[pl.BlockSpec((DENSE_ROWS, w), row) for w in widths],
        out_shape=[jax.ShapeDtypeStruct((N_TOK, w), F32) for w in widths],
        compiler_params=_cparams(("parallel",)),
        name="inproj_even",
    )(*x_args, mods, w_bf, b.reshape(1, D_IN_EVEN), cos, sin)


def _sink_attend(q, keys, vals, sink, masks):
    scores = []
    for kk, mask in zip(keys, masks):
        s = _dot_nt(q, kk) * SCALE_A
        if mask is not None:
            s = jnp.where(mask, s, MASK_VALUE)
        scores.append(s)
    m = sink
    for s in scores:
        m = jnp.maximum(m, jnp.max(s, axis=-1, keepdims=True))
    denom = jnp.exp(sink - m)
    acc = None
    for s, vv in zip(scores, vals):
        p = jnp.exp(s - m)
        denom = denom + jnp.sum(p, axis=-1, keepdims=True)
        pv = _dot(p.astype(BF16), vv)
        acc = pv if acc is None else acc + pv
    return acc / denom


def _attn_ctx_kernel(sink_ref, q_ref, k_ref, v_ref, o_ref):
    k = k_ref[...].astype(BF16)
    v = v_ref[...].astype(BF16)
    q = q_ref[...].astype(BF16)
    for h in range(N_HEADS_A):
        kv = h // GROUP_A
        qh = q[:, h * HEAD_DIM:(h + 1) * HEAD_DIM]
        kh = k[:, kv * HEAD_DIM:(kv + 1) * HEAD_DIM]
        vh = v[:, kv * HEAD_DIM:(kv + 1) * HEAD_DIM]
        o_ref[:, h * HEAD_DIM:(h + 1) * HEAD_DIM] = _sink_attend(qh, [kh], [vh], sink_ref[h], [None])


def attn_context(sink, q, k, v):
    row = lambda b: (b, 0)
    return pl.pallas_call(
        _attn_ctx_kernel,
        grid=(BATCH,),
        in_specs=[
            pl.BlockSpec(memory_space=pltpu.SMEM),
            pl.BlockSpec((SEQ, A_Q), row),
            pl.BlockSpec((SEQ, A_KV), row),
            pl.BlockSpec((SEQ, A_KV), row),
        ],
        out_specs=pl.BlockSpec((SEQ, A_Q), row),
        out_shape=jax.ShapeDtypeStruct((N_PROMPT, A_Q), F32),
        compiler_params=_cparams(("parallel",)),
        name="attn_context",
    )(sink, q, k, v)


def _attn_lat_kernel(sink_ref, q_ref, kp_ref, kc_ref, kn_ref, vp_ref, vc_ref, vn_ref, ck_ref, cv_ref, o_ref):
    n = pl.program_id(1)
    nb = DEC_SEQ // ATTN_BLOCK
    qi = lax.broadcasted_iota(jnp.int32, (ATTN_BLOCK, ATTN_BLOCK), 0)
    kj = lax.broadcasted_iota(jnp.int32, (ATTN_BLOCK, ATTN_BLOCK), 1)
    mask_prev = jnp.logical_and(kj - qi >= ATTN_BLOCK - WINDOW, n > 0)
    mask_next = jnp.logical_and(kj - qi <= WINDOW - ATTN_BLOCK, n < nb - 1)
    masks = [mask_prev, None, mask_next, None]
    q = q_ref[...].astype(BF16)
    kband = [r[...].astype(BF16) for r in (kp_ref, kc_ref, kn_ref)]
    vband = [r[...].astype(BF16) for r in (vp_ref, vc_ref, vn_ref)]
    ck = ck_ref[0].astype(BF16)
    cv = cv_ref[0].astype(BF16)
    for h in range(N_HEADS_A):
        kv = h // GROUP_A
        sl = slice(kv * HEAD_DIM, (kv + 1) * HEAD_DIM)
        qh = q[:, h * HEAD_DIM:(h + 1) * HEAD_DIM]
        keys = [kb[:, sl] for kb in kband] + [ck[:, sl]]
        vals = [vb[:, sl] for vb in vband] + [cv[:, sl]]
        o_ref[:, h * HEAD_DIM:(h + 1) * HEAD_DIM] = _sink_attend(qh, keys, vals, sink_ref[h], masks)


def attn_latent(sink, q, k, v, cache_k, cache_v):
    nb = DEC_SEQ // ATTN_BLOCK
    base = N_PROMPT // ATTN_BLOCK

    def blk(delta):
        return lambda b, n: (base + b * nb + jnp.clip(n + delta, 0, nb - 1), 0)

    kv_spec = lambda delta: pl.BlockSpec((ATTN_BLOCK, A_KV), blk(delta))
    cache_spec = pl.BlockSpec((1, PAST_LEN, A_KV), lambda b, n: (b, 0, 0))
    return pl.pallas_call(
        _attn_lat_kernel,
        grid=(DEC_BATCH, nb),
        in_specs=[
            pl.BlockSpec(memory_space=pltpu.SMEM),
            pl.BlockSpec((ATTN_BLOCK, A_Q), blk(0)),
            kv_spec(-1), kv_spec(0), kv_spec(1),
            kv_spec(-1), kv_spec(0), kv_spec(1),
            cache_spec, cache_spec,
        ],
        out_specs=pl.BlockSpec((ATTN_BLOCK, A_Q), lambda b, n: (b * nb + n, 0)),
        out_shape=jax.ShapeDtypeStruct((N_SAMPLE, A_Q), F32),
        compiler_params=_cparams(("parallel", "parallel")),
        name="attn_latent",
    )(sink, q, k, k, k, v, v, v, cache_k, cache_v)


def _hgrn_gate(f_pre, lb):
    f = jnp.maximum(lb, LB_FLOOR) + (1.0 - lb) * jax.nn.sigmoid(f_pre)
    return 1.0 - f, jnp.log(f)


def _scan_rows(x, reverse):
    n = x.shape[0]
    row = lax.broadcasted_iota(jnp.int32, x.shape, 0)
    sh = 1
    while sh < n:
        if reverse:
            x = x + jnp.where(row < n - sh, pltpu.roll(x, n - sh, 0), 0.0)
        else:
            x = x + jnp.where(row >= sh, pltpu.roll(x, sh, 0), 0.0)
        sh *= 2
    return x


def _chunk_bcast(x, c, pick, shift):
    nc = HG_BLOCK // c
    rows = x.reshape(nc, c, x.shape[-1])[:, pick:pick + 1, :]
    zero = jnp.zeros((1, 1, x.shape[-1]), x.dtype)
    if shift == -1:
        rows = jnp.concatenate([zero, rows[:-1]], axis=0)
    elif shift == 1:
        rows = jnp.concatenate([rows[1:], zero], axis=0)
    return jnp.broadcast_to(rows, (nc, c, x.shape[-1])).reshape(HG_BLOCK, x.shape[-1])


def _hgrn_block(q, kk, v, log_f, st, reverse):
    cum = _scan_rows(log_f, reverse)
    tot = cum[0:1, :] if reverse else cum[HG_BLOCK - 1:HG_BLOCK, :]
    o = _dot_nt((q * jnp.exp(cum)).astype(BF16), st.astype(BF16))
    kd = (kk * jnp.exp(tot - cum)).astype(BF16)
    v_bf = v.astype(BF16)
    u_t = lax.dot_general(v_bf, kd, (((0,), (0,)), ((), ())), preferred_element_type=F32)
    st_new = st * jnp.exp(tot) + u_t
    ti = lax.broadcasted_iota(jnp.int32, (HG_BLOCK, HG_BLOCK), 0)
    si = lax.broadcasted_iota(jnp.int32, (HG_BLOCK, HG_BLOCK), 1)
    a = jnp.zeros((HG_BLOCK, HG_BLOCK), F32)
    c = HG_SUB
    while c < HG_BLOCK:
        if reverse:
            bound_t = _chunk_bcast(cum, c, 0, 1)
            bound_s = _chunk_bcast(cum, c, 0, 0)
            mask = jnp.logical_and((ti // c) % 2 == 0, si // c == ti // c + 1)
        else:
            bound_t = _chunk_bcast(cum, c, c - 1, -1)
            bound_s = _chunk_bcast(cum, c, c - 1, 0)
            mask = jnp.logical_and((ti // c) % 2 == 1, si // c == ti // c - 1)
        qc = (q * jnp.exp(cum - bound_t)).astype(BF16)
        kc = (kk * jnp.exp(bound_s - cum)).astype(BF16)
        a = a + jnp.where(mask, _dot_nt(qc, kc), 0.0)
        c *= 2
    o = o + _dot(a.astype(BF16), v_bf)
    nsub = HG_BLOCK // HG_SUB
    q3 = q.reshape(nsub, HG_SUB, HGRN_DK)
    k3 = kk.reshape(nsub, HG_SUB, HGRN_DK)
    v3 = v.reshape(nsub, HG_SUB, HGRN_DV)
    cum3 = cum.reshape(nsub, HG_SUB, HGRN_DK)
    t_off = lax.broadcasted_iota(jnp.int32, (1, HG_SUB, 1), 1)
    od = jnp.zeros((nsub, HG_SUB, HGRN_DV), F32)
    for s in range(HG_SUB):
        causal = (t_off <= s) if reverse else (t_off >= s)
        decay = jnp.exp(jnp.where(causal, cum3 - cum3[:, s:s + 1, :], MASK_VALUE))
        score = jnp.sum(q3 * decay * k3[:, s:s + 1, :], axis=-1, keepdims=True)
        od = od + score * v3[:, s:s + 1, :]
    return o + od.reshape(HG_BLOCK, HGRN_DV), st_new


def _hgrn_kernel(*refs, n_blocks, has_state_in, has_state_out):
    refs = list(refs)
    qb_ref, ib_ref, ff_ref, fb_ref, go_ref, lbf_ref, lbb_ref, nw_ref = refs[:8]
    refs = refs[8:]
    s0_ref = refs.pop(0) if has_state_in else None
    o_ref = refs.pop(0)
    so_ref = refs.pop(0) if has_state_out else None
    acc_ref = refs.pop(0)

    def run(reverse):
        f_ref, lb_ref = (fb_ref, lbb_ref) if reverse else (ff_ref, lbf_ref)
        lb = lb_ref[...]
        if has_state_in:
            st0 = s0_ref[0, 0, 1 if reverse else 0, 0].T
        else:
            st0 = jnp.zeros((HGRN_DV, HGRN_DK), F32)

        def body(it, st):
            blk = (n_blocks - 1 - it) if reverse else it
            rows = pl.ds(pl.multiple_of(blk * HG_BLOCK, HG_BLOCK), HG_BLOCK)
            qpre = qb_ref[rows, :]
            q = qpre * jax.nn.sigmoid(qpre)
            kk, log_f = _hgrn_gate(f_ref[rows, :], lb)
            o, st = _hgrn_block(q, kk, ib_ref[rows, :], log_f, st, reverse)
            if reverse:
                acc_ref[rows, :] = acc_ref[rows, :] + o
            else:
                acc_ref[rows, :] = o
            return st

        return lax.fori_loop(0, n_blocks, body, st0)

    st_f = run(False)
    st_b = run(True)
    if has_state_out:
        so_ref[0, 0, 0] = st_f.T
        so_ref[0, 1, 0] = st_b.T
    o = acc_ref[...]
    o = o * lax.rsqrt(jnp.mean(o * o, axis=-1, keepdims=True) + RMS_EPS) * nw_ref[...]
    g = go_ref[...]
    o_ref[...] = o * (g * jax.nn.sigmoid(g))


def hgrn_mixer(qb, ib, ff, fb, go, lb_f, lb_b, norm_w, *, latent, state_in=None, layer_j=0):
    t_len = DEC_SEQ if latent else SEQ
    n_seq = DEC_BATCH if latent else BATCH
    row0 = N_PROMPT // t_len if latent else 0
    tok = pl.BlockSpec((t_len, HGRN_DK), lambda b, h: (row0 + b, h))
    vec = pl.BlockSpec((1, HGRN_DK), lambda b, h: (0, h))
    in_specs = [tok] * 5 + [vec] * 3
    args = [qb, ib, ff, fb, go, lb_f.reshape(1, B_QK), lb_b.reshape(1, B_QK), norm_w.reshape(1, B_V)]
    if latent:
        in_specs.append(pl.BlockSpec((1, 1, 2, 1, HGRN_DK, HGRN_DV), lambda b, h: (b, layer_j, 0, h, 0, 0)))
        args.append(state_in)
    out_specs = [pl.BlockSpec((t_len, HGRN_DV), lambda b, h: (b, h))]
    out_shape = [jax.ShapeDtypeStruct((n_seq * t_len, B_V), F32)]
    if not latent:
        out_specs.append(pl.BlockSpec((1, 2, 1, HGRN_DK, HGRN_DV), lambda b, h: (b, 0, h, 0, 0)))
        out_shape.append(jax.ShapeDtypeStruct((BATCH, 2, N_HEADS_B, HGRN_DK, HGRN_DV), F32))
    kern = functools.partial(_hgrn_kernel, n_blocks=t_len // HG_BLOCK, has_state_in=latent, has_state_out=not latent)
    return pl.pallas_call(
        kern,
        grid=(n_seq, N_HEADS_B),
        in_specs=in_specs,
        out_specs=out_specs,
        out_shape=out_shape,
        scratch_shapes=[pltpu.VMEM((t_len, HGRN_DV), F32)],
        compiler_params=_cparams(("parallel", "parallel")),
        name="hgrn_latent" if latent else "hgrn_context",
    )(*args)


def _outproj_kernel(xp_ref, xs_ref, mod_ref, ap_ref, as_ref, rp_ref, rs_ref, w_ref, g_ref, b_ref, o_ref):
    is_latent = pl.program_id(0) * DENSE_ROWS >= N_PROMPT
    x = jnp.where(is_latent, xs_ref[...], xp_ref[...])
    attn = jnp.where(is_latent, as_ref[...], ap_ref[...]).astype(BF16)
    rec = jnp.where(is_latent, rs_ref[...], rp_ref[...]).astype(BF16)
    y = _dot(attn, w_ref[0:A_Q, :]) + _dot(rec, w_ref[A_Q:A_Q + B_V, :])
    z = DN_ALPHA * x + mod_ref[0, 2:3, :] * y
    o_ref[...] = _layer_norm(z, g_ref[...], b_ref[...])


def outproj_even(x, mods, attn_p, attn_s, rec_p, rec_s, w_bf, ln_g, ln_b):
    nblk = N_TOK // DENSE_ROWS
    npb = N_PROMPT // DENSE_ROWS
    row = lambda i: (i, 0)
    prow = lambda i: (jnp.minimum(i, npb - 1), 0)
    srow = lambda i: (jnp.maximum(i - npb, 0), 0)
    vec = pl.BlockSpec((1, D_MODEL), lambda i: (0, 0))
    x_specs, x_args = _token_rows(x, DENSE_ROWS)
    return pl.pallas_call(
        _outproj_kernel,
        grid=(nblk,),
        in_specs=x_specs + [
            pl.BlockSpec((1, 6, D_MODEL), lambda i: (_cond_of_block(i, DENSE_ROWS), 0, 0)),
            pl.BlockSpec((DENSE_ROWS, A_Q), prow),
            pl.BlockSpec((DENSE_ROWS, A_Q), srow),
            pl.BlockSpec((DENSE_ROWS, B_V), prow),
            pl.BlockSpec((DENSE_ROWS, B_V), srow),
            pl.BlockSpec((A_Q + B_V, D_MODEL), lambda i: (0, 0)),
            vec, vec,
        ],
        out_specs=pl.BlockSpec((DENSE_ROWS, D_MODEL), row),
        out_shape=jax.ShapeDtypeStruct((N_TOK, D_MODEL), F32),
        compiler_params=_cparams(("parallel",)),
        name="outproj_even",
    )(*x_args, mods, attn_p, attn_s, rec_p, rec_s, w_bf, ln_g.reshape(1, D_MODEL), ln_b.reshape(1, D_MODEL))


def _conv_in_kernel(x_ref, mod_ref, w_ref, b_ref, u_ref):
    h = (x_ref[...] * (1.0 + mod_ref[0, 1:2, :]) + mod_ref[0, 0:1, :]).astype(BF16)
    a = _dot(h, w_ref[:, 0:D_MODEL]) + b_ref[:, 0:D_MODEL]
    gt = _dot(h, w_ref[:, D_MODEL:2 * D_MODEL]) + b_ref[:, D_MODEL:2 * D_MODEL]
    u_ref[...] = a * jax.nn.sigmoid(gt)


def conv_in(x, mods, w_bf, b):
    row = lambda i: (i, 0)
    return pl.pallas_call(
        _conv_in_kernel,
        grid=(N_TOK // DENSE_ROWS,),
        in_specs=[
            pl.BlockSpec((DENSE_ROWS, D_MODEL), row),
            pl.BlockSpec((1, 6, D_MODEL), lambda i: (_cond_of_block(i, DENSE_ROWS), 0, 0)),
            pl.BlockSpec((D_MODEL, 2 * D_MODEL), lambda i: (0, 0)),
            pl.BlockSpec((1, 2 * D_MODEL), lambda i: (0, 0)),
        ],
        out_specs=pl.BlockSpec((DENSE_ROWS, D_MODEL), row),
        out_shape=jax.ShapeDtypeStruct((N_TOK, D_MODEL), F32),
        compiler_params=_cparams(("parallel",)),
        name="conv_in",
    )(x, mods, w_bf, b.reshape(1, 2 * D_MODEL))


CONV_HALO = 16
CONV_LANES = 256


CONV_SHIFT_ROWS = ROW_BLOCK + 2 * CONV_HALO - SUBLANES
CONV_ROW_CHUNK = 128


def _conv_out_kernel(x_ref, mod_ref, up_ref, uc_ref, un_ref, dw_ref, dwb_ref, cg_ref, cb_ref,
                     w_ref, b_ref, g_ref, bb_ref, o_ref, pad_ref, acc_ref, sh_ref):
    i = pl.program_id(0)
    blocks_per_seq = DEC_SEQ // ROW_BLOCK
    j = i - N_PROMPT // ROW_BLOCK
    is_latent = j >= 0
    has_prev = jnp.logical_and(is_latent, j % blocks_per_seq != 0)
    has_next = jnp.logical_and(is_latent, j % blocks_per_seq != blocks_per_seq - 1)
    pad_ref[0:CONV_HALO, :] = jnp.where(has_prev, up_ref[...], 0.0)
    pad_ref[CONV_HALO:CONV_HALO + ROW_BLOCK, :] = uc_ref[...]
    pad_ref[CONV_HALO + ROW_BLOCK:, :] = jnp.where(has_next, un_ref[...], 0.0)
    first = CONV_HALO - CONV_WIDTH // 2
    for c in range(D_MODEL // CONV_LANES):
        lanes = slice(c * CONV_LANES, (c + 1) * CONV_LANES)
        for s in range(SUBLANES):
            sh_ref[s] = pad_ref[s:s + CONV_SHIFT_ROWS, lanes]
        for r0 in range(0, ROW_BLOCK, CONV_ROW_CHUNK):
            acc = jnp.zeros((CONV_ROW_CHUNK, CONV_LANES), F32) + dwb_ref[:, lanes]
            for tap in range(CONV_WIDTH):
                whole, s = divmod(first + tap, SUBLANES)
                rows = slice(r0 + whole * SUBLANES, r0 + whole * SUBLANES + CONV_ROW_CHUNK)
                acc = acc + sh_ref[s, rows, :] * dw_ref[tap:tap + 1, lanes]
            acc_ref[r0:r0 + CONV_ROW_CHUNK, lanes] = acc
    u = _layer_norm(acc_ref[...], cg_ref[...], cb_ref[...])
    u = (u * jax.nn.sigmoid(u)).astype(BF16)
    y = _dot(u, w_ref[...]) + b_ref[...]
    z = DN_ALPHA * x_ref[...] + mod_ref[0, 2:3, :] * y
    o_ref[...] = _layer_norm(z, g_ref[...], bb_ref[...])


def conv_out(x, mods, u, dw, dw_b, cln_g, cln_b, w_bf, b_out, ln_g, ln_b):
    nblk = N_TOK // ROW_BLOCK
    ratio = ROW_BLOCK // CONV_HALO
    nhalo = N_TOK // CONV_HALO
    row = lambda i: (i, 0)
    vec = pl.BlockSpec((1, D_MODEL), lambda i: (0, 0))
    r1 = lambda a: a.reshape(1, D_MODEL)
    return pl.pallas_call(
        _conv_out_kernel,
        grid=(nblk,),
        in_specs=[
            pl.BlockSpec((ROW_BLOCK, D_MODEL), row),
            pl.BlockSpec((1, 6, D_MODEL), lambda i: (_cond_of_block(i, ROW_BLOCK), 0, 0)),
            pl.BlockSpec((CONV_HALO, D_MODEL), lambda i: (jnp.maximum(i * ratio - 1, 0), 0)),
            pl.BlockSpec((ROW_BLOCK, D_MODEL), row),
            pl.BlockSpec((CONV_HALO, D_MODEL), lambda i: (jnp.minimum((i + 1) * ratio, nhalo - 1), 0)),
            pl.BlockSpec((CONV_WIDTH, D_MODEL), lambda i: (0, 0)),
            vec, vec, vec,
            pl.BlockSpec((D_MODEL, D_MODEL), lambda i: (0, 0)),
            vec, vec, vec,
        ],
        out_specs=pl.BlockSpec((ROW_BLOCK, D_MODEL), row),
        out_shape=jax.ShapeDtypeStruct((N_TOK, D_MODEL), F32),
        scratch_shapes=[pltpu.VMEM((ROW_BLOCK + 2 * CONV_HALO, D_MODEL), F32),
                        pltpu.VMEM((ROW_BLOCK, D_MODEL), F32),
                        pltpu.VMEM((SUBLANES, CONV_SHIFT_ROWS, CONV_LANES), F32)],
        compiler_params=_cparams(("parallel",)),
        name="conv_out",
    )(x, mods, u, u, u, dw, r1(dw_b), r1(cln_g), r1(cln_b), w_bf, r1(b_out), r1(ln_g), r1(ln_b))


def _modulate2(x_ref, mod_ref):
    return x_ref[...] * (1.0 + mod_ref[0, 4:5, :]) + mod_ref[0, 3:4, :]


def _route_kernel(x_ref, mod_ref, wr_ref, br_ref, idx_ref, gate_ref, rank_ref, cnt_ref, carry_ref):
    i = pl.program_id(0)

    @pl.when(i == 0)
    def _():
        carry_ref[...] = jnp.zeros_like(carry_ref)

    h = _modulate2(x_ref, mod_ref)
    h_hi = h.astype(BF16)
    h_lo = (h - h_hi.astype(F32)).astype(BF16)
    w = wr_ref[...]
    w_hi = w.astype(BF16)
    w_lo = (w - w_hi.astype(F32)).astype(BF16)
    logits = _dot_nt(w_hi, h_hi) + (_dot_nt(w_hi, h_lo) + _dot_nt(w_lo, h_hi)) + br_ref[...]
    eidx = lax.broadcasted_iota(jnp.int32, logits.shape, 0)
    vals = logits
    sels, tops = [], []
    for k in range(TOP_K):
        m = jnp.max(vals, axis=0, keepdims=True)
        idx = jnp.min(jnp.where(vals == m, eidx, N_EXPERTS), axis=0, keepdims=True)
        sel = eidx == idx
        idx_ref[k:k + 1, :] = idx
        sels.append(sel)
        tops.append(m)
        vals = jnp.where(sel, -jnp.inf, vals)
    exps = [jnp.exp(t - tops[0]) for t in tops]
    total = exps[0] + exps[1] + exps[2] + exps[3]
    for k in range(TOP_K):
        gate_ref[k:k + 1, :] = exps[k] / total
    onehot = jnp.zeros(logits.shape, F32)
    for sel in sels:
        onehot = onehot + sel.astype(F32)
    ta = lax.broadcasted_iota(jnp.int32, (ROW_BLOCK, ROW_BLOCK), 0)
    tb = lax.broadcasted_iota(jnp.int32, (ROW_BLOCK, ROW_BLOCK), 1)
    before = _dot(onehot.astype(BF16), (ta < tb).astype(BF16)) + carry_ref[:, 0:1]
    for k in range(TOP_K):
        rank = jnp.sum(jnp.where(sels[k], before, 0.0), axis=0, keepdims=True)
        rank_ref[k:k + 1, :] = rank.astype(jnp.int32)
    carry = carry_ref[...] + jnp.sum(onehot, axis=1, keepdims=True)
    carry_ref[...] = carry
    cnt_ref[...] = carry.astype(jnp.int32)


def moe_route(x, mods, wr_t, b_r):
    tok = pl.BlockSpec((TOP_K, ROW_BLOCK), lambda i: (0, i))
    return pl.pallas_call(
        _route_kernel,
        grid=(N_TOK // ROW_BLOCK,),
        in_specs=[
            pl.BlockSpec((ROW_BLOCK, D_MODEL), lambda i: (i, 0)),
            pl.BlockSpec((1, 6, D_MODEL), lambda i: (_cond_of_block(i, ROW_BLOCK), 0, 0)),
            pl.BlockSpec((N_EXPERTS, D_MODEL), lambda i: (0, 0)),
            pl.BlockSpec((N_EXPERTS, 1), lambda i: (0, 0)),
        ],
        out_specs=[tok, tok, tok, pl.BlockSpec((N_EXPERTS, LANES), lambda i: (0, 0))],
        out_shape=[jax.ShapeDtypeStruct((TOP_K, N_TOK), jnp.int32),
                   jax.ShapeDtypeStruct((TOP_K, N_TOK), F32),
                   jax.ShapeDtypeStruct((TOP_K, N_TOK), jnp.int32),
                   jax.ShapeDtypeStruct((N_EXPERTS, LANES), jnp.int32)],
        scratch_shapes=[pltpu.VMEM((N_EXPERTS, LANES), F32)],
        compiler_params=_cparams(("arbitrary",)),
        name="moe_route",
    )(x, mods, wr_t, b_r.reshape(N_EXPERTS, 1))


HALF_D = D_MODEL // 2
HI_MASK = 0xFFFF0000


def _pack_bf16_pairs(h):
    half = h.shape[1] // 2
    lo = lax.bitcast_convert_type(h[:, :half].astype(BF16).astype(F32), jnp.uint32)
    hi = lax.bitcast_convert_type(h[:, half:].astype(BF16).astype(F32), jnp.uint32)
    return (lo >> 16) | (hi & jnp.uint32(HI_MASK))


def _unpack_bf16_pairs(w):
    lo = lax.bitcast_convert_type(w << 16, F32).astype(BF16)
    hi = lax.bitcast_convert_type(w & jnp.uint32(HI_MASK), F32).astype(BF16)
    return lo, hi


def _scatter_kernel(dest_ref, zstart_ref, x_ref, mod_ref, xs_ref, pk_ref, zero_ref, sem, zsem):
    i = pl.program_id(0)

    def zero_copy(start):
        start = pl.multiple_of(start, MOE_ROWS)
        return pltpu.make_async_copy(zero_ref, xs_ref.at[pl.ds(start, MOE_ROWS), :], zsem)

    def zero_blocks(fn):
        for e in range(N_EXPERTS):
            @pl.when(zstart_ref[e] >= 0)
            def _():
                fn(zero_copy(jnp.maximum(zstart_ref[e], 0)))

        def unused(b, carry):
            fn(zero_copy(b * MOE_ROWS))
            return carry

        lax.fori_loop(zstart_ref[N_EXPERTS], MOE_BLOCKS, unused, 0)

    @pl.when(i == 0)
    def _():
        zero_ref[...] = jnp.zeros_like(zero_ref)
        zero_blocks(lambda cp: cp.start())
        zero_blocks(lambda cp: cp.wait())

    base = i * ROW_BLOCK
    last = pl.num_programs(0) - 1

    def drain(slot):
        for k in range(TOP_K):
            pltpu.make_async_copy(pk_ref.at[slot], xs_ref.at[pl.ds(0, ROW_BLOCK), :], sem.at[slot]).wait()

    for slot in range(2):
        @pl.when(i % 2 == slot)
        def _():
            @pl.when(i >= 2)
            def _():
                drain(slot)

            pk_ref[slot] = _pack_bf16_pairs(_modulate2(x_ref, mod_ref))
            for t in range(ROW_BLOCK):
                for k in range(TOP_K):
                    row = dest_ref[k * N_TOK + base + t]
                    pltpu.make_async_copy(pk_ref.at[slot, pl.ds(t, 1), :], xs_ref.at[pl.ds(row, 1), :],
                                          sem.at[slot]).start(priority=k % 2)

            @pl.when(i == last)
            def _():
                drain(slot)

                @pl.when(i >= 1)
                def _():
                    drain(1 - slot)


def moe_scatter(dest_flat, zstart, x, mods):
    return pl.pallas_call(
        _scatter_kernel,
        grid_spec=pltpu.PrefetchScalarGridSpec(
            num_scalar_prefetch=2,
            grid=(N_TOK // ROW_BLOCK,),
            in_specs=[
                pl.BlockSpec((ROW_BLOCK, D_MODEL), lambda i, d, z: (i, 0)),
                pl.BlockSpec((1, 6, D_MODEL), lambda i, d, z: (_cond_of_block(i, ROW_BLOCK), 0, 0)),
            ],
            out_specs=pl.BlockSpec(memory_space=pl.ANY),
            scratch_shapes=[pltpu.VMEM((2, ROW_BLOCK, HALF_D), jnp.uint32),
                            pltpu.VMEM((MOE_ROWS, HALF_D), jnp.uint32),
                            pltpu.SemaphoreType.DMA((2,)), pltpu.SemaphoreType.DMA],
        ),
        out_shape=jax.ShapeDtypeStruct((MOE_R, HALF_D), jnp.uint32),
        compiler_params=_cparams(("arbitrary",)),
        name="moe_scatter",
    )(dest_flat, zstart, x, mods)


N_CHUNK = 256


def _expert_kernel(be_ref, nused_ref, nexte_ref, par_ref, xs_ref, wg_hbm, wu_hbm, wd_hbm, bg_ref, bu_ref, bd_ref,
                   ys_ref, wbuf, act_ref, wg_bf, wu_bf, wd_bf, wsem, *, layer):
    i = pl.program_id(0)
    new_expert = jnp.logical_or(i == 0, be_ref[i] != be_ref[jnp.maximum(i - 1, 0)])
    slot = par_ref[i]

    def weight_copies(e, s):
        return [pltpu.make_async_copy(w.at[layer, e], wbuf.at[s, j], wsem.at[s])
                for j, w in enumerate((wg_hbm, wu_hbm, wd_hbm))]

    @pl.when(i == 0)
    def _():
        for cp in weight_copies(be_ref[0], 0):
            cp.start()

    @pl.when(jnp.logical_and(new_expert, i < nused_ref[0]))
    def _():
        for cp in weight_copies(be_ref[i], slot):
            cp.wait()

        @pl.when(nexte_ref[i] >= 0)
        def _():
            for cp in weight_copies(jnp.maximum(nexte_ref[i], 0), 1 - slot):
                cp.start()

        wg_bf[...] = wbuf[slot, 0].astype(BF16)
        wu_bf[...] = wbuf[slot, 1].astype(BF16)
        wd_bf[...] = wbuf[slot, 2].astype(BF16)

    @pl.when(i < nused_ref[0])
    def _():
        lo, hi = _unpack_bf16_pairs(xs_ref[...])
        x = jnp.concatenate([lo, hi], axis=1)
        for n in range(D_EXPERT // N_CHUNK):
            cols = slice(n * N_CHUNK, (n + 1) * N_CHUNK)
            gt = _dot(x, wg_bf[:, cols]) + bg_ref[0, 0, :, cols]
            up = _dot(x, wu_bf[:, cols]) + bu_ref[0, 0, :, cols]
            gt = jnp.minimum(gt, SWIGLU_LIMIT)
            up = jnp.clip(up, -SWIGLU_LIMIT, SWIGLU_LIMIT)
            act_ref[:, cols] = ((up + 1.0) * gt * jax.nn.sigmoid(SWIGLU_ALPHA * gt)).astype(BF16)
        act = act_ref[...]
        for n in range(HALF_D // N_CHUNK):
            cols = slice(n * N_CHUNK, (n + 1) * N_CHUNK)
            cols_hi = slice(HALF_D + n * N_CHUNK, HALF_D + (n + 1) * N_CHUNK)
            y_lo = _dot(act, wd_bf[:, cols]) + bd_ref[0, 0, :, cols]
            y_hi = _dot(act, wd_bf[:, cols_hi]) + bd_ref[0, 0, :, cols_hi]
            ys_ref[:, cols] = _pack_bf16_pairs(jnp.concatenate([y_lo, y_hi], axis=1))

    @pl.when(i >= nused_ref[0])
    def _():
        ys_ref[...] = jnp.zeros_like(ys_ref)


def moe_experts(block_e, n_used, next_e, parity, xs, w_g, w_u, w_d, b_g, b_u, b_d, layer):
    rows = lambda i, be, nu, ne, pa: (i, 0)
    wspec = pl.BlockSpec(memory_space=pl.ANY)
    bspec = pl.BlockSpec((1, 1, 1, D_EXPERT), lambda i, be, nu, ne, pa: (layer, be[i], 0, 0))
    r4 = lambda b: b.reshape(DEPTH, N_EXPERTS, 1, D_EXPERT)
    return pl.pallas_call(
        functools.partial(_expert_kernel, layer=layer),
        grid_spec=pltpu.PrefetchScalarGridSpec(
            num_scalar_prefetch=4,
            grid=(MOE_BLOCKS,),
            in_specs=[pl.BlockSpec((MOE_ROWS, HALF_D), rows), wspec, wspec, wspec, bspec, bspec, bspec],
            out_specs=pl.BlockSpec((MOE_ROWS, HALF_D), rows),
            scratch_shapes=[pltpu.VMEM((2, 3, D_MODEL, D_EXPERT), F32),
                            pltpu.VMEM((MOE_ROWS, D_EXPERT), BF16)]
                           + [pltpu.VMEM((D_MODEL, D_EXPERT), BF16)] * 3
                           + [pltpu.SemaphoreType.DMA((2,))],
        ),
        out_shape=jax.ShapeDtypeStruct((MOE_R, HALF_D), jnp.uint32),
        compiler_params=_cparams(("arbitrary",)),
        name="moe_experts",
    )(block_e, n_used, next_e, parity, xs, w_g, w_u, w_d, r4(b_g), r4(b_u), r4(b_d))


def _combine_kernel(dest_ref, x_ref, mod_ref, gate_ref, ys_ref, g_ref, b_ref, *rest, split_out):
    out_refs, (buf_ref, sem) = rest[:-2], rest[-2:]
    i = pl.program_id(0)
    last = pl.num_programs(0) - 1

    def gather(step, slot):
        base = step * COMB_TOK
        for t in range(COMB_TOK):
            for k in range(TOP_K):
                row = dest_ref[k * N_TOK + base + t]
                pltpu.make_async_copy(ys_ref.at[pl.ds(row, 1), :], buf_ref.at[slot, k, pl.ds(t, 1), :],
                                      sem.at[slot]).start(priority=k % 2)

    @pl.when(i == 0)
    def _():
        gather(0, 0)

    for slot in range(2):
        @pl.when(jnp.logical_and(i < last, i % 2 == slot))
        def _():
            gather(i + 1, 1 - slot)

    eye = (lax.broadcasted_iota(jnp.int32, (COMB_TOK, COMB_TOK), 0)
           == lax.broadcasted_iota(jnp.int32, (COMB_TOK, COMB_TOK), 1))
    gates = gate_ref[...]
    cols = [jnp.sum(jnp.where(eye, gates[k:k + 1, :], 0.0), axis=1, keepdims=True) for k in range(TOP_K)]
    cur = i % 2
    for k in range(TOP_K):
        pltpu.make_async_copy(ys_ref.at[pl.ds(0, COMB_TOK), :], buf_ref.at[cur, k], sem.at[cur]).wait()
    y_lo = y_hi = None
    for k in range(TOP_K):
        w = buf_ref[cur, k]
        lo = cols[k] * lax.bitcast_convert_type(w << 16, F32)
        hi = cols[k] * lax.bitcast_convert_type(w & jnp.uint32(HI_MASK), F32)
        y_lo = lo if y_lo is None else y_lo + lo
        y_hi = hi if y_hi is None else y_hi + hi
    y = jnp.concatenate([y_lo, y_hi], axis=1)
    z = DN_ALPHA * x_ref[...] + mod_ref[0, 5:6, :] * y
    res = _layer_norm(z, g_ref[...], b_ref[...])
    if split_out:
        is_prompt = i < N_PROMPT // COMB_TOK

        @pl.when(is_prompt)
        def _():
            out_refs[0][...] = res

        @pl.when(jnp.logical_not(is_prompt))
        def _():
            out_refs[1][...] = res
    else:
        out_refs[0][...] = res


def moe_combine(dest_flat, x, mods, gates_t, ys, ln_g, ln_b, split_out=False):
    vec = pl.BlockSpec((1, D_MODEL), lambda i, d: (0, 0))
    npb = N_PROMPT // COMB_TOK
    if split_out:
        out_specs = [pl.BlockSpec((COMB_TOK, D_MODEL), lambda i, d: (jnp.minimum(i, npb - 1), 0)),
                     pl.BlockSpec((COMB_TOK, D_MODEL), lambda i, d: (jnp.maximum(i - npb, 0), 0))]
        out_shape = [jax.ShapeDtypeStruct((N_PROMPT, D_MODEL), F32), jax.ShapeDtypeStruct((N_SAMPLE, D_MODEL), F32)]
    else:
        out_specs = pl.BlockSpec((COMB_TOK, D_MODEL), lambda i, d: (i, 0))
        out_shape = jax.ShapeDtypeStruct((N_TOK, D_MODEL), F32)
    return pl.pallas_call(
        functools.partial(_combine_kernel, split_out=split_out),
        grid_spec=pltpu.PrefetchScalarGridSpec(
            num_scalar_prefetch=1,
            grid=(N_TOK // COMB_TOK,),
            in_specs=[
                pl.BlockSpec((COMB_TOK, D_MODEL), lambda i, d: (i, 0)),
                pl.BlockSpec((1, 6, D_MODEL), lambda i, d: (_cond_of_block(i, COMB_TOK), 0, 0)),
                pl.BlockSpec((TOP_K, COMB_TOK), lambda i, d: (0, i)),
                pl.BlockSpec(memory_space=pl.ANY),
                vec, vec,
            ],
            out_specs=out_specs,
            scratch_shapes=[pltpu.VMEM((2, TOP_K, COMB_TOK, HALF_D), jnp.uint32), pltpu.SemaphoreType.DMA((2,))],
        ),
        out_shape=out_shape,
        compiler_params=_cparams(("arbitrary",)),
        name="moe_combine",
    )(dest_flat, x, mods, gates_t, ys, ln_g.reshape(1, D_MODEL), ln_b.reshape(1, D_MODEL))


def moe_layer(x, mods, layer, router_w, router_b, w_g, b_g, w_u, b_u, w_d, b_d, ln_g, ln_b, split_out=False):
    idx_t, gates_t, rank_t, counts = moe_route(x, mods, router_w[layer].T, router_b[layer])
    counts = counts[:, 0]
    padded = (counts + MOE_ROWS - 1) // MOE_ROWS * MOE_ROWS
    ends = jnp.cumsum(padded)
    base = ends - padded
    n_used = (ends[-1] // MOE_ROWS).astype(jnp.int32)
    block_start = jnp.arange(MOE_BLOCKS, dtype=jnp.int32) * MOE_ROWS
    block_e = jnp.sum(block_start[:, None] >= ends[None, :], axis=1).astype(jnp.int32)
    used = padded > 0
    experts = jnp.arange(N_EXPERTS, dtype=jnp.int32)
    block_e = jnp.minimum(block_e, jnp.max(jnp.where(used, experts, 0)))
    zstart = jnp.where(padded > 0, ends - MOE_ROWS, -1).astype(jnp.int32)
    zstart = jnp.concatenate([zstart, n_used.reshape(1)])
    onehot = idx_t[:, :, None] == jnp.arange(N_EXPERTS, dtype=jnp.int32)[None, None, :]
    dest = rank_t + jnp.sum(jnp.where(onehot, base[None, None, :], 0), axis=-1)
    dest_flat = dest.reshape(-1).astype(jnp.int32)
    xs = moe_scatter(dest_flat, zstart, x, mods)
    later_used = jnp.logical_and(used[None, :], experts[None, :] > experts[:, None])
    next_used = jnp.min(jnp.where(later_used, experts[None, :], N_EXPERTS), axis=1)
    next_used = jnp.where(next_used == N_EXPERTS, -1, next_used).astype(jnp.int32)
    run_parity = ((jnp.cumsum(used.astype(jnp.int32)) - 1) % 2).astype(jnp.int32)
    of_block = block_e[:, None] == experts[None, :]
    block_next = jnp.sum(jnp.where(of_block, next_used[None, :], 0), axis=1).astype(jnp.int32)
    block_parity = jnp.sum(jnp.where(of_block, run_parity[None, :], 0), axis=1).astype(jnp.int32)
    ys = moe_experts(block_e, n_used.reshape(1), block_next, block_parity, xs,
                     w_g, w_u, w_d, b_g, b_u, b_d, layer)
    return moe_combine(dest_flat, x, mods, gates_t, ys, ln_g, ln_b, split_out=split_out)


def kernel(x_prompt, x_sample, c, cache_k, cache_v, state_hgrn, c_ctx, w_ada, b_ada, ln_g, ln_b,
           w_in_even, b_in_even, attn_sink, hgrn_lb, hgrn_norm, w_out_even,
           conv_w_in, conv_b_in, conv_dw, conv_dw_b, conv_ln_g, conv_ln_b, conv_w_out, conv_b_out,
           router_w, router_b, moe_w_gate, moe_b_gate, moe_w_up, moe_b_up, moe_w_down, moe_b_down):
    x = (x_prompt.reshape(N_PROMPT, D_MODEL), x_sample.reshape(N_SAMPLE, D_MODEL))
    cond = jnp.concatenate([c_ctx[None, :], c, jnp.zeros((COND_ROWS - N_COND, D_MODEL), F32)], axis=0)
    mods_all = adaln_all(cond, w_ada, b_ada).reshape(DEPTH, COND_ROWS, 6, D_MODEL)
    lb = jax.nn.softmax(hgrn_lb.astype(F32), axis=1)
    lb = jnp.cumsum(lb, axis=1) - lb[:, :1]
    cos, sin = _rope_tables()
    new_k, new_v, new_s = [], [], []
    for layer in range(DEPTH):
        j = layer // 2
        mods = mods_all[layer]
        if layer % 2 == 0:
            q, k, v, qb, ib, ff, fb, go = inproj_even(x, mods, w_in_even[j].astype(BF16), b_in_even[j], cos, sin)
            new_k.append(k[:N_PROMPT].reshape(BATCH, SEQ, N_KV_A, HEAD_DIM))
            new_v.append(v[:N_PROMPT].reshape(BATCH, SEQ, N_KV_A, HEAD_DIM))
            attn_p = attn_context(attn_sink[j], q, k, v)
            attn_s = attn_latent(attn_sink[j], q, k, v,
                                 cache_k[:, j].reshape(DEC_BATCH, PAST_LEN, A_KV),
                                 cache_v[:, j].reshape(DEC_BATCH, PAST_LEN, A_KV))
            hg = (qb, ib, ff, fb, go, lb[0, j], lb[1, j], hgrn_norm[j])
            rec_p, states = hgrn_mixer(*hg, latent=False)
            rec_s, = hgrn_mixer(*hg, latent=True, state_in=state_hgrn, layer_j=j)
            new_s.append(states)
            x = outproj_even(x, mods, attn_p, attn_s, rec_p, rec_s, w_out_even[j].astype(BF16),
                             ln_g[layer, 0], ln_b[layer, 0])
        else:
            u = conv_in(x, mods, conv_w_in[j].astype(BF16), conv_b_in[j])
            x = conv_out(x, mods, u, conv_dw[j], conv_dw_b[j], conv_ln_g[j], conv_ln_b[j],
                         conv_w_out[j].astype(BF16), conv_b_out[j], ln_g[layer, 0], ln_b[layer, 0])
        x = moe_layer(x, mods, layer, router_w, router_b, moe_w_gate, moe_b_gate, moe_w_up, moe_b_up,
                      moe_w_down, moe_b_down, ln_g[layer, 1], ln_b[layer, 1], split_out=layer == DEPTH - 1)
    y_prompt, y_sample = x
    return (y_prompt.reshape(BATCH, SEQ, D_MODEL),
            y_sample.reshape(DEC_BATCH, DEC_SEQ, D_MODEL),
            jnp.stack(new_k, axis=1), jnp.stack(new_v, axis=1), jnp.stack(new_s, axis=1))
```
